```python
import jax, jax.numpy as jnp
from jax import lax
import numpy as np

D_MODEL = 1024
BATCH = 4
SEQ = 8192
DEPTH = 2
DEC_BATCH = 32
DEC_SEQ = 8
PAST_LEN = 16384
PAGE_SIZE = 128

EPS = 1e-6
NEG_INF = -1e30

CHUNK = 128
A_GROUPS = 4
A_GROUP_DIM = 128
A_WIDTH = A_GROUPS * A_GROUP_DIM

SWA_PATTERNS = ((128, 1), (512, 4), (2048, 16))
N_SWA_GROUPS = 3
SWA_HEADS = 4
SWA_HEAD_DIM = 128
SWA_GROUP_WIDTH = SWA_HEADS * SWA_HEAD_DIM
SWA_WIDTH = N_SWA_GROUPS * SWA_GROUP_WIDTH
ROPE_THETA = 10000.0

RET_HEADS = 4
RET_QK_DIM = 64
RET_V_DIM = 128
RET_QK_WIDTH = RET_HEADS * RET_QK_DIM
RET_V_WIDTH = RET_HEADS * RET_V_DIM
RET_CHUNK = 128

N_BRANCH = 3
BRANCH_WIDTH = 512
IN_WIDTH = 2 * A_WIDTH + 3 * SWA_WIDTH + 2 * RET_QK_WIDTH + 2 * RET_V_WIDTH + N_BRANCH * D_MODEL

D_FF = 2816
N_EXPERTS = 8
TOP_K = 2
D_FF_EXPERT = 3584
MOE_BLOCK = 128
N_DENSE = (DEPTH + 1) // 2
N_MOE = DEPTH // 2

kernel_name = 'hybrid_gated_chunkmlp_dilswa_retention_step'


def _in_offsets():
    sizes = (A_WIDTH, A_WIDTH, SWA_WIDTH, SWA_WIDTH, SWA_WIDTH,
             RET_QK_WIDTH, RET_QK_WIDTH, RET_V_WIDTH, RET_V_WIDTH)
    offs, acc = [], 0
    for s_ in sizes:
        acc += s_
        offs.append(acc)
    return offs


def rms_norm(x, gain):
    xf = x.astype(jnp.float32)
    y = xf * lax.rsqrt(jnp.mean(xf * xf, axis=-1, keepdims=True) + EPS)
    return (y * gain.astype(jnp.float32)).astype(x.dtype)


def head_rms(x):
    xf = x.astype(jnp.float32)
    return (xf * lax.rsqrt(jnp.mean(xf * xf, axis=-1, keepdims=True) + EPS)).astype(x.dtype)


def apply_rope(x, ang):
    shape = (ang.shape[0],) + (1,) * (x.ndim - 3) + (ang.shape[1],)
    c = jnp.cos(ang).reshape(shape)
    s = jnp.sin(ang).reshape(shape)
    x1, x2 = jnp.split(x.astype(jnp.float32), 2, axis=-1)
    return jnp.concatenate([x1 * c - x2 * s, x1 * s + x2 * c], axis=-1).astype(x.dtype)


def chunk_mlp(u, v, w_s, b_s):
    n, l, _ = v.shape
    c = min(CHUNK, l)
    vr = v.reshape(n, l // c, c, A_GROUPS, A_GROUP_DIM)
    w = jnp.tril(w_s[:, :c, :c]).astype(v.dtype)
    z = jnp.einsum('gij,nkjgc->nkigc', w, vr) + b_s[:, :c].T.astype(v.dtype)[None, None, :, :, None]
    return u * z.reshape(n, l, A_WIDTH)


def dilated_window_prompt(q, k, v, dil, span):
    n, s, h, d = q.shape
    m = s // dil
    nb = -(-m // span)
    pad = nb * span - m

    def blocks(x):
        x = x.reshape(n, m, dil, h, d).transpose(0, 2, 1, 3, 4).reshape(n * dil, m, h, d)
        x = jnp.pad(x, ((0, 0), (0, pad), (0, 0), (0, 0)))
        return x.reshape(n * dil, nb, span, h, d)

    def with_prev(x):
        prev = jnp.pad(x, ((0, 0), (1, 0), (0, 0), (0, 0), (0, 0)))[:, :-1]
        return jnp.concatenate([prev, x], axis=2)

    qb = blocks(q)
    kk = with_prev(blocks(k))
    vv = with_prev(blocks(v))
    qi = jnp.arange(span)[:, None] + span
    ki = jnp.arange(2 * span)[None, :]
    band = (qi - ki >= 0) & (qi - ki <= span)
    kpos = jnp.arange(nb)[:, None] * span - span + jnp.arange(2 * span)[None, :]
    mask = band[None] & (kpos >= 0)[:, None, :]
    sc = jnp.einsum('rbqhd,rbkhd->rbhqk', qb, kk).astype(jnp.float32)
    sc = jnp.where(mask[None, :, None], sc, NEG_INF)
    lse = jax.nn.logsumexp(sc, axis=-1)
    p = jnp.exp(sc - lse[..., None]).astype(v.dtype)
    o = jnp.einsum('rbhqk,rbkhd->rbqhd', p, vv)

    def unblock(x):
        x = x.reshape((n, dil, nb * span) + x.shape[3:])[:, :, :m]
        return jnp.swapaxes(x, 1, 2).reshape((n, s) + x.shape[3:])

    return unblock(o), unblock(jnp.swapaxes(lse, 2, 3))


def dilated_window_sample(q, k_new, v_new, k_buf, v_buf, dil, span):
    n, t, h, d = q.shape
    lbuf = k_buf.shape[1]
    k_all = jnp.concatenate([k_buf.astype(k_new.dtype), k_new], axis=1)
    v_all = jnp.concatenate([v_buf.astype(v_new.dtype), v_new], axis=1)
    idx = lbuf + jnp.arange(t)[:, None] - dil * jnp.arange(span + 1)[None, :]
    valid = idx >= 0
    idx = jnp.maximum(idx, 0)
    kg = k_all[:, idx]
    vg = v_all[:, idx]
    sc = jnp.einsum('nthd,ntkhd->nhtk', q, kg).astype(jnp.float32)
    sc = jnp.where(valid[None, None], sc, NEG_INF)
    lse = jax.nn.logsumexp(sc, axis=-1)
    p = jnp.exp(sc - lse[..., None]).astype(v_new.dtype)
    o = jnp.einsum('nhtk,ntkhd->nthd', p, vg)
    return o, jnp.swapaxes(lse, 1, 2)


def retention(q, k, v, s0, chunk):
    n, l, h, dk = q.shape
    dv = v.shape[-1]
    nc = l // chunk
    log_g = jnp.log1p(-jnp.exp2(-5.0 - jnp.arange(h, dtype=jnp.float32)))
    i = jnp.arange(chunk, dtype=jnp.float32)
    dist = i[:, None] - i[None, :]
    decay = jnp.where(dist >= 0, jnp.exp(log_g[:, None, None] * jnp.maximum(dist, 0.0)), 0.0)
    q_in = jnp.exp(log_g[:, None] * (i[None, :] + 1.0)).T[None, :, :, None]
    k_out = jnp.exp(log_g[:, None] * (chunk - 1.0 - i)[None, :]).T[None, :, :, None]
    g_c = jnp.exp(log_g * chunk)[None, :, None, None]

    def to_chunks(x):
        return x.astype(jnp.float32).reshape(n, nc, chunk, h, x.shape[-1]).transpose(1, 0, 2, 3, 4)

    def step(s, inp):
        qc, kc, vc = inp
        att = jnp.einsum('nihd,njhd->nhij', qc, kc) * decay[None]
        o = jnp.einsum('nhij,njhe->nihe', att, vc) + jnp.einsum('nihd,nhde->nihe', qc, s) * q_in
        s = s * g_c + jnp.einsum('njhd,njhe->nhde', kc * k_out, vc)
        return s, o

    s_new, o = lax.scan(step, s0, (to_chunks(q), to_chunks(k), to_chunks(v)))
    return o.transpose(1, 0, 2, 3, 4).reshape(n, l, h, dv), s_new


def token_mixers(h, pos, w_in, norm_v_g, w_s, b_s, q_norm_g, k_norm_g, w_branch, w_out, kv_bufs, ret_s0):
    n, l, _ = h.shape
    f32 = jnp.float32
    a_u, a_v, b_q, b_k, b_v, c_q, c_k, c_v, c_g, gate_logits = jnp.split(h @ w_in, _in_offsets(), axis=-1)

    u = jax.nn.gelu(a_u, approximate=False)
    v_a = rms_norm(jax.nn.gelu(a_v, approximate=False), norm_v_g)
    out_a = chunk_mlp(u, v_a, w_s, b_s)

    freqs = ROPE_THETA ** (-jnp.arange(0, SWA_HEAD_DIM, 2, dtype=f32) / SWA_HEAD_DIM)
    ang = pos.astype(f32)[:, None] * freqs[None, :]
    hs5 = (n, l, N_SWA_GROUPS, SWA_HEADS, SWA_HEAD_DIM)
    q = apply_rope(rms_norm(b_q.reshape(hs5), q_norm_g), ang) * (SWA_HEAD_DIM ** -0.5)
    k = apply_rope(rms_norm(b_k.reshape(hs5), k_norm_g), ang)
    v = b_v.reshape(hs5)
    outs, lses, kv_rows = [], [], []
    for g, (win, dil) in enumerate(SWA_PATTERNS):
        span = win // dil
        qg, kg, vg = q[:, :, g], k[:, :, g], v[:, :, g]
        rows = jnp.stack([kg, vg], axis=2)
        if kv_bufs is None:
            o, lse = dilated_window_prompt(qg, kg, vg, dil, span)
            rows = rows[:, l - min(win, l):]
        else:
            buf = kv_bufs[g]
            o, lse = dilated_window_sample(qg, kg, vg, buf[:, :, 0], buf[:, :, 1], dil, span)
        outs.append(o)
        lses.append(lse)
        kv_rows.append(rows)
    mix_w = jax.nn.softmax(jnp.stack(lses), axis=0)
    out_b = jnp.einsum('gnlh,gnlhd->nlhd', mix_w, jnp.stack(outs).astype(f32))
    out_b = out_b.reshape(n, l, SWA_GROUP_WIDTH).astype(h.dtype)

    rfreqs = 1.0 / (ROPE_THETA ** jnp.linspace(0.0, 1.0, RET_QK_DIM // 2, dtype=f32))
    rang = pos.astype(f32)[:, None] * rfreqs[None, :]
    rq = apply_rope(c_q.reshape(n, l, RET_HEADS, RET_QK_DIM), rang)
    rk = apply_rope(c_k.reshape(n, l, RET_HEADS, RET_QK_DIM), rang) * (RET_QK_DIM ** -0.5)
    rv = c_v.reshape(n, l, RET_HEADS, RET_V_DIM)
    if ret_s0 is None:
        s0 = jnp.zeros((n, RET_HEADS, RET_QK_DIM, RET_V_DIM), f32)
    else:
        s0 = ret_s0.astype(f32)
    ro, s_new = retention(rq, rk, rv, s0, min(RET_CHUNK, l))
    out_c = jax.nn.silu(c_g) * head_rms(ro).reshape(n, l, RET_V_WIDTH).astype(h.dtype)

    gates = jax.nn.sigmoid(gate_logits.reshape(n, l, N_BRANCH, D_MODEL))
    merged = (gates[:, :, 0] * (out_a @ w_branch[0])
              + gates[:, :, 1] * (out_b @ w_branch[1])
              + gates[:, :, 2] * (out_c @ w_branch[2]))
    return merged @ w_out, kv_rows, s_new.astype(h.dtype), v_a


def swiglu(h, w1, w3, w2):
    return (jax.nn.silu(h @ w1) * (h @ w3)) @ w2


def moe_swiglu(h, w_router, we1, we3, we2):
    shape = h.shape
    xt = h.reshape(-1, shape[-1])
    nt = xt.shape[0]
    n_assign = nt * TOP_K
    logits = (xt @ w_router).astype(jnp.float32)
    top_val, top_idx = lax.top_k(logits, TOP_K)
    top_w = jax.nn.softmax(top_val, axis=-1)
    e_flat = top_idx.reshape(-1)
    tok_flat = jnp.arange(n_assign, dtype=jnp.int32) // TOP_K
    w_flat = top_w.reshape(-1)
    order = jnp.argsort(e_flat, stable=True)
    e_sorted = e_flat[order]
    counts = jnp.bincount(e_flat, length=N_EXPERTS)
    padded = (counts + MOE_BLOCK - 1) // MOE_BLOCK * MOE_BLOCK
    start = jnp.cumsum(counts) - counts
    pend = jnp.cumsum(padded)
    pstart = pend - padded
    dest = pstart[e_sorted] + jnp.arange(n_assign) - start[e_sorted]
    n_blocks = -(-n_assign // MOE_BLOCK) + N_EXPERTS
    n_rows = n_blocks * MOE_BLOCK
    row_tok = jnp.zeros((n_rows,), jnp.int32).at[dest].set(tok_flat[order])
    row_w = jnp.zeros((n_rows,), jnp.float32).at[dest].set(w_flat[order])
    blk_expert = jnp.minimum(jnp.searchsorted(pend, jnp.arange(n_blocks) * MOE_BLOCK, side='right'), N_EXPERTS - 1)

    def expert_block(args):
        tok, e = args
        return swiglu(xt[tok], we1[e], we3[e], we2[e])

    y_rows = lax.map(expert_block, (row_tok.reshape(n_blocks, MOE_BLOCK), blk_expert))
    y_rows = y_rows.reshape(n_rows, shape[-1]) * row_w[:, None].astype(h.dtype)
    return jnp.zeros_like(xt).at[row_tok].add(y_rows).reshape(shape)


def setup_inputs(seed: int = 0) -> dict:
    key = jax.random.key(seed)
    ks = jax.random.split(key, 26)
    f32 = jnp.float32

    def nrm(k, shape, scale):
        return jax.random.normal(k, shape, f32) * scale

    lb = [min(w, PAST_LEN) for w, _ in SWA_PATTERNS]
    kvs = (SWA_HEADS, SWA_HEAD_DIM)
    return {
        'x_prompt': nrm(ks[0], (BATCH, SEQ, D_MODEL), 1.0),
        'x_sample': nrm(ks[1], (DEC_BATCH, DEC_SEQ, D_MODEL), 1.0),
        'cache_swa_kv0': nrm(ks[2], (DEPTH, DEC_BATCH, lb[0], 2) + kvs, 1.0),
        'cache_swa_kv1': nrm(ks[3], (DEPTH, DEC_BATCH, lb[1], 2) + kvs, 1.0),
        'cache_swa_kv2': nrm(ks[4], (DEPTH, DEC_BATCH, lb[2], 2) + kvs, 1.0),
        'state_ret': nrm(ks[5], (DEPTH, DEC_BATCH, RET_HEADS, RET_QK_DIM, RET_V_DIM), 1.0),
        'norm1_g': 1.0 + nrm(ks[6], (DEPTH, D_MODEL), 0.02),
        'w_in': nrm(ks[7], (DEPTH, D_MODEL, IN_WIDTH), D_MODEL ** -0.5),
        'norm_v_g': 1.0 + nrm(ks[8], (DEPTH, A_WIDTH), 0.02),
        'w_s': nrm(ks[9], (DEPTH, A_GROUPS, CHUNK, CHUNK), CHUNK ** -0.5),
        'b_s': 1.0 + nrm(ks[10], (DEPTH, A_GROUPS, CHUNK), 0.02),
        'q_norm_g': 1.0 + nrm(ks[11], (DEPTH, SWA_HEAD_DIM), 0.02),
        'k_norm_g': 1.0 + nrm(ks[12], (DEPTH, SWA_HEAD_DIM), 0.02),
        'w_branch': nrm(ks[13], (DEPTH, N_BRANCH, BRANCH_WIDTH, D_MODEL), BRANCH_WIDTH ** -0.5),
        'w_out': nrm(ks[14], (DEPTH, D_MODEL, D_MODEL), D_MODEL ** -0.5),
        'norm2_g': 1.0 + nrm(ks[15], (DEPTH, D_MODEL), 0.02),
        'w1': nrm(ks[16], (N_DENSE, D_MODEL, D_FF), D_MODEL ** -0.5),
        'w3': nrm(ks[17], (N_DENSE, D_MODEL, D_FF), D_MODEL ** -0.5),
        'w2': nrm(ks[18], (N_DENSE, D_FF, D_MODEL), D_FF ** -0.5),
        'w_router': nrm(ks[19], (N_MOE, D_MODEL, N_EXPERTS), D_MODEL ** -0.5),
        'we1': nrm(ks[20], (N_MOE, N_EXPERTS, D_MODEL, D_FF_EXPERT), D_MODEL ** -0.5),
        'we3': nrm(ks[21], (N_MOE, N_EXPERTS, D_MODEL, D_FF_EXPERT), D_MODEL ** -0.5),
        'we2': nrm(ks[22], (N_MOE, N_EXPERTS, D_FF_EXPERT, D_MODEL), D_FF_EXPERT ** -0.5),
    }


def reference(x_prompt, x_sample, cache_swa_kv0, cache_swa_kv1, cache_swa_kv2, state_ret,
              norm1_g, w_in, norm_v_g, w_s, b_s, q_norm_g, k_norm_g, w_branch, w_out,
              norm2_g, w1, w3, w2, w_router, we1, we3, we2):
    xp, xs = x_prompt, x_sample
    pos_p = jnp.arange(xp.shape[1], dtype=jnp.int32)
    pos_s = PAST_LEN + jnp.arange(xs.shape[1], dtype=jnp.int32)
    caches = (cache_swa_kv0, cache_swa_kv1, cache_swa_kv2)
    p_kv = [[] for _ in SWA_PATTERNS]
    s_kv = [[] for _ in SWA_PATTERNS]
    p_ret, s_ret, s_v = [], [], []
    for layer in range(DEPTH):
        mw = (w_in[layer], norm_v_g[layer], w_s[layer], b_s[layer], q_norm_g[layer],
              k_norm_g[layer], w_branch[layer], w_out[layer])
        yp, kvp, rp, _ = token_mixers(rms_norm(xp, norm1_g[layer]), pos_p, *mw, None, None)
        ys, kvs, rs, vs = token_mixers(rms_norm(xs, norm1_g[layer]), pos_s, *mw,
                                       tuple(c[layer] for c in caches), state_ret[layer])
        xp = xp + yp
        xs = xs + ys
        for g in range(N_SWA_GROUPS):
            p_kv[g].append(kvp[g])
            s_kv[g].append(kvs[g])
        p_ret.append(rp)
        s_ret.append(rs)
        s_v.append(vs)
        hp = rms_norm(xp, norm2_g[layer])
        hs = rms_norm(xs, norm2_g[layer])
        i = layer // 2
        if layer % 2 == 0:
            xp = xp + swiglu(hp, w1[i], w3[i], w2[i])
            xs = xs + swiglu(hs, w1[i], w3[i], w2[i])
        else:
            xp = xp + moe_swiglu(hp, w_router[i], we1[i], we3[i], we2[i])
            xs = xs + moe_swiglu(hs, w_router[i], we1[i], we3[i], we2[i])
    y_prompt = xp
    y_sample = xs
    new_kv0_prompt = jnp.stack(p_kv[0])
    new_kv1_prompt = jnp.stack(p_kv[1])
    new_kv2_prompt = jnp.stack(p_kv[2])
    new_ret_prompt = jnp.stack(p_ret)
    new_kv0_sample = jnp.stack(s_kv[0])
    new_kv1_sample = jnp.stack(s_kv[1])
    new_kv2_sample = jnp.stack(s_kv[2])
    new_ret_sample = jnp.stack(s_ret)
    new_chunk_v_sample = jnp.stack(s_v)
    return (y_prompt, y_sample, new_kv0_prompt, new_kv1_prompt, new_kv2_prompt, new_ret_prompt,
            new_kv0_sample, new_kv1_sample, new_kv2_sample, new_ret_sample, new_chunk_v_sample)
```

```python
import functools
import math

import jax
import jax.numpy as jnp
from jax import lax
from jax.experimental import pallas as pl
from jax.experimental.pallas import tpu as pltpu

F32 = jnp.float32
BF16 = jnp.bfloat16

PAST_LEN = 16384
EPS = 1e-6
NEG_INF = -1e30
ROPE_THETA = 10000.0

D_MODEL = 1024
LANE = 128
CHUNK = 128
COL = 512
A_GROUPS = 4
SWA_PATTERNS = ((128, 1), (512, 4), (2048, 16))
SWA_HEADS = 4
SWA_HEAD_DIM = 128
RET_HEADS = 4
RET_QK_DIM = 64
RET_V_DIM = 128
N_EXPERTS = 8
TOP_K = 2
IN_WIDTH = 10240
N_COL = IN_WIDTH // COL

CB_AU, CB_AV, CB_Q, CB_K, CB_V, CB_RQK, CB_RV, CB_RG, CB_GATE = 0, 1, 2, 5, 8, 11, 12, 13, 14

VMEM_INTERNAL_SCRATCH = 8 * 1024 * 1024


def _pick(n, candidates):
    for c in candidates:
        if n % c == 0:
            return c
    raise ValueError(f"no tile in {candidates} divides {n}")


def _params(block_bytes, semantics=None):
    limit = int(block_bytes) + VMEM_INTERNAL_SCRATCH
    return pltpu.CompilerParams(dimension_semantics=semantics, vmem_limit_bytes=limit)


def _nbytes(shape, dtype):
    return math.prod(shape) * jnp.dtype(dtype).itemsize


def _rms(x):
    return x * lax.rsqrt(jnp.mean(x * x, axis=-1, keepdims=True) + EPS)


def _gelu(x):
    return 0.5 * x * (1.0 + lax.erf(x * (0.5 ** 0.5)))


def _sigmoid(x):
    return 1.0 / (1.0 + jnp.exp(-x))


def _idiv(x, n):
    assert n & (n - 1) == 0
    return x >> (n.bit_length() - 1)


def _imod(x, n):
    assert n & (n - 1) == 0
    return x & (n - 1)


def _dot(a, b):
    return jnp.dot(a, b, preferred_element_type=F32)


def _dot_nt(a, b):
    return lax.dot_general(a, b, (((1,), (1,)), ((), ())), preferred_element_type=F32)


def _inproj_kernel(x_ref, g1_ref, w_ref, cosb_ref, sinb_ref, cosc_ref, sinc_ref,
                   nvg_ref, qg_ref, kg_ref, o_ref, h_ref):
    j = pl.program_id(1)

    @pl.when(j == 0)
    def _():
        h_ref[...] = (_rms(x_ref[...]) * g1_ref[...]).astype(BF16)

    acc = _dot(h_ref[...], w_ref[...])

    def qk_heads(gain_ref, scale):
        for hh in range(SWA_HEADS):
            cs = slice(hh * LANE, (hh + 1) * LANE)
            y = _rms(acc[:, cs]) * gain_ref[...]
            rot = y * cosb_ref[...] + pltpu.roll(y, LANE // 2, axis=1) * sinb_ref[...]
            o_ref[:, cs] = (rot * scale).astype(o_ref.dtype)

    @pl.when(j == CB_AU)
    def _():
        o_ref[...] = _gelu(acc).astype(o_ref.dtype)

    @pl.when(j == CB_AV)
    def _():
        o_ref[...] = (_rms(_gelu(acc)) * nvg_ref[...]).astype(o_ref.dtype)

    @pl.when((j >= CB_Q) & (j < CB_K))
    def _():
        qk_heads(qg_ref, SWA_HEAD_DIM ** -0.5)

    @pl.when((j >= CB_K) & (j < CB_V))
    def _():
        qk_heads(kg_ref, 1.0)

    @pl.when(((j >= CB_V) & (j < CB_RQK)) | (j == CB_RV))
    def _():
        o_ref[...] = acc.astype(o_ref.dtype)

    @pl.when(j == CB_RQK)
    def _():
        lane = lax.broadcasted_iota(jnp.int32, (acc.shape[0], LANE), 1)
        first_half = _imod(lane, RET_QK_DIM) < (RET_QK_DIM // 2)
        for tt in range(COL // LANE):
            cs = slice(tt * LANE, (tt + 1) * LANE)
            y = acc[:, cs]
            partner = jnp.where(first_half,
                                pltpu.roll(y, LANE - RET_QK_DIM // 2, axis=1),
                                pltpu.roll(y, RET_QK_DIM // 2, axis=1))
            rot = y * cosc_ref[...] + partner * sinc_ref[...]
            scale = 1.0 if tt < (COL // LANE) // 2 else RET_QK_DIM ** -0.5
            o_ref[:, cs] = (rot * scale).astype(o_ref.dtype)

    @pl.when(j == CB_RG)
    def _():
        o_ref[...] = (acc * _sigmoid(acc)).astype(o_ref.dtype)

    @pl.when(j >= CB_GATE)
    def _():
        o_ref[...] = _sigmoid(acc).astype(o_ref.dtype)


def _inproj(x, g1, w_in_b, tabs, nvg, qg, kg):
    T = x.shape[0]
    tm = _pick(T, (768, 512, 384, 256, 128))
    cosb, sinb, cosc, sinc = tabs
    row = lambda i, j: (i, 0)
    const = lambda i, j: (0, 0)
    tab_spec = pl.BlockSpec((tm, LANE), row)
    vmem = (2 * _nbytes((tm, D_MODEL), F32) + _nbytes((tm, D_MODEL), BF16)
            + 2 * _nbytes((D_MODEL, COL), BF16) + 2 * _nbytes((tm, COL), BF16)
            + 8 * _nbytes((tm, LANE), F32) + 6 * _nbytes((tm, COL), F32))
    return pl.pallas_call(
        _inproj_kernel,
        grid=(T // tm, N_COL),
        in_specs=[
            pl.BlockSpec((tm, D_MODEL), row),
            pl.BlockSpec((1, D_MODEL), const),
            pl.BlockSpec((D_MODEL, COL), lambda i, j: (0, j)),
            tab_spec, tab_spec, tab_spec, tab_spec,
            pl.BlockSpec((1, COL), const),
            pl.BlockSpec((1, LANE), const),
            pl.BlockSpec((1, LANE), const),
        ],
        out_specs=pl.BlockSpec((tm, COL), lambda i, j: (i, j)),
        out_shape=jax.ShapeDtypeStruct((T, IN_WIDTH), BF16),
        scratch_shapes=[pltpu.VMEM((tm, D_MODEL), BF16)],
        compiler_params=_params(vmem, ("parallel", "arbitrary")),
        name="inproj",
    )(x, g1, w_in_b, cosb, sinb, cosc, sinc, nvg, qg, kg)


def _mixa_kernel(u_ref, v_ref, w_ref, b_ref, o_ref, *, cps, n_prompt_chunks):
    i = pl.program_id(0)
    for c in range(cps):
        var = ((i * cps + c) >= n_prompt_chunks).astype(jnp.int32)
        rows = slice(c * CHUNK, (c + 1) * CHUNK)
        for g in range(A_GROUPS):
            cols = slice(g * LANE, (g + 1) * LANE)
            z = _dot(w_ref[var, g], v_ref[rows, cols]) + b_ref[var, :, cols]
            o_ref[rows, cols] = (u_ref[rows, cols].astype(F32) * z).astype(o_ref.dtype)


def _mixer_a(P, w2, b2, n_prompt_chunks):
    T = P.shape[0]
    n_chunks = T // CHUNK
    cps = _pick(n_chunks, (8, 6, 4, 3, 2, 1))
    rows = cps * CHUNK
    vmem = 6 * _nbytes((rows, COL), BF16) + 2 * _nbytes(w2.shape, BF16) + 2 * _nbytes(b2.shape, F32)
    return pl.pallas_call(
        functools.partial(_mixa_kernel, cps=cps, n_prompt_chunks=n_prompt_chunks),
        grid=(n_chunks // cps,),
        in_specs=[
            pl.BlockSpec((rows, COL), lambda i: (i, CB_AU)),
            pl.BlockSpec((rows, COL), lambda i: (i, CB_AV)),
            pl.BlockSpec(w2.shape, lambda i: (0, 0, 0, 0)),
            pl.BlockSpec(b2.shape, lambda i: (0, 0, 0)),
        ],
        out_specs=pl.BlockSpec((rows, COL), lambda i: (i, 0)),
        out_shape=jax.ShapeDtypeStruct((T, COL), BF16),
        compiler_params=_params(vmem, ("parallel",)),
        name="mixer_a",
    )(P, P, w2, b2)


def _swa_kernel(q_ref, kp_ref, kc_ref, vp_ref, vc_ref, o_ref, l_ref):
    i = pl.program_id(2)
    row = lax.broadcasted_iota(jnp.int32, (CHUNK, CHUNK), 0)
    col = lax.broadcasted_iota(jnp.int32, (CHUNK, CHUNK), 1)
    mask_c = col <= row
    mask_p = (col >= row) & (i > 0)
    for h in range(SWA_HEADS):
        cs = slice(h * LANE, (h + 1) * LANE)
        q = q_ref[:, cs]
        s_c = jnp.where(mask_c, _dot_nt(q, kc_ref[:, cs]), NEG_INF)
        s_p = jnp.where(mask_p, _dot_nt(q, kp_ref[:, cs]), NEG_INF)
        m = jnp.maximum(jnp.max(s_c, axis=-1, keepdims=True), jnp.max(s_p, axis=-1, keepdims=True))
        p_c = jnp.exp(s_c - m)
        p_p = jnp.exp(s_p - m)
        den = jnp.sum(p_c, axis=-1, keepdims=True) + jnp.sum(p_p, axis=-1, keepdims=True)
        o = _dot(p_c.astype(BF16), vc_ref[:, cs]) + _dot(p_p.astype(BF16), vp_ref[:, cs])
        o_ref[:, cs] = (o / den).astype(o_ref.dtype)
        l_ref[:, cs] = jnp.broadcast_to(m + jnp.log(den), (CHUNK, LANE))


def _swa_prompt(P, g, dil, n_p, s):
    T = P.shape[0]
    nb = s // dil // CHUNK
    pv = P.reshape(T // dil, dil * IN_WIDTH)

    def spec(cb, prev):
        def imap(b, r, i):
            blk = jnp.maximum(i - 1, 0) if prev else i
            return (b * nb + blk, r * N_COL + cb + g)
        return pl.BlockSpec((CHUNK, COL), imap)

    out_map = lambda b, r, i: (b * nb + i, r)
    vmem = 10 * _nbytes((CHUNK, COL), BF16) + 2 * _nbytes((CHUNK, COL), BF16) + 2 * _nbytes((CHUNK, COL), F32)
    o, l = pl.pallas_call(
        _swa_kernel,
        grid=(n_p, dil, nb),
        in_specs=[spec(CB_Q, False), spec(CB_K, True), spec(CB_K, False), spec(CB_V, True), spec(CB_V, False)],
        out_specs=[pl.BlockSpec((CHUNK, COL), out_map), pl.BlockSpec((CHUNK, COL), out_map)],
        out_shape=[jax.ShapeDtypeStruct((T // dil, dil * COL), BF16),
                   jax.ShapeDtypeStruct((T // dil, dil * COL), F32)],
        compiler_params=_params(vmem, ("parallel", "parallel", "arbitrary")),
        name=f"swa_prompt_g{g}",
    )(pv, pv, pv, pv, pv)
    return o.reshape(T, COL), l.reshape(T, COL)


def _swa_sample_kernel(q_ref, kn_ref, vn_ref, cache_ref, o_ref, l_ref, *, dil, lbuf, t_s):
    nq = SWA_HEADS * t_s
    q = q_ref[...]
    qrep = jnp.concatenate([q] * SWA_HEADS, axis=0)
    rq = lax.broadcasted_iota(jnp.int32, (nq, COL), 0)
    cq = lax.broadcasted_iota(jnp.int32, (nq, COL), 1)
    qbd = jnp.where(_idiv(rq, t_s) == _idiv(cq, LANE), qrep, 0.0).astype(BF16)

    kc = cache_ref[:, 0:COL].astype(BF16)
    vc = cache_ref[:, COL:2 * COL].astype(BF16)
    kn = kn_ref[...].astype(BF16)
    vn = vn_ref[...].astype(BF16)

    s_c = _dot_nt(qbd, kc)
    s_n = _dot_nt(qbd, kn)
    t_c = _imod(lax.broadcasted_iota(jnp.int32, (nq, lbuf), 0), t_s)
    c_c = lax.broadcasted_iota(jnp.int32, (nq, lbuf), 1)
    diff_c = lbuf + t_c - c_c
    ok_c = (_imod(diff_c, dil) == 0) & (diff_c <= lbuf)
    t_n = _imod(lax.broadcasted_iota(jnp.int32, (nq, t_s), 0), t_s)
    c_n = lax.broadcasted_iota(jnp.int32, (nq, t_s), 1)
    diff_n = t_n - c_n
    ok_n = (diff_n >= 0) & (_imod(diff_n, dil) == 0)
    s_c = jnp.where(ok_c, s_c, NEG_INF)
    s_n = jnp.where(ok_n, s_n, NEG_INF)
    m = jnp.maximum(jnp.max(s_c, axis=-1, keepdims=True), jnp.max(s_n, axis=-1, keepdims=True))
    p_c = jnp.exp(s_c - m)
    p_n = jnp.exp(s_n - m)
    den = jnp.sum(p_c, axis=-1, keepdims=True) + jnp.sum(p_n, axis=-1, keepdims=True)
    o_all = (_dot(p_c.astype(BF16), vc) + _dot(p_n.astype(BF16), vn)) / den
    lse = m + jnp.log(den)
    for h in range(SWA_HEADS):
        cs = slice(h * LANE, (h + 1) * LANE)
        rs = slice(h * t_s, (h + 1) * t_s)
        o_ref[:, cs] = o_all[rs, cs]
        l_ref[:, cs] = jnp.broadcast_to(lse[rs], (t_s, LANE))


def _swa_sample(Ps, cache, layer, g, dil):
    n_s, t_s, _ = Ps.shape
    lbuf = cache.shape[2]
    assert lbuf == dil * CHUNK, "window buffer must hold exactly one full window"
    blk = lambda cb: pl.BlockSpec((None, t_s, COL), lambda b: (b, 0, cb + g))
    vmem = (2 * _nbytes((lbuf, 2 * COL), F32) + 2 * _nbytes((lbuf, 2 * COL), BF16)
            + 8 * _nbytes((SWA_HEADS * t_s, lbuf), F32))
    return pl.pallas_call(
        functools.partial(_swa_sample_kernel, dil=dil, lbuf=lbuf, t_s=t_s),
        grid=(n_s,),
        in_specs=[blk(CB_Q), blk(CB_K), blk(CB_V),
                  pl.BlockSpec((None, None, lbuf, 2 * COL), lambda b: (layer, b, 0, 0))],
        out_specs=[pl.BlockSpec((None, t_s, COL), lambda b: (b, 0, 0)),
                   pl.BlockSpec((None, t_s, COL), lambda b: (b, 0, 0))],
        out_shape=[jax.ShapeDtypeStruct((n_s, t_s, COL), F32),
                   jax.ShapeDtypeStruct((n_s, t_s, COL), F32)],
        compiler_params=_params(vmem, ("parallel",)),
        name=f"swa_sample_g{g}",
    )(Ps, Ps, Ps, cache)


def _ret_head_inputs(qk_ref, v_ref, kout_ref, h):
    pair, half = h // 2, h % 2
    lane = lax.broadcasted_iota(jnp.int32, (CHUNK, LANE), 1)
    head_lanes = _idiv(lane, RET_QK_DIM) == half
    qt = qk_ref[:, pair * LANE:(pair + 1) * LANE]
    kt = qk_ref[:, COL // 2 + pair * LANE:COL // 2 + (pair + 1) * LANE]
    qm = jnp.where(head_lanes, qt, jnp.zeros_like(qt))
    kw = jnp.where(head_lanes, kt.astype(F32) * kout_ref[:, pair * LANE:(pair + 1) * LANE], 0.0)
    vh = v_ref[:, h * LANE:(h + 1) * LANE]
    return qm, kt, kw, vh


def _ret_finish(o, gate_ref, o_ref, h):
    cs = slice(h * LANE, (h + 1) * LANE)
    o_ref[:, cs] = (gate_ref[:, cs].astype(F32) * _rms(o)).astype(o_ref.dtype)


def _ret_prompt_kernel(qk_ref, v_ref, gate_ref, decay_ref, qin_ref, kout_ref, gc_ref,
                       o_ref, s_out_ref, s_ref):
    i = pl.program_id(1)

    @pl.when(i == 0)
    def _():
        s_ref[...] = jnp.zeros_like(s_ref)

    for h in range(RET_HEADS):
        qm, kt, kw, vh = _ret_head_inputs(qk_ref, v_ref, kout_ref, h)
        att = _dot_nt(qm, kt) * decay_ref[h]
        state = s_ref[h]
        o = _dot(att.astype(BF16), vh) + _dot(qm, state.astype(BF16)) * qin_ref[:, h * LANE:(h + 1) * LANE]
        s_ref[h] = state * gc_ref[h:h + 1, :] + _dot(kw.T.astype(BF16), vh)
        _ret_finish(o, gate_ref, o_ref, h)

    @pl.when(i == pl.num_programs(1) - 1)
    def _():
        for h in range(RET_HEADS):
            lo = (h % 2) * RET_QK_DIM
            s_out_ref[h] = s_ref[h, lo:lo + RET_QK_DIM, :]


def _ret_prompt(P, rt, n_p, s):
    T = P.shape[0]
    nblk = s // CHUNK
    decay, qin, kout, gc = rt
    blk = lambda cb: pl.BlockSpec((CHUNK, COL), lambda b, i: (b * nblk + i, cb))
    const2 = lambda b, i: (0, 0)
    vmem = (8 * _nbytes((CHUNK, COL), BF16) + 2 * _nbytes(decay.shape, F32) + 4 * _nbytes(qin.shape, F32)
            + 3 * _nbytes((RET_HEADS, LANE, LANE), F32))
    return pl.pallas_call(
        _ret_prompt_kernel,
        grid=(n_p, nblk),
        in_specs=[blk(CB_RQK), blk(CB_RV), blk(CB_RG),
                  pl.BlockSpec(decay.shape, lambda b, i: (0, 0, 0)),
                  pl.BlockSpec(qin.shape, const2), pl.BlockSpec(kout.shape, const2),
                  pl.BlockSpec(gc.shape, const2)],
        out_specs=[pl.BlockSpec((CHUNK, COL), lambda b, i: (b * nblk + i, 0)),
                   pl.BlockSpec((None, RET_HEADS, RET_QK_DIM, RET_V_DIM), lambda b, i: (b, 0, 0, 0))],
        out_shape=[jax.ShapeDtypeStruct((T, COL), BF16),
                   jax.ShapeDtypeStruct((n_p, RET_HEADS, RET_QK_DIM, RET_V_DIM), F32)],
        scratch_shapes=[pltpu.VMEM((RET_HEADS, LANE, LANE), F32)],
        compiler_params=_params(vmem, ("parallel", "arbitrary")),
        name="ret_prompt",
    )(P, P, P, decay, qin, kout, gc)


def _ret_sample_kernel(qk_ref, v_ref, gate_ref, s0_ref, decay_ref, qin_ref, kout_ref, gc_ref,
                       oc_in_ref, o_ref, s_out_ref, *, t_s):
    del oc_in_ref
    row = lax.broadcasted_iota(jnp.int32, (CHUNK, LANE), 0)
    for h in range(RET_HEADS):
        lo = (h % 2) * RET_QK_DIM
        qm, kt, kw, vh = _ret_head_inputs(qk_ref, v_ref, kout_ref, h)
        att = _dot_nt(qm, kt) * decay_ref[h]
        o = _dot(att.astype(BF16), vh)
        inter = jnp.zeros((CHUNK, LANE), F32)
        for sq in range(CHUNK // t_s):
            seq_rows = _idiv(row, t_s) == sq
            st = s0_ref[sq, h]
            st2 = jnp.concatenate([st, st], axis=0).astype(BF16)
            inter = jnp.where(seq_rows, _dot(qm, st2), inter)
            upd = _dot(jnp.where(seq_rows, kw, 0.0).T.astype(BF16), vh)
            s_out_ref[sq, h] = st * gc_ref[h:h + 1, :] + upd[lo:lo + RET_QK_DIM, :]
        o = o + inter * qin_ref[:, h * LANE:(h + 1) * LANE]
        _ret_finish(o, gate_ref, o_ref, h)


def _ret_sample(P, out_c, state, layer, rt, Tp, t_s):
    T = P.shape[0]
    n_s = state.shape[1]
    spb = CHUNK // t_s
    base = Tp // CHUNK
    decay, qin, kout, gc = rt
    blk = lambda cb: pl.BlockSpec((CHUNK, COL), lambda i: (base + i, cb))
    const2 = lambda i: (0, 0)
    st_shape = (spb, RET_HEADS, RET_QK_DIM, RET_V_DIM)
    vmem = (8 * _nbytes((CHUNK, COL), BF16) + 2 * _nbytes(decay.shape, F32) + 4 * _nbytes(qin.shape, F32)
            + 4 * _nbytes(st_shape, F32))
    return pl.pallas_call(
        functools.partial(_ret_sample_kernel, t_s=t_s),
        grid=(n_s // spb,),
        in_specs=[blk(CB_RQK), blk(CB_RV), blk(CB_RG),
                  pl.BlockSpec((None,) + st_shape, lambda i: (layer, i, 0, 0, 0)),
                  pl.BlockSpec(decay.shape, lambda i: (0, 0, 0)),
                  pl.BlockSpec(qin.shape, const2), pl.BlockSpec(kout.shape, const2),
                  pl.BlockSpec(gc.shape, const2),
                  pl.BlockSpec(memory_space=pl.ANY)],
        out_specs=[pl.BlockSpec((CHUNK, COL), lambda i: (base + i, 0)),
                   pl.BlockSpec(st_shape, lambda i: (i, 0, 0, 0))],
        out_shape=[jax.ShapeDtypeStruct((T, COL), BF16),
                   jax.ShapeDtypeStruct((n_s, RET_HEADS, RET_QK_DIM, RET_V_DIM), F32)],
        input_output_aliases={8: 0},
        compiler_params=_params(vmem, ("parallel",)),
        name="ret_sample",
    )(P, P, P, state, decay, qin, kout, gc, out_c)


def _branch_kernel(x_ref, a_ref, o0_ref, o1_ref, o2_ref, l0_ref, l1_ref, l2_ref, c_ref,
                   g0_ref, g1_ref, g2_ref, wb_ref, wo_ref, y_ref):
    l0, l1, l2 = l0_ref[...], l1_ref[...], l2_ref[...]
    lmax = jnp.maximum(jnp.maximum(l0, l1), l2)
    e0, e1, e2 = jnp.exp(l0 - lmax), jnp.exp(l1 - lmax), jnp.exp(l2 - lmax)
    mix = (e0 * o0_ref[...].astype(F32) + e1 * o1_ref[...].astype(F32) + e2 * o2_ref[...].astype(F32)) / (e0 + e1 + e2)
    merged = (g0_ref[...].astype(F32) * _dot(a_ref[...], wb_ref[0])
              + g1_ref[...].astype(F32) * _dot(mix.astype(BF16), wb_ref[1])
              + g2_ref[...].astype(F32) * _dot(c_ref[...], wb_ref[2]))
    y_ref[...] = x_ref[...] + _dot(merged.astype(BF16), wo_ref[...])


def _branch(x, out_a, o_list, l_list, out_c, P, wb, wo):
    T = x.shape[0]
    tm = _pick(T, (768, 512, 384, 256, 128))
    row = lambda i: (i, 0)
    sb = pl.BlockSpec((tm, COL), row)
    gate = lambda k: pl.BlockSpec((tm, D_MODEL), lambda i: (i, CB_GATE * COL // D_MODEL + k))
    once = pl.Buffered(1)
    vmem = (4 * _nbytes((tm, D_MODEL), F32) + 10 * _nbytes((tm, COL), BF16) + 6 * _nbytes((tm, COL), F32)
            + 6 * _nbytes((tm, D_MODEL), BF16) + _nbytes(wb.shape, BF16) + _nbytes(wo.shape, BF16)
            + 4 * _nbytes((tm, D_MODEL), F32))
    return pl.pallas_call(
        _branch_kernel,
        grid=(T // tm,),
        in_specs=[pl.BlockSpec((tm, D_MODEL), row), sb, sb, sb, sb, sb, sb, sb, sb,
                  gate(0), gate(1), gate(2),
                  pl.BlockSpec(wb.shape, lambda i: (0, 0, 0), pipeline_mode=once),
                  pl.BlockSpec(wo.shape, lambda i: (0, 0), pipeline_mode=once)],
        out_specs=pl.BlockSpec((tm, D_MODEL), row),
        out_shape=jax.ShapeDtypeStruct((T, D_MODEL), F32),
        compiler_params=_params(vmem, ("parallel",)),
        name="branch_merge",
    )(x, out_a, *o_list, *l_list, out_c, P, P, P, wb, wo)


def _swiglu_into(hb, w1_ref, w3_ref, w2_ref, acc_ref, fc):
    n_fc = w1_ref.shape[-1] // fc
    for f in range(n_fc):
        cols = slice(f * fc, (f + 1) * fc)
        a = _dot(hb, w1_ref[:, cols])
        b = _dot(hb, w3_ref[:, cols])
        part = _dot((a * _sigmoid(a) * b).astype(BF16), w2_ref[cols, :])
        if f == 0:
            acc_ref[...] = part
        else:
            acc_ref[...] += part


def _dense_ffn_kernel(x_ref, g2_ref, w1_ref, w3_ref, w2_ref, y_ref, acc_ref, *, fc):
    x = x_ref[...]
    hb = (_rms(x) * g2_ref[...]).astype(BF16)
    _swiglu_into(hb, w1_ref, w3_ref, w2_ref, acc_ref, fc)
    y_ref[...] = x + acc_ref[...]


def _dense_ffn(x, g2, w1b, w3b, w2b):
    T = x.shape[0]
    F = w1b.shape[1]
    tm = _pick(T, (768, 512, 384, 256, 128))
    fc = _pick(F, (512, 256, 128))
    row = lambda i: (i, 0)
    once = pl.Buffered(1)
    vmem = (5 * _nbytes((tm, D_MODEL), F32) + 3 * _nbytes(w1b.shape, BF16) + _nbytes((tm, D_MODEL), BF16)
            + 4 * _nbytes((tm, fc), F32))
    return pl.pallas_call(
        functools.partial(_dense_ffn_kernel, fc=fc),
        grid=(T // tm,),
        in_specs=[pl.BlockSpec((tm, D_MODEL), row),
                  pl.BlockSpec((1, D_MODEL), lambda i: (0, 0)),
                  pl.BlockSpec(w1b.shape, lambda i: (0, 0), pipeline_mode=once),
                  pl.BlockSpec(w3b.shape, lambda i: (0, 0), pipeline_mode=once),
                  pl.BlockSpec(w2b.shape, lambda i: (0, 0), pipeline_mode=once)],
        out_specs=pl.BlockSpec((tm, D_MODEL), row),
        out_shape=jax.ShapeDtypeStruct((T, D_MODEL), F32),
        scratch_shapes=[pltpu.VMEM((tm, D_MODEL), F32)],
        compiler_params=_params(vmem, ("parallel",)),
        name="dense_ffn",
    )(x, g2, w1b, w3b, w2b)


def _router_kernel(x_ref, g2_ref, wr_ref, idx_ref, wgt_ref):
    hb = (_rms(x_ref[...]) * g2_ref[...]).astype(BF16)
    logits = _dot(hb, wr_ref[...])
    lane = lax.broadcasted_iota(jnp.int32, logits.shape, 1)
    logits = jnp.where(lane < N_EXPERTS, logits, -jnp.inf)
    lane_f = lane.astype(F32)
    m1 = jnp.max(logits, axis=-1, keepdims=True)
    i1 = jnp.min(jnp.where(logits == m1, lane_f, float(LANE)), axis=-1, keepdims=True)
    rest = jnp.where(lane_f == i1, -jnp.inf, logits)
    m2 = jnp.max(rest, axis=-1, keepdims=True)
    i2 = jnp.min(jnp.where(rest == m2, lane_f, float(LANE)), axis=-1, keepdims=True)
    e = jnp.exp(m2 - m1)
    idx_ref[...] = jnp.where(lane == 0, i1, jnp.where(lane == 1, i2, 0.0)).astype(jnp.int32)
    wgt_ref[...] = jnp.where(lane == 0, 1.0 / (1.0 + e), jnp.where(lane == 1, e / (1.0 + e), 0.0))


def _router(x, g2, wr_pad):
    T = x.shape[0]
    tm = _pick(T, (768, 512, 384, 256, 128))
    row = lambda i: (i, 0)
    vmem = 4 * _nbytes((tm, D_MODEL), F32) + 2 * _nbytes(wr_pad.shape, BF16) + 8 * _nbytes((tm, LANE), F32)
    return pl.pallas_call(
        _router_kernel,
        grid=(T // tm,),
        in_specs=[pl.BlockSpec((tm, D_MODEL), row),
                  pl.BlockSpec((1, D_MODEL), lambda i: (0, 0)),
                  pl.BlockSpec(wr_pad.shape, lambda i: (0, 0))],
        out_specs=[pl.BlockSpec((tm, LANE), row), pl.BlockSpec((tm, LANE), row)],
        out_shape=[jax.ShapeDtypeStruct((T, LANE), jnp.int32), jax.ShapeDtypeStruct((T, LANE), F32)],
        compiler_params=_params(vmem, ("parallel",)),
        name="moe_router",
    )(x, g2, wr_pad)


def _row_copy(src_ref, src_row, dst_ref, dst_row, sem):
    return pltpu.make_async_copy(src_ref.at[pl.ds(src_row, 1), :], dst_ref.at[pl.ds(dst_row, 1), :], sem)


def _dispatch_kernel(dest_ref, x_ref, g2_ref, xs_in_ref, xs_ref, h_ref, sem, *, tm):
    del xs_in_ref
    h_ref[...] = _rms(x_ref[...]) * g2_ref[...]

    def issue(r, carry):
        for k in range(TOP_K):
            _row_copy(h_ref, r, xs_ref, dest_ref[0, 0, TOP_K * r + k], sem).start()
        return carry

    lax.fori_loop(0, tm, issue, 0)

    def drain(r, carry):
        for k in range(TOP_K):
            _row_copy(h_ref, 0, xs_ref, 0, sem).wait()
        return carry

    lax.fori_loop(0, tm, drain, 0)


def _dispatch(x, g2, dest3, n_rows):
    T = x.shape[0]
    tm = dest3.shape[2] // TOP_K
    xs0 = jnp.zeros((n_rows, D_MODEL), F32)
    vmem = 3 * _nbytes((tm, D_MODEL), F32)
    return pl.pallas_call(
        functools.partial(_dispatch_kernel, tm=tm),
        grid=(T // tm,),
        in_specs=[pl.BlockSpec((1, 1, TOP_K * tm), lambda i: (i, 0, 0), memory_space=pltpu.SMEM),
                  pl.BlockSpec((tm, D_MODEL), lambda i: (i, 0)),
                  pl.BlockSpec((1, D_MODEL), lambda i: (0, 0)),
                  pl.BlockSpec(memory_space=pl.ANY)],
        out_specs=pl.BlockSpec(memory_space=pl.ANY),
        out_shape=jax.ShapeDtypeStruct((n_rows, D_MODEL), F32),
        scratch_shapes=[pltpu.VMEM((tm, D_MODEL), F32), pltpu.SemaphoreType.DMA(())],
        input_output_aliases={3: 0},
        compiler_params=_params(vmem, ("arbitrary",)),
        name="moe_dispatch",
    )(dest3, x, g2, xs0)


def _expert_ffn_kernel(te_ref, xs_ref, w1_ref, w3_ref, w2_ref, y_ref, acc_ref, *, fc):
    del te_ref
    _swiglu_into(xs_ref[...].astype(BF16), w1_ref, w3_ref, w2_ref, acc_ref, fc)
    y_ref[...] = acc_ref[...]


def _expert_ffn(xs, tile_expert, we1b, we3b, we2b, tme):
    n_rows = xs.shape[0]
    F = we1b.shape[2]
    fc = _pick(F, (512, 256, 128))
    once = pl.Buffered(1)
    wspec = lambda shape: pl.BlockSpec((None,) + shape, lambda i, te: (te[i], 0, 0), pipeline_mode=once)
    vmem = (5 * _nbytes((tme, D_MODEL), F32) + 3 * _nbytes(we1b.shape[1:], BF16) + _nbytes((tme, D_MODEL), BF16)
            + 4 * _nbytes((tme, fc), F32))
    grid_spec = pltpu.PrefetchScalarGridSpec(
        num_scalar_prefetch=1,
        grid=(n_rows // tme,),
        in_specs=[pl.BlockSpec((tme, D_MODEL), lambda i, te: (i, 0)),
                  wspec(we1b.shape[1:]), wspec(we3b.shape[1:]), wspec(we2b.shape[1:])],
        out_specs=pl.BlockSpec((tme, D_MODEL), lambda i, te: (i, 0)),
        scratch_shapes=[pltpu.VMEM((tme, D_MODEL), F32)],
    )
    return pl.pallas_call(
        functools.partial(_expert_ffn_kernel, fc=fc),
        grid_spec=grid_spec,
        out_shape=jax.ShapeDtypeStruct((n_rows, D_MODEL), F32),
        compiler_params=_params(vmem, ("arbitrary",)),
        name="moe_expert_ffn",
    )(tile_expert, xs, we1b, we3b, we2b)


def _combine_kernel(dest_ref, x_ref, wgt_ref, y_hbm_ref, o_ref, buf_ref, sem, *, tm):
    def issue(r, carry):
        for k in range(TOP_K):
            _row_copy(y_hbm_ref, dest_ref[0, 0, TOP_K * r + k], buf_ref.at[k], r, sem).start()
        return carry

    lax.fori_loop(0, tm, issue, 0)

    def drain(r, carry):
        for k in range(TOP_K):
            _row_copy(y_hbm_ref, 0, buf_ref.at[k], 0, sem).wait()
        return carry

    lax.fori_loop(0, tm, drain, 0)
    wgt = wgt_ref[...]
    o_ref[...] = x_ref[...] + wgt[:, 0:1] * buf_ref[0] + wgt[:, 1:2] * buf_ref[1]


def _combine(x, wgt, dest3, y_rows):
    T = x.shape[0]
    tm = dest3.shape[2] // TOP_K
    row = lambda i: (i, 0)
    vmem = 7 * _nbytes((tm, D_MODEL), F32) + 2 * _nbytes((tm, LANE), F32)
    return pl.pallas_call(
        functools.partial(_combine_kernel, tm=tm),
        grid=(T // tm,),
        in_specs=[pl.BlockSpec((1, 1, TOP_K * tm), lambda i: (i, 0, 0), memory_space=pltpu.SMEM),
                  pl.BlockSpec((tm, D_MODEL), row),
                  pl.BlockSpec((tm, LANE), row),
                  pl.BlockSpec(memory_space=pl.ANY)],
        out_specs=pl.BlockSpec((tm, D_MODEL), row),
        out_shape=jax.ShapeDtypeStruct((T, D_MODEL), F32),
        scratch_shapes=[pltpu.VMEM((TOP_K, tm, D_MODEL), F32), pltpu.SemaphoreType.DMA(())],
        compiler_params=_params(vmem, ("arbitrary",)),
        name="moe_combine",
    )(dest3, x, wgt, y_rows)


def _moe(x, g2, wr_pad, we1b, we3b, we2b):
    T = x.shape[0]
    tme = 512
    tmd = _pick(T, (256, 128))
    idx, wgt = _router(x, g2, wr_pad)
    e_flat = idx[:, :TOP_K].reshape(-1)
    onehot = (e_flat[None, :] == jnp.arange(N_EXPERTS, dtype=jnp.int32)[:, None]).astype(jnp.int32)
    csum = jnp.cumsum(onehot, axis=1)
    counts = csum[:, -1]
    padded = (counts + tme - 1) // tme * tme
    pend = jnp.cumsum(padded)
    pstart = pend - padded
    dest = jnp.sum(onehot * (csum - 1 + pstart[:, None]), axis=0).astype(jnp.int32)
    n_tiles = -(-(T * TOP_K) // tme) + N_EXPERTS
    tile_start = jnp.arange(n_tiles, dtype=jnp.int32) * tme
    tile_expert = jnp.minimum(jnp.sum(pend[None, :] <= tile_start[:, None], axis=1), N_EXPERTS - 1).astype(jnp.int32)
    dest3 = dest.reshape(T // tmd, 1, TOP_K * tmd)
    xs = _dispatch(x, g2, dest3, n_tiles * tme)
    y_rows = _expert_ffn(xs, tile_expert, we1b, we3b, we2b, tme)
    return _combine(x, wgt, dest3, y_rows)


def _rope_tables(pos):
    posf = pos.astype(F32)[:, None]
    freqs = ROPE_THETA ** (-jnp.arange(0, SWA_HEAD_DIM, 2, dtype=F32) / SWA_HEAD_DIM)
    ang = posf * freqs[None, :]
    c, s = jnp.cos(ang), jnp.sin(ang)
    cosb = jnp.concatenate([c, c], axis=1)
    sinb = jnp.concatenate([-s, s], axis=1)
    rfreqs = 1.0 / (ROPE_THETA ** jnp.linspace(0.0, 1.0, RET_QK_DIM // 2, dtype=F32))
    rang = posf * rfreqs[None, :]
    c, s = jnp.cos(rang), jnp.sin(rang)
    cosc = jnp.concatenate([c, c, c, c], axis=1)
    sinc = jnp.concatenate([-s, s, -s, s], axis=1)
    return cosb, sinb, cosc, sinc


def _ret_tables(c_len):
    log_g = jnp.log1p(-jnp.exp2(-5.0 - jnp.arange(RET_HEADS, dtype=F32)))
    r = jnp.arange(CHUNK)
    i = (r % c_len).astype(F32)
    same = (r[:, None] // c_len) == (r[None, :] // c_len)
    dist = i[:, None] - i[None, :]
    decay = jnp.where(same[None] & (dist >= 0)[None],
                      jnp.exp(log_g[:, None, None] * jnp.maximum(dist, 0.0)[None]), 0.0)
    qin = jnp.repeat(jnp.exp(log_g[None, :] * (i[:, None] + 1.0)), RET_V_DIM, axis=1)
    kout = jnp.repeat(jnp.exp(log_g[None, :] * (c_len - 1.0 - i)[:, None]), RET_QK_DIM, axis=1)
    gc = jnp.broadcast_to(jnp.exp(log_g * c_len)[:, None], (RET_HEADS, LANE))
    gc = jnp.concatenate([gc, jnp.zeros((8 - RET_HEADS, LANE), F32)], axis=0)
    return decay.astype(F32), qin.astype(F32), kout.astype(F32), gc.astype(F32)


def _mixa_tables(w_s, b_s, t_s):
    w_p = jnp.tril(w_s)
    w8 = jnp.tril(w_s[:, :t_s, :t_s])
    eye = jnp.eye(CHUNK // t_s, dtype=w_s.dtype)
    w_smp = jax.vmap(lambda m: jnp.kron(eye, m))(w8)
    w2 = jnp.stack([w_p, w_smp]).astype(BF16)
    b_p = jnp.repeat(b_s.T, LANE, axis=1)
    b_smp = jnp.repeat(jnp.tile(b_s[:, :t_s].T, (CHUNK // t_s, 1)), LANE, axis=1)
    return w2, jnp.stack([b_p, b_smp]).astype(F32)


def kernel(x_prompt, x_sample, cache_swa_kv0, cache_swa_kv1, cache_swa_kv2, state_ret, norm1_g, w_in, norm_v_g, w_s, b_s, q_norm_g, k_norm_g, w_branch, w_out, norm2_g, w1, w3, w2, w_router, we1, we3, we2):
    n_p, s, d = x_prompt.shape
    n_s, t_s, _ = x_sample.shape
    depth = w_in.shape[0]
    Tp, Ts = n_p * s, n_s * t_s
    T = Tp + Ts
    assert d == D_MODEL and Tp % CHUNK == 0 and Ts % CHUNK == 0 and CHUNK % t_s == 0
    assert T % SWA_PATTERNS[-1][1] == 0

    x = jnp.concatenate([x_prompt.reshape(Tp, d), x_sample.reshape(Ts, d)], axis=0)
    pos = jnp.concatenate([jnp.tile(jnp.arange(s, dtype=jnp.int32), n_p),
                           jnp.tile(PAST_LEN + jnp.arange(t_s, dtype=jnp.int32), n_s)])
    tabs = _rope_tables(pos)
    rt_prompt = _ret_tables(CHUNK)
    rt_sample = _ret_tables(t_s)
    caches = tuple(c.reshape(c.shape[0], c.shape[1], c.shape[2], 2 * COL)
                   for c in (cache_swa_kv0, cache_swa_kv1, cache_swa_kv2))

    p_kv = [[] for _ in SWA_PATTERNS]
    s_kv = [[] for _ in SWA_PATTERNS]
    p_ret, s_ret, s_v = [], [], []
    for layer in range(depth):
        w_in_b = w_in[layer].astype(BF16)
        P = _inproj(x, norm1_g[layer][None, :], w_in_b, tabs, norm_v_g[layer][None, :],
                    q_norm_g[layer][None, :], k_norm_g[layer][None, :])

        w2a, b2a = _mixa_tables(w_s[layer], b_s[layer], t_s)
        out_a = _mixer_a(P, w2a, b2a, Tp // CHUNK)

        Ps = P[Tp:].astype(F32).reshape(n_s, t_s, IN_WIDTH)
        o_list, l_list = [], []
        for g, (win, dil) in enumerate(SWA_PATTERNS):
            o_g, l_g = _swa_prompt(P, g, dil, n_p, s)
            os_g, ls_g = _swa_sample(Ps, caches[g], layer, g, dil)
            o_list.append(lax.dynamic_update_slice(o_g, os_g.reshape(Ts, COL).astype(BF16), (Tp, 0)))
            l_list.append(lax.dynamic_update_slice(l_g, ls_g.reshape(Ts, COL), (Tp, 0)))

        out_c, ret_p = _ret_prompt(P, rt_prompt, n_p, s)
        out_c, ret_s = _ret_sample(P, out_c, state_ret, layer, rt_sample, Tp, t_s)

        x = _branch(x, out_a, o_list, l_list, out_c, P, w_branch[layer].astype(BF16), w_out[layer].astype(BF16))

        g2 = norm2_g[layer][None, :]
        i = layer // 2
        if layer % 2 == 0:
            x = _dense_ffn(x, g2, w1[i].astype(BF16), w3[i].astype(BF16), w2[i].astype(BF16))
        else:
            wr_pad = jnp.zeros((D_MODEL, LANE), BF16).at[:, :N_EXPERTS].set(w_router[i].astype(BF16))
            x = _moe(x, g2, wr_pad, we1[i].astype(BF16), we3[i].astype(BF16), we2[i].astype(BF16))

        for g, (win, dil) in enumerate(SWA_PATTERNS):
            kcols = slice((CB_K + g) * COL, (CB_K + g + 1) * COL)
            vcols = slice((CB_V + g) * COL, (CB_V + g + 1) * COL)
            keep = min(win, s)
            kp = P[:Tp, kcols].reshape(n_p, s, SWA_HEADS, SWA_HEAD_DIM)[:, s - keep:]
            vp = P[:Tp, vcols].reshape(n_p, s, SWA_HEADS, SWA_HEAD_DIM)[:, s - keep:]
            p_kv[g].append(jnp.stack([kp, vp], axis=2).astype(F32))
            ks = P[Tp:, kcols].reshape(n_s, t_s, SWA_HEADS, SWA_HEAD_DIM)
            vs = P[Tp:, vcols].reshape(n_s, t_s, SWA_HEADS, SWA_HEAD_DIM)
            s_kv[g].append(jnp.stack([ks, vs], axis=2).astype(F32))
        p_ret.append(ret_p)
        s_ret.append(ret_s)
        s_v.append(P[Tp:, CB_AV * COL:(CB_AV + 1) * COL].astype(F32).reshape(n_s, t_s, COL))

    y_prompt = x[:Tp].reshape(n_p, s, d)
    y_sample = x[Tp:].reshape(n_s, t_s, d)
    return (y_prompt, y_sample,
            jnp.stack(p_kv[0]), jnp.stack(p_kv[1]), jnp.stack(p_kv[2]), jnp.stack(p_ret),
            jnp.stack(s_kv[0]), jnp.stack(s_kv[1]), jnp.stack(s_kv[2]), jnp.stack(s_ret),
            jnp.stack(s_v))
```

```python
import functools
import math

import jax
import jax.numpy as jnp
from jax import lax
from jax.experimental import pallas as pl
from jax.experimental.pallas import tpu as pltpu

F32 = jnp.float32
BF16 = jnp.bfloat16

PAST_LEN = 16384
EPS = 1e-6
NEG_INF = -1e30
ROPE_THETA = 10000.0

D_MODEL = 1024
LANE = 128
CHUNK = 128
TILE = 256
COL = 512
A_GROUPS = 4
SWA_PATTERNS = ((128, 1), (512, 4), (2048, 16))
SWA_HEADS = 4
SWA_HEAD_DIM = 128
RET_HEADS = 4
RET_QK_DIM = 64
RET_V_DIM = 128
N_EXPERTS = 8
TOP_K = 2
IN_WIDTH = 10240
N_COL = IN_WIDTH // COL

CB_AU, CB_AV, CB_Q, CB_K, CB_V, CB_RQK, CB_RV, CB_RG, CB_GATE = 0, 1, 2, 5, 8, 11, 12, 13, 14

VMEM_INTERNAL_SCRATCH = 8 * 1024 * 1024


def _pick(n, candidates):
    for c in candidates:
        if n % c == 0:
            return c
    raise ValueError(f"no tile in {candidates} divides {n}")


def _params(block_bytes, semantics=None):
    limit = int(block_bytes) + VMEM_INTERNAL_SCRATCH
    return pltpu.CompilerParams(dimension_semantics=semantics, vmem_limit_bytes=limit)


def _nbytes(shape, dtype):
    return math.prod(shape) * jnp.dtype(dtype).itemsize


def _rms(x):
    return x * lax.rsqrt(jnp.mean(x * x, axis=-1, keepdims=True) + EPS)


def _gelu(x):
    return 0.5 * x * (1.0 + lax.erf(x * (0.5 ** 0.5)))


def _sigmoid(x):
    return 1.0 / (1.0 + jnp.exp(-x))


def _idiv(x, n):
    assert n & (n - 1) == 0
    return x >> (n.bit_length() - 1)


def _imod(x, n):
    assert n & (n - 1) == 0
    return x & (n - 1)


def _dot(a, b):
    return jnp.dot(a, b, preferred_element_type=F32)


def _dot_nt(a, b):
    return lax.dot_general(a, b, (((1,), (1,)), ((), ())), preferred_element_type=F32)


def _inproj_kernel(x_ref, g1_ref, w_ref, perm_ref, cosb_ref, sinb_ref, cosc_ref, sinc_ref,
                   nvg_ref, qg_ref, kg_ref, o_ref, h_ref, *, n_prompt_tiles):
    i = pl.program_id(0)
    hn = (_rms(x_ref[...]) * g1_ref[...]).astype(BF16)
    h_ref[0] = hn

    @pl.when(i < n_prompt_tiles)
    def _():
        for k in range(1, len(SWA_PATTERNS)):
            h_ref[k] = _dot(perm_ref[k - 1], hn).astype(BF16)

    @pl.when(i >= n_prompt_tiles)
    def _():
        for k in range(1, len(SWA_PATTERNS)):
            h_ref[k] = hn

    def qk_heads(acc, g, cols0, gain_ref, scale):
        for hh in range(SWA_HEADS):
            cs = slice(hh * LANE, (hh + 1) * LANE)
            y = _rms(acc[:, cs]) * gain_ref[...]
            rot = y * cosb_ref[g] + pltpu.roll(y, LANE // 2, axis=1) * sinb_ref[g]
            o_ref[:, cols0 + hh * LANE:cols0 + (hh + 1) * LANE] = (rot * scale).astype(o_ref.dtype)

    for j in range(N_COL):
        cols = slice(j * COL, (j + 1) * COL)
        g = (j - CB_Q) % len(SWA_PATTERNS) if CB_Q <= j < CB_RQK else 0
        acc = _dot(h_ref[g], w_ref[:, cols])
        if j == CB_AU:
            o_ref[:, cols] = _gelu(acc).astype(o_ref.dtype)
        elif j == CB_AV:
            o_ref[:, cols] = (_rms(_gelu(acc)) * nvg_ref[...]).astype(o_ref.dtype)
        elif CB_Q <= j < CB_K:
            qk_heads(acc, g, j * COL, qg_ref, SWA_HEAD_DIM ** -0.5)
        elif CB_K <= j < CB_V:
            qk_heads(acc, g, j * COL, kg_ref, 1.0)
        elif j < CB_RQK or j == CB_RV:
            o_ref[:, cols] = acc.astype(o_ref.dtype)
        elif j == CB_RQK:
            lane = lax.broadcasted_iota(jnp.int32, (acc.shape[0], LANE), 1)
            first_half = _imod(lane, RET_QK_DIM) < (RET_QK_DIM // 2)
            for tt in range(COL // LANE):
                y = acc[:, tt * LANE:(tt + 1) * LANE]
                partner = jnp.where(first_half,
                                    pltpu.roll(y, LANE - RET_QK_DIM // 2, axis=1),
                                    pltpu.roll(y, RET_QK_DIM // 2, axis=1))
                rot = y * cosc_ref[...] + partner * sinc_ref[...]
                scale = 1.0 if tt < (COL // LANE) // 2 else RET_QK_DIM ** -0.5
                o_ref[:, j * COL + tt * LANE:j * COL + (tt + 1) * LANE] = (rot * scale).astype(o_ref.dtype)
        elif j == CB_RG:
            o_ref[:, cols] = (acc * _sigmoid(acc)).astype(o_ref.dtype)
        else:
            o_ref[:, cols] = _sigmoid(acc).astype(o_ref.dtype)


def _inproj(x, g1, w_in_b, perm, tabs, nvg, qg, kg, n_prompt_tiles):
    T = x.shape[0]
    cosb, sinb, cosc, sinc = tabs
    n_g = len(SWA_PATTERNS)
    row = lambda i: (i, 0)
    const = lambda i: (0, 0)
    tab3 = pl.BlockSpec((n_g, TILE, LANE), lambda i: (0, i, 0))
    tab = pl.BlockSpec((TILE, LANE), row)
    vmem = (2 * _nbytes((TILE, D_MODEL), F32) + n_g * _nbytes((TILE, D_MODEL), BF16)
            + _nbytes(w_in_b.shape, BF16) + 2 * _nbytes((TILE, IN_WIDTH), BF16)
            + 2 * (2 * n_g + 2) * _nbytes((TILE, LANE), F32) + 2 * _nbytes(perm.shape, BF16)
            + 8 * _nbytes((TILE, COL), F32))
    return pl.pallas_call(
        functools.partial(_inproj_kernel, n_prompt_tiles=n_prompt_tiles),
        grid=(T // TILE,),
        in_specs=[
            pl.BlockSpec((TILE, D_MODEL), row),
            pl.BlockSpec((1, D_MODEL), const),
            pl.BlockSpec(w_in_b.shape, const, pipeline_mode=pl.Buffered(1)),
            pl.BlockSpec(perm.shape, lambda i: (0, 0, 0)),
            tab3, tab3, tab, tab,
            pl.BlockSpec((1, COL), const),
            pl.BlockSpec((1, LANE), const),
            pl.BlockSpec((1, LANE), const),
        ],
        out_specs=pl.BlockSpec((TILE, IN_WIDTH), row),
        out_shape=jax.ShapeDtypeStruct((T, IN_WIDTH), BF16),
        scratch_shapes=[pltpu.VMEM((n_g, TILE, D_MODEL), BF16)],
        compiler_params=_params(vmem, ("parallel",)),
        name="inproj",
    )(x, g1, w_in_b, perm, cosb, sinb, cosc, sinc, nvg, qg, kg)


def _mixa_kernel(u_ref, v_ref, w_ref, b_ref, o_ref, *, cps, n_prompt_chunks):
    i = pl.program_id(0)
    for c in range(cps):
        var = ((i * cps + c) >= n_prompt_chunks).astype(jnp.int32)
        rows = slice(c * CHUNK, (c + 1) * CHUNK)
        for g in range(A_GROUPS):
            cols = slice(g * LANE, (g + 1) * LANE)
            z = _dot(w_ref[var, g], v_ref[rows, cols]) + b_ref[var, :, cols]
            o_ref[rows, cols] = (u_ref[rows, cols].astype(F32) * z).astype(o_ref.dtype)


def _mixer_a(P, w2, b2, n_prompt_chunks):
    T = P.shape[0]
    n_chunks = T // CHUNK
    cps = _pick(n_chunks, (8, 6, 4, 3, 2, 1))
    rows = cps * CHUNK
    vmem = 6 * _nbytes((rows, COL), BF16) + 2 * _nbytes(w2.shape, BF16) + 2 * _nbytes(b2.shape, F32)
    return pl.pallas_call(
        functools.partial(_mixa_kernel, cps=cps, n_prompt_chunks=n_prompt_chunks),
        grid=(n_chunks // cps,),
        in_specs=[
            pl.BlockSpec((rows, COL), lambda i: (i, CB_AU)),
            pl.BlockSpec((rows, COL), lambda i: (i, CB_AV)),
            pl.BlockSpec(w2.shape, lambda i: (0, 0, 0, 0)),
            pl.BlockSpec(b2.shape, lambda i: (0, 0, 0)),
        ],
        out_specs=pl.BlockSpec((rows, COL), lambda i: (i, 0)),
        out_shape=jax.ShapeDtypeStruct((T, COL), BF16),
        compiler_params=_params(vmem, ("parallel",)),
        name="mixer_a",
    )(P, P, w2, b2)


def _swa_kernel(q_ref, kp_ref, kc_ref, vp_ref, vc_ref, o_ref, l_ref):
    i = pl.program_id(2)
    row = lax.broadcasted_iota(jnp.int32, (CHUNK, CHUNK), 0)
    col = lax.broadcasted_iota(jnp.int32, (CHUNK, CHUNK), 1)
    mask_c = col <= row
    mask_p = (col >= row) & (i > 0)
    lead = q_ref.shape[:-1]

    def head(ref, cs):
        return ref[..., cs].reshape(CHUNK, LANE)

    for h in range(SWA_HEADS):
        cs = slice(h * LANE, (h + 1) * LANE)
        q = head(q_ref, cs)
        s_c = jnp.where(mask_c, _dot_nt(q, head(kc_ref, cs)), NEG_INF)
        s_p = jnp.where(mask_p, _dot_nt(q, head(kp_ref, cs)), NEG_INF)
        m = jnp.maximum(jnp.max(s_c, axis=-1, keepdims=True), jnp.max(s_p, axis=-1, keepdims=True))
        p_c = jnp.exp(s_c - m)
        p_p = jnp.exp(s_p - m)
        den = jnp.sum(p_c, axis=-1, keepdims=True) + jnp.sum(p_p, axis=-1, keepdims=True)
        o = _dot(p_c.astype(BF16), head(vc_ref, cs)) + _dot(p_p.astype(BF16), head(vp_ref, cs))
        o_ref[..., cs] = (o / den).astype(o_ref.dtype).reshape(lead + (LANE,))
        l_ref[h] = jnp.broadcast_to(m + jnp.log(den), (CHUNK, LANE)).reshape(lead + (LANE,))


def _swa_prompt(P, g, dil, n_p, s):
    T = P.shape[0]
    Tp = n_p * s
    nb = s // dil // CHUNK
    if dil == 1:
        src = P
        lead, lead_o = (CHUNK,), (CHUNK,)
        o_shape = (Tp, COL)
        l_shape = (SWA_HEADS, Tp, LANE)

        def in_map(cb, prev):
            return lambda b, r, i: (b * nb + (jnp.maximum(i - 1, 0) if prev else i), cb + g)

        o_map = lambda b, r, i: (b * nb + i, 0)
        l_map = lambda b, r, i: (0, b * nb + i, 0)
    else:
        rpt = TILE // dil
        tpb = CHUNK // rpt
        src = P.reshape(T // TILE, dil, rpt, IN_WIDTH)
        lead = (tpb, None, rpt)
        o_shape = (Tp // TILE, dil, rpt, COL)
        l_shape = (SWA_HEADS, Tp // TILE, dil, rpt, LANE)

        def in_map(cb, prev):
            return lambda b, r, i: (b * nb + (jnp.maximum(i - 1, 0) if prev else i), r, 0, cb + g)

        o_map = lambda b, r, i: (b * nb + i, r, 0, 0)
        l_map = lambda b, r, i: (0, b * nb + i, r, 0, 0)

    in_spec = lambda cb, prev: pl.BlockSpec(lead + (COL,), in_map(cb, prev))
    vmem = 12 * _nbytes((CHUNK, COL), BF16) + 2 * _nbytes((CHUNK, COL), F32) + 8 * _nbytes((CHUNK, CHUNK), F32)
    o, l = pl.pallas_call(
        _swa_kernel,
        grid=(n_p, dil, nb),
        in_specs=[in_spec(CB_Q, False), in_spec(CB_K, True), in_spec(CB_K, False),
                  in_spec(CB_V, True), in_spec(CB_V, False)],
        out_specs=[pl.BlockSpec(lead + (COL,), o_map),
                   pl.BlockSpec((SWA_HEADS,) + lead + (LANE,), l_map)],
        out_shape=[jax.ShapeDtypeStruct(o_shape, BF16), jax.ShapeDtypeStruct(l_shape, F32)],
        compiler_params=_params(vmem, ("parallel", "parallel", "arbitrary")),
        name=f"swa_prompt_g{g}",
    )(src, src, src, src, src)
    return o.reshape(Tp, COL), l.reshape(SWA_HEADS, Tp, LANE)


def _swa_sample_kernel(q_ref, kn_ref, vn_ref, cache_ref, o_ref, l_ref, *, dil, lbuf, t_s):
    nq = SWA_HEADS * t_s
    q = q_ref[...]
    qrep = jnp.concatenate([q] * SWA_HEADS, axis=0)
    rq = lax.broadcasted_iota(jnp.int32, (nq, COL), 0)
    cq = lax.broadcasted_iota(jnp.int32, (nq, COL), 1)
    qbd = jnp.where(_idiv(rq, t_s) == _idiv(cq, LANE), qrep, 0.0).astype(BF16)

    kc = cache_ref[:, 0:COL].astype(BF16)
    vc = cache_ref[:, COL:2 * COL].astype(BF16)
    kn = kn_ref[...].astype(BF16)
    vn = vn_ref[...].astype(BF16)

    s_c = _dot_nt(qbd, kc)
    s_n = _dot_nt(qbd, kn)
    t_c = _imod(lax.broadcasted_iota(jnp.int32, (nq, lbuf), 0), t_s)
    c_c = lax.broadcasted_iota(jnp.int32, (nq, lbuf), 1)
    diff_c = lbuf + t_c - c_c
    ok_c = (_imod(diff_c, dil) == 0) & (diff_c <= lbuf)
    t_n = _imod(lax.broadcasted_iota(jnp.int32, (nq, t_s), 0), t_s)
    c_n = lax.broadcasted_iota(jnp.int32, (nq, t_s), 1)
    diff_n = t_n - c_n
    ok_n = (diff_n >= 0) & (_imod(diff_n, dil) == 0)
    s_c = jnp.where(ok_c, s_c, NEG_INF)
    s_n = jnp.where(ok_n, s_n, NEG_INF)
    m = jnp.maximum(jnp.max(s_c, axis=-1, keepdims=True), jnp.max(s_n, axis=-1, keepdims=True))
    p_c = jnp.exp(s_c - m)
    p_n = jnp.exp(s_n - m)
    den = jnp.sum(p_c, axis=-1, keepdims=True) + jnp.sum(p_n, axis=-1, keepdims=True)
    o_all = (_dot(p_c.astype(BF16), vc) + _dot(p_n.astype(BF16), vn)) / den
    lse = m + jnp.log(den)
    for h in range(SWA_HEADS):
        cs = slice(h * LANE, (h + 1) * LANE)
        rs = slice(h * t_s, (h + 1) * t_s)
        o_ref[:, cs] = o_all[rs, cs]
        l_ref[h] = jnp.broadcast_to(lse[rs], (t_s, LANE))


def _swa_sample(Ps, cache, layer, g, dil):
    n_s, t_s, _ = Ps.shape
    lbuf = cache.shape[2]
    assert lbuf == dil * CHUNK, "window buffer must hold exactly one full window"
    blk = lambda cb: pl.BlockSpec((None, t_s, COL), lambda b: (b, 0, cb + g))
    vmem = (2 * _nbytes((lbuf, 2 * COL), F32) + 2 * _nbytes((lbuf, 2 * COL), BF16)
            + 8 * _nbytes((SWA_HEADS * t_s, lbuf), F32))
    return pl.pallas_call(
        functools.partial(_swa_sample_kernel, dil=dil, lbuf=lbuf, t_s=t_s),
        grid=(n_s,),
        in_specs=[blk(CB_Q), blk(CB_K), blk(CB_V),
                  pl.BlockSpec((None, None, lbuf, 2 * COL), lambda b: (layer, b, 0, 0))],
        out_specs=[pl.BlockSpec((None, t_s, COL), lambda b: (b, 0, 0)),
                   pl.BlockSpec((SWA_HEADS, None, t_s, LANE), lambda b: (0, b, 0, 0))],
        out_shape=[jax.ShapeDtypeStruct((n_s, t_s, COL), F32),
                   jax.ShapeDtypeStruct((SWA_HEADS, n_s, t_s, LANE), F32)],
        compiler_params=_params(vmem, ("parallel",)),
        name=f"swa_sample_g{g}",
    )(Ps, Ps, Ps, cache)


def _ret_head_inputs(qk_ref, v_ref, kout_ref, h):
    pair, half = h // 2, h % 2
    lane = lax.broadcasted_iota(jnp.int32, (CHUNK, LANE), 1)
    head_lanes = _idiv(lane, RET_QK_DIM) == half
    qt = qk_ref[:, pair * LANE:(pair + 1) * LANE]
    kt = qk_ref[:, COL // 2 + pair * LANE:COL // 2 + (pair + 1) * LANE]
    qm = jnp.where(head_lanes, qt, jnp.zeros_like(qt))
    kw = jnp.where(head_lanes, kt.astype(F32) * kout_ref[:, pair * LANE:(pair + 1) * LANE], 0.0)
    vh = v_ref[:, h * LANE:(h + 1) * LANE]
    return qm, kt, kw, vh


def _ret_finish(o, gate_ref, o_ref, h):
    cs = slice(h * LANE, (h + 1) * LANE)
    o_ref[:, cs] = (gate_ref[:, cs].astype(F32) * _rms(o)).astype(o_ref.dtype)


def _ret_prompt_kernel(qk_ref, v_ref, gate_ref, decay_ref, qin_ref, kout_ref, gc_ref,
                       o_ref, s_out_ref, s_ref):
    i = pl.program_id(1)

    @pl.when(i == 0)
    def _():
        s_ref[...] = jnp.zeros_like(s_ref)

    for h in range(RET_HEADS):
        qm, kt, kw, vh = _ret_head_inputs(qk_ref, v_ref, kout_ref, h)
        att = _dot_nt(qm, kt) * decay_ref[h]
        state = s_ref[h]
        o = _dot(att.astype(BF16), vh) + _dot(qm, state.astype(BF16)) * qin_ref[:, h * LANE:(h + 1) * LANE]
        s_ref[h] = state * gc_ref[h:h + 1, :] + _dot(kw.T.astype(BF16), vh)
        _ret_finish(o, gate_ref, o_ref, h)

    @pl.when(i == pl.num_programs(1) - 1)
    def _():
        for h in range(RET_HEADS):
            lo = (h % 2) * RET_QK_DIM
            s_out_ref[h] = s_ref[h, lo:lo + RET_QK_DIM, :]


def _ret_prompt(P, rt, n_p, s):
    nblk = s // CHUNK
    decay, qin, kout, gc = rt
    blk = lambda cb: pl.BlockSpec((CHUNK, COL), lambda b, i: (b * nblk + i, cb))
    const2 = lambda b, i: (0, 0)
    vmem = (8 * _nbytes((CHUNK, COL), BF16) + 2 * _nbytes(decay.shape, F32) + 4 * _nbytes(qin.shape, F32)
            + 3 * _nbytes((RET_HEADS, LANE, LANE), F32))
    return pl.pallas_call(
        _ret_prompt_kernel,
        grid=(n_p, nblk),
        in_specs=[blk(CB_RQK), blk(CB_RV), blk(CB_RG),
                  pl.BlockSpec(decay.shape, lambda b, i: (0, 0, 0)),
                  pl.BlockSpec(qin.shape, const2), pl.BlockSpec(kout.shape, const2),
                  pl.BlockSpec(gc.shape, const2)],
        out_specs=[pl.BlockSpec((CHUNK, COL), lambda b, i: (b * nblk + i, 0)),
                   pl.BlockSpec((None, RET_HEADS, RET_QK_DIM, RET_V_DIM), lambda b, i: (b, 0, 0, 0))],
        out_shape=[jax.ShapeDtypeStruct((n_p * s, COL), BF16),
                   jax.ShapeDtypeStruct((n_p, RET_HEADS, RET_QK_DIM, RET_V_DIM), F32)],
        scratch_shapes=[pltpu.VMEM((RET_HEADS, LANE, LANE), F32)],
        compiler_params=_params(vmem, ("parallel", "arbitrary")),
        name="ret_prompt",
    )(P, P, P, decay, qin, kout, gc)


def _ret_sample_kernel(qk_ref, v_ref, gate_ref, s0_ref, decay_ref, qin_ref, kout_ref, gc_ref,
                       o_ref, s_out_ref, *, t_s):
    row = lax.broadcasted_iota(jnp.int32, (CHUNK, LANE), 0)
    for h in range(RET_HEADS):
        lo = (h % 2) * RET_QK_DIM
        qm, kt, kw, vh = _ret_head_inputs(qk_ref, v_ref, kout_ref, h)
        att = _dot_nt(qm, kt) * decay_ref[h]
        o = _dot(att.astype(BF16), vh)
        inter = jnp.zeros((CHUNK, LANE), F32)
        for sq in range(CHUNK // t_s):
            seq_rows = _idiv(row, t_s) == sq
            st = s0_ref[sq, h]
            st2 = jnp.concatenate([st, st], axis=0).astype(BF16)
            inter = jnp.where(seq_rows, _dot(qm, st2), inter)
            upd = _dot(jnp.where(seq_rows, kw, 0.0).T.astype(BF16), vh)
            s_out_ref[sq, h] = st * gc_ref[h:h + 1, :] + upd[lo:lo + RET_QK_DIM, :]
        o = o + inter * qin_ref[:, h * LANE:(h + 1) * LANE]
        _ret_finish(o, gate_ref, o_ref, h)


def _ret_sample(P, state, layer, rt, Tp, t_s):
    n_s = state.shape[1]
    spb = CHUNK // t_s
    base = Tp // CHUNK
    decay, qin, kout, gc = rt
    blk = lambda cb: pl.BlockSpec((CHUNK, COL), lambda i: (base + i, cb))
    const2 = lambda i: (0, 0)
    st_shape = (spb, RET_HEADS, RET_QK_DIM, RET_V_DIM)
    vmem = (8 * _nbytes((CHUNK, COL), BF16) + 2 * _nbytes(decay.shape, F32) + 4 * _nbytes(qin.shape, F32)
            + 4 * _nbytes(st_shape, F32))
    return pl.pallas_call(
        functools.partial(_ret_sample_kernel, t_s=t_s),
        grid=(n_s // spb,),
        in_specs=[blk(CB_RQK), blk(CB_RV), blk(CB_RG),
                  pl.BlockSpec((None,) + st_shape, lambda i: (layer, i, 0, 0, 0)),
                  pl.BlockSpec(decay.shape, lambda i: (0, 0, 0)),
                  pl.BlockSpec(qin.shape, const2), pl.BlockSpec(kout.shape, const2),
                  pl.BlockSpec(gc.shape, const2)],
        out_specs=[pl.BlockSpec((CHUNK, COL), lambda i: (i, 0)),
                   pl.BlockSpec(st_shape, lambda i: (i, 0, 0, 0))],
        out_shape=[jax.ShapeDtypeStruct((n_s * t_s, COL), BF16),
                   jax.ShapeDtypeStruct((n_s, RET_HEADS, RET_QK_DIM, RET_V_DIM), F32)],
        compiler_params=_params(vmem, ("parallel",)),
        name="ret_sample",
    )(P, P, P, state, decay, qin, kout, gc)


def _branch_kernel(x_ref, a_ref, g0_ref, g1_ref, g2_ref, wb_ref, wo_ref,
                   po0, po1, po2, pl0, pl1, pl2, pc, so0, so1, so2, sl0, sl1, sl2, sc,
                   y_ref, on_ref, ln_ref, *, n_prompt_tiles):
    i = pl.program_id(0)

    def natural(o_ref, l_ref, k, dil):
        if dil == 1:
            return (lambda h: o_ref[:, h * LANE:(h + 1) * LANE].astype(F32)), (lambda h: l_ref[h])
        rpt = TILE // dil
        for h in range(SWA_HEADS):
            for r in range(dil):
                rows = slice(r * rpt, (r + 1) * rpt)
                on_ref[k, h, pl.ds(r, rpt, stride=dil), :] = o_ref[rows, h * LANE:(h + 1) * LANE].astype(F32)
                ln_ref[k, h, pl.ds(r, rpt, stride=dil), :] = l_ref[h, rows, :]
        return (lambda h: on_ref[k, h]), (lambda h: ln_ref[k, h])

    def body(o_refs, l_refs, c_ref, permuted):
        getters = [natural(o_refs[k], l_refs[k], k, SWA_PATTERNS[k][1] if permuted else 1)
                   for k in range(len(SWA_PATTERNS))]
        heads = []
        for h in range(SWA_HEADS):
            ls = [gl(h) for _, gl in getters]
            lmax = functools.reduce(jnp.maximum, ls)
            es = [jnp.exp(l - lmax) for l in ls]
            num = sum(e * go(h) for e, (go, _) in zip(es, getters))
            heads.append((num / sum(es)).astype(BF16))
        mix = jnp.concatenate(heads, axis=1)
        merged = (g0_ref[...].astype(F32) * _dot(a_ref[...], wb_ref[0])
                  + g1_ref[...].astype(F32) * _dot(mix, wb_ref[1])
                  + g2_ref[...].astype(F32) * _dot(c_ref[...], wb_ref[2]))
        y_ref[...] = x_ref[...] + _dot(merged.astype(BF16), wo_ref[...])

    @pl.when(i < n_prompt_tiles)
    def _():
        body((po0, po1, po2), (pl0, pl1, pl2), pc, True)

    @pl.when(i >= n_prompt_tiles)
    def _():
        body((so0, so1, so2), (sl0, sl1, sl2), sc, False)


def _branch(x, out_a, P, wb, wo, prompt_set, sample_set, n_prompt_tiles):
    T = x.shape[0]
    assert sample_set[0].shape[0] == TILE, "the sample rows must form exactly one tile"
    n_g = len(SWA_PATTERNS)
    last = n_prompt_tiles - 1
    row = lambda i: (i, 0)
    gate = lambda k: pl.BlockSpec((TILE, D_MODEL), lambda i: (i, CB_GATE * COL // D_MODEL + k))
    once = pl.Buffered(1)
    p_o = pl.BlockSpec((TILE, COL), lambda i: (jnp.minimum(i, last), 0))
    p_l = pl.BlockSpec((SWA_HEADS, TILE, LANE), lambda i: (0, jnp.minimum(i, last), 0))
    s_o = pl.BlockSpec((TILE, COL), lambda i: (0, 0))
    s_l = pl.BlockSpec((SWA_HEADS, TILE, LANE), lambda i: (0, 0, 0))
    vmem = (4 * _nbytes((TILE, D_MODEL), F32) + 6 * _nbytes((TILE, D_MODEL), BF16)
            + 2 * (2 * n_g + 2) * (_nbytes((TILE, COL), BF16) + _nbytes((TILE, COL), F32))
            + _nbytes(wb.shape, BF16) + _nbytes(wo.shape, BF16)
            + 4 * n_g * _nbytes((TILE, COL), F32) + 4 * _nbytes((TILE, D_MODEL), F32))
    return pl.pallas_call(
        functools.partial(_branch_kernel, n_prompt_tiles=n_prompt_tiles),
        grid=(T // TILE,),
        in_specs=[pl.BlockSpec((TILE, D_MODEL), row), pl.BlockSpec((TILE, COL), row),
                  gate(0), gate(1), gate(2),
                  pl.BlockSpec(wb.shape, lambda i: (0, 0, 0), pipeline_mode=once),
                  pl.BlockSpec(wo.shape, lambda i: (0, 0), pipeline_mode=once),
                  p_o, p_o, p_o, p_l, p_l, p_l, p_o,
                  s_o, s_o, s_o, s_l, s_l, s_l, s_o],
        out_specs=pl.BlockSpec((TILE, D_MODEL), row),
        out_shape=jax.ShapeDtypeStruct((T, D_MODEL), F32),
        scratch_shapes=[pltpu.VMEM((n_g, SWA_HEADS, TILE, LANE), F32),
                        pltpu.VMEM((n_g, SWA_HEADS, TILE, LANE), F32)],
        compiler_params=_params(vmem, ("parallel",)),
        name="branch_merge",
    )(x, out_a, P, P, P, wb, wo, *prompt_set, *sample_set)


def _swiglu_into(hb, w1_ref, w3_ref, w2_ref, acc_ref, fc):
    n_fc = w1_ref.shape[-1] // fc
    for f in range(n_fc):
        cols = slice(f * fc, (f + 1) * fc)
        a = _dot(hb, w1_ref[:, cols])
        b = _dot(hb, w3_ref[:, cols])
        part = _dot((a * _sigmoid(a) * b).astype(BF16), w2_ref[cols, :])
        if f == 0:
            acc_ref[...] = part
        else:
            acc_ref[...] += part


def _dense_ffn_kernel(x_ref, g2_ref, w1_ref, w3_ref, w2_ref, y_ref, acc_ref, *, fc):
    x = x_ref[...]
    hb = (_rms(x) * g2_ref[...]).astype(BF16)
    _swiglu_into(hb, w1_ref, w3_ref, w2_ref, acc_ref, fc)
    y_ref[...] = x + acc_ref[...]


def _dense_ffn(x, g2, w1b, w3b, w2b):
    T = x.shape[0]
    F = w1b.shape[1]
    tm = _pick(T, (768, 512, 384, 256, 128))
    fc = _pick(F, (512, 256, 128))
    row = lambda i: (i, 0)
    once = pl.Buffered(1)
    vmem = (5 * _nbytes((tm, D_MODEL), F32) + 3 * _nbytes(w1b.shape, BF16) + _nbytes((tm, D_MODEL), BF16)
            + 4 * _nbytes((tm, fc), F32))
    return pl.pallas_call(
        functools.partial(_dense_ffn_kernel, fc=fc),
        grid=(T // tm,),
        in_specs=[pl.BlockSpec((tm, D_MODEL), row),
                  pl.BlockSpec((1, D_MODEL), lambda i: (0, 0)),
                  pl.BlockSpec(w1b.shape, lambda i: (0, 0), pipeline_mode=once),
                  pl.BlockSpec(w3b.shape, lambda i: (0, 0), pipeline_mode=once),
                  pl.BlockSpec(w2b.shape, lambda i: (0, 0), pipeline_mode=once)],
        out_specs=pl.BlockSpec((tm, D_MODEL), row),
        out_shape=jax.ShapeDtypeStruct((T, D_MODEL), F32),
        scratch_shapes=[pltpu.VMEM((tm, D_MODEL), F32)],
        compiler_params=_params(vmem, ("parallel",)),
        name="dense_ffn",
    )(x, g2, w1b, w3b, w2b)


def _router_kernel(x_ref, g2_ref, wr_ref, idx_ref, wgt_ref):
    hb = (_rms(x_ref[...]) * g2_ref[...]).astype(BF16)
    logits = _dot(hb, wr_ref[...])
    lane = lax.broadcasted_iota(jnp.int32, logits.shape, 1)
    logits = jnp.where(lane < N_EXPERTS, logits, -jnp.inf)
    lane_f = lane.astype(F32)
    m1 = jnp.max(logits, axis=-1, keepdims=True)
    i1 = jnp.min(jnp.where(logits == m1, lane_f, float(LANE)), axis=-1, keepdims=True)
    rest = jnp.where(lane_f == i1, -jnp.inf, logits)
    m2 = jnp.max(rest, axis=-1, keepdims=True)
    i2 = jnp.min(jnp.where(rest == m2, lane_f, float(LANE)), axis=-1, keepdims=True)
    e = jnp.exp(m2 - m1)
    idx_ref[...] = jnp.where(lane == 0, i1, jnp.where(lane == 1, i2, 0.0)).astype(jnp.int32)
    wgt_ref[...] = jnp.where(lane == 0, 1.0 / (1.0 + e), jnp.where(lane == 1, e / (1.0 + e), 0.0))


def _router(x, g2, wr_pad):
    T = x.shape[0]
    tm = _pick(T, (768, 512, 384, 256, 128))
    row = lambda i: (i, 0)
    vmem = 4 * _nbytes((tm, D_MODEL), F32) + 2 * _nbytes(wr_pad.shape, BF16) + 8 * _nbytes((tm, LANE), F32)
    return pl.pallas_call(
        _router_kernel,
        grid=(T // tm,),
        in_specs=[pl.BlockSpec((tm, D_MODEL), row),
                  pl.BlockSpec((1, D_MODEL), lambda i: (0, 0)),
                  pl.BlockSpec(wr_pad.shape, lambda i: (0, 0))],
        out_specs=[pl.BlockSpec((tm, LANE), row), pl.BlockSpec((tm, LANE), row)],
        out_shape=[jax.ShapeDtypeStruct((T, LANE), jnp.int32), jax.ShapeDtypeStruct((T, LANE), F32)],
        compiler_params=_params(vmem, ("parallel",)),
        name="moe_router",
    )(x, g2, wr_pad)


def _row_copy(src_ref, src_row, dst_ref, dst_row, sem):
    return pltpu.make_async_copy(src_ref.at[pl.ds(src_row, 1), :], dst_ref.at[pl.ds(dst_row, 1), :], sem)


def _dispatch_kernel(dest_ref, x_ref, g2_ref, xs_in_ref, xs_ref, h_ref, sem, *, tm):
    del xs_in_ref
    h_ref[...] = _rms(x_ref[...]) * g2_ref[...]

    def issue(r, carry):
        for k in range(TOP_K):
            _row_copy(h_ref, r, xs_ref, dest_ref[0, 0, TOP_K * r + k], sem).start()
        return carry

    lax.fori_loop(0, tm, issue, 0)

    def drain(r, carry):
        for k in range(TOP_K):
            _row_copy(h_ref, 0, xs_ref, 0, sem).wait()
        return carry

    lax.fori_loop(0, tm, drain, 0)


def _dispatch(x, g2, dest3, n_rows):
    T = x.shape[0]
    tm = dest3.shape[2] // TOP_K
    xs0 = jnp.zeros((n_rows, D_MODEL), F32)
    vmem = 3 * _nbytes((tm, D_MODEL), F32)
    return pl.pallas_call(
        functools.partial(_dispatch_kernel, tm=tm),
        grid=(T // tm,),
        in_specs=[pl.BlockSpec((1, 1, TOP_K * tm), lambda i: (i, 0, 0), memory_space=pltpu.SMEM),
                  pl.BlockSpec((tm, D_MODEL), lambda i: (i, 0)),
                  pl.BlockSpec((1, D_MODEL), lambda i: (0, 0)),
                  pl.BlockSpec(memory_space=pl.ANY)],
        out_specs=pl.BlockSpec(memory_space=pl.ANY),
        out_shape=jax.ShapeDtypeStruct((n_rows, D_MODEL), F32),
        scratch_shapes=[pltpu.VMEM((tm, D_MODEL), F32), pltpu.SemaphoreType.DMA(())],
        input_output_aliases={3: 0},
        compiler_params=_params(vmem, ("arbitrary",)),
        name="moe_dispatch",
    )(dest3, x, g2, xs0)


def _expert_ffn_kernel(te_ref, xs_ref, w1_ref, w3_ref, w2_ref, y_ref, acc_ref, *, fc):
    del te_ref
    _swiglu_into(xs_ref[...].astype(BF16), w1_ref, w3_ref, w2_ref, acc_ref, fc)
    y_ref[...] = acc_ref[...]


def _expert_ffn(xs, tile_expert, we1b, we3b, we2b, tme):
    n_rows = xs.shape[0]
    F = we1b.shape[2]
    fc = _pick(F, (512, 256, 128))
    once = pl.Buffered(1)
    wspec = lambda shape: pl.BlockSpec((None,) + shape, lambda i, te: (te[i], 0, 0), pipeline_mode=once)
    vmem = (5 * _nbytes((tme, D_MODEL), F32) + 3 * _nbytes(we1b.shape[1:], BF16) + _nbytes((tme, D_MODEL), BF16)
            + 4 * _nbytes((tme, fc), F32))
    grid_spec = pltpu.PrefetchScalarGridSpec(
        num_scalar_prefetch=1,
        grid=(n_rows // tme,),
        in_specs=[pl.BlockSpec((tme, D_MODEL), lambda i, te: (i, 0)),
                  wspec(we1b.shape[1:]), wspec(we3b.shape[1:]), wspec(we2b.shape[1:])],
        out_specs=pl.BlockSpec((tme, D_MODEL), lambda i, te: (i, 0)),
        scratch_shapes=[pltpu.VMEM((tme, D_MODEL), F32)],
    )
    return pl.pallas_call(
        functools.partial(_expert_ffn_kernel, fc=fc),
        grid_spec=grid_spec,
        out_shape=jax.ShapeDtypeStruct((n_rows, D_MODEL), F32),
        compiler_params=_params(vmem, ("arbitrary",)),
        name="moe_expert_ffn",
    )(tile_expert, xs, we1b, we3b, we2b)


def _combine_kernel(dest_ref, x_ref, wgt_ref, y_hbm_ref, o_ref, buf_ref, sem, *, tm):
    def issue(r, carry):
        for k in range(TOP_K):
            _row_copy(y_hbm_ref, dest_ref[0, 0, TOP_K * r + k], buf_ref.at[k], r, sem).start()
        return carry

    lax.fori_loop(0, tm, issue, 0)

    def drain(r, carry):
        for k in range(TOP_K):
            _row_copy(y_hbm_ref, 0, buf_ref.at[k], 0, sem).wait()
        return carry

    lax.fori_loop(0, tm, drain, 0)
    wgt = wgt_ref[...]
    o_ref[...] = x_ref[...] + wgt[:, 0:1] * buf_ref[0] + wgt[:, 1:2] * buf_ref[1]


def _combine(x, wgt, dest3, y_rows):
    T = x.shape[0]
    tm = dest3.shape[2] // TOP_K
    row = lambda i: (i, 0)
    vmem = 7 * _nbytes((tm, D_MODEL), F32) + 2 * _nbytes((tm, LANE), F32)
    return pl.pallas_call(
        functools.partial(_combine_kernel, tm=tm),
        grid=(T // tm,),
        in_specs=[pl.BlockSpec((1, 1, TOP_K * tm), lambda i: (i, 0, 0), memory_space=pltpu.SMEM),
                  pl.BlockSpec((tm, D_MODEL), row),
                  pl.BlockSpec((tm, LANE), row),
                  pl.BlockSpec(memory_space=pl.ANY)],
        out_specs=pl.BlockSpec((tm, D_MODEL), row),
        out_shape=jax.ShapeDtypeStruct((T, D_MODEL), F32),
        scratch_shapes=[pltpu.VMEM((TOP_K, tm, D_MODEL), F32), pltpu.SemaphoreType.DMA(())],
        compiler_params=_params(vmem, ("arbitrary",)),
        name="moe_combine",
    )(dest3, x, wgt, y_rows)


def _moe(x, g2, wr_pad, we1b, we3b, we2b):
    T = x.shape[0]
    tme = 512
    tmd = _pick(T, (256, 128))
    idx, wgt = _router(x, g2, wr_pad)
    e_flat = idx[:, :TOP_K].reshape(-1)
    onehot = (e_flat[None, :] == jnp.arange(N_EXPERTS, dtype=jnp.int32)[:, None]).astype(jnp.int32)
    csum = jnp.cumsum(onehot, axis=1)
    counts = csum[:, -1]
    padded = (counts + tme - 1) // tme * tme
    pend = jnp.cumsum(padded)
    pstart = pend - padded
    dest = jnp.sum(onehot * (csum - 1 + pstart[:, None]), axis=0).astype(jnp.int32)
    n_tiles = -(-(T * TOP_K) // tme) + N_EXPERTS
    tile_start = jnp.arange(n_tiles, dtype=jnp.int32) * tme
    tile_expert = jnp.minimum(jnp.sum(pend[None, :] <= tile_start[:, None], axis=1), N_EXPERTS - 1).astype(jnp.int32)
    dest3 = dest.reshape(T // tmd, 1, TOP_K * tmd)
    xs = _dispatch(x, g2, dest3, n_tiles * tme)
    y_rows = _expert_ffn(xs, tile_expert, we1b, we3b, we2b, tme)
    return _combine(x, wgt, dest3, y_rows)


def _class_major(a, dil):
    n = a.shape[0]
    return a.reshape((n // TILE, TILE // dil, dil) + a.shape[1:]).swapaxes(1, 2).reshape(a.shape)


def _natural(a, dil):
    n = a.shape[0]
    return a.reshape((n // TILE, dil, TILE // dil) + a.shape[1:]).swapaxes(1, 2).reshape(a.shape)


def _perm_matrices():
    eye = jnp.eye(TILE, dtype=BF16)
    return jnp.stack([_class_major(eye, dil) for _, dil in SWA_PATTERNS[1:]])


def _rope_tables(pos, Tp):
    freqs = ROPE_THETA ** (-jnp.arange(0, SWA_HEAD_DIM, 2, dtype=F32) / SWA_HEAD_DIM)
    rfreqs = 1.0 / (ROPE_THETA ** jnp.linspace(0.0, 1.0, RET_QK_DIM // 2, dtype=F32))

    def swa(p):
        ang = p.astype(F32)[:, None] * freqs[None, :]
        c, s = jnp.cos(ang), jnp.sin(ang)
        return jnp.concatenate([c, c], axis=1), jnp.concatenate([-s, s], axis=1)

    per_group = [swa(jnp.concatenate([_class_major(pos[:Tp], dil), pos[Tp:]])) for _, dil in SWA_PATTERNS]
    cosb = jnp.stack([c for c, _ in per_group])
    sinb = jnp.stack([s for _, s in per_group])
    rang = pos.astype(F32)[:, None] * rfreqs[None, :]
    c, s = jnp.cos(rang), jnp.sin(rang)
    cosc = jnp.concatenate([c, c, c, c], axis=1)
    sinc = jnp.concatenate([-s, s, -s, s], axis=1)
    return cosb, sinb, cosc, sinc


def _ret_tables(c_len):
    log_g = jnp.log1p(-jnp.exp2(-5.0 - jnp.arange(RET_HEADS, dtype=F32)))
    r = jnp.arange(CHUNK)
    i = (r % c_len).astype(F32)
    same = (r[:, None] // c_len) == (r[None, :] // c_len)
    dist = i[:, None] - i[None, :]
    decay = jnp.where(same[None] & (dist >= 0)[None],
                      jnp.exp(log_g[:, None, None] * jnp.maximum(dist, 0.0)[None]), 0.0)
    qin = jnp.repeat(jnp.exp(log_g[None, :] * (i[:, None] + 1.0)), RET_V_DIM, axis=1)
    kout = jnp.repeat(jnp.exp(log_g[None, :] * (c_len - 1.0 - i)[:, None]), RET_QK_DIM, axis=1)
    gc = jnp.broadcast_to(jnp.exp(log_g * c_len)[:, None], (RET_HEADS, LANE))
    gc = jnp.concatenate([gc, jnp.zeros((8 - RET_HEADS, LANE), F32)], axis=0)
    return decay.astype(F32), qin.astype(F32), kout.astype(F32), gc.astype(F32)


def _mixa_tables(w_s, b_s, t_s):
    w_p = jnp.tril(w_s)
    w8 = jnp.tril(w_s[:, :t_s, :t_s])
    eye = jnp.eye(CHUNK // t_s, dtype=w_s.dtype)
    w_smp = jax.vmap(lambda m: jnp.kron(eye, m))(w8)
    w2 = jnp.stack([w_p, w_smp]).astype(BF16)
    b_p = jnp.repeat(b_s.T, LANE, axis=1)
    b_smp = jnp.repeat(jnp.tile(b_s[:, :t_s].T, (CHUNK // t_s, 1)), LANE, axis=1)
    return w2, jnp.stack([b_p, b_smp]).astype(F32)


def kernel(x_prompt, x_sample, cache_swa_kv0, cache_swa_kv1, cache_swa_kv2, state_ret, norm1_g, w_in, norm_v_g, w_s, b_s, q_norm_g, k_norm_g, w_branch, w_out, norm2_g, w1, w3, w2, w_router, we1, we3, we2):
    n_p, s, d = x_prompt.shape
    n_s, t_s, _ = x_sample.shape
    depth = w_in.shape[0]
    Tp, Ts = n_p * s, n_s * t_s
    T = Tp + Ts
    max_win, max_dil = SWA_PATTERNS[-1]
    assert d == D_MODEL and Ts == TILE and CHUNK % t_s == 0
    assert s % (CHUNK * max_dil) == 0 and s >= max_win
    n_pt = Tp // TILE

    x = jnp.concatenate([x_prompt.reshape(Tp, d), x_sample.reshape(Ts, d)], axis=0)
    pos = jnp.concatenate([jnp.tile(jnp.arange(s, dtype=jnp.int32), n_p),
                           jnp.tile(PAST_LEN + jnp.arange(t_s, dtype=jnp.int32), n_s)])
    tabs = _rope_tables(pos, Tp)
    perm = _perm_matrices()
    rt_prompt = _ret_tables(CHUNK)
    rt_sample = _ret_tables(t_s)
    caches = tuple(c.reshape(c.shape[0], c.shape[1], c.shape[2], 2 * COL)
                   for c in (cache_swa_kv0, cache_swa_kv1, cache_swa_kv2))

    p_kv = [[] for _ in SWA_PATTERNS]
    s_kv = [[] for _ in SWA_PATTERNS]
    p_ret, s_ret, s_v = [], [], []
    for layer in range(depth):
        P = _inproj(x, norm1_g[layer][None, :], w_in[layer].astype(BF16), perm, tabs,
                    norm_v_g[layer][None, :], q_norm_g[layer][None, :], k_norm_g[layer][None, :], n_pt)

        w2a, b2a = _mixa_tables(w_s[layer], b_s[layer], t_s)
        out_a = _mixer_a(P, w2a, b2a, Tp // CHUNK)

        Ps = P[Tp:].astype(F32).reshape(n_s, t_s, IN_WIDTH)
        po, plse, so, slse = [], [], [], []
        for g, (win, dil) in enumerate(SWA_PATTERNS):
            o_g, l_g = _swa_prompt(P, g, dil, n_p, s)
            os_g, ls_g = _swa_sample(Ps, caches[g], layer, g, dil)
            po.append(o_g)
            plse.append(l_g)
            so.append(os_g.reshape(Ts, COL).astype(BF16))
            slse.append(ls_g.reshape(SWA_HEADS, Ts, LANE))

        out_c, ret_p = _ret_prompt(P, rt_prompt, n_p, s)
        out_cs, ret_s = _ret_sample(P, state_ret, layer, rt_sample, Tp, t_s)

        x = _branch(x, out_a, P, w_branch[layer].astype(BF16), w_out[layer].astype(BF16),
                    (*po, *plse, out_c), (*so, *slse, out_cs), n_pt)

        g2 = norm2_g[layer][None, :]
        i = layer // 2
        if layer % 2 == 0:
            x = _dense_ffn(x, g2, w1[i].astype(BF16), w3[i].astype(BF16), w2[i].astype(BF16))
        else:
            wr_pad = jnp.zeros((D_MODEL, LANE), BF16).at[:, :N_EXPERTS].set(w_router[i].astype(BF16))
            x = _moe(x, g2, wr_pad, we1[i].astype(BF16), we3[i].astype(BF16), we2[i].astype(BF16))

        for g, (win, dil) in enumerate(SWA_PATTERNS):
            kcols = slice((CB_K + g) * COL, (CB_K + g + 1) * COL)
            vcols = slice((CB_V + g) * COL, (CB_V + g + 1) * COL)
            keep = -(-min(win, s) // TILE) * TILE
            tail = P[:Tp].reshape(n_p, s, IN_WIDTH)[:, s - keep:]

            def rows(cols):
                blk = tail[:, :, cols].reshape(n_p * keep, COL)
                blk = _natural(blk, dil).reshape(n_p, keep, SWA_HEADS, SWA_HEAD_DIM)
                return blk[:, keep - min(win, s):]

            p_kv[g].append(jnp.stack([rows(kcols), rows(vcols)], axis=2).astype(F32))
            ks = P[Tp:, kcols].reshape(n_s, t_s, SWA_HEADS, SWA_HEAD_DIM)
            vs = P[Tp:, vcols].reshape(n_s, t_s, SWA_HEADS, SWA_HEAD_DIM)
            s_kv[g].append(jnp.stack([ks, vs], axis=2).astype(F32))
        p_ret.append(ret_p)
        s_ret.append(ret_s)
        s_v.append(P[Tp:, CB_AV * COL:(CB_AV + 1) * COL].astype(F32).reshape(n_s, t_s, COL))

    y_prompt = x[:Tp].reshape(n_p, s, d)
    y_sample = x[Tp:].reshape(n_s, t_s, d)
    return (y_prompt, y_sample,
            jnp.stack(p_kv[0]), jnp.stack(p_kv[1]), jnp.stack(p_kv[2]), jnp.stack(p_ret),
            jnp.stack(s_kv[0]), jnp.stack(s_kv[1]), jnp.stack(s_kv[2]), jnp.stack(s_ret),
            jnp.stack(s_v))
```

```python
import functools
import math

import jax
import jax.numpy as jnp
from jax import lax
from jax.experimental import pallas as pl
from jax.experimental.pallas import tpu as pltpu

F32 = jnp.float32
BF16 = jnp.bfloat16

PAST_LEN = 16384
EPS = 1e-6
NEG_INF = -1e30
ROPE_THETA = 10000.0

D_MODEL = 1024
LANE = 128
CHUNK = 128
TILE = 256
COL = 512
A_GROUPS = 4
SWA_PATTERNS = ((128, 1), (512, 4), (2048, 16))
SWA_HEADS = 4
SWA_HEAD_DIM = 128
RET_HEADS = 4
RET_QK_DIM = 64
RET_V_DIM = 128
N_EXPERTS = 8
TOP_K = 2
IN_WIDTH = 10240
N_COL = IN_WIDTH // COL

CB_AU, CB_AV, CB_Q, CB_K, CB_V, CB_RQK, CB_RV, CB_RG, CB_GATE = 0, 1, 2, 5, 8, 11, 12, 13, 14

VMEM_INTERNAL_SCRATCH = 8 * 1024 * 1024
DMA_ISSUE_UNROLL = 8


def _pick(n, candidates):
    for c in candidates:
        if n % c == 0:
            return c
    raise ValueError(f"no tile in {candidates} divides {n}")


def _params(block_bytes, semantics=None):
    limit = int(block_bytes) + VMEM_INTERNAL_SCRATCH
    return pltpu.CompilerParams(dimension_semantics=semantics, vmem_limit_bytes=limit)


def _nbytes(shape, dtype):
    return math.prod(shape) * jnp.dtype(dtype).itemsize


def _rms(x):
    return x * lax.rsqrt(jnp.mean(x * x, axis=-1, keepdims=True) + EPS)


def _gelu(x):
    return 0.5 * x * (1.0 + lax.erf(x * (0.5 ** 0.5)))


def _sigmoid(x):
    return 1.0 / (1.0 + jnp.exp(-x))


def _idiv(x, n):
    assert n & (n - 1) == 0
    return x >> (n.bit_length() - 1)


def _imod(x, n):
    assert n & (n - 1) == 0
    return x & (n - 1)


def _dot(a, b):
    return jnp.dot(a, b, preferred_element_type=F32)


def _dot_nt(a, b):
    return lax.dot_general(a, b, (((1,), (1,)), ((), ())), preferred_element_type=F32)


def _inproj_kernel(x_ref, g1_ref, w_ref, perm_ref, cosb_ref, sinb_ref, cosc_ref, sinc_ref,
                   nvg_ref, qg_ref, kg_ref, o_ref, h_ref, *, n_prompt_tiles):
    i = pl.program_id(0)
    hn = (_rms(x_ref[...]) * g1_ref[...]).astype(BF16)
    h_ref[0] = hn

    @pl.when(i < n_prompt_tiles)
    def _():
        for k in range(1, len(SWA_PATTERNS)):
            h_ref[k] = _dot(perm_ref[k - 1], hn).astype(BF16)

    @pl.when(i >= n_prompt_tiles)
    def _():
        for k in range(1, len(SWA_PATTERNS)):
            h_ref[k] = hn

    def qk_heads(acc, g, cols0, gain_ref, scale):
        for hh in range(SWA_HEADS):
            cs = slice(hh * LANE, (hh + 1) * LANE)
            y = _rms(acc[:, cs]) * gain_ref[...]
            rot = y * cosb_ref[g] + pltpu.roll(y, LANE // 2, axis=1) * sinb_ref[g]
            o_ref[:, cols0 + hh * LANE:cols0 + (hh + 1) * LANE] = (rot * scale).astype(o_ref.dtype)

    for j in range(N_COL):
        cols = slice(j * COL, (j + 1) * COL)
        g = (j - CB_Q) % len(SWA_PATTERNS) if CB_Q <= j < CB_RQK else 0
        acc = _dot(h_ref[g], w_ref[:, cols])
        if j == CB_AU:
            o_ref[:, cols] = _gelu(acc).astype(o_ref.dtype)
        elif j == CB_AV:
            o_ref[:, cols] = (_rms(_gelu(acc)) * nvg_ref[...]).astype(o_ref.dtype)
        elif CB_Q <= j < CB_K:
            qk_heads(acc, g, j * COL, qg_ref, SWA_HEAD_DIM ** -0.5)
        elif CB_K <= j < CB_V:
            qk_heads(acc, g, j * COL, kg_ref, 1.0)
        elif j < CB_RQK or j == CB_RV:
            o_ref[:, cols] = acc.astype(o_ref.dtype)
        elif j == CB_RQK:
            lane = lax.broadcasted_iota(jnp.int32, (acc.shape[0], LANE), 1)
            first_half = _imod(lane, RET_QK_DIM) < (RET_QK_DIM // 2)
            for tt in range(COL // LANE):
                y = acc[:, tt * LANE:(tt + 1) * LANE]
                partner = jnp.where(first_half,
                                    pltpu.roll(y, LANE - RET_QK_DIM // 2, axis=1),
                                    pltpu.roll(y, RET_QK_DIM // 2, axis=1))
                rot = y * cosc_ref[...] + partner * sinc_ref[...]
                scale = 1.0 if tt < (COL // LANE) // 2 else RET_QK_DIM ** -0.5
                o_ref[:, j * COL + tt * LANE:j * COL + (tt + 1) * LANE] = (rot * scale).astype(o_ref.dtype)
        elif j == CB_RG:
            o_ref[:, cols] = (acc * _sigmoid(acc)).astype(o_ref.dtype)
        else:
            o_ref[:, cols] = _sigmoid(acc).astype(o_ref.dtype)


def _inproj(x, g1, w_in_b, perm, tabs, nvg, qg, kg, n_prompt_tiles, tiles_per_seq):
    T = x.shape[0]
    cosb, sinb, cosc, sinc = tabs
    n_g = len(SWA_PATTERNS)
    row = lambda i: (i, 0)
    const = lambda i: (0, 0)
    tab_blk = lambda i: jnp.where(i < n_prompt_tiles, i % tiles_per_seq, tiles_per_seq)
    tab3 = pl.BlockSpec((n_g, TILE, LANE), lambda i: (0, tab_blk(i), 0))
    tab = pl.BlockSpec((TILE, LANE), lambda i: (tab_blk(i), 0))
    vmem = (2 * _nbytes((TILE, D_MODEL), F32) + n_g * _nbytes((TILE, D_MODEL), BF16)
            + _nbytes(w_in_b.shape, BF16) + 2 * _nbytes((TILE, IN_WIDTH), BF16)
            + 2 * (2 * n_g + 2) * _nbytes((TILE, LANE), F32) + 2 * _nbytes(perm.shape, BF16)
            + 8 * _nbytes((TILE, COL), F32))
    return pl.pallas_call(
        functools.partial(_inproj_kernel, n_prompt_tiles=n_prompt_tiles),
        grid=(T // TILE,),
        in_specs=[
            pl.BlockSpec((TILE, D_MODEL), row),
            pl.BlockSpec((1, D_MODEL), const),
            pl.BlockSpec(w_in_b.shape, const, pipeline_mode=pl.Buffered(1)),
            pl.BlockSpec(perm.shape, lambda i: (0, 0, 0)),
            tab3, tab3, tab, tab,
            pl.BlockSpec((1, COL), const),
            pl.BlockSpec((1, LANE), const),
            pl.BlockSpec((1, LANE), const),
        ],
        out_specs=pl.BlockSpec((TILE, IN_WIDTH), row),
        out_shape=jax.ShapeDtypeStruct((T, IN_WIDTH), BF16),
        scratch_shapes=[pltpu.VMEM((n_g, TILE, D_MODEL), BF16)],
        compiler_params=_params(vmem, ("parallel",)),
        name="inproj",
    )(x, g1, w_in_b, perm, cosb, sinb, cosc, sinc, nvg, qg, kg)


def _mixa_kernel(u_ref, v_ref, w_ref, b_ref, o_ref, *, cps, n_prompt_chunks):
    i = pl.program_id(0)
    for c in range(cps):
        var = ((i * cps + c) >= n_prompt_chunks).astype(jnp.int32)
        rows = slice(c * CHUNK, (c + 1) * CHUNK)
        for g in range(A_GROUPS):
            cols = slice(g * LANE, (g + 1) * LANE)
            z = _dot(w_ref[var, g], v_ref[rows, cols]) + b_ref[var, :, cols]
            o_ref[rows, cols] = (u_ref[rows, cols].astype(F32) * z).astype(o_ref.dtype)


def _mixer_a(P, w2, b2, n_prompt_chunks):
    T = P.shape[0]
    n_chunks = T // CHUNK
    cps = _pick(n_chunks, (8, 6, 4, 3, 2, 1))
    rows = cps * CHUNK
    vmem = 6 * _nbytes((rows, COL), BF16) + 2 * _nbytes(w2.shape, BF16) + 2 * _nbytes(b2.shape, F32)
    return pl.pallas_call(
        functools.partial(_mixa_kernel, cps=cps, n_prompt_chunks=n_prompt_chunks),
        grid=(n_chunks // cps,),
        in_specs=[
            pl.BlockSpec((rows, COL), lambda i: (i, CB_AU)),
            pl.BlockSpec((rows, COL), lambda i: (i, CB_AV)),
            pl.BlockSpec(w2.shape, lambda i: (0, 0, 0, 0)),
            pl.BlockSpec(b2.shape, lambda i: (0, 0, 0)),
        ],
        out_specs=pl.BlockSpec((rows, COL), lambda i: (i, 0)),
        out_shape=jax.ShapeDtypeStruct((T, COL), BF16),
        compiler_params=_params(vmem, ("parallel",)),
        name="mixer_a",
    )(P, P, w2, b2)


def _swa_kernel(q_ref, kp_ref, kc_ref, vp_ref, vc_ref, o_ref, l_ref, k_ref, v_ref, s_ref, p_ref):
    i = pl.program_id(2)
    lead = q_ref.shape[:-1]
    k_ref[0:CHUNK] = kp_ref[...].reshape(CHUNK, COL)
    k_ref[CHUNK:2 * CHUNK] = kc_ref[...].reshape(CHUNK, COL)
    v_ref[0:CHUNK] = vp_ref[...].reshape(CHUNK, COL)
    v_ref[CHUNK:2 * CHUNK] = vc_ref[...].reshape(CHUNK, COL)
    row = lax.broadcasted_iota(jnp.int32, (CHUNK, 2 * CHUNK), 0)
    col = lax.broadcasted_iota(jnp.int32, (CHUNK, 2 * CHUNK), 1)
    mask = ((col < CHUNK) & (col >= row) & (i > 0)) | ((col >= CHUNK) & (col - CHUNK <= row))
    for h in range(SWA_HEADS):
        cs = slice(h * LANE, (h + 1) * LANE)
        q = q_ref[..., cs].reshape(CHUNK, LANE)
        s_ref[h] = jnp.where(mask, _dot_nt(q, k_ref[:, cs]), NEG_INF)
    s = s_ref[...]
    m = jnp.max(s, axis=-1, keepdims=True)
    p = jnp.exp(s - m)
    den = jnp.sum(p, axis=-1, keepdims=True)
    p_ref[...] = p.astype(BF16)
    lse = m + jnp.log(den)
    for h in range(SWA_HEADS):
        cs = slice(h * LANE, (h + 1) * LANE)
        o = _dot(p_ref[h], v_ref[:, cs])
        o_ref[..., cs] = (o / den[h]).astype(o_ref.dtype).reshape(lead + (LANE,))
        l_ref[h] = jnp.broadcast_to(lse[h], (CHUNK, LANE)).reshape(lead + (LANE,))


def _swa_prompt(P, g, dil, n_p, s):
    T = P.shape[0]
    Tp = n_p * s
    nb = s // dil // CHUNK
    if dil == 1:
        src = P
        lead, lead_o = (CHUNK,), (CHUNK,)
        o_shape = (Tp, COL)
        l_shape = (SWA_HEADS, Tp, LANE)

        def in_map(cb, prev):
            return lambda b, r, i: (b * nb + (jnp.maximum(i - 1, 0) if prev else i), cb + g)

        o_map = lambda b, r, i: (b * nb + i, 0)
        l_map = lambda b, r, i: (0, b * nb + i, 0)
    else:
        rpt = TILE // dil
        tpb = CHUNK // rpt
        src = P.reshape(T // TILE, dil, rpt, IN_WIDTH)
        lead = (tpb, None, rpt)
        o_shape = (Tp // TILE, dil, rpt, COL)
        l_shape = (SWA_HEADS, Tp // TILE, dil, rpt, LANE)

        def in_map(cb, prev):
            return lambda b, r, i: (b * nb + (jnp.maximum(i - 1, 0) if prev else i), r, 0, cb + g)

        o_map = lambda b, r, i: (b * nb + i, r, 0, 0)
        l_map = lambda b, r, i: (0, b * nb + i, r, 0, 0)

    in_spec = lambda cb, prev: pl.BlockSpec(lead + (COL,), in_map(cb, prev))
    vmem = 16 * _nbytes((CHUNK, COL), BF16) + 2 * _nbytes((CHUNK, COL), F32) + 24 * _nbytes((CHUNK, 2 * CHUNK), F32)
    o, l = pl.pallas_call(
        _swa_kernel,
        grid=(n_p, dil, nb),
        in_specs=[in_spec(CB_Q, False), in_spec(CB_K, True), in_spec(CB_K, False),
                  in_spec(CB_V, True), in_spec(CB_V, False)],
        out_specs=[pl.BlockSpec(lead + (COL,), o_map),
                   pl.BlockSpec((SWA_HEADS,) + lead + (LANE,), l_map)],
        out_shape=[jax.ShapeDtypeStruct(o_shape, BF16), jax.ShapeDtypeStruct(l_shape, F32)],
        scratch_shapes=[pltpu.VMEM((2 * CHUNK, COL), BF16), pltpu.VMEM((2 * CHUNK, COL), BF16),
                        pltpu.VMEM((SWA_HEADS, CHUNK, 2 * CHUNK), F32),
                        pltpu.VMEM((SWA_HEADS, CHUNK, 2 * CHUNK), BF16)],
        compiler_params=_params(vmem, ("parallel", "parallel", "arbitrary")),
        name=f"swa_prompt_g{g}",
    )(src, src, src, src, src)
    return o.reshape(Tp, COL), l.reshape(SWA_HEADS, Tp, LANE)


def _swa_sample_kernel(q_ref, kn_ref, vn_ref, cache_ref, o_ref, l_ref, *, dil, lbuf, t_s):
    nq = SWA_HEADS * t_s
    q = q_ref[...]
    qrep = jnp.concatenate([q] * SWA_HEADS, axis=0)
    rq = lax.broadcasted_iota(jnp.int32, (nq, COL), 0)
    cq = lax.broadcasted_iota(jnp.int32, (nq, COL), 1)
    qbd = jnp.where(_idiv(rq, t_s) == _idiv(cq, LANE), qrep, 0.0).astype(BF16)

    per_pos = 2 * SWA_HEADS
    kc = jnp.concatenate([cache_ref[pl.ds(h, lbuf, stride=per_pos), :].astype(BF16)
                          for h in range(SWA_HEADS)], axis=1)
    vc = jnp.concatenate([cache_ref[pl.ds(SWA_HEADS + h, lbuf, stride=per_pos), :].astype(BF16)
                          for h in range(SWA_HEADS)], axis=1)
    kn = kn_ref[...].astype(BF16)
    vn = vn_ref[...].astype(BF16)

    s_c = _dot_nt(qbd, kc)
    s_n = _dot_nt(qbd, kn)
    t_c = _imod(lax.broadcasted_iota(jnp.int32, (nq, lbuf), 0), t_s)
    c_c = lax.broadcasted_iota(jnp.int32, (nq, lbuf), 1)
    diff_c = lbuf + t_c - c_c
    ok_c = (_imod(diff_c, dil) == 0) & (diff_c <= lbuf)
    t_n = _imod(lax.broadcasted_iota(jnp.int32, (nq, t_s), 0), t_s)
    c_n = lax.broadcasted_iota(jnp.int32, (nq, t_s), 1)
    diff_n = t_n - c_n
    ok_n = (diff_n >= 0) & (_imod(diff_n, dil) == 0)
    s_c = jnp.where(ok_c, s_c, NEG_INF)
    s_n = jnp.where(ok_n, s_n, NEG_INF)
    m = jnp.maximum(jnp.max(s_c, axis=-1, keepdims=True), jnp.max(s_n, axis=-1, keepdims=True))
    p_c = jnp.exp(s_c - m)
    p_n = jnp.exp(s_n - m)
    den = jnp.sum(p_c, axis=-1, keepdims=True) + jnp.sum(p_n, axis=-1, keepdims=True)
    o_all = (_dot(p_c.astype(BF16), vc) + _dot(p_n.astype(BF16), vn)) / den
    lse = m + jnp.log(den)
    for h in range(SWA_HEADS):
        cs = slice(h * LANE, (h + 1) * LANE)
        rs = slice(h * t_s, (h + 1) * t_s)
        o_ref[:, cs] = o_all[rs, cs]
        l_ref[h] = jnp.broadcast_to(lse[rs], (t_s, LANE))


def _swa_sample(Ps, cache, layer, g, dil):
    n_s, t_s, _ = Ps.shape
    rows = cache.shape[2]
    lbuf = rows // (2 * SWA_HEADS)
    assert lbuf == dil * CHUNK, "window buffer must hold exactly one full window"
    blk = lambda cb: pl.BlockSpec((None, t_s, COL), lambda b: (b, 0, cb + g))
    vmem = (2 * _nbytes((lbuf, 2 * COL), F32) + 2 * _nbytes((lbuf, 2 * COL), BF16)
            + 8 * _nbytes((SWA_HEADS * t_s, lbuf), F32))
    return pl.pallas_call(
        functools.partial(_swa_sample_kernel, dil=dil, lbuf=lbuf, t_s=t_s),
        grid=(n_s,),
        in_specs=[blk(CB_Q), blk(CB_K), blk(CB_V),
                  pl.BlockSpec((None, None, rows, LANE), lambda b: (layer, b, 0, 0))],
        out_specs=[pl.BlockSpec((None, t_s, COL), lambda b: (b, 0, 0)),
                   pl.BlockSpec((SWA_HEADS, None, t_s, LANE), lambda b: (0, b, 0, 0))],
        out_shape=[jax.ShapeDtypeStruct((n_s, t_s, COL), F32),
                   jax.ShapeDtypeStruct((SWA_HEADS, n_s, t_s, LANE), F32)],
        compiler_params=_params(vmem, ("parallel",)),
        name=f"swa_sample_g{g}",
    )(Ps, Ps, Ps, cache)


def _ret_head_inputs(qk_ref, v_ref, kout_ref, h):
    pair, half = h // 2, h % 2
    lane = lax.broadcasted_iota(jnp.int32, (CHUNK, LANE), 1)
    head_lanes = _idiv(lane, RET_QK_DIM) == half
    qt = qk_ref[:, pair * LANE:(pair + 1) * LANE]
    kt = qk_ref[:, COL // 2 + pair * LANE:COL // 2 + (pair + 1) * LANE]
    qm = jnp.where(head_lanes, qt, jnp.zeros_like(qt))
    kw = jnp.where(head_lanes, kt.astype(F32) * kout_ref[:, pair * LANE:(pair + 1) * LANE], 0.0)
    vh = v_ref[:, h * LANE:(h + 1) * LANE]
    return qm, kt, kw, vh


def _ret_finish(o, gate_ref, o_ref, h):
    cs = slice(h * LANE, (h + 1) * LANE)
    o_ref[:, cs] = (gate_ref[:, cs].astype(F32) * _rms(o)).astype(o_ref.dtype)


def _ret_prompt_kernel(*refs, n_seq):
    ins = refs[:3 * n_seq]
    decay_ref, qin_ref, kout_ref, gc_ref, o_ref, s_out_ref, s_ref = refs[3 * n_seq:]
    i = pl.program_id(0)

    @pl.when(i == 0)
    def _():
        s_ref[...] = jnp.zeros_like(s_ref)

    for b in range(n_seq):
        qk_ref, v_ref, gate_ref = ins[3 * b:3 * b + 3]
        for h in range(RET_HEADS):
            qm, kt, kw, vh = _ret_head_inputs(qk_ref, v_ref, kout_ref, h)
            att = _dot_nt(qm, kt) * decay_ref[h]
            state = s_ref[b, h]
            o = (_dot(att.astype(BF16), vh)
                 + _dot(qm, state.astype(BF16)) * qin_ref[:, h * LANE:(h + 1) * LANE])
            s_ref[b, h] = state * gc_ref[h:h + 1, :] + _dot(kw.T.astype(BF16), vh)
            _ret_finish(o, gate_ref, o_ref.at[b], h)

    @pl.when(i == pl.num_programs(0) - 1)
    def _():
        for b in range(n_seq):
            for h in range(RET_HEADS):
                lo = (h % 2) * RET_QK_DIM
                s_out_ref[b, h] = s_ref[b, h, lo:lo + RET_QK_DIM, :]


def _ret_prompt(P, rt, n_p, s):
    nblk = s // CHUNK
    decay, qin, kout, gc = rt
    blk = lambda b, cb: pl.BlockSpec((CHUNK, COL), lambda i: (b * nblk + i, cb))
    const2 = lambda i: (0, 0)
    seq_specs = [blk(b, cb) for b in range(n_p) for cb in (CB_RQK, CB_RV, CB_RG)]
    st_shape = (n_p, RET_HEADS, RET_QK_DIM, RET_V_DIM)
    vmem = (8 * n_p * _nbytes((CHUNK, COL), BF16) + 2 * _nbytes(decay.shape, F32) + 4 * _nbytes(qin.shape, F32)
            + 3 * n_p * _nbytes((RET_HEADS, LANE, LANE), F32))
    o, st = pl.pallas_call(
        functools.partial(_ret_prompt_kernel, n_seq=n_p),
        grid=(nblk,),
        in_specs=seq_specs + [pl.BlockSpec(decay.shape, lambda i: (0, 0, 0)),
                              pl.BlockSpec(qin.shape, const2), pl.BlockSpec(kout.shape, const2),
                              pl.BlockSpec(gc.shape, const2)],
        out_specs=[pl.BlockSpec((n_p, CHUNK, COL), lambda i: (0, i, 0)),
                   pl.BlockSpec(st_shape, lambda i: (0, 0, 0, 0))],
        out_shape=[jax.ShapeDtypeStruct((n_p, s, COL), BF16), jax.ShapeDtypeStruct(st_shape, F32)],
        scratch_shapes=[pltpu.VMEM((n_p, RET_HEADS, LANE, LANE), F32)],
        compiler_params=_params(vmem, ("arbitrary",)),
        name="ret_prompt",
    )(*([P] * (3 * n_p)), decay, qin, kout, gc)
    return o.reshape(n_p * s, COL), st


def _ret_sample_kernel(qk_ref, v_ref, gate_ref, s0_ref, decay_ref, qin_ref, kout_ref, gc_ref,
                       o_ref, s_out_ref, *, t_s):
    row = lax.broadcasted_iota(jnp.int32, (CHUNK, LANE), 0)
    for h in range(RET_HEADS):
        lo = (h % 2) * RET_QK_DIM
        qm, kt, kw, vh = _ret_head_inputs(qk_ref, v_ref, kout_ref, h)
        att = _dot_nt(qm, kt) * decay_ref[h]
        o = _dot(att.astype(BF16), vh)
        inter = jnp.zeros((CHUNK, LANE), F32)
        for sq in range(CHUNK // t_s):
            seq_rows = _idiv(row, t_s) == sq
            st = s0_ref[sq, h]
            st2 = jnp.concatenate([st, st], axis=0).astype(BF16)
            inter = jnp.where(seq_rows, _dot(qm, st2), inter)
            upd = _dot(jnp.where(seq_rows, kw, 0.0).T.astype(BF16), vh)
            s_out_ref[sq, h] = st * gc_ref[h:h + 1, :] + upd[lo:lo + RET_QK_DIM, :]
        o = o + inter * qin_ref[:, h * LANE:(h + 1) * LANE]
        _ret_finish(o, gate_ref, o_ref, h)


def _ret_sample(P, state, layer, rt, Tp, t_s):
    n_s = state.shape[1]
    spb = CHUNK // t_s
    base = Tp // CHUNK
    decay, qin, kout, gc = rt
    blk = lambda cb: pl.BlockSpec((CHUNK, COL), lambda i: (base + i, cb))
    const2 = lambda i: (0, 0)
    st_shape = (spb, RET_HEADS, RET_QK_DIM, RET_V_DIM)
    vmem = (8 * _nbytes((CHUNK, COL), BF16) + 2 * _nbytes(decay.shape, F32) + 4 * _nbytes(qin.shape, F32)
            + 4 * _nbytes(st_shape, F32))
    return pl.pallas_call(
        functools.partial(_ret_sample_kernel, t_s=t_s),
        grid=(n_s // spb,),
        in_specs=[blk(CB_RQK), blk(CB_RV), blk(CB_RG),
                  pl.BlockSpec((None,) + st_shape, lambda i: (layer, i, 0, 0, 0)),
                  pl.BlockSpec(decay.shape, lambda i: (0, 0, 0)),
                  pl.BlockSpec(qin.shape, const2), pl.BlockSpec(kout.shape, const2),
                  pl.BlockSpec(gc.shape, const2)],
        out_specs=[pl.BlockSpec((CHUNK, COL), lambda i: (i, 0)),
                   pl.BlockSpec(st_shape, lambda i: (i, 0, 0, 0))],
        out_shape=[jax.ShapeDtypeStruct((n_s * t_s, COL), BF16),
                   jax.ShapeDtypeStruct((n_s, RET_HEADS, RET_QK_DIM, RET_V_DIM), F32)],
        compiler_params=_params(vmem, ("parallel",)),
        name="ret_sample",
    )(P, P, P, state, decay, qin, kout, gc)


def _branch_kernel(x_ref, a_ref, g0_ref, g1_ref, g2_ref, wb_ref, wo_ref,
                   po0, po1, po2, pl0, pl1, pl2, pc, so0, so1, so2, sl0, sl1, sl2, sc,
                   y_ref, on_ref, ln_ref, *, n_prompt_tiles):
    i = pl.program_id(0)

    def natural(o_ref, l_ref, k, dil):
        if dil == 1:
            return (lambda h: o_ref[:, h * LANE:(h + 1) * LANE].astype(F32)), (lambda h: l_ref[h])
        rpt = TILE // dil
        for h in range(SWA_HEADS):
            for r in range(dil):
                rows = slice(r * rpt, (r + 1) * rpt)
                on_ref[k, h, pl.ds(r, rpt, stride=dil), :] = o_ref[rows, h * LANE:(h + 1) * LANE].astype(F32)
                ln_ref[k, h, pl.ds(r, rpt, stride=dil), :] = l_ref[h, rows, :]
        return (lambda h: on_ref[k, h]), (lambda h: ln_ref[k, h])

    def body(o_refs, l_refs, c_ref, permuted):
        getters = [natural(o_refs[k], l_refs[k], k, SWA_PATTERNS[k][1] if permuted else 1)
                   for k in range(len(SWA_PATTERNS))]
        heads = []
        for h in range(SWA_HEADS):
            ls = [gl(h) for _, gl in getters]
            lmax = functools.reduce(jnp.maximum, ls)
            es = [jnp.exp(l - lmax) for l in ls]
            num = sum(e * go(h) for e, (go, _) in zip(es, getters))
            heads.append((num / sum(es)).astype(BF16))
        mix = jnp.concatenate(heads, axis=1)
        merged = (g0_ref[...].astype(F32) * _dot(a_ref[...], wb_ref[0])
                  + g1_ref[...].astype(F32) * _dot(mix, wb_ref[1])
                  + g2_ref[...].astype(F32) * _dot(c_ref[...], wb_ref[2]))
        y_ref[...] = x_ref[...] + _dot(merged.astype(BF16), wo_ref[...])

    @pl.when(i < n_prompt_tiles)
    def _():
        body((po0, po1, po2), (pl0, pl1, pl2), pc, True)

    @pl.when(i >= n_prompt_tiles)
    def _():
        body((so0, so1, so2), (sl0, sl1, sl2), sc, False)


def _branch(x, out_a, P, wb, wo, prompt_set, sample_set, n_prompt_tiles):
    T = x.shape[0]
    assert sample_set[0].shape[0] == TILE, "the sample rows must form exactly one tile"
    n_g = len(SWA_PATTERNS)
    last = n_prompt_tiles - 1
    row = lambda i: (i, 0)
    gate = lambda k: pl.BlockSpec((TILE, D_MODEL), lambda i: (i, CB_GATE * COL // D_MODEL + k))
    once = pl.Buffered(1)
    p_o = pl.BlockSpec((TILE, COL), lambda i: (jnp.minimum(i, last), 0))
    p_l = pl.BlockSpec((SWA_HEADS, TILE, LANE), lambda i: (0, jnp.minimum(i, last), 0))
    s_o = pl.BlockSpec((TILE, COL), lambda i: (0, 0))
    s_l = pl.BlockSpec((SWA_HEADS, TILE, LANE), lambda i: (0, 0, 0))
    vmem = (4 * _nbytes((TILE, D_MODEL), F32) + 6 * _nbytes((TILE, D_MODEL), BF16)
            + 2 * (2 * n_g + 2) * (_nbytes((TILE, COL), BF16) + _nbytes((TILE, COL), F32))
            + _nbytes(wb.shape, BF16) + _nbytes(wo.shape, BF16)
            + 4 * n_g * _nbytes((TILE, COL), F32) + 4 * _nbytes((TILE, D_MODEL), F32))
    return pl.pallas_call(
        functools.partial(_branch_kernel, n_prompt_tiles=n_prompt_tiles),
        grid=(T // TILE,),
        in_specs=[pl.BlockSpec((TILE, D_MODEL), row), pl.BlockSpec((TILE, COL), row),
                  gate(0), gate(1), gate(2),
                  pl.BlockSpec(wb.shape, lambda i: (0, 0, 0), pipeline_mode=once),
                  pl.BlockSpec(wo.shape, lambda i: (0, 0), pipeline_mode=once),
                  p_o, p_o, p_o, p_l, p_l, p_l, p_o,
                  s_o, s_o, s_o, s_l, s_l, s_l, s_o],
        out_specs=pl.BlockSpec((TILE, D_MODEL), row),
        out_shape=jax.ShapeDtypeStruct((T, D_MODEL), F32),
        scratch_shapes=[pltpu.VMEM((n_g, SWA_HEADS, TILE, LANE), F32),
                        pltpu.VMEM((n_g, SWA_HEADS, TILE, LANE), F32)],
        compiler_params=_params(vmem, ("parallel",)),
        name="branch_merge",
    )(x, out_a, P, P, P, wb, wo, *prompt_set, *sample_set)


def _swiglu_into(hb, w1_ref, w3_ref, w2_ref, acc_ref, fc):
    n_fc = w1_ref.shape[-1] // fc
    for f in range(n_fc):
        cols = slice(f * fc, (f + 1) * fc)
        a = _dot(hb, w1_ref[:, cols])
        b = _dot(hb, w3_ref[:, cols])
        part = _dot((a * _sigmoid(a) * b).astype(BF16), w2_ref[cols, :])
        if f == 0:
            acc_ref[...] = part
        else:
            acc_ref[...] += part


def _dense_ffn_kernel(x_ref, g2_ref, w1_ref, w3_ref, w2_ref, y_ref, acc_ref, *, fc):
    x = x_ref[...]
    hb = (_rms(x) * g2_ref[...]).astype(BF16)
    _swiglu_into(hb, w1_ref, w3_ref, w2_ref, acc_ref, fc)
    y_ref[...] = x + acc_ref[...]


def _dense_ffn(x, g2, w1b, w3b, w2b):
    T = x.shape[0]
    F = w1b.shape[1]
    tm = _pick(T, (768, 512, 384, 256, 128))
    fc = _pick(F, (512, 256, 128))
    row = lambda i: (i, 0)
    once = pl.Buffered(1)
    vmem = (5 * _nbytes((tm, D_MODEL), F32) + 3 * _nbytes(w1b.shape, BF16) + _nbytes((tm, D_MODEL), BF16)
            + 4 * _nbytes((tm, fc), F32))
    return pl.pallas_call(
        functools.partial(_dense_ffn_kernel, fc=fc),
        grid=(T // tm,),
        in_specs=[pl.BlockSpec((tm, D_MODEL), row),
                  pl.BlockSpec((1, D_MODEL), lambda i: (0, 0)),
                  pl.BlockSpec(w1b.shape, lambda i: (0, 0), pipeline_mode=once),
                  pl.BlockSpec(w3b.shape, lambda i: (0, 0), pipeline_mode=once),
                  pl.BlockSpec(w2b.shape, lambda i: (0, 0), pipeline_mode=once)],
        out_specs=pl.BlockSpec((tm, D_MODEL), row),
        out_shape=jax.ShapeDtypeStruct((T, D_MODEL), F32),
        scratch_shapes=[pltpu.VMEM((tm, D_MODEL), F32)],
        compiler_params=_params(vmem, ("parallel",)),
        name="dense_ffn",
    )(x, g2, w1b, w3b, w2b)


def _router_kernel(x_ref, g2_ref, wr_ref, idx_ref, wgt_ref):
    hb = (_rms(x_ref[...]) * g2_ref[...]).astype(BF16)
    logits = _dot(hb, wr_ref[...])
    lane = lax.broadcasted_iota(jnp.int32, logits.shape, 1)
    logits = jnp.where(lane < N_EXPERTS, logits, -jnp.inf)
    lane_f = lane.astype(F32)
    m1 = jnp.max(logits, axis=-1, keepdims=True)
    i1 = jnp.min(jnp.where(logits == m1, lane_f, float(LANE)), axis=-1, keepdims=True)
    rest = jnp.where(lane_f == i1, -jnp.inf, logits)
    m2 = jnp.max(rest, axis=-1, keepdims=True)
    i2 = jnp.min(jnp.where(rest == m2, lane_f, float(LANE)), axis=-1, keepdims=True)
    e = jnp.exp(m2 - m1)
    idx_ref[...] = jnp.where(lane == 0, i1, jnp.where(lane == 1, i2, 0.0)).astype(jnp.int32)
    wgt_ref[...] = jnp.where(lane == 0, 1.0 / (1.0 + e), jnp.where(lane == 1, e / (1.0 + e), 0.0))


def _router(x, g2, wr_pad):
    T = x.shape[0]
    tm = _pick(T, (768, 512, 384, 256, 128))
    row = lambda i: (i, 0)
    vmem = 4 * _nbytes((tm, D_MODEL), F32) + 2 * _nbytes(wr_pad.shape, BF16) + 8 * _nbytes((tm, LANE), F32)
    return pl.pallas_call(
        _router_kernel,
        grid=(T // tm,),
        in_specs=[pl.BlockSpec((tm, D_MODEL), row),
                  pl.BlockSpec((1, D_MODEL), lambda i: (0, 0)),
                  pl.BlockSpec(wr_pad.shape, lambda i: (0, 0))],
        out_specs=[pl.BlockSpec((tm, LANE), row), pl.BlockSpec((tm, LANE), row)],
        out_shape=[jax.ShapeDtypeStruct((T, LANE), jnp.int32), jax.ShapeDtypeStruct((T, LANE), F32)],
        compiler_params=_params(vmem, ("parallel",)),
        name="moe_router",
    )(x, g2, wr_pad)


def _row_copy(src_ref, src_row, dst_ref, dst_row, sem):
    return pltpu.make_async_copy(src_ref.at[pl.ds(src_row, 1), :], dst_ref.at[pl.ds(dst_row, 1), :], sem)


def _dispatch_kernel(dest_ref, x_ref, g2_ref, xs_in_ref, xs_ref, h_ref, sem, *, tm):
    del xs_in_ref
    h_ref[...] = _rms(x_ref[...]) * g2_ref[...]

    def issue(r, carry):
        for k in range(TOP_K):
            _row_copy(h_ref, r, xs_ref, dest_ref[0, 0, TOP_K * r + k], sem).start()
        return carry

    lax.fori_loop(0, tm, issue, 0, unroll=DMA_ISSUE_UNROLL)
    for k in range(TOP_K):
        pltpu.make_async_copy(h_ref, xs_ref.at[pl.ds(0, tm), :], sem).wait()


def _dispatch(x, g2, dest3, n_rows):
    T = x.shape[0]
    tm = dest3.shape[2] // TOP_K
    xs0 = jnp.zeros((n_rows, D_MODEL), F32)
    vmem = 3 * _nbytes((tm, D_MODEL), F32)
    return pl.pallas_call(
        functools.partial(_dispatch_kernel, tm=tm),
        grid=(T // tm,),
        in_specs=[pl.BlockSpec((1, 1, TOP_K * tm), lambda i: (i, 0, 0), memory_space=pltpu.SMEM),
                  pl.BlockSpec((tm, D_MODEL), lambda i: (i, 0)),
                  pl.BlockSpec((1, D_MODEL), lambda i: (0, 0)),
                  pl.BlockSpec(memory_space=pl.ANY)],
        out_specs=pl.BlockSpec(memory_space=pl.ANY),
        out_shape=jax.ShapeDtypeStruct((n_rows, D_MODEL), F32),
        scratch_shapes=[pltpu.VMEM((tm, D_MODEL), F32), pltpu.SemaphoreType.DMA(())],
        input_output_aliases={3: 0},
        compiler_params=_params(vmem, ("arbitrary",)),
        name="moe_dispatch",
    )(dest3, x, g2, xs0)


def _expert_ffn_kernel(te_ref, xs_ref, w1_ref, w3_ref, w2_ref, y_ref, acc_ref, *, fc):
    del te_ref
    _swiglu_into(xs_ref[...].astype(BF16), w1_ref, w3_ref, w2_ref, acc_ref, fc)
    y_ref[...] = acc_ref[...]


def _expert_ffn(xs, tile_expert, we1b, we3b, we2b, tme):
    n_rows = xs.shape[0]
    F = we1b.shape[2]
    fc = _pick(F, (512, 256, 128))
    once = pl.Buffered(1)
    wspec = lambda shape: pl.BlockSpec((None,) + shape, lambda i, te: (te[i], 0, 0), pipeline_mode=once)
    vmem = (5 * _nbytes((tme, D_MODEL), F32) + 3 * _nbytes(we1b.shape[1:], BF16) + _nbytes((tme, D_MODEL), BF16)
            + 4 * _nbytes((tme, fc), F32))
    grid_spec = pltpu.PrefetchScalarGridSpec(
        num_scalar_prefetch=1,
        grid=(n_rows // tme,),
        in_specs=[pl.BlockSpec((tme, D_MODEL), lambda i, te: (i, 0)),
                  wspec(we1b.shape[1:]), wspec(we3b.shape[1:]), wspec(we2b.shape[1:])],
        out_specs=pl.BlockSpec((tme, D_MODEL), lambda i, te: (i, 0)),
        scratch_shapes=[pltpu.VMEM((tme, D_MODEL), F32)],
    )
    return pl.pallas_call(
        functools.partial(_expert_ffn_kernel, fc=fc),
        grid_spec=grid_spec,
        out_shape=jax.ShapeDtypeStruct((n_rows, D_MODEL), F32),
        compiler_params=_params(vmem, ("arbitrary",)),
        name="moe_expert_ffn",
    )(tile_expert, xs, we1b, we3b, we2b)


def _combine_kernel(dest_ref, x_ref, wgt_ref, y_hbm_ref, o_ref, buf_ref, sem, *, tm):
    def issue(r, carry):
        for k in range(TOP_K):
            _row_copy(y_hbm_ref, dest_ref[0, 0, TOP_K * r + k], buf_ref.at[k], r, sem).start()
        return carry

    lax.fori_loop(0, tm, issue, 0, unroll=DMA_ISSUE_UNROLL)
    for k in range(TOP_K):
        pltpu.make_async_copy(y_hbm_ref.at[pl.ds(0, tm), :], buf_ref.at[k], sem).wait()
    wgt = wgt_ref[...]
    o_ref[...] = x_ref[...] + wgt[:, 0:1] * buf_ref[0] + wgt[:, 1:2] * buf_ref[1]


def _combine(x, wgt, dest3, y_rows):
    T = x.shape[0]
    tm = dest3.shape[2] // TOP_K
    row = lambda i: (i, 0)
    vmem = 7 * _nbytes((tm, D_MODEL), F32) + 2 * _nbytes((tm, LANE), F32)
    return pl.pallas_call(
        functools.partial(_combine_kernel, tm=tm),
        grid=(T // tm,),
        in_specs=[pl.BlockSpec((1, 1, TOP_K * tm), lambda i: (i, 0, 0), memory_space=pltpu.SMEM),
                  pl.BlockSpec((tm, D_MODEL), row),
                  pl.BlockSpec((tm, LANE), row),
                  pl.BlockSpec(memory_space=pl.ANY)],
        out_specs=pl.BlockSpec((tm, D_MODEL), row),
        out_shape=jax.ShapeDtypeStruct((T, D_MODEL), F32),
        scratch_shapes=[pltpu.VMEM((TOP_K, tm, D_MODEL), F32), pltpu.SemaphoreType.DMA(())],
        compiler_params=_params(vmem, ("arbitrary",)),
        name="moe_combine",
    )(dest3, x, wgt, y_rows)


def _moe(x, g2, wr_pad, we1b, we3b, we2b):
    T = x.shape[0]
    tme = 512
    tmd = _pick(T, (256, 128))
    idx, wgt = _router(x, g2, wr_pad)
    e_flat = idx[:, :TOP_K].reshape(-1)
    onehot = (e_flat[None, :] == jnp.arange(N_EXPERTS, dtype=jnp.int32)[:, None]).astype(jnp.int32)
    csum = jnp.cumsum(onehot, axis=1)
    counts = csum[:, -1]
    padded = (counts + tme - 1) // tme * tme
    pend = jnp.cumsum(padded)
    pstart = pend - padded
    dest = jnp.sum(onehot * (csum - 1 + pstart[:, None]), axis=0).astype(jnp.int32)
    n_tiles = -(-(T * TOP_K) // tme) + N_EXPERTS
    tile_start = jnp.arange(n_tiles, dtype=jnp.int32) * tme
    tile_expert = jnp.minimum(jnp.sum(pend[None, :] <= tile_start[:, None], axis=1), N_EXPERTS - 1).astype(jnp.int32)
    dest3 = dest.reshape(T // tmd, 1, TOP_K * tmd)
    xs = _dispatch(x, g2, dest3, n_tiles * tme)
    y_rows = _expert_ffn(xs, tile_expert, we1b, we3b, we2b, tme)
    return _combine(x, wgt, dest3, y_rows)


def _class_major(a, dil):
    n = a.shape[0]
    return a.reshape((n // TILE, TILE // dil, dil) + a.shape[1:]).swapaxes(1, 2).reshape(a.shape)


def _natural(a, dil):
    n = a.shape[0]
    return a.reshape((n // TILE, dil, TILE // dil) + a.shape[1:]).swapaxes(1, 2).reshape(a.shape)


def _perm_matrices():
    eye = jnp.eye(TILE, dtype=BF16)
    return jnp.stack([_class_major(eye, dil) for _, dil in SWA_PATTERNS[1:]])


def _rope_tables(pos, Tp):
    freqs = ROPE_THETA ** (-jnp.arange(0, SWA_HEAD_DIM, 2, dtype=F32) / SWA_HEAD_DIM)
    rfreqs = 1.0 / (ROPE_THETA ** jnp.linspace(0.0, 1.0, RET_QK_DIM // 2, dtype=F32))

    def swa(p):
        ang = p.astype(F32)[:, None] * freqs[None, :]
        c, s = jnp.cos(ang), jnp.sin(ang)
        return jnp.concatenate([c, c], axis=1), jnp.concatenate([-s, s], axis=1)

    per_group = [swa(jnp.concatenate([_class_major(pos[:Tp], dil), pos[Tp:]])) for _, dil in SWA_PATTERNS]
    cosb = jnp.stack([c for c, _ in per_group])
    sinb = jnp.stack([s for _, s in per_group])
    rang = pos.astype(F32)[:, None] * rfreqs[None, :]
    c, s = jnp.cos(rang), jnp.sin(rang)
    cosc = jnp.concatenate([c, c, c, c], axis=1)
    sinc = jnp.concatenate([-s, s, -s, s], axis=1)
    return cosb, sinb, cosc, sinc


def _ret_tables(c_len):
    log_g = jnp.log1p(-jnp.exp2(-5.0 - jnp.arange(RET_HEADS, dtype=F32)))
    r = jnp.arange(CHUNK)
    i = (r % c_len).astype(F32)
    same = (r[:, None] // c_len) == (r[None, :] // c_len)
    dist = i[:, None] - i[None, :]
    decay = jnp.where(same[None] & (dist >= 0)[None],
                      jnp.exp(log_g[:, None, None] * jnp.maximum(dist, 0.0)[None]), 0.0)
    qin = jnp.repeat(jnp.exp(log_g[None, :] * (i[:, None] + 1.0)), RET_V_DIM, axis=1)
    kout = jnp.repeat(jnp.exp(log_g[None, :] * (c_len - 1.0 - i)[:, None]), RET_QK_DIM, axis=1)
    gc = jnp.broadcast_to(jnp.exp(log_g * c_len)[:, None], (RET_HEADS, LANE))
    gc = jnp.concatenate([gc, jnp.zeros((8 - RET_HEADS, LANE), F32)], axis=0)
    return decay.astype(F32), qin.astype(F32), kout.astype(F32), gc.astype(F32)


def _mixa_tables(w_s, b_s, t_s):
    w_p = jnp.tril(w_s)
    w8 = jnp.tril(w_s[:, :t_s, :t_s])
    eye = jnp.eye(CHUNK // t_s, dtype=w_s.dtype)
    w_smp = jax.vmap(lambda m: jnp.kron(eye, m))(w8)
    w2 = jnp.stack([w_p, w_smp]).astype(BF16)
    b_p = jnp.repeat(b_s.T, LANE, axis=1)
    b_smp = jnp.repeat(jnp.tile(b_s[:, :t_s].T, (CHUNK // t_s, 1)), LANE, axis=1)
    return w2, jnp.stack([b_p, b_smp]).astype(F32)


def kernel(x_prompt, x_sample, cache_swa_kv0, cache_swa_kv1, cache_swa_kv2, state_ret, norm1_g, w_in, norm_v_g, w_s, b_s, q_norm_g, k_norm_g, w_branch, w_out, norm2_g, w1, w3, w2, w_router, we1, we3, we2):
    n_p, s, d = x_prompt.shape
    n_s, t_s, _ = x_sample.shape
    depth = w_in.shape[0]
    Tp, Ts = n_p * s, n_s * t_s
    T = Tp + Ts
    max_win, max_dil = SWA_PATTERNS[-1]
    assert d == D_MODEL and Ts == TILE and CHUNK % t_s == 0
    assert s % (CHUNK * max_dil) == 0 and s >= max_win
    n_pt = Tp // TILE

    x = jnp.concatenate([x_prompt.reshape(Tp, d), x_sample.reshape(Ts, d)], axis=0)
    pos = jnp.concatenate([jnp.arange(s, dtype=jnp.int32),
                           jnp.tile(PAST_LEN + jnp.arange(t_s, dtype=jnp.int32), n_s)])
    tabs = _rope_tables(pos, s)
    perm = _perm_matrices()
    rt_prompt = _ret_tables(CHUNK)
    rt_sample = _ret_tables(t_s)
    caches = tuple(c.reshape(c.shape[0], c.shape[1], -1, SWA_HEAD_DIM)
                   for c in (cache_swa_kv0, cache_swa_kv1, cache_swa_kv2))

    p_kv = [[] for _ in SWA_PATTERNS]
    s_kv = [[] for _ in SWA_PATTERNS]
    p_ret, s_ret, s_v = [], [], []
    for layer in range(depth):
        P = _inproj(x, norm1_g[layer][None, :], w_in[layer].astype(BF16), perm, tabs,
                    norm_v_g[layer][None, :], q_norm_g[layer][None, :], k_norm_g[layer][None, :],
                    n_pt, s // TILE)

        w2a, b2a = _mixa_tables(w_s[layer], b_s[layer], t_s)
        out_a = _mixer_a(P, w2a, b2a, Tp // CHUNK)

        Ps = P[Tp:].astype(F32).reshape(n_s, t_s, IN_WIDTH)
        po, plse, so, slse = [], [], [], []
        for g, (win, dil) in enumerate(SWA_PATTERNS):
            o_g, l_g = _swa_prompt(P, g, dil, n_p, s)
            os_g, ls_g = _swa_sample(Ps, caches[g], layer, g, dil)
            po.append(o_g)
            plse.append(l_g)
            so.append(os_g.reshape(Ts, COL).astype(BF16))
            slse.append(ls_g.reshape(SWA_HEADS, Ts, LANE))

        out_c, ret_p = _ret_prompt(P, rt_prompt, n_p, s)
        out_cs, ret_s = _ret_sample(P, state_ret, layer, rt_sample, Tp, t_s)

        x = _branch(x, out_a, P, w_branch[layer].astype(BF16), w_out[layer].astype(BF16),
                    (*po, *plse, out_c), (*so, *slse, out_cs), n_pt)

        g2 = norm2_g[layer][None, :]
        i = layer // 2
        if layer % 2 == 0:
            x = _dense_ffn(x, g2, w1[i].astype(BF16), w3[i].astype(BF16), w2[i].astype(BF16))
        else:
            wr_pad = jnp.zeros((D_MODEL, LANE), BF16).at[:, :N_EXPERTS].set(w_router[i].astype(BF16))
            x = _moe(x, g2, wr_pad, we1[i].astype(BF16), we3[i].astype(BF16), we2[i].astype(BF16))

        for g, (win, dil) in enumerate(SWA_PATTERNS):
            kcols = slice((CB_K + g) * COL, (CB_K + g + 1) * COL)
            vcols = slice((CB_V + g) * COL, (CB_V + g + 1) * COL)
            keep = -(-min(win, s) // TILE) * TILE

            def rows(cols):
                blk = jnp.concatenate([P[(b + 1) * s - keep:(b + 1) * s, cols] for b in range(n_p)])
                blk = _natural(blk, dil).reshape(n_p, keep, SWA_HEADS, SWA_HEAD_DIM)
                return blk[:, keep - min(win, s):]

            p_kv[g].append(jnp.stack([rows(kcols), rows(vcols)], axis=2).astype(F32))
            ks = P[Tp:, kcols].reshape(n_s, t_s, SWA_HEADS, SWA_HEAD_DIM)
            vs = P[Tp:, vcols].reshape(n_s, t_s, SWA_HEADS, SWA_HEAD_DIM)
            s_kv[g].append(jnp.stack([ks, vs], axis=2).astype(F32))
        p_ret.append(ret_p)
        s_ret.append(ret_s)
        s_v.append(P[Tp:, CB_AV * COL:(CB_AV + 1) * COL].astype(F32).reshape(n_s, t_s, COL))

    y_prompt = x[:Tp].reshape(n_p, s, d)
    y_sample = x[Tp:].reshape(n_s, t_s, d)
    return (y_prompt, y_sample,
            jnp.stack(p_kv[0]), jnp.stack(p_kv[1]), jnp.stack(p_kv[2]), jnp.stack(p_ret),
            jnp.stack(s_kv[0]), jnp.stack(s_kv[1]), jnp.stack(s_kv[2]), jnp.stack(s_ret),
            jnp.stack(s_v))
```

```python
import functools
import math

import jax
import jax.numpy as jnp
from jax import lax
from jax.experimental import pallas as pl
from jax.experimental.pallas import tpu as pltpu

F32 = jnp.float32
BF16 = jnp.bfloat16

PAST_LEN = 16384
EPS = 1e-6
NEG_INF = -1e30
ROPE_THETA = 10000.0

D_MODEL = 1024
LANE = 128
CHUNK = 128
TILE = 256
COL = 512
A_GROUPS = 4
SWA_PATTERNS = ((128, 1), (512, 4), (2048, 16))
SWA_HEADS = 4
SWA_HEAD_DIM = 128
RET_HEADS = 4
RET_QK_DIM = 64
RET_V_DIM = 128
N_EXPERTS = 8
TOP_K = 2
IN_WIDTH = 10240
N_COL = IN_WIDTH // COL

CB_AU, CB_AV, CB_Q, CB_K, CB_V, CB_RQK, CB_RV, CB_RG, CB_GATE = 0, 1, 2, 5, 8, 11, 12, 13, 14

VMEM_INTERNAL_SCRATCH = 8 * 1024 * 1024


def _pick(n, candidates):
    for c in candidates:
        if n % c == 0:
            return c
    raise ValueError(f"no tile in {candidates} divides {n}")


def _params(block_bytes, semantics=None):
    limit = int(block_bytes) + VMEM_INTERNAL_SCRATCH
    return pltpu.CompilerParams(dimension_semantics=semantics, vmem_limit_bytes=limit)


def _nbytes(shape, dtype):
    return math.prod(shape) * jnp.dtype(dtype).itemsize


def _rms(x):
    return x * lax.rsqrt(jnp.mean(x * x, axis=-1, keepdims=True) + EPS)


def _gelu(x):
    return 0.5 * x * (1.0 + lax.erf(x * (0.5 ** 0.5)))


def _sigmoid(x):
    return 1.0 / (1.0 + jnp.exp(-x))


def _idiv(x, n):
    assert n & (n - 1) == 0
    return x >> (n.bit_length() - 1)


def _imod(x, n):
    assert n & (n - 1) == 0
    return x & (n - 1)


def _dot(a, b):
    return jnp.dot(a, b, preferred_element_type=F32)


def _dot_nt(a, b):
    return lax.dot_general(a, b, (((1,), (1,)), ((), ())), preferred_element_type=F32)


def _tile_rows(i, n_prompt_tiles, xa_ref, xb_ref):
    return jnp.where(i < n_prompt_tiles, xa_ref[...], xb_ref[...])


def _x_specs(xa, xb):
    last_a, last_b = xa.shape[0] // TILE - 1, xb.shape[0] // TILE - 1
    return [pl.BlockSpec((TILE, D_MODEL), lambda i, *_: (jnp.minimum(i, last_a), 0)),
            pl.BlockSpec((TILE, D_MODEL), lambda i, *_: (last_b, 0))]


def _inproj_kernel(xa_ref, xb_ref, g1_ref, w_ref, perm_ref, cosb_ref, sinb_ref, cosc_ref, sinc_ref,
                   nvg_ref, qg_ref, kg_ref, o_ref, h_ref, *, n_prompt_tiles):
    i = pl.program_id(0)
    hn = (_rms(_tile_rows(i, n_prompt_tiles, xa_ref, xb_ref)) * g1_ref[...]).astype(BF16)
    h_ref[0] = hn

    @pl.when(i < n_prompt_tiles)
    def _():
        for k in range(1, len(SWA_PATTERNS)):
            h_ref[k] = _dot(perm_ref[k - 1], hn).astype(BF16)

    @pl.when(i >= n_prompt_tiles)
    def _():
        for k in range(1, len(SWA_PATTERNS)):
            h_ref[k] = hn

    def qk_heads(acc, g, cols0, gain_ref, scale):
        for hh in range(SWA_HEADS):
            cs = slice(hh * LANE, (hh + 1) * LANE)
            y = _rms(acc[:, cs]) * gain_ref[...]
            rot = y * cosb_ref[g] + pltpu.roll(y, LANE // 2, axis=1) * sinb_ref[g]
            o_ref[:, cols0 + hh * LANE:cols0 + (hh + 1) * LANE] = (rot * scale).astype(o_ref.dtype)

    for j in range(N_COL):
        cols = slice(j * COL, (j + 1) * COL)
        g = (j - CB_Q) % len(SWA_PATTERNS) if CB_Q <= j < CB_RQK else 0
        acc = _dot(h_ref[g], w_ref[:, cols])
        if j == CB_AU:
            o_ref[:, cols] = _gelu(acc).astype(o_ref.dtype)
        elif j == CB_AV:
            o_ref[:, cols] = (_rms(_gelu(acc)) * nvg_ref[...]).astype(o_ref.dtype)
        elif CB_Q <= j < CB_K:
            qk_heads(acc, g, j * COL, qg_ref, SWA_HEAD_DIM ** -0.5)
        elif CB_K <= j < CB_V:
            qk_heads(acc, g, j * COL, kg_ref, 1.0)
        elif j < CB_RQK or j == CB_RV:
            o_ref[:, cols] = acc.astype(o_ref.dtype)
        elif j == CB_RQK:
            lane = lax.broadcasted_iota(jnp.int32, (acc.shape[0], LANE), 1)
            first_half = _imod(lane, RET_QK_DIM) < (RET_QK_DIM // 2)
            for tt in range(COL // LANE):
                y = acc[:, tt * LANE:(tt + 1) * LANE]
                partner = jnp.where(first_half,
                                    pltpu.roll(y, LANE - RET_QK_DIM // 2, axis=1),
                                    pltpu.roll(y, RET_QK_DIM // 2, axis=1))
                rot = y * cosc_ref[...] + partner * sinc_ref[...]
                scale = 1.0 if tt < (COL // LANE) // 2 else RET_QK_DIM ** -0.5
                o_ref[:, j * COL + tt * LANE:j * COL + (tt + 1) * LANE] = (rot * scale).astype(o_ref.dtype)
        elif j == CB_RG:
            o_ref[:, cols] = (acc * _sigmoid(acc)).astype(o_ref.dtype)
        else:
            o_ref[:, cols] = _sigmoid(acc).astype(o_ref.dtype)


def _inproj(xa, xb, g1, w_in_b, perm, tabs, nvg, qg, kg, n_prompt_tiles, tiles_per_seq):
    T = (n_prompt_tiles + 1) * TILE
    cosb, sinb, cosc, sinc = tabs
    n_g = len(SWA_PATTERNS)
    row = lambda i: (i, 0)
    const = lambda i: (0, 0)
    tab_blk = lambda i: jnp.where(i < n_prompt_tiles, i % tiles_per_seq, tiles_per_seq)
    tab3 = pl.BlockSpec((n_g, TILE, LANE), lambda i: (0, tab_blk(i), 0))
    tab = pl.BlockSpec((TILE, LANE), lambda i: (tab_blk(i), 0))
    vmem = (2 * _nbytes((TILE, D_MODEL), F32) + n_g * _nbytes((TILE, D_MODEL), BF16)
            + _nbytes(w_in_b.shape, BF16) + 2 * _nbytes((TILE, IN_WIDTH), BF16)
            + 2 * (2 * n_g + 2) * _nbytes((TILE, LANE), F32) + 2 * _nbytes(perm.shape, BF16)
            + 8 * _nbytes((TILE, COL), F32))
    return pl.pallas_call(
        functools.partial(_inproj_kernel, n_prompt_tiles=n_prompt_tiles),
        grid=(T // TILE,),
        in_specs=_x_specs(xa, xb) + [
            pl.BlockSpec((1, D_MODEL), const),
            pl.BlockSpec(w_in_b.shape, const, pipeline_mode=pl.Buffered(1)),
            pl.BlockSpec(perm.shape, lambda i: (0, 0, 0)),
            tab3, tab3, tab, tab,
            pl.BlockSpec((1, COL), const),
            pl.BlockSpec((1, LANE), const),
            pl.BlockSpec((1, LANE), const),
        ],
        out_specs=pl.BlockSpec((TILE, IN_WIDTH), row),
        out_shape=jax.ShapeDtypeStruct((T, IN_WIDTH), BF16),
        scratch_shapes=[pltpu.VMEM((n_g, TILE, D_MODEL), BF16)],
        compiler_params=_params(vmem, ("parallel",)),
        name="inproj",
    )(xa, xb, g1, w_in_b, perm, cosb, sinb, cosc, sinc, nvg, qg, kg)


def _mixa_kernel(u_ref, v_ref, w_ref, b_ref, o_ref, *, cps, n_prompt_chunks):
    i = pl.program_id(0)
    for c in range(cps):
        var = ((i * cps + c) >= n_prompt_chunks).astype(jnp.int32)
        rows = slice(c * CHUNK, (c + 1) * CHUNK)
        for g in range(A_GROUPS):
            cols = slice(g * LANE, (g + 1) * LANE)
            z = _dot(w_ref[var, g], v_ref[rows, cols]) + b_ref[var, :, cols]
            o_ref[rows, cols] = (u_ref[rows, cols].astype(F32) * z).astype(o_ref.dtype)


def _mixer_a(P, w2, b2, n_prompt_chunks):
    T = P.shape[0]
    n_chunks = T // CHUNK
    cps = _pick(n_chunks, (8, 6, 4, 3, 2, 1))
    rows = cps * CHUNK
    vmem = 6 * _nbytes((rows, COL), BF16) + 2 * _nbytes(w2.shape, BF16) + 2 * _nbytes(b2.shape, F32)
    return pl.pallas_call(
        functools.partial(_mixa_kernel, cps=cps, n_prompt_chunks=n_prompt_chunks),
        grid=(n_chunks // cps,),
        in_specs=[
            pl.BlockSpec((rows, COL), lambda i: (i, CB_AU)),
            pl.BlockSpec((rows, COL), lambda i: (i, CB_AV)),
            pl.BlockSpec(w2.shape, lambda i: (0, 0, 0, 0)),
            pl.BlockSpec(b2.shape, lambda i: (0, 0, 0)),
        ],
        out_specs=pl.BlockSpec((rows, COL), lambda i: (i, 0)),
        out_shape=jax.ShapeDtypeStruct((T, COL), BF16),
        compiler_params=_params(vmem, ("parallel",)),
        name="mixer_a",
    )(P, P, w2, b2)


def _swa_kernel(q_ref, kp_ref, kc_ref, vp_ref, vc_ref, o_ref, l_ref, k_ref, v_ref, s_ref, p_ref, *, qb):
    i = pl.program_id(2)
    rows = qb * CHUNK
    lead = q_ref.shape[:-1]
    k_ref[0:CHUNK] = kp_ref[...].reshape(CHUNK, COL)
    k_ref[CHUNK:CHUNK + rows] = kc_ref[...].reshape(rows, COL)
    v_ref[0:CHUNK] = vp_ref[...].reshape(CHUNK, COL)
    v_ref[CHUNK:CHUNK + rows] = vc_ref[...].reshape(rows, COL)
    row = lax.broadcasted_iota(jnp.int32, (CHUNK, 2 * CHUNK), 0)
    col = lax.broadcasted_iota(jnp.int32, (CHUNK, 2 * CHUNK), 1)
    mask_cur = (col >= CHUNK) & (col - CHUNK <= row)
    mask_all = mask_cur | ((col < CHUNK) & (col >= row))
    mask_first = mask_cur | ((col < CHUNK) & (col >= row) & (i > 0))
    q_all = q_ref[...].reshape(rows, COL)
    for j in range(qb):
        mask = mask_first if j == 0 else mask_all
        for h in range(SWA_HEADS):
            cs = slice(h * LANE, (h + 1) * LANE)
            sc = _dot_nt(q_all[j * CHUNK:(j + 1) * CHUNK, cs], k_ref[j * CHUNK:(j + 2) * CHUNK, cs])
            s_ref[j * SWA_HEADS + h] = jnp.where(mask, sc, NEG_INF)
    s = s_ref[...]
    m = jnp.max(s, axis=-1, keepdims=True)
    p = jnp.exp(s - m)
    den = jnp.sum(p, axis=-1, keepdims=True)
    p_ref[...] = p.astype(BF16)
    lse = m + jnp.log(den)
    for h in range(SWA_HEADS):
        cs = slice(h * LANE, (h + 1) * LANE)
        o_h, l_h = [], []
        for j in range(qb):
            n = j * SWA_HEADS + h
            o_h.append(_dot(p_ref[n], v_ref[j * CHUNK:(j + 2) * CHUNK, cs]) / den[n])
            l_h.append(jnp.broadcast_to(lse[n], (CHUNK, LANE)))
        o_ref[..., cs] = jnp.concatenate(o_h, axis=0).astype(o_ref.dtype).reshape(lead + (LANE,))
        l_ref[h] = jnp.concatenate(l_h, axis=0).reshape(lead + (LANE,))


def _swa_prompt(P, g, dil, n_p, s):
    T = P.shape[0]
    Tp = n_p * s
    nb = s // dil // CHUNK
    qb = _pick(nb, (4, 2, 1))
    steps = nb // qb
    prev_blk = lambda b, i: b * nb + jnp.maximum(i * qb - 1, 0)
    if dil == 1:
        src = P
        lead_q, lead_p = (qb * CHUNK,), (CHUNK,)
        o_shape = (Tp, COL)
        l_shape = (SWA_HEADS, Tp, LANE)
        q_map = lambda cb: (lambda b, r, i: (b * steps + i, cb + g))
        p_map = lambda cb: (lambda b, r, i: (prev_blk(b, i), cb + g))
        o_map = lambda b, r, i: (b * steps + i, 0)
        l_map = lambda b, r, i: (0, b * steps + i, 0)
    else:
        rpt = TILE // dil
        tpb = CHUNK // rpt
        src = P.reshape(T // TILE, dil, rpt, IN_WIDTH)
        lead_q, lead_p = (qb * tpb, None, rpt), (tpb, None, rpt)
        o_shape = (Tp // TILE, dil, rpt, COL)
        l_shape = (SWA_HEADS, Tp // TILE, dil, rpt, LANE)
        q_map = lambda cb: (lambda b, r, i: (b * steps + i, r, 0, cb + g))
        p_map = lambda cb: (lambda b, r, i: (prev_blk(b, i), r, 0, cb + g))
        o_map = lambda b, r, i: (b * steps + i, r, 0, 0)
        l_map = lambda b, r, i: (0, b * steps + i, r, 0, 0)

    q_spec = lambda cb: pl.BlockSpec(lead_q + (COL,), q_map(cb))
    p_spec = lambda cb: pl.BlockSpec(lead_p + (COL,), p_map(cb))
    n_pairs = qb * SWA_HEADS
    vmem = ((10 * qb + 6) * _nbytes((CHUNK, COL), BF16) + 2 * qb * _nbytes((CHUNK, COL), F32)
            + 6 * n_pairs * _nbytes((CHUNK, 2 * CHUNK), F32))
    o, l = pl.pallas_call(
        functools.partial(_swa_kernel, qb=qb),
        grid=(n_p, dil, steps),
        in_specs=[q_spec(CB_Q), p_spec(CB_K), q_spec(CB_K), p_spec(CB_V), q_spec(CB_V)],
        out_specs=[pl.BlockSpec(lead_q + (COL,), o_map),
                   pl.BlockSpec((SWA_HEADS,) + lead_q + (LANE,), l_map)],
        out_shape=[jax.ShapeDtypeStruct(o_shape, BF16), jax.ShapeDtypeStruct(l_shape, F32)],
        scratch_shapes=[pltpu.VMEM(((qb + 1) * CHUNK, COL), BF16), pltpu.VMEM(((qb + 1) * CHUNK, COL), BF16),
                        pltpu.VMEM((n_pairs, CHUNK, 2 * CHUNK), F32),
                        pltpu.VMEM((n_pairs, CHUNK, 2 * CHUNK), BF16)],
        compiler_params=_params(vmem, ("parallel", "parallel", "arbitrary")),
        name=f"swa_prompt_g{g}",
    )(src, src, src, src, src)
    return o.reshape(Tp, COL), l.reshape(SWA_HEADS, Tp, LANE)


def _swa_sample_kernel(q_ref, kn_ref, vn_ref, cache_ref, o_ref, l_ref, *, dil, lbuf, t_s):
    nq = SWA_HEADS * t_s
    q = q_ref[...]
    qrep = jnp.concatenate([q] * SWA_HEADS, axis=0)
    rq = lax.broadcasted_iota(jnp.int32, (nq, COL), 0)
    cq = lax.broadcasted_iota(jnp.int32, (nq, COL), 1)
    qbd = jnp.where(_idiv(rq, t_s) == _idiv(cq, LANE), qrep, 0.0).astype(BF16)

    per_pos = 2 * SWA_HEADS
    kc = jnp.concatenate([cache_ref[pl.ds(h, lbuf, stride=per_pos), :].astype(BF16)
                          for h in range(SWA_HEADS)], axis=1)
    vc = jnp.concatenate([cache_ref[pl.ds(SWA_HEADS + h, lbuf, stride=per_pos), :].astype(BF16)
                          for h in range(SWA_HEADS)], axis=1)
    kn = kn_ref[...].astype(BF16)
    vn = vn_ref[...].astype(BF16)

    s_c = _dot_nt(qbd, kc)
    s_n = _dot_nt(qbd, kn)
    t_c = _imod(lax.broadcasted_iota(jnp.int32, (nq, lbuf), 0), t_s)
    c_c = lax.broadcasted_iota(jnp.int32, (nq, lbuf), 1)
    diff_c = lbuf + t_c - c_c
    ok_c = (_imod(diff_c, dil) == 0) & (diff_c <= lbuf)
    t_n = _imod(lax.broadcasted_iota(jnp.int32, (nq, t_s), 0), t_s)
    c_n = lax.broadcasted_iota(jnp.int32, (nq, t_s), 1)
    diff_n = t_n - c_n
    ok_n = (diff_n >= 0) & (_imod(diff_n, dil) == 0)
    s_c = jnp.where(ok_c, s_c, NEG_INF)
    s_n = jnp.where(ok_n, s_n, NEG_INF)
    m = jnp.maximum(jnp.max(s_c, axis=-1, keepdims=True), jnp.max(s_n, axis=-1, keepdims=True))
    p_c = jnp.exp(s_c - m)
    p_n = jnp.exp(s_n - m)
    den = jnp.sum(p_c, axis=-1, keepdims=True) + jnp.sum(p_n, axis=-1, keepdims=True)
    o_all = (_dot(p_c.astype(BF16), vc) + _dot(p_n.astype(BF16), vn)) / den
    lse = m + jnp.log(den)
    for h in range(SWA_HEADS):
        cs = slice(h * LANE, (h + 1) * LANE)
        rs = slice(h * t_s, (h + 1) * t_s)
        o_ref[:, cs] = o_all[rs, cs]
        l_ref[h] = jnp.broadcast_to(lse[rs], (t_s, LANE))


def _swa_sample(Ps, cache, layer, g, dil):
    n_s, t_s, _ = Ps.shape
    rows = cache.shape[2]
    lbuf = rows // (2 * SWA_HEADS)
    assert lbuf == dil * CHUNK, "window buffer must hold exactly one full window"
    blk = lambda cb: pl.BlockSpec((None, t_s, COL), lambda b: (b, 0, cb + g))
    vmem = (2 * _nbytes((lbuf, 2 * COL), F32) + 2 * _nbytes((lbuf, 2 * COL), BF16)
            + 8 * _nbytes((SWA_HEADS * t_s, lbuf), F32))
    return pl.pallas_call(
        functools.partial(_swa_sample_kernel, dil=dil, lbuf=lbuf, t_s=t_s),
        grid=(n_s,),
        in_specs=[blk(CB_Q), blk(CB_K), blk(CB_V),
                  pl.BlockSpec((None, None, rows, LANE), lambda b: (layer, b, 0, 0))],
        out_specs=[pl.BlockSpec((None, t_s, COL), lambda b: (b, 0, 0)),
                   pl.BlockSpec((SWA_HEADS, None, t_s, LANE), lambda b: (0, b, 0, 0))],
        out_shape=[jax.ShapeDtypeStruct((n_s, t_s, COL), F32),
                   jax.ShapeDtypeStruct((SWA_HEADS, n_s, t_s, LANE), F32)],
        compiler_params=_params(vmem, ("parallel",)),
        name=f"swa_sample_g{g}",
    )(Ps, Ps, Ps, cache)


def _ret_head_inputs(qk_ref, v_ref, kout_ref, h):
    pair, half = h // 2, h % 2
    lane = lax.broadcasted_iota(jnp.int32, (CHUNK, LANE), 1)
    head_lanes = _idiv(lane, RET_QK_DIM) == half
    qt = qk_ref[:, pair * LANE:(pair + 1) * LANE]
    kt = qk_ref[:, COL // 2 + pair * LANE:COL // 2 + (pair + 1) * LANE]
    qm = jnp.where(head_lanes, qt, jnp.zeros_like(qt))
    kw = jnp.where(head_lanes, kt.astype(F32) * kout_ref[:, pair * LANE:(pair + 1) * LANE], 0.0)
    vh = v_ref[:, h * LANE:(h + 1) * LANE]
    return qm, kt, kw, vh


def _ret_finish(o, gate_ref, o_ref, h):
    cs = slice(h * LANE, (h + 1) * LANE)
    o_ref[:, cs] = (gate_ref[:, cs].astype(F32) * _rms(o)).astype(o_ref.dtype)


def _ret_prompt_kernel(*refs, n_seq):
    ins = refs[:3 * n_seq]
    decay_ref, qin_ref, kout_ref, gc_ref, o_ref, s_out_ref, s_ref, att_ref = refs[3 * n_seq:]
    i = pl.program_id(0)

    @pl.when(i == 0)
    def _():
        s_ref[...] = jnp.zeros_like(s_ref)

    pairs = [(b, h) for b in range(n_seq) for h in range(RET_HEADS)]
    head_in = lambda b, h: _ret_head_inputs(ins[3 * b], ins[3 * b + 1], kout_ref, h)
    for b, h in pairs:
        qm, kt, _, _ = head_in(b, h)
        att_ref[b, h] = (_dot_nt(qm, kt) * decay_ref[h]).astype(BF16)
    for b, h in pairs:
        qm, _, _, vh = head_in(b, h)
        o = (_dot(att_ref[b, h], vh)
             + _dot(qm, s_ref[b, h].astype(BF16)) * qin_ref[:, h * LANE:(h + 1) * LANE])
        _ret_finish(o, ins[3 * b + 2], o_ref.at[b], h)
    for b, h in pairs:
        _, _, kw, vh = head_in(b, h)
        s_ref[b, h] = s_ref[b, h] * gc_ref[h:h + 1, :] + _dot(kw.T.astype(BF16), vh)

    @pl.when(i == pl.num_programs(0) - 1)
    def _():
        for b in range(n_seq):
            for h in range(RET_HEADS):
                lo = (h % 2) * RET_QK_DIM
                s_out_ref[b, h] = s_ref[b, h, lo:lo + RET_QK_DIM, :]


def _ret_prompt(P, rt, n_p, s):
    nblk = s // CHUNK
    decay, qin, kout, gc = rt
    blk = lambda b, cb: pl.BlockSpec((CHUNK, COL), lambda i: (b * nblk + i, cb))
    const2 = lambda i: (0, 0)
    seq_specs = [blk(b, cb) for b in range(n_p) for cb in (CB_RQK, CB_RV, CB_RG)]
    st_shape = (n_p, RET_HEADS, RET_QK_DIM, RET_V_DIM)
    vmem = (8 * n_p * _nbytes((CHUNK, COL), BF16) + 2 * _nbytes(decay.shape, F32) + 4 * _nbytes(qin.shape, F32)
            + 3 * n_p * _nbytes((RET_HEADS, LANE, LANE), F32))
    o, st = pl.pallas_call(
        functools.partial(_ret_prompt_kernel, n_seq=n_p),
        grid=(nblk,),
        in_specs=seq_specs + [pl.BlockSpec(decay.shape, lambda i: (0, 0, 0)),
                              pl.BlockSpec(qin.shape, const2), pl.BlockSpec(kout.shape, const2),
                              pl.BlockSpec(gc.shape, const2)],
        out_specs=[pl.BlockSpec((n_p, CHUNK, COL), lambda i: (0, i, 0)),
                   pl.BlockSpec(st_shape, lambda i: (0, 0, 0, 0))],
        out_shape=[jax.ShapeDtypeStruct((n_p, s, COL), BF16), jax.ShapeDtypeStruct(st_shape, F32)],
        scratch_shapes=[pltpu.VMEM((n_p, RET_HEADS, LANE, LANE), F32),
                        pltpu.VMEM((n_p, RET_HEADS, CHUNK, CHUNK), BF16)],
        compiler_params=_params(vmem, ("arbitrary",)),
        name="ret_prompt",
    )(*([P] * (3 * n_p)), decay, qin, kout, gc)
    return o.reshape(n_p * s, COL), st


def _ret_sample_kernel(qk_ref, v_ref, gate_ref, s0_ref, decay_ref, qin_ref, kout_ref, gc_ref,
                       o_ref, s_out_ref, *, t_s):
    row = lax.broadcasted_iota(jnp.int32, (CHUNK, LANE), 0)
    for h in range(RET_HEADS):
        lo = (h % 2) * RET_QK_DIM
        qm, kt, kw, vh = _ret_head_inputs(qk_ref, v_ref, kout_ref, h)
        att = _dot_nt(qm, kt) * decay_ref[h]
        o = _dot(att.astype(BF16), vh)
        inter = jnp.zeros((CHUNK, LANE), F32)
        for sq in range(CHUNK // t_s):
            seq_rows = _idiv(row, t_s) == sq
            st = s0_ref[sq, h]
            st2 = jnp.concatenate([st, st], axis=0).astype(BF16)
            inter = jnp.where(seq_rows, _dot(qm, st2), inter)
            upd = _dot(jnp.where(seq_rows, kw, 0.0).T.astype(BF16), vh)
            s_out_ref[sq, h] = st * gc_ref[h:h + 1, :] + upd[lo:lo + RET_QK_DIM, :]
        o = o + inter * qin_ref[:, h * LANE:(h + 1) * LANE]
        _ret_finish(o, gate_ref, o_ref, h)


def _ret_sample(P, state, layer, rt, Tp, t_s):
    n_s = state.shape[1]
    spb = CHUNK // t_s
    base = Tp // CHUNK
    decay, qin, kout, gc = rt
    blk = lambda cb: pl.BlockSpec((CHUNK, COL), lambda i: (base + i, cb))
    const2 = lambda i: (0, 0)
    st_shape = (spb, RET_HEADS, RET_QK_DIM, RET_V_DIM)
    vmem = (8 * _nbytes((CHUNK, COL), BF16) + 2 * _nbytes(decay.shape, F32) + 4 * _nbytes(qin.shape, F32)
            + 4 * _nbytes(st_shape, F32))
    return pl.pallas_call(
        functools.partial(_ret_sample_kernel, t_s=t_s),
        grid=(n_s // spb,),
        in_specs=[blk(CB_RQK), blk(CB_RV), blk(CB_RG),
                  pl.BlockSpec((None,) + st_shape, lambda i: (layer, i, 0, 0, 0)),
                  pl.BlockSpec(decay.shape, lambda i: (0, 0, 0)),
                  pl.BlockSpec(qin.shape, const2), pl.BlockSpec(kout.shape, const2),
                  pl.BlockSpec(gc.shape, const2)],
        out_specs=[pl.BlockSpec((CHUNK, COL), lambda i: (i, 0)),
                   pl.BlockSpec(st_shape, lambda i: (i, 0, 0, 0))],
        out_shape=[jax.ShapeDtypeStruct((n_s * t_s, COL), BF16),
                   jax.ShapeDtypeStruct((n_s, RET_HEADS, RET_QK_DIM, RET_V_DIM), F32)],
        compiler_params=_params(vmem, ("parallel",)),
        name="ret_sample",
    )(P, P, P, state, decay, qin, kout, gc)


def _branch_kernel(xa_ref, xb_ref, a_ref, g0_ref, g1_ref, g2_ref, wb_ref, wo_ref,
                   po0, po1, po2, pl0, pl1, pl2, pc, so0, so1, so2, sl0, sl1, sl2, sc,
                   y_ref, on_ref, ln_ref, *, n_prompt_tiles):
    i = pl.program_id(0)

    def natural(o_ref, l_ref, k, dil):
        if dil == 1:
            return (lambda h: o_ref[:, h * LANE:(h + 1) * LANE].astype(F32)), (lambda h: l_ref[h])
        rpt = TILE // dil
        for h in range(SWA_HEADS):
            for r in range(dil):
                rows = slice(r * rpt, (r + 1) * rpt)
                on_ref[k, h, pl.ds(r, rpt, stride=dil), :] = o_ref[rows, h * LANE:(h + 1) * LANE].astype(F32)
                ln_ref[k, h, pl.ds(r, rpt, stride=dil), :] = l_ref[h, rows, :]
        return (lambda h: on_ref[k, h]), (lambda h: ln_ref[k, h])

    def body(o_refs, l_refs, c_ref, permuted):
        getters = [natural(o_refs[k], l_refs[k], k, SWA_PATTERNS[k][1] if permuted else 1)
                   for k in range(len(SWA_PATTERNS))]
        heads = []
        for h in range(SWA_HEADS):
            ls = [gl(h) for _, gl in getters]
            lmax = functools.reduce(jnp.maximum, ls)
            es = [jnp.exp(l - lmax) for l in ls]
            num = sum(e * go(h) for e, (go, _) in zip(es, getters))
            heads.append((num / sum(es)).astype(BF16))
        mix = jnp.concatenate(heads, axis=1)
        merged = (g0_ref[...].astype(F32) * _dot(a_ref[...], wb_ref[0])
                  + g1_ref[...].astype(F32) * _dot(mix, wb_ref[1])
                  + g2_ref[...].astype(F32) * _dot(c_ref[...], wb_ref[2]))
        y_ref[...] = _tile_rows(i, n_prompt_tiles, xa_ref, xb_ref) + _dot(merged.astype(BF16), wo_ref[...])

    @pl.when(i < n_prompt_tiles)
    def _():
        body((po0, po1, po2), (pl0, pl1, pl2), pc, True)

    @pl.when(i >= n_prompt_tiles)
    def _():
        body((so0, so1, so2), (sl0, sl1, sl2), sc, False)


def _branch(xa, xb, out_a, P, wb, wo, prompt_set, sample_set, n_prompt_tiles):
    T = (n_prompt_tiles + 1) * TILE
    assert sample_set[0].shape[0] == TILE, "the sample rows must form exactly one tile"
    n_g = len(SWA_PATTERNS)
    last = n_prompt_tiles - 1
    row = lambda i: (i, 0)
    gate = lambda k: pl.BlockSpec((TILE, D_MODEL), lambda i: (i, CB_GATE * COL // D_MODEL + k))
    once = pl.Buffered(1)
    p_o = pl.BlockSpec((TILE, COL), lambda i: (jnp.minimum(i, last), 0))
    p_l = pl.BlockSpec((SWA_HEADS, TILE, LANE), lambda i: (0, jnp.minimum(i, last), 0))
    s_o = pl.BlockSpec((TILE, COL), lambda i: (0, 0))
    s_l = pl.BlockSpec((SWA_HEADS, TILE, LANE), lambda i: (0, 0, 0))
    vmem = (4 * _nbytes((TILE, D_MODEL), F32) + 6 * _nbytes((TILE, D_MODEL), BF16)
            + 2 * (2 * n_g + 2) * (_nbytes((TILE, COL), BF16) + _nbytes((TILE, COL), F32))
            + _nbytes(wb.shape, BF16) + _nbytes(wo.shape, BF16)
            + 4 * n_g * _nbytes((TILE, COL), F32) + 4 * _nbytes((TILE, D_MODEL), F32))
    return pl.pallas_call(
        functools.partial(_branch_kernel, n_prompt_tiles=n_prompt_tiles),
        grid=(T // TILE,),
        in_specs=_x_specs(xa, xb) + [
                  pl.BlockSpec((TILE, COL), row),
                  gate(0), gate(1), gate(2),
                  pl.BlockSpec(wb.shape, lambda i: (0, 0, 0), pipeline_mode=once),
                  pl.BlockSpec(wo.shape, lambda i: (0, 0), pipeline_mode=once),
                  p_o, p_o, p_o, p_l, p_l, p_l, p_o,
                  s_o, s_o, s_o, s_l, s_l, s_l, s_o],
        out_specs=pl.BlockSpec((TILE, D_MODEL), row),
        out_shape=jax.ShapeDtypeStruct((T, D_MODEL), F32),
        scratch_shapes=[pltpu.VMEM((n_g, SWA_HEADS, TILE, LANE), F32),
                        pltpu.VMEM((n_g, SWA_HEADS, TILE, LANE), F32)],
        compiler_params=_params(vmem, ("parallel",)),
        name="branch_merge",
    )(xa, xb, out_a, P, P, P, wb, wo, *prompt_set, *sample_set)


def _swiglu_into(hb, w1_ref, w3_ref, w2_ref, acc_ref, fc):
    n_fc = w1_ref.shape[-1] // fc
    for f in range(n_fc):
        cols = slice(f * fc, (f + 1) * fc)
        a = _dot(hb, w1_ref[:, cols])
        b = _dot(hb, w3_ref[:, cols])
        part = _dot((a * _sigmoid(a) * b).astype(BF16), w2_ref[cols, :])
        if f == 0:
            acc_ref[...] = part
        else:
            acc_ref[...] += part


def _dense_ffn_kernel(x_ref, g2_ref, w1_ref, w3_ref, w2_ref, y_ref, acc_ref, *, fc):
    x = x_ref[...]
    hb = (_rms(x) * g2_ref[...]).astype(BF16)
    _swiglu_into(hb, w1_ref, w3_ref, w2_ref, acc_ref, fc)
    y_ref[...] = x + acc_ref[...]


def _dense_ffn(x, g2, w1b, w3b, w2b):
    T = x.shape[0]
    F = w1b.shape[1]
    tm = _pick(T, (768, 512, 384, 256, 128))
    fc = _pick(F, (512, 256, 128))
    row = lambda i: (i, 0)
    once = pl.Buffered(1)
    vmem = (5 * _nbytes((tm, D_MODEL), F32) + 3 * _nbytes(w1b.shape, BF16) + _nbytes((tm, D_MODEL), BF16)
            + 4 * _nbytes((tm, fc), F32))
    return pl.pallas_call(
        functools.partial(_dense_ffn_kernel, fc=fc),
        grid=(T // tm,),
        in_specs=[pl.BlockSpec((tm, D_MODEL), row),
                  pl.BlockSpec((1, D_MODEL), lambda i: (0, 0)),
                  pl.BlockSpec(w1b.shape, lambda i: (0, 0), pipeline_mode=once),
                  pl.BlockSpec(w3b.shape, lambda i: (0, 0), pipeline_mode=once),
                  pl.BlockSpec(w2b.shape, lambda i: (0, 0), pipeline_mode=once)],
        out_specs=pl.BlockSpec((tm, D_MODEL), row),
        out_shape=jax.ShapeDtypeStruct((T, D_MODEL), F32),
        scratch_shapes=[pltpu.VMEM((tm, D_MODEL), F32)],
        compiler_params=_params(vmem, ("parallel",)),
        name="dense_ffn",
    )(x, g2, w1b, w3b, w2b)


def _router_kernel(x_ref, g2_ref, wr_ref, e0_ref, e1_ref, w0_ref, w1_ref):
    hb = (_rms(x_ref[...]) * g2_ref[...]).astype(BF16)
    logits = _dot(hb, wr_ref[...])
    lane = lax.broadcasted_iota(jnp.int32, logits.shape, 1)
    logits = jnp.where(lane < N_EXPERTS, logits, -jnp.inf)
    lane_f = lane.astype(F32)
    m1 = jnp.max(logits, axis=-1, keepdims=True)
    i1 = jnp.min(jnp.where(logits == m1, lane_f, float(LANE)), axis=-1, keepdims=True)
    rest = jnp.where(lane_f == i1, -jnp.inf, logits)
    m2 = jnp.max(rest, axis=-1, keepdims=True)
    i2 = jnp.min(jnp.where(rest == m2, lane_f, float(LANE)), axis=-1, keepdims=True)
    e = jnp.exp(m2 - m1)
    e0_ref[...] = jnp.broadcast_to(i1, logits.shape).astype(jnp.int32)
    e1_ref[...] = jnp.broadcast_to(i2, logits.shape).astype(jnp.int32)
    w0_ref[...] = jnp.broadcast_to(1.0 / (1.0 + e), logits.shape)
    w1_ref[...] = jnp.broadcast_to(e / (1.0 + e), logits.shape)


def _router(x, g2, wr_pad):
    T = x.shape[0]
    tm = _pick(T, (768, 512, 384, 256, 128))
    row = lambda i: (i, 0)
    out = pl.BlockSpec((tm, LANE), row)
    vmem = 4 * _nbytes((tm, D_MODEL), F32) + 2 * _nbytes(wr_pad.shape, BF16) + 16 * _nbytes((tm, LANE), F32)
    return pl.pallas_call(
        _router_kernel,
        grid=(T // tm,),
        in_specs=[pl.BlockSpec((tm, D_MODEL), row),
                  pl.BlockSpec((1, D_MODEL), lambda i: (0, 0)),
                  pl.BlockSpec(wr_pad.shape, lambda i: (0, 0))],
        out_specs=[out, out, out, out],
        out_shape=[jax.ShapeDtypeStruct((T, LANE), jnp.int32), jax.ShapeDtypeStruct((T, LANE), jnp.int32),
                   jax.ShapeDtypeStruct((T, LANE), F32), jax.ShapeDtypeStruct((T, LANE), F32)],
        compiler_params=_params(vmem, ("parallel",)),
        name="moe_router",
    )(x, g2, wr_pad)


RUN_ALIGN = 8


def _run_pieces(count, max_rows):
    pieces, off = [], 0
    size = max_rows
    while size >= RUN_ALIGN:
        active = (count & size) != 0
        pieces.append((size, off, active))
        off = off + jnp.where(active, size, 0)
        size //= 2
    return pieces


def _dispatch_kernel(base_ref, cnt_ref, pend_ref, x_ref, g2_ref, e0_ref, e1_ref, upper_ref, xs_ref,
                     comp_ref, sem, *, tme, n_tiles):
    i = pl.program_id(0)
    n_steps = pl.num_programs(0)
    cur = i % 2
    half = tme // 2

    @pl.when(i == 0)
    def _():
        comp_ref[0, 0] = jnp.zeros(comp_ref.shape[2:], comp_ref.dtype)
        assert comp_ref.shape[2] == half

        def zero_tile(start):
            return [pltpu.make_async_copy(
                        comp_ref.at[0, 0], xs_ref.at[pl.ds(pl.multiple_of(start + k * half, half), half), :],
                        sem.at[0])
                    for k in range(2)]

        def tail_loop(fn):
            def body(t, carry):
                for c in zero_tile(t * tme):
                    fn(c)
                return carry
            lax.fori_loop(pend_ref[N_EXPERTS - 1] // tme, n_tiles, body, 0)

        def group_tails(fn):
            for e in range(N_EXPERTS):
                prev_end = pend_ref[e - 1] if e else 0

                @pl.when(pend_ref[e] > prev_end)
                def _():
                    for c in zero_tile(pend_ref[e] - tme):
                        fn(c)

        group_tails(lambda c: c.start())
        tail_loop(lambda c: c.start())
        group_tails(lambda c: c.wait())
        tail_loop(lambda c: c.wait())

    h = (_rms(x_ref[...]) * g2_ref[...]).astype(BF16)
    lane = lax.broadcasted_iota(jnp.int32, e0_ref.shape, 1)
    picks = jnp.where((lane == e0_ref[...]) | (lane == e1_ref[...]), 1.0, 0.0)
    picks_t = picks.T
    rank_t = _dot(picks_t.astype(BF16), upper_ref[...])
    slot = lax.broadcasted_iota(jnp.int32, (TILE, TILE), 0).astype(F32)
    for e in range(N_EXPERTS):
        sel = jnp.where((rank_t[e:e + 1, :] == slot) & (picks_t[e:e + 1, :] > 0.0), 1.0, 0.0)
        comp_ref[cur, e] = _dot(sel.astype(BF16), h)

    def for_copies(tile, buf, fn):
        for e in range(N_EXPERTS):
            cnt, dst = cnt_ref[tile * N_EXPERTS + e], base_ref[tile * N_EXPERTS + e]
            for size, off, active in _run_pieces(cnt, TILE):
                c = pltpu.make_async_copy(
                    comp_ref.at[buf, e, pl.ds(pl.multiple_of(off, RUN_ALIGN), size), :],
                    xs_ref.at[pl.ds(pl.multiple_of(dst + off, RUN_ALIGN), size), :], sem.at[buf])
                pl.when(active)(functools.partial(fn, c))

    for_copies(i, cur, lambda c: c.start())

    @pl.when(i > 0)
    def _():
        for_copies(i - 1, 1 - cur, lambda c: c.wait())

    @pl.when(i == n_steps - 1)
    def _():
        for_copies(i, cur, lambda c: c.wait())


def _dispatch(x, g2, e0, e1, base, cnt, pend, upper, n_tiles, tme):
    T = x.shape[0]
    assert tme == 2 * TILE
    row = lambda i, *_: (i, 0)
    vmem = (3 * _nbytes((TILE, D_MODEL), F32) + 2 * N_EXPERTS * _nbytes((TILE, D_MODEL), F32)
            + 8 * _nbytes((TILE, TILE), F32))
    grid_spec = pltpu.PrefetchScalarGridSpec(
        num_scalar_prefetch=3,
        grid=(T // TILE,),
        in_specs=[pl.BlockSpec((TILE, D_MODEL), row),
                  pl.BlockSpec((1, D_MODEL), lambda i, *_: (0, 0)),
                  pl.BlockSpec((TILE, LANE), row), pl.BlockSpec((TILE, LANE), row),
                  pl.BlockSpec((TILE, TILE), lambda i, *_: (0, 0))],
        out_specs=pl.BlockSpec(memory_space=pl.ANY),
        scratch_shapes=[pltpu.VMEM((2, N_EXPERTS, TILE, D_MODEL), F32), pltpu.SemaphoreType.DMA((2,))],
    )
    return pl.pallas_call(
        functools.partial(_dispatch_kernel, tme=tme, n_tiles=n_tiles),
        grid_spec=grid_spec,
        out_shape=jax.ShapeDtypeStruct((n_tiles * tme, D_MODEL), F32),
        compiler_params=_params(vmem, ("arbitrary",)),
        name="moe_dispatch",
    )(base, cnt, pend, x, g2, e0, e1, upper)


def _expert_ffn_kernel(te_ref, used_ref, xs_ref, w1_ref, w3_ref, w2_ref, y_ref, acc_ref, *, fc):
    i = pl.program_id(0)

    @pl.when(i < used_ref[0])
    def _():
        _swiglu_into(xs_ref[...].astype(BF16), w1_ref, w3_ref, w2_ref, acc_ref, fc)
        y_ref[...] = acc_ref[...]

    @pl.when(i >= used_ref[0])
    def _():
        y_ref[...] = jnp.zeros_like(y_ref)


def _expert_ffn(xs, tile_expert, n_used, we1b, we3b, we2b, tme):
    n_rows = xs.shape[0]
    F = we1b.shape[2]
    fc = _pick(F, (512, 256, 128))
    once = pl.Buffered(1)
    wspec = lambda shape: pl.BlockSpec((None,) + shape, lambda i, te, nu: (te[i], 0, 0), pipeline_mode=once)
    vmem = (5 * _nbytes((tme, D_MODEL), F32) + 3 * _nbytes(we1b.shape[1:], BF16) + _nbytes((tme, D_MODEL), BF16)
            + 4 * _nbytes((tme, fc), F32))
    grid_spec = pltpu.PrefetchScalarGridSpec(
        num_scalar_prefetch=2,
        grid=(n_rows // tme,),
        in_specs=[pl.BlockSpec((tme, D_MODEL), lambda i, te, nu: (i, 0)),
                  wspec(we1b.shape[1:]), wspec(we3b.shape[1:]), wspec(we2b.shape[1:])],
        out_specs=pl.BlockSpec((tme, D_MODEL), lambda i, te, nu: (i, 0)),
        scratch_shapes=[pltpu.VMEM((tme, D_MODEL), F32)],
    )
    return pl.pallas_call(
        functools.partial(_expert_ffn_kernel, fc=fc),
        grid_spec=grid_spec,
        out_shape=jax.ShapeDtypeStruct((n_rows, D_MODEL), F32),
        compiler_params=_params(vmem, ("arbitrary",)),
        name="moe_expert_ffn",
    )(tile_expert, n_used, xs, we1b, we3b, we2b)


COMBINE_PIECE = 64


def _combine_kernel(base_ref, cnt_ref, x_ref, e0_ref, e1_ref, w0_ref, w1_ref, lower_ref, y_hbm_ref,
                    o_ref, o_last_ref, buf_ref, sem):
    i = pl.program_id(0)
    n_steps = pl.num_programs(0)
    cur = i % 2

    def for_copies(tile, buf, fn):
        for e in range(N_EXPERTS):
            cnt, src = cnt_ref[tile * N_EXPERTS + e], base_ref[tile * N_EXPERTS + e]
            for q in range(TILE // COMBINE_PIECE):
                c = pltpu.make_async_copy(
                    y_hbm_ref.at[pl.ds(pl.multiple_of(src + q * COMBINE_PIECE, RUN_ALIGN), COMBINE_PIECE), :],
                    buf_ref.at[buf, e, pl.ds(q * COMBINE_PIECE, COMBINE_PIECE), :], sem.at[buf])
                pl.when(cnt > q * COMBINE_PIECE)(functools.partial(fn, c))

    @pl.when(i == 0)
    def _():
        buf_ref[...] = jnp.zeros_like(buf_ref)
        for_copies(0, 0, lambda c: c.start())

    @pl.when(i + 1 < n_steps)
    def _():
        for_copies(i + 1, 1 - cur, lambda c: c.start())

    for_copies(i, cur, lambda c: c.wait())

    e0, e1 = e0_ref[...], e1_ref[...]
    slot = lax.broadcasted_iota(jnp.int32, (TILE, TILE), 1).astype(F32)
    wide = lambda a, n: jnp.concatenate([a] * n, axis=1)
    acc = x_ref[...]
    for e in range(N_EXPERTS):
        is0, is1 = e0 == e, e1 == e
        picks = wide(jnp.where(is0 | is1, 1.0, 0.0), TILE // LANE)
        rank = _dot(lower_ref[...], picks.astype(BF16))
        picked = jnp.where((rank == slot) & (picks > 0.0), 1.0, 0.0).astype(BF16)
        rows = _dot(picked, buf_ref[cur, e].astype(BF16))
        weight = jnp.where(is0, w0_ref[...], jnp.where(is1, w1_ref[...], 0.0))
        acc = acc + wide(weight, D_MODEL // LANE) * rows

    @pl.when(i < n_steps - 1)
    def _():
        o_ref[...] = acc

    @pl.when(i == n_steps - 1)
    def _():
        o_last_ref[...] = acc


def _combine(x, e0, e1, w0, w1, base, cnt, lower, y_rows):
    T = x.shape[0]
    row = lambda i, *_: (i, 0)
    lane_spec = pl.BlockSpec((TILE, LANE), row)
    vmem = (5 * _nbytes((TILE, D_MODEL), F32) + 2 * N_EXPERTS * _nbytes((TILE, D_MODEL), F32)
            + 8 * _nbytes((TILE, LANE), F32) + 8 * _nbytes((TILE, TILE), F32)
            + 4 * _nbytes((TILE, D_MODEL), F32))
    grid_spec = pltpu.PrefetchScalarGridSpec(
        num_scalar_prefetch=2,
        grid=(T // TILE,),
        in_specs=[pl.BlockSpec((TILE, D_MODEL), row), lane_spec, lane_spec, lane_spec, lane_spec,
                  pl.BlockSpec((TILE, TILE), lambda i, *_: (0, 0)),
                  pl.BlockSpec(memory_space=pl.ANY)],
        out_specs=[pl.BlockSpec((TILE, D_MODEL), lambda i, *_: (jnp.minimum(i, T // TILE - 2), 0)),
                   pl.BlockSpec((TILE, D_MODEL), lambda i, *_: (0, 0))],
        scratch_shapes=[pltpu.VMEM((2, N_EXPERTS, TILE, D_MODEL), F32), pltpu.SemaphoreType.DMA((2,))],
    )
    return pl.pallas_call(
        _combine_kernel,
        grid_spec=grid_spec,
        out_shape=[jax.ShapeDtypeStruct((T - TILE, D_MODEL), F32), jax.ShapeDtypeStruct((TILE, D_MODEL), F32)],
        compiler_params=_params(vmem, ("arbitrary",)),
        name="moe_combine",
    )(base, cnt, x, e0, e1, w0, w1, lower, y_rows)


def _moe(x, g2, wr_pad, we1b, we3b, we2b):
    T = x.shape[0]
    tme = 2 * TILE
    n_tt = T // TILE
    e0, e1, w0, w1 = _router(x, g2, wr_pad)
    experts = jnp.arange(N_EXPERTS, dtype=jnp.int32)
    picks = (e0[:, :1] == experts[None, :]) | (e1[:, :1] == experts[None, :])
    cnt = jnp.sum(picks.reshape(n_tt, TILE, N_EXPERTS).astype(jnp.int32), axis=1)
    cnt = (cnt + RUN_ALIGN - 1) // RUN_ALIGN * RUN_ALIGN
    counts = jnp.sum(cnt, axis=0)
    padded = (counts + tme - 1) // tme * tme
    pend = jnp.cumsum(padded).astype(jnp.int32)
    base = ((pend - padded)[None, :] + jnp.cumsum(cnt, axis=0) - cnt).astype(jnp.int32)
    max_rows = T * TOP_K + n_tt * N_EXPERTS * (RUN_ALIGN - 1)
    n_tiles = -(-max_rows // tme) + N_EXPERTS + 1
    tile_start = jnp.arange(n_tiles, dtype=jnp.int32) * tme
    tile_expert = jnp.minimum(jnp.sum(pend[None, :] <= tile_start[:, None], axis=1), N_EXPERTS - 1).astype(jnp.int32)
    n_used = (pend[N_EXPERTS - 1:] // tme).astype(jnp.int32)
    tok = jnp.arange(TILE)
    upper = (tok[:, None] < tok[None, :]).astype(BF16)
    base_f, cnt_f = base.reshape(-1), cnt.reshape(-1).astype(jnp.int32)
    xs = _dispatch(x, g2, e0, e1, base_f, cnt_f, pend, upper, n_tiles, tme)
    y_rows = _expert_ffn(xs, tile_expert, n_used, we1b, we3b, we2b, tme)
    return _combine(x, e0, e1, w0, w1, base_f, cnt_f, upper.T, y_rows)


def _class_major(a, dil):
    n = a.shape[0]
    return a.reshape((n // TILE, TILE // dil, dil) + a.shape[1:]).swapaxes(1, 2).reshape(a.shape)


def _natural(a, dil):
    n = a.shape[0]
    return a.reshape((n // TILE, dil, TILE // dil) + a.shape[1:]).swapaxes(1, 2).reshape(a.shape)


def _perm_matrices():
    eye = jnp.eye(TILE, dtype=BF16)
    return jnp.stack([_class_major(eye, dil) for _, dil in SWA_PATTERNS[1:]])


def _rope_tables(pos, Tp):
    freqs = ROPE_THETA ** (-jnp.arange(0, SWA_HEAD_DIM, 2, dtype=F32) / SWA_HEAD_DIM)
    rfreqs = 1.0 / (ROPE_THETA ** jnp.linspace(0.0, 1.0, RET_QK_DIM // 2, dtype=F32))

    def swa(p):
        ang = p.astype(F32)[:, None] * freqs[None, :]
        c, s = jnp.cos(ang), jnp.sin(ang)
        return jnp.concatenate([c, c], axis=1), jnp.concatenate([-s, s], axis=1)

    per_group = [swa(jnp.concatenate([_class_major(pos[:Tp], dil), pos[Tp:]])) for _, dil in SWA_PATTERNS]
    cosb = jnp.stack([c for c, _ in per_group])
    sinb = jnp.stack([s for _, s in per_group])
    rang = pos.astype(F32)[:, None] * rfreqs[None, :]
    c, s = jnp.cos(rang), jnp.sin(rang)
    cosc = jnp.concatenate([c, c, c, c], axis=1)
    sinc = jnp.concatenate([-s, s, -s, s], axis=1)
    return cosb, sinb, cosc, sinc


def _ret_tables(c_len):
    log_g = jnp.log1p(-jnp.exp2(-5.0 - jnp.arange(RET_HEADS, dtype=F32)))
    r = jnp.arange(CHUNK)
    i = (r % c_len).astype(F32)
    same = (r[:, None] // c_len) == (r[None, :] // c_len)
    dist = i[:, None] - i[None, :]
    decay = jnp.where(same[None] & (dist >= 0)[None],
                      jnp.exp(log_g[:, None, None] * jnp.maximum(dist, 0.0)[None]), 0.0)
    qin = jnp.repeat(jnp.exp(log_g[None, :] * (i[:, None] + 1.0)), RET_V_DIM, axis=1)
    kout = jnp.repeat(jnp.exp(log_g[None, :] * (c_len - 1.0 - i)[:, None]), RET_QK_DIM, axis=1)
    gc = jnp.broadcast_to(jnp.exp(log_g * c_len)[:, None], (RET_HEADS, LANE))
    gc = jnp.concatenate([gc, jnp.zeros((8 - RET_HEADS, LANE), F32)], axis=0)
    return decay.astype(F32), qin.astype(F32), kout.astype(F32), gc.astype(F32)


def _mixa_tables(w_s, b_s, t_s):
    w_p = jnp.tril(w_s)
    w8 = jnp.tril(w_s[:, :t_s, :t_s])
    eye = jnp.eye(CHUNK // t_s, dtype=w_s.dtype)
    w_smp = jax.vmap(lambda m: jnp.kron(eye, m))(w8)
    w2 = jnp.stack([w_p, w_smp]).astype(BF16)
    b_p = jnp.repeat(b_s.T, LANE, axis=1)
    b_smp = jnp.repeat(jnp.tile(b_s[:, :t_s].T, (CHUNK // t_s, 1)), LANE, axis=1)
    return w2, jnp.stack([b_p, b_smp]).astype(F32)


def kernel(x_prompt, x_sample, cache_swa_kv0, cache_swa_kv1, cache_swa_kv2, state_ret, norm1_g, w_in, norm_v_g, w_s, b_s, q_norm_g, k_norm_g, w_branch, w_out, norm2_g, w1, w3, w2, w_router, we1, we3, we2):
    n_p, s, d = x_prompt.shape
    n_s, t_s, _ = x_sample.shape
    depth = w_in.shape[0]
    Tp, Ts = n_p * s, n_s * t_s
    T = Tp + Ts
    max_win, max_dil = SWA_PATTERNS[-1]
    assert d == D_MODEL and Ts == TILE and CHUNK % t_s == 0
    assert s % (CHUNK * max_dil) == 0 and s >= max_win
    n_pt = Tp // TILE

    xa, xb = x_prompt.reshape(Tp, d), x_sample.reshape(Ts, d)
    pos = jnp.concatenate([jnp.arange(s, dtype=jnp.int32),
                           jnp.tile(PAST_LEN + jnp.arange(t_s, dtype=jnp.int32), n_s)])
    tabs = _rope_tables(pos, s)
    perm = _perm_matrices()
    rt_prompt = _ret_tables(CHUNK)
    rt_sample = _ret_tables(t_s)
    caches = tuple(c.reshape(c.shape[0], c.shape[1], -1, SWA_HEAD_DIM)
                   for c in (cache_swa_kv0, cache_swa_kv1, cache_swa_kv2))

    p_kv = [[] for _ in SWA_PATTERNS]
    s_kv = [[] for _ in SWA_PATTERNS]
    p_ret, s_ret, s_v = [], [], []
    for layer in range(depth):
        P = _inproj(xa, xb, norm1_g[layer][None, :], w_in[layer].astype(BF16), perm, tabs,
                    norm_v_g[layer][None, :], q_norm_g[layer][None, :], k_norm_g[layer][None, :],
                    n_pt, s // TILE)

        w2a, b2a = _mixa_tables(w_s[layer], b_s[layer], t_s)
        out_a = _mixer_a(P, w2a, b2a, Tp // CHUNK)

        Ps = P[Tp:].astype(F32).reshape(n_s, t_s, IN_WIDTH)
        po, plse, so, slse = [], [], [], []
        for g, (win, dil) in enumerate(SWA_PATTERNS):
            o_g, l_g = _swa_prompt(P, g, dil, n_p, s)
            os_g, ls_g = _swa_sample(Ps, caches[g], layer, g, dil)
            po.append(o_g)
            plse.append(l_g)
            so.append(os_g.reshape(Ts, COL).astype(BF16))
            slse.append(ls_g.reshape(SWA_HEADS, Ts, LANE))

        out_c, ret_p = _ret_prompt(P, rt_prompt, n_p, s)
        out_cs, ret_s = _ret_sample(P, state_ret, layer, rt_sample, Tp, t_s)

        x = _branch(xa, xb, out_a, P, w_branch[layer].astype(BF16), w_out[layer].astype(BF16),
                    (*po, *plse, out_c), (*so, *slse, out_cs), n_pt)

        g2 = norm2_g[layer][None, :]
        i = layer // 2
        if layer % 2 == 0:
            x = _dense_ffn(x, g2, w1[i].astype(BF16), w3[i].astype(BF16), w2[i].astype(BF16))
            xa = xb = x
        else:
            wr_pad = jnp.zeros((D_MODEL, LANE), BF16).at[:, :N_EXPERTS].set(w_router[i].astype(BF16))
            xa, xb = _moe(x, g2, wr_pad, we1[i].astype(BF16), we3[i].astype(BF16), we2[i].astype(BF16))

        for g, (win, dil) in enumerate(SWA_PATTERNS):
            kcols = slice((CB_K + g) * COL, (CB_K + g + 1) * COL)
            vcols = slice((CB_V + g) * COL, (CB_V + g + 1) * COL)
            keep = -(-min(win, s) // TILE) * TILE

            def rows(cols):
                blk = jnp.concatenate([P[(b + 1) * s - keep:(b + 1) * s, cols] for b in range(n_p)])
                blk = _natural(blk, dil).reshape(n_p, keep, SWA_HEADS, SWA_HEAD_DIM)
                return blk[:, keep - min(win, s):]

            p_kv[g].append(jnp.stack([rows(kcols), rows(vcols)], axis=2).astype(F32))
            ks = P[Tp:, kcols].reshape(n_s, t_s, SWA_HEADS, SWA_HEAD_DIM)
            vs = P[Tp:, vcols].reshape(n_s, t_s, SWA_HEADS, SWA_HEAD_DIM)
            s_kv[g].append(jnp.stack([ks, vs], axis=2).astype(F32))
        p_ret.append(ret_p)
        s_ret.append(ret_s)
        s_v.append(P[Tp:, CB_AV * COL:(CB_AV + 1) * COL].astype(F32).reshape(n_s, t_s, COL))

    y_prompt = xa[:Tp].reshape(n_p, s, d)
    y_sample = xb[xb.shape[0] - Ts:].reshape(n_s, t_s, d)
    return (y_prompt, y_sample,
            jnp.stack(p_kv[0]), jnp.stack(p_kv[1]), jnp.stack(p_kv[2]), jnp.stack(p_ret),
            jnp.stack(s_kv[0]), jnp.stack(s_kv[1]), jnp.stack(s_kv[2]), jnp.stack(s_ret),
            jnp.stack(s_v))
```

```python
import functools
import math

import jax
import jax.numpy as jnp
from jax import lax
from jax.experimental import pallas as pl
from jax.experimental.pallas import tpu as pltpu

F32 = jnp.float32
BF16 = jnp.bfloat16

PAST_LEN = 16384
EPS = 1e-6
NEG_INF = -1e30
ROPE_THETA = 10000.0

D_MODEL = 1024
LANE = 128
CHUNK = 128
TILE = 256
COL = 512
A_GROUPS = 4
SWA_PATTERNS = ((128, 1), (512, 4), (2048, 16))
SWA_HEADS = 4
SWA_HEAD_DIM = 128
RET_HEADS = 4
RET_QK_DIM = 64
RET_V_DIM = 128
N_EXPERTS = 8
TOP_K = 2
IN_WIDTH = 10240
N_COL = IN_WIDTH // COL

CB_AU, CB_AV, CB_Q, CB_K, CB_V, CB_RQK, CB_RV, CB_RG, CB_GATE = 0, 1, 2, 5, 8, 11, 12, 13, 14

VMEM_INTERNAL_SCRATCH = 8 * 1024 * 1024


def _pick(n, candidates):
    for c in candidates:
        if n % c == 0:
            return c
    raise ValueError(f"no tile in {candidates} divides {n}")


def _params(block_bytes, semantics=None):
    limit = int(block_bytes) + VMEM_INTERNAL_SCRATCH
    return pltpu.CompilerParams(dimension_semantics=semantics, vmem_limit_bytes=limit)


def _nbytes(shape, dtype):
    return math.prod(shape) * jnp.dtype(dtype).itemsize


def _rms(x):
    return x * lax.rsqrt(jnp.mean(x * x, axis=-1, keepdims=True) + EPS)


def _gelu(x):
    return 0.5 * x * (1.0 + lax.erf(x * (0.5 ** 0.5)))


def _sigmoid(x):
    return 1.0 / (1.0 + jnp.exp(-x))


def _idiv(x, n):
    assert n & (n - 1) == 0
    return x >> (n.bit_length() - 1)


def _imod(x, n):
    assert n & (n - 1) == 0
    return x & (n - 1)


def _dot(a, b):
    return jnp.dot(a, b, preferred_element_type=F32)


LSE_LANES = LANE // 4


def _pack_heads(cols):
    rows = cols[0].shape[0]
    grp = _idiv(lax.broadcasted_iota(jnp.int32, (rows, LANE), 1), LSE_LANES)
    out = jnp.broadcast_to(cols[-1], (rows, LANE))
    for h in range(len(cols) - 2, -1, -1):
        out = jnp.where(grp == h, cols[h], out)
    return out


def _dot_nt(a, b):
    return lax.dot_general(a, b, (((1,), (1,)), ((), ())), preferred_element_type=F32)


def _tile_rows(i, n_prompt_tiles, xa_ref, xb_ref):
    return jnp.where(i < n_prompt_tiles, xa_ref[...], xb_ref[...])


def _x_specs(xa, xb):
    last_a, last_b = xa.shape[0] // TILE - 1, xb.shape[0] // TILE - 1
    return [pl.BlockSpec((TILE, D_MODEL), lambda i, *_: (jnp.minimum(i, last_a), 0)),
            pl.BlockSpec((TILE, D_MODEL), lambda i, *_: (last_b, 0))]


def _inproj_kernel(xa_ref, xb_ref, g1_ref, w_ref, perm_ref, cosb_ref, sinb_ref, cosc_ref, sinc_ref,
                   nvg_ref, qg_ref, kg_ref, o_ref, h_ref, *, n_prompt_tiles):
    i = pl.program_id(0)
    hn = (_rms(_tile_rows(i, n_prompt_tiles, xa_ref, xb_ref)) * g1_ref[...]).astype(BF16)
    h_ref[0] = hn

    @pl.when(i < n_prompt_tiles)
    def _():
        for k in range(1, len(SWA_PATTERNS)):
            h_ref[k] = _dot(perm_ref[k - 1], hn).astype(BF16)

    @pl.when(i >= n_prompt_tiles)
    def _():
        for k in range(1, len(SWA_PATTERNS)):
            h_ref[k] = hn

    def qk_heads(acc, g, cols0, gain_ref, scale):
        for hh in range(SWA_HEADS):
            cs = slice(hh * LANE, (hh + 1) * LANE)
            y = _rms(acc[:, cs]) * gain_ref[...]
            rot = y * cosb_ref[g] + pltpu.roll(y, LANE // 2, axis=1) * sinb_ref[g]
            o_ref[:, cols0 + hh * LANE:cols0 + (hh + 1) * LANE] = (rot * scale).astype(o_ref.dtype)

    for j in range(N_COL):
        cols = slice(j * COL, (j + 1) * COL)
        g = (j - CB_Q) % len(SWA_PATTERNS) if CB_Q <= j < CB_RQK else 0
        acc = _dot(h_ref[g], w_ref[:, cols])
        if j == CB_AU:
            o_ref[:, cols] = _gelu(acc).astype(o_ref.dtype)
        elif j == CB_AV:
            o_ref[:, cols] = (_rms(_gelu(acc)) * nvg_ref[...]).astype(o_ref.dtype)
        elif CB_Q <= j < CB_K:
            qk_heads(acc, g, j * COL, qg_ref, SWA_HEAD_DIM ** -0.5)
        elif CB_K <= j < CB_V:
            qk_heads(acc, g, j * COL, kg_ref, 1.0)
        elif j < CB_RQK or j == CB_RV:
            o_ref[:, cols] = acc.astype(o_ref.dtype)
        elif j == CB_RQK:
            lane = lax.broadcasted_iota(jnp.int32, (acc.shape[0], LANE), 1)
            first_half = _imod(lane, RET_QK_DIM) < (RET_QK_DIM // 2)
            for tt in range(COL // LANE):
                y = acc[:, tt * LANE:(tt + 1) * LANE]
                partner = jnp.where(first_half,
                                    pltpu.roll(y, LANE - RET_QK_DIM // 2, axis=1),
                                    pltpu.roll(y, RET_QK_DIM // 2, axis=1))
                rot = y * cosc_ref[...] + partner * sinc_ref[...]
                scale = 1.0 if tt < (COL // LANE) // 2 else RET_QK_DIM ** -0.5
                o_ref[:, j * COL + tt * LANE:j * COL + (tt + 1) * LANE] = (rot * scale).astype(o_ref.dtype)
        elif j == CB_RG:
            o_ref[:, cols] = (acc * _sigmoid(acc)).astype(o_ref.dtype)
        else:
            o_ref[:, cols] = _sigmoid(acc).astype(o_ref.dtype)


def _inproj(xa, xb, g1, w_in_b, perm, tabs, nvg, qg, kg, n_prompt_tiles, tiles_per_seq):
    T = (n_prompt_tiles + 1) * TILE
    cosb, sinb, cosc, sinc = tabs
    n_g = len(SWA_PATTERNS)
    row = lambda i: (i, 0)
    const = lambda i: (0, 0)
    tab_blk = lambda i: jnp.where(i < n_prompt_tiles, i % tiles_per_seq, tiles_per_seq)
    tab3 = pl.BlockSpec((n_g, TILE, LANE), lambda i: (0, tab_blk(i), 0))
    tab = pl.BlockSpec((TILE, LANE), lambda i: (tab_blk(i), 0))
    vmem = (2 * _nbytes((TILE, D_MODEL), F32) + n_g * _nbytes((TILE, D_MODEL), BF16)
            + _nbytes(w_in_b.shape, BF16) + 2 * _nbytes((TILE, IN_WIDTH), BF16)
            + 2 * (2 * n_g + 2) * _nbytes((TILE, LANE), F32) + 2 * _nbytes(perm.shape, BF16)
            + 8 * _nbytes((TILE, COL), F32))
    return pl.pallas_call(
        functools.partial(_inproj_kernel, n_prompt_tiles=n_prompt_tiles),
        grid=(T // TILE,),
        in_specs=_x_specs(xa, xb) + [
            pl.BlockSpec((1, D_MODEL), const),
            pl.BlockSpec(w_in_b.shape, const, pipeline_mode=pl.Buffered(1)),
            pl.BlockSpec(perm.shape, lambda i: (0, 0, 0)),
            tab3, tab3, tab, tab,
            pl.BlockSpec((1, COL), const),
            pl.BlockSpec((1, LANE), const),
            pl.BlockSpec((1, LANE), const),
        ],
        out_specs=pl.BlockSpec((TILE, IN_WIDTH), row),
        out_shape=jax.ShapeDtypeStruct((T, IN_WIDTH), BF16),
        scratch_shapes=[pltpu.VMEM((n_g, TILE, D_MODEL), BF16)],
        compiler_params=_params(vmem, ("parallel",)),
        name="inproj",
    )(xa, xb, g1, w_in_b, perm, cosb, sinb, cosc, sinc, nvg, qg, kg)


def _mixa_kernel(u_ref, v_ref, w_ref, b_ref, o_ref, *, cps, n_prompt_chunks):
    i = pl.program_id(0)
    for c in range(cps):
        var = ((i * cps + c) >= n_prompt_chunks).astype(jnp.int32)
        rows = slice(c * CHUNK, (c + 1) * CHUNK)
        for g in range(A_GROUPS):
            cols = slice(g * LANE, (g + 1) * LANE)
            z = _dot(w_ref[var, g], v_ref[rows, cols]) + b_ref[var, :, cols]
            o_ref[rows, cols] = (u_ref[rows, cols].astype(F32) * z).astype(o_ref.dtype)


def _mixer_a(P, w2, b2, n_prompt_chunks):
    T = P.shape[0]
    n_chunks = T // CHUNK
    cps = _pick(n_chunks, (8, 6, 4, 3, 2, 1))
    rows = cps * CHUNK
    vmem = 6 * _nbytes((rows, COL), BF16) + 2 * _nbytes(w2.shape, BF16) + 2 * _nbytes(b2.shape, F32)
    return pl.pallas_call(
        functools.partial(_mixa_kernel, cps=cps, n_prompt_chunks=n_prompt_chunks),
        grid=(n_chunks // cps,),
        in_specs=[
            pl.BlockSpec((rows, COL), lambda i: (i, CB_AU)),
            pl.BlockSpec((rows, COL), lambda i: (i, CB_AV)),
            pl.BlockSpec(w2.shape, lambda i: (0, 0, 0, 0)),
            pl.BlockSpec(b2.shape, lambda i: (0, 0, 0)),
        ],
        out_specs=pl.BlockSpec((rows, COL), lambda i: (i, 0)),
        out_shape=jax.ShapeDtypeStruct((T, COL), BF16),
        compiler_params=_params(vmem, ("parallel",)),
        name="mixer_a",
    )(P, P, w2, b2)


def _swa_kernel(q_ref, kp_ref, kc_ref, vp_ref, vc_ref, o_ref, l_ref, k_ref, v_ref, s_ref, p_ref, *, qb):
    i = pl.program_id(2)
    rows = qb * CHUNK
    lead = q_ref.shape[:-1]
    k_ref[0:CHUNK] = kp_ref[...].reshape(CHUNK, COL)
    k_ref[CHUNK:CHUNK + rows] = kc_ref[...].reshape(rows, COL)
    v_ref[0:CHUNK] = vp_ref[...].reshape(CHUNK, COL)
    v_ref[CHUNK:CHUNK + rows] = vc_ref[...].reshape(rows, COL)
    row = lax.broadcasted_iota(jnp.int32, (CHUNK, 2 * CHUNK), 0)
    col = lax.broadcasted_iota(jnp.int32, (CHUNK, 2 * CHUNK), 1)
    mask_cur = (col >= CHUNK) & (col - CHUNK <= row)
    mask_all = mask_cur | ((col < CHUNK) & (col >= row))
    mask_first = mask_cur | ((col < CHUNK) & (col >= row) & (i > 0))
    q_all = q_ref[...].reshape(rows, COL)
    for j in range(qb):
        mask = mask_first if j == 0 else mask_all
        for h in range(SWA_HEADS):
            cs = slice(h * LANE, (h + 1) * LANE)
            sc = _dot_nt(q_all[j * CHUNK:(j + 1) * CHUNK, cs], k_ref[j * CHUNK:(j + 2) * CHUNK, cs])
            s_ref[j * SWA_HEADS + h] = jnp.where(mask, sc, NEG_INF)
    s = s_ref[...]
    m = jnp.max(s, axis=-1, keepdims=True)
    p = jnp.exp(s - m)
    den = jnp.sum(p, axis=-1, keepdims=True)
    p_ref[...] = p.astype(BF16)
    lse = m + jnp.log(den)
    for h in range(SWA_HEADS):
        cs = slice(h * LANE, (h + 1) * LANE)
        o_h = [_dot(p_ref[j * SWA_HEADS + h], v_ref[j * CHUNK:(j + 2) * CHUNK, cs]) / den[j * SWA_HEADS + h]
               for j in range(qb)]
        o_ref[..., cs] = jnp.concatenate(o_h, axis=0).astype(o_ref.dtype).reshape(lead + (LANE,))
    packed = [_pack_heads([lse[j * SWA_HEADS + h] for h in range(SWA_HEADS)]) for j in range(qb)]
    l_ref[...] = jnp.concatenate(packed, axis=0).reshape(lead + (LANE,))


def _swa_prompt(P, g, dil, n_p, s):
    T = P.shape[0]
    Tp = n_p * s
    nb = s // dil // CHUNK
    qb = _pick(nb, (8, 4, 2, 1))
    steps = nb // qb
    prev_blk = lambda b, i: b * nb + jnp.maximum(i * qb - 1, 0)
    if dil == 1:
        src = P
        lead_q, lead_p = (qb * CHUNK,), (CHUNK,)
        o_shape = (Tp, COL)
        l_shape = (Tp, LANE)
        q_map = lambda cb: (lambda b, r, i: (b * steps + i, cb + g))
        p_map = lambda cb: (lambda b, r, i: (prev_blk(b, i), cb + g))
        o_map = lambda b, r, i: (b * steps + i, 0)
        l_map = lambda b, r, i: (b * steps + i, 0)
    else:
        rpt = TILE // dil
        tpb = CHUNK // rpt
        src = P.reshape(T // TILE, dil, rpt, IN_WIDTH)
        lead_q, lead_p = (qb * tpb, None, rpt), (tpb, None, rpt)
        o_shape = (Tp // TILE, dil, rpt, COL)
        l_shape = (Tp // TILE, dil, rpt, LANE)
        q_map = lambda cb: (lambda b, r, i: (b * steps + i, r, 0, cb + g))
        p_map = lambda cb: (lambda b, r, i: (prev_blk(b, i), r, 0, cb + g))
        o_map = lambda b, r, i: (b * steps + i, r, 0, 0)
        l_map = lambda b, r, i: (b * steps + i, r, 0, 0)

    q_spec = lambda cb: pl.BlockSpec(lead_q + (COL,), q_map(cb))
    p_spec = lambda cb: pl.BlockSpec(lead_p + (COL,), p_map(cb))
    n_pairs = qb * SWA_HEADS
    vmem = ((10 * qb + 6) * _nbytes((CHUNK, COL), BF16) + 2 * qb * _nbytes((CHUNK, COL), F32)
            + 6 * n_pairs * _nbytes((CHUNK, 2 * CHUNK), F32))
    o, l = pl.pallas_call(
        functools.partial(_swa_kernel, qb=qb),
        grid=(n_p, dil, steps),
        in_specs=[q_spec(CB_Q), p_spec(CB_K), q_spec(CB_K), p_spec(CB_V), q_spec(CB_V)],
        out_specs=[pl.BlockSpec(lead_q + (COL,), o_map),
                   pl.BlockSpec(lead_q + (LANE,), l_map)],
        out_shape=[jax.ShapeDtypeStruct(o_shape, BF16), jax.ShapeDtypeStruct(l_shape, F32)],
        scratch_shapes=[pltpu.VMEM(((qb + 1) * CHUNK, COL), BF16), pltpu.VMEM(((qb + 1) * CHUNK, COL), BF16),
                        pltpu.VMEM((n_pairs, CHUNK, 2 * CHUNK), F32),
                        pltpu.VMEM((n_pairs, CHUNK, 2 * CHUNK), BF16)],
        compiler_params=_params(vmem, ("parallel", "parallel", "arbitrary")),
        name=f"swa_prompt_g{g}",
    )(src, src, src, src, src)
    return o.reshape(Tp, COL), l.reshape(Tp, LANE)


def _swa_sample_kernel(q_ref, kn_ref, vn_ref, cache_ref, o_ref, l_ref, *, dil, lbuf, t_s):
    nq = SWA_HEADS * t_s
    q = q_ref[...]
    qrep = jnp.concatenate([q] * SWA_HEADS, axis=0)
    rq = lax.broadcasted_iota(jnp.int32, (nq, COL), 0)
    cq = lax.broadcasted_iota(jnp.int32, (nq, COL), 1)
    qbd = jnp.where(_idiv(rq, t_s) == _idiv(cq, LANE), qrep, 0.0).astype(BF16)

    per_pos = 2 * SWA_HEADS
    kc = jnp.concatenate([cache_ref[pl.ds(h, lbuf, stride=per_pos), :].astype(BF16)
                          for h in range(SWA_HEADS)], axis=1)
    vc = jnp.concatenate([cache_ref[pl.ds(SWA_HEADS + h, lbuf, stride=per_pos), :].astype(BF16)
                          for h in range(SWA_HEADS)], axis=1)
    kn = kn_ref[...].astype(BF16)
    vn = vn_ref[...].astype(BF16)

    s_c = _dot_nt(qbd, kc)
    s_n = _dot_nt(qbd, kn)
    t_c = _imod(lax.broadcasted_iota(jnp.int32, (nq, lbuf), 0), t_s)
    c_c = lax.broadcasted_iota(jnp.int32, (nq, lbuf), 1)
    diff_c = lbuf + t_c - c_c
    ok_c = (_imod(diff_c, dil) == 0) & (diff_c <= lbuf)
    t_n = _imod(lax.broadcasted_iota(jnp.int32, (nq, t_s), 0), t_s)
    c_n = lax.broadcasted_iota(jnp.int32, (nq, t_s), 1)
    diff_n = t_n - c_n
    ok_n = (diff_n >= 0) & (_imod(diff_n, dil) == 0)
    s_c = jnp.where(ok_c, s_c, NEG_INF)
    s_n = jnp.where(ok_n, s_n, NEG_INF)
    m = jnp.maximum(jnp.max(s_c, axis=-1, keepdims=True), jnp.max(s_n, axis=-1, keepdims=True))
    p_c = jnp.exp(s_c - m)
    p_n = jnp.exp(s_n - m)
    den = jnp.sum(p_c, axis=-1, keepdims=True) + jnp.sum(p_n, axis=-1, keepdims=True)
    o_all = (_dot(p_c.astype(BF16), vc) + _dot(p_n.astype(BF16), vn)) / den
    lse = m + jnp.log(den)
    for h in range(SWA_HEADS):
        cs = slice(h * LANE, (h + 1) * LANE)
        o_ref[:, cs] = o_all[h * t_s:(h + 1) * t_s, cs]
    l_ref[...] = _pack_heads([lse[h * t_s:(h + 1) * t_s] for h in range(SWA_HEADS)])


def _swa_sample(Ps, cache, layer, g, dil):
    n_s, t_s, _ = Ps.shape
    rows = cache.shape[2]
    lbuf = rows // (2 * SWA_HEADS)
    assert lbuf == dil * CHUNK, "window buffer must hold exactly one full window"
    blk = lambda cb: pl.BlockSpec((None, t_s, COL), lambda b: (b, 0, cb + g))
    vmem = (2 * _nbytes((lbuf, 2 * COL), F32) + 2 * _nbytes((lbuf, 2 * COL), BF16)
            + 8 * _nbytes((SWA_HEADS * t_s, lbuf), F32))
    return pl.pallas_call(
        functools.partial(_swa_sample_kernel, dil=dil, lbuf=lbuf, t_s=t_s),
        grid=(n_s,),
        in_specs=[blk(CB_Q), blk(CB_K), blk(CB_V),
                  pl.BlockSpec((None, None, rows, LANE), lambda b: (layer, b, 0, 0))],
        out_specs=[pl.BlockSpec((None, t_s, COL), lambda b: (b, 0, 0)),
                   pl.BlockSpec((None, t_s, LANE), lambda b: (b, 0, 0))],
        out_shape=[jax.ShapeDtypeStruct((n_s, t_s, COL), F32),
                   jax.ShapeDtypeStruct((n_s, t_s, LANE), F32)],
        compiler_params=_params(vmem, ("parallel",)),
        name=f"swa_sample_g{g}",
    )(Ps, Ps, Ps, cache)


def _ret_head_inputs(qk_ref, v_ref, kout_ref, h):
    pair, half = h // 2, h % 2
    lane = lax.broadcasted_iota(jnp.int32, (CHUNK, LANE), 1)
    head_lanes = _idiv(lane, RET_QK_DIM) == half
    qt = qk_ref[:, pair * LANE:(pair + 1) * LANE]
    kt = qk_ref[:, COL // 2 + pair * LANE:COL // 2 + (pair + 1) * LANE]
    qm = jnp.where(head_lanes, qt, jnp.zeros_like(qt))
    kw = jnp.where(head_lanes, kt.astype(F32) * kout_ref[:, pair * LANE:(pair + 1) * LANE], 0.0)
    vh = v_ref[:, h * LANE:(h + 1) * LANE]
    return qm, kt, kw, vh


def _ret_finish(o, gate_ref, o_ref, h):
    cs = slice(h * LANE, (h + 1) * LANE)
    o_ref[:, cs] = (gate_ref[:, cs].astype(F32) * _rms(o)).astype(o_ref.dtype)


def _ret_prompt_kernel(*refs, n_seq):
    ins = refs[:3 * n_seq]
    decay_ref, qin_ref, kout_ref, gc_ref, o_ref, s_out_ref, s_ref, att_ref = refs[3 * n_seq:]
    i = pl.program_id(0)

    @pl.when(i == 0)
    def _():
        s_ref[...] = jnp.zeros_like(s_ref)

    pairs = [(b, h) for b in range(n_seq) for h in range(RET_HEADS)]
    head_in = lambda b, h: _ret_head_inputs(ins[3 * b], ins[3 * b + 1], kout_ref, h)
    for b, h in pairs:
        qm, kt, _, _ = head_in(b, h)
        att_ref[b, h] = (_dot_nt(qm, kt) * decay_ref[h]).astype(BF16)
    for b, h in pairs:
        qm, _, _, vh = head_in(b, h)
        o = (_dot(att_ref[b, h], vh)
             + _dot(qm, s_ref[b, h].astype(BF16)) * qin_ref[:, h * LANE:(h + 1) * LANE])
        _ret_finish(o, ins[3 * b + 2], o_ref.at[b], h)
    for b, h in pairs:
        _, _, kw, vh = head_in(b, h)
        s_ref[b, h] = s_ref[b, h] * gc_ref[h:h + 1, :] + _dot(kw.T.astype(BF16), vh)

    @pl.when(i == pl.num_programs(0) - 1)
    def _():
        for b in range(n_seq):
            for h in range(RET_HEADS):
                lo = (h % 2) * RET_QK_DIM
                s_out_ref[b, h] = s_ref[b, h, lo:lo + RET_QK_DIM, :]


def _ret_prompt(P, rt, n_p, s):
    nblk = s // CHUNK
    decay, qin, kout, gc = rt
    blk = lambda b, cb: pl.BlockSpec((CHUNK, COL), lambda i: (b * nblk + i, cb))
    const2 = lambda i: (0, 0)
    seq_specs = [blk(b, cb) for b in range(n_p) for cb in (CB_RQK, CB_RV, CB_RG)]
    st_shape = (n_p, RET_HEADS, RET_QK_DIM, RET_V_DIM)
    vmem = (8 * n_p * _nbytes((CHUNK, COL), BF16) + 2 * _nbytes(decay.shape, F32) + 4 * _nbytes(qin.shape, F32)
            + 3 * n_p * _nbytes((RET_HEADS, LANE, LANE), F32))
    o, st = pl.pallas_call(
        functools.partial(_ret_prompt_kernel, n_seq=n_p),
        grid=(nblk,),
        in_specs=seq_specs + [pl.BlockSpec(decay.shape, lambda i: (0, 0, 0)),
                              pl.BlockSpec(qin.shape, const2), pl.BlockSpec(kout.shape, const2),
                              pl.BlockSpec(gc.shape, const2)],
        out_specs=[pl.BlockSpec((n_p, CHUNK, COL), lambda i: (0, i, 0)),
                   pl.BlockSpec(st_shape, lambda i: (0, 0, 0, 0))],
        out_shape=[jax.ShapeDtypeStruct((n_p, s, COL), BF16), jax.ShapeDtypeStruct(st_shape, F32)],
        scratch_shapes=[pltpu.VMEM((n_p, RET_HEADS, LANE, LANE), F32),
                        pltpu.VMEM((n_p, RET_HEADS, CHUNK, CHUNK), BF16)],
        compiler_params=_params(vmem, ("arbitrary",)),
        name="ret_prompt",
    )(*([P] * (3 * n_p)), decay, qin, kout, gc)
    return o.reshape(n_p * s, COL), st


def _ret_sample_kernel(qk_ref, v_ref, gate_ref, s0_ref, decay_ref, qin_ref, kout_ref, gc_ref,
                       o_ref, s_out_ref, *, t_s):
    row = lax.broadcasted_iota(jnp.int32, (CHUNK, LANE), 0)
    for h in range(RET_HEADS):
        lo = (h % 2) * RET_QK_DIM
        qm, kt, kw, vh = _ret_head_inputs(qk_ref, v_ref, kout_ref, h)
        att = _dot_nt(qm, kt) * decay_ref[h]
        o = _dot(att.astype(BF16), vh)
        inter = jnp.zeros((CHUNK, LANE), F32)
        for sq in range(CHUNK // t_s):
            seq_rows = _idiv(row, t_s) == sq
            st = s0_ref[sq, h]
            st2 = jnp.concatenate([st, st], axis=0).astype(BF16)
            inter = jnp.where(seq_rows, _dot(qm, st2), inter)
            upd = _dot(jnp.where(seq_rows, kw, 0.0).T.astype(BF16), vh)
            s_out_ref[sq, h] = st * gc_ref[h:h + 1, :] + upd[lo:lo + RET_QK_DIM, :]
        o = o + inter * qin_ref[:, h * LANE:(h + 1) * LANE]
        _ret_finish(o, gate_ref, o_ref, h)


def _ret_sample(P, state, layer, rt, Tp, t_s):
    n_s = state.shape[1]
    spb = CHUNK // t_s
    base = Tp // CHUNK
    decay, qin, kout, gc = rt
    blk = lambda cb: pl.BlockSpec((CHUNK, COL), lambda i: (base + i, cb))
    const2 = lambda i: (0, 0)
    st_shape = (spb, RET_HEADS, RET_QK_DIM, RET_V_DIM)
    vmem = (8 * _nbytes((CHUNK, COL), BF16) + 2 * _nbytes(decay.shape, F32) + 4 * _nbytes(qin.shape, F32)
            + 4 * _nbytes(st_shape, F32))
    return pl.pallas_call(
        functools.partial(_ret_sample_kernel, t_s=t_s),
        grid=(n_s // spb,),
        in_specs=[blk(CB_RQK), blk(CB_RV), blk(CB_RG),
                  pl.BlockSpec((None,) + st_shape, lambda i: (layer, i, 0, 0, 0)),
                  pl.BlockSpec(decay.shape, lambda i: (0, 0, 0)),
                  pl.BlockSpec(qin.shape, const2), pl.BlockSpec(kout.shape, const2),
                  pl.BlockSpec(gc.shape, const2)],
        out_specs=[pl.BlockSpec((CHUNK, COL), lambda i: (i, 0)),
                   pl.BlockSpec(st_shape, lambda i: (i, 0, 0, 0))],
        out_shape=[jax.ShapeDtypeStruct((n_s * t_s, COL), BF16),
                   jax.ShapeDtypeStruct((n_s, RET_HEADS, RET_QK_DIM, RET_V_DIM), F32)],
        compiler_params=_params(vmem, ("parallel",)),
        name="ret_sample",
    )(P, P, P, state, decay, qin, kout, gc)


def _branch_kernel(xa_ref, xb_ref, a_ref, g0_ref, g1_ref, g2_ref, wb_ref, wo_ref, ex_ref,
                   po0, po1, po2, pl0, pl1, pl2, pc, so0, so1, so2, sl0, sl1, sl2, sc,
                   y_ref, on_ref, ln_ref, *, n_prompt_tiles):
    i = pl.program_id(0)

    def spread(packed):
        hi = packed.astype(BF16)
        r1 = packed - hi.astype(F32)
        mid = r1.astype(BF16)
        lo = (r1 - mid.astype(F32)).astype(BF16)
        return _dot(jnp.concatenate([hi, mid, lo], axis=1), ex_ref[...])

    def natural(o_ref, l_ref, k, dil):
        if dil == 1:
            wide = spread(l_ref[...])
            return ((lambda h: o_ref[:, h * LANE:(h + 1) * LANE].astype(F32)),
                    (lambda h: wide[:, h * LANE:(h + 1) * LANE]))
        rpt = TILE // dil
        for r in range(dil):
            rows = slice(r * rpt, (r + 1) * rpt)
            ln_ref[k, pl.ds(r, rpt, stride=dil), :] = l_ref[rows, :]
            for h in range(SWA_HEADS):
                on_ref[k, h, pl.ds(r, rpt, stride=dil), :] = o_ref[rows, h * LANE:(h + 1) * LANE].astype(F32)
        wide = spread(ln_ref[k])
        return (lambda h: on_ref[k, h]), (lambda h: wide[:, h * LANE:(h + 1) * LANE])

    def body(o_refs, l_refs, c_ref, permuted):
        getters = [natural(o_refs[k], l_refs[k], k, SWA_PATTERNS[k][1] if permuted else 1)
                   for k in range(len(SWA_PATTERNS))]
        heads = []
        for h in range(SWA_HEADS):
            ls = [gl(h) for _, gl in getters]
            lmax = functools.reduce(jnp.maximum, ls)
            es = [jnp.exp(l - lmax) for l in ls]
            num = sum(e * go(h) for e, (go, _) in zip(es, getters))
            heads.append((num / sum(es)).astype(BF16))
        mix = jnp.concatenate(heads, axis=1)
        merged = (g0_ref[...].astype(F32) * _dot(a_ref[...], wb_ref[0])
                  + g1_ref[...].astype(F32) * _dot(mix, wb_ref[1])
                  + g2_ref[...].astype(F32) * _dot(c_ref[...], wb_ref[2]))
        y_ref[...] = _tile_rows(i, n_prompt_tiles, xa_ref, xb_ref) + _dot(merged.astype(BF16), wo_ref[...])

    @pl.when(i < n_prompt_tiles)
    def _():
        body((po0, po1, po2), (pl0, pl1, pl2), pc, True)

    @pl.when(i >= n_prompt_tiles)
    def _():
        body((so0, so1, so2), (sl0, sl1, sl2), sc, False)


def _branch(xa, xb, out_a, P, wb, wo, prompt_set, sample_set, n_prompt_tiles):
    T = (n_prompt_tiles + 1) * TILE
    assert sample_set[0].shape[0] == TILE, "the sample rows must form exactly one tile"
    n_g = len(SWA_PATTERNS)
    last = n_prompt_tiles - 1
    row = lambda i: (i, 0)
    gate = lambda k: pl.BlockSpec((TILE, D_MODEL), lambda i: (i, CB_GATE * COL // D_MODEL + k))
    once = pl.Buffered(1)
    p_o = pl.BlockSpec((TILE, COL), lambda i: (jnp.minimum(i, last), 0))
    p_l = pl.BlockSpec((TILE, LANE), lambda i: (jnp.minimum(i, last), 0))
    s_o = pl.BlockSpec((TILE, COL), lambda i: (0, 0))
    s_l = pl.BlockSpec((TILE, LANE), lambda i: (0, 0))
    src = jnp.arange(3 * LANE) % LANE
    dst_head = jnp.arange(COL) // LANE
    expand = (src[:, None] == dst_head[None, :] * LSE_LANES).astype(BF16)
    vmem = (4 * _nbytes((TILE, D_MODEL), F32) + 6 * _nbytes((TILE, D_MODEL), BF16)
            + 2 * (2 * n_g + 2) * (_nbytes((TILE, COL), BF16) + _nbytes((TILE, COL), F32))
            + _nbytes(wb.shape, BF16) + _nbytes(wo.shape, BF16)
            + 4 * n_g * _nbytes((TILE, COL), F32) + 4 * _nbytes((TILE, D_MODEL), F32))
    return pl.pallas_call(
        functools.partial(_branch_kernel, n_prompt_tiles=n_prompt_tiles),
        grid=(T // TILE,),
        in_specs=_x_specs(xa, xb) + [
                  pl.BlockSpec((TILE, COL), row),
                  gate(0), gate(1), gate(2),
                  pl.BlockSpec(wb.shape, lambda i: (0, 0, 0), pipeline_mode=once),
                  pl.BlockSpec(wo.shape, lambda i: (0, 0), pipeline_mode=once),
                  pl.BlockSpec(expand.shape, lambda i: (0, 0)),
                  p_o, p_o, p_o, p_l, p_l, p_l, p_o,
                  s_o, s_o, s_o, s_l, s_l, s_l, s_o],
        out_specs=pl.BlockSpec((TILE, D_MODEL), row),
        out_shape=jax.ShapeDtypeStruct((T, D_MODEL), F32),
        scratch_shapes=[pltpu.VMEM((n_g, SWA_HEADS, TILE, LANE), F32),
                        pltpu.VMEM((n_g, TILE, LANE), F32)],
        compiler_params=_params(vmem, ("parallel",)),
        name="branch_merge",
    )(xa, xb, out_a, P, P, P, wb, wo, expand, *prompt_set, *sample_set)


def _swiglu_into(hb, w1_ref, w3_ref, w2_ref, acc_ref, fc):
    n_fc = w1_ref.shape[-1] // fc
    for f in range(n_fc):
        cols = slice(f * fc, (f + 1) * fc)
        a = _dot(hb, w1_ref[:, cols])
        b = _dot(hb, w3_ref[:, cols])
        part = _dot((a * _sigmoid(a) * b).astype(BF16), w2_ref[cols, :])
        if f == 0:
            acc_ref[...] = part
        else:
            acc_ref[...] += part


def _dense_ffn_kernel(x_ref, g2_ref, w1_ref, w3_ref, w2_ref, y_ref, acc_ref, *, fc):
    x = x_ref[...]
    hb = (_rms(x) * g2_ref[...]).astype(BF16)
    _swiglu_into(hb, w1_ref, w3_ref, w2_ref, acc_ref, fc)
    y_ref[...] = x + acc_ref[...]


def _dense_ffn(x, g2, w1b, w3b, w2b):
    T = x.shape[0]
    F = w1b.shape[1]
    tm = _pick(T, (768, 512, 384, 256, 128))
    fc = _pick(F, (512, 256, 128))
    row = lambda i: (i, 0)
    once = pl.Buffered(1)
    vmem = (5 * _nbytes((tm, D_MODEL), F32) + 3 * _nbytes(w1b.shape, BF16) + _nbytes((tm, D_MODEL), BF16)
            + 4 * _nbytes((tm, fc), F32))
    return pl.pallas_call(
        functools.partial(_dense_ffn_kernel, fc=fc),
        grid=(T // tm,),
        in_specs=[pl.BlockSpec((tm, D_MODEL), row),
                  pl.BlockSpec((1, D_MODEL), lambda i: (0, 0)),
                  pl.BlockSpec(w1b.shape, lambda i: (0, 0), pipeline_mode=once),
                  pl.BlockSpec(w3b.shape, lambda i: (0, 0), pipeline_mode=once),
                  pl.BlockSpec(w2b.shape, lambda i: (0, 0), pipeline_mode=once)],
        out_specs=pl.BlockSpec((tm, D_MODEL), row),
        out_shape=jax.ShapeDtypeStruct((T, D_MODEL), F32),
        scratch_shapes=[pltpu.VMEM((tm, D_MODEL), F32)],
        compiler_params=_params(vmem, ("parallel",)),
        name="dense_ffn",
    )(x, g2, w1b, w3b, w2b)


def _router_kernel(x_ref, g2_ref, wr_ref, e0_ref, e1_ref, w0_ref, w1_ref):
    hb = (_rms(x_ref[...]) * g2_ref[...]).astype(BF16)
    logits = _dot(hb, wr_ref[...])
    lane = lax.broadcasted_iota(jnp.int32, logits.shape, 1)
    logits = jnp.where(lane < N_EXPERTS, logits, -jnp.inf)
    lane_f = lane.astype(F32)
    m1 = jnp.max(logits, axis=-1, keepdims=True)
    i1 = jnp.min(jnp.where(logits == m1, lane_f, float(LANE)), axis=-1, keepdims=True)
    rest = jnp.where(lane_f == i1, -jnp.inf, logits)
    m2 = jnp.max(rest, axis=-1, keepdims=True)
    i2 = jnp.min(jnp.where(rest == m2, lane_f, float(LANE)), axis=-1, keepdims=True)
    e = jnp.exp(m2 - m1)
    e0_ref[...] = jnp.broadcast_to(i1, logits.shape).astype(jnp.int32)
    e1_ref[...] = jnp.broadcast_to(i2, logits.shape).astype(jnp.int32)
    w0_ref[...] = jnp.broadcast_to(1.0 / (1.0 + e), logits.shape)
    w1_ref[...] = jnp.broadcast_to(e / (1.0 + e), logits.shape)


def _router(x, g2, wr_pad):
    T = x.shape[0]
    tm = _pick(T, (768, 512, 384, 256, 128))
    row = lambda i: (i, 0)
    out = pl.BlockSpec((tm, LANE), row)
    vmem = 4 * _nbytes((tm, D_MODEL), F32) + 2 * _nbytes(wr_pad.shape, BF16) + 16 * _nbytes((tm, LANE), F32)
    return pl.pallas_call(
        _router_kernel,
        grid=(T // tm,),
        in_specs=[pl.BlockSpec((tm, D_MODEL), row),
                  pl.BlockSpec((1, D_MODEL), lambda i: (0, 0)),
                  pl.BlockSpec(wr_pad.shape, lambda i: (0, 0))],
        out_specs=[out, out, out, out],
        out_shape=[jax.ShapeDtypeStruct((T, LANE), jnp.int32), jax.ShapeDtypeStruct((T, LANE), jnp.int32),
                   jax.ShapeDtypeStruct((T, LANE), F32), jax.ShapeDtypeStruct((T, LANE), F32)],
        compiler_params=_params(vmem, ("parallel",)),
        name="moe_router",
    )(x, g2, wr_pad)


RUN_ALIGN = 8


def _run_pieces(count, max_rows):
    pieces, off = [], 0
    size = max_rows
    while size >= RUN_ALIGN:
        active = (count & size) != 0
        pieces.append((size, off, active))
        off = off + jnp.where(active, size, 0)
        size //= 2
    return pieces


def _dispatch_kernel(base_ref, cnt_ref, pend_ref, x_ref, g2_ref, e0_ref, e1_ref, upper_ref, xs_ref,
                     comp_ref, sem, *, tme, n_tiles):
    i = pl.program_id(0)
    n_steps = pl.num_programs(0)
    cur = i % 2
    half = tme // 2

    @pl.when(i == 0)
    def _():
        comp_ref[0, 0] = jnp.zeros(comp_ref.shape[2:], comp_ref.dtype)
        assert comp_ref.shape[2] == half

        def zero_tile(start):
            return [pltpu.make_async_copy(
                        comp_ref.at[0, 0], xs_ref.at[pl.ds(pl.multiple_of(start + k * half, half), half), :],
                        sem.at[0])
                    for k in range(2)]

        def tail_loop(fn):
            def body(t, carry):
                for c in zero_tile(t * tme):
                    fn(c)
                return carry
            lax.fori_loop(pend_ref[N_EXPERTS - 1] // tme, n_tiles, body, 0)

        def group_tails(fn):
            for e in range(N_EXPERTS):
                prev_end = pend_ref[e - 1] if e else 0

                @pl.when(pend_ref[e] > prev_end)
                def _():
                    for c in zero_tile(pend_ref[e] - tme):
                        fn(c)

        group_tails(lambda c: c.start())
        tail_loop(lambda c: c.start())
        group_tails(lambda c: c.wait())
        tail_loop(lambda c: c.wait())

    h = (_rms(x_ref[...]) * g2_ref[...]).astype(BF16)
    lane = lax.broadcasted_iota(jnp.int32, e0_ref.shape, 1)
    picks = jnp.where((lane == e0_ref[...]) | (lane == e1_ref[...]), 1.0, 0.0)
    picks_t = picks.T
    rank_t = _dot(picks_t.astype(BF16), upper_ref[...])
    slot = lax.broadcasted_iota(jnp.int32, (TILE, TILE), 0).astype(F32)
    for e in range(N_EXPERTS):
        sel = jnp.where((rank_t[e:e + 1, :] == slot) & (picks_t[e:e + 1, :] > 0.0), 1.0, 0.0)
        comp_ref[cur, e] = _dot(sel.astype(BF16), h)

    def for_copies(tile, buf, fn):
        for e in range(N_EXPERTS):
            cnt, dst = cnt_ref[tile * N_EXPERTS + e], base_ref[tile * N_EXPERTS + e]
            for size, off, active in _run_pieces(cnt, TILE):
                c = pltpu.make_async_copy(
                    comp_ref.at[buf, e, pl.ds(pl.multiple_of(off, RUN_ALIGN), size), :],
                    xs_ref.at[pl.ds(pl.multiple_of(dst + off, RUN_ALIGN), size), :], sem.at[buf])
                pl.when(active)(functools.partial(fn, c))

    for_copies(i, cur, lambda c: c.start())

    @pl.when(i > 0)
    def _():
        for_copies(i - 1, 1 - cur, lambda c: c.wait())

    @pl.when(i == n_steps - 1)
    def _():
        for_copies(i, cur, lambda c: c.wait())


def _dispatch(x, g2, e0, e1, base, cnt, pend, upper, n_tiles, tme):
    T = x.shape[0]
    assert tme == 2 * TILE
    row = lambda i, *_: (i, 0)
    vmem = (3 * _nbytes((TILE, D_MODEL), F32) + 2 * N_EXPERTS * _nbytes((TILE, D_MODEL), F32)
            + 8 * _nbytes((TILE, TILE), F32))
    grid_spec = pltpu.PrefetchScalarGridSpec(
        num_scalar_prefetch=3,
        grid=(T // TILE,),
        in_specs=[pl.BlockSpec((TILE, D_MODEL), row),
                  pl.BlockSpec((1, D_MODEL), lambda i, *_: (0, 0)),
                  pl.BlockSpec((TILE, LANE), row), pl.BlockSpec((TILE, LANE), row),
                  pl.BlockSpec((TILE, TILE), lambda i, *_: (0, 0))],
        out_specs=pl.BlockSpec(memory_space=pl.ANY),
        scratch_shapes=[pltpu.VMEM((2, N_EXPERTS, TILE, D_MODEL), F32), pltpu.SemaphoreType.DMA((2,))],
    )
    return pl.pallas_call(
        functools.partial(_dispatch_kernel, tme=tme, n_tiles=n_tiles),
        grid_spec=grid_spec,
        out_shape=jax.ShapeDtypeStruct((n_tiles * tme, D_MODEL), F32),
        compiler_params=_params(vmem, ("arbitrary",)),
        name="moe_dispatch",
    )(base, cnt, pend, x, g2, e0, e1, upper)


def _expert_ffn_kernel(te_ref, used_ref, xs_ref, w1_ref, w3_ref, w2_ref, y_ref, acc_ref, *, fc):
    i = pl.program_id(0)

    @pl.when(i < used_ref[0])
    def _():
        _swiglu_into(xs_ref[...].astype(BF16), w1_ref, w3_ref, w2_ref, acc_ref, fc)
        y_ref[...] = acc_ref[...]

    @pl.when(i >= used_ref[0])
    def _():
        y_ref[...] = jnp.zeros_like(y_ref)


def _expert_ffn(xs, tile_expert, n_used, we1b, we3b, we2b, tme):
    n_rows = xs.shape[0]
    F = we1b.shape[2]
    fc = _pick(F, (512, 256, 128))
    once = pl.Buffered(1)
    wspec = lambda shape: pl.BlockSpec((None,) + shape, lambda i, te, nu: (te[i], 0, 0), pipeline_mode=once)
    vmem = (5 * _nbytes((tme, D_MODEL), F32) + 3 * _nbytes(we1b.shape[1:], BF16) + _nbytes((tme, D_MODEL), BF16)
            + 4 * _nbytes((tme, fc), F32))
    grid_spec = pltpu.PrefetchScalarGridSpec(
        num_scalar_prefetch=2,
        grid=(n_rows // tme,),
        in_specs=[pl.BlockSpec((tme, D_MODEL), lambda i, te, nu: (i, 0)),
                  wspec(we1b.shape[1:]), wspec(we3b.shape[1:]), wspec(we2b.shape[1:])],
        out_specs=pl.BlockSpec((tme, D_MODEL), lambda i, te, nu: (i, 0)),
        scratch_shapes=[pltpu.VMEM((tme, D_MODEL), F32)],
    )
    return pl.pallas_call(
        functools.partial(_expert_ffn_kernel, fc=fc),
        grid_spec=grid_spec,
        out_shape=jax.ShapeDtypeStruct((n_rows, D_MODEL), F32),
        compiler_params=_params(vmem, ("arbitrary",)),
        name="moe_expert_ffn",
    )(tile_expert, n_used, xs, we1b, we3b, we2b)


COMBINE_PIECE = 64


def _combine_kernel(base_ref, cnt_ref, x_ref, e0_ref, e1_ref, w0_ref, w1_ref, lower_ref, y_hbm_ref,
                    o_ref, o_last_ref, buf_ref, sem):
    i = pl.program_id(0)
    n_steps = pl.num_programs(0)
    cur = i % 2

    def for_copies(tile, buf, fn):
        for e in range(N_EXPERTS):
            cnt, src = cnt_ref[tile * N_EXPERTS + e], base_ref[tile * N_EXPERTS + e]
            for q in range(TILE // COMBINE_PIECE):
                c = pltpu.make_async_copy(
                    y_hbm_ref.at[pl.ds(pl.multiple_of(src + q * COMBINE_PIECE, RUN_ALIGN), COMBINE_PIECE), :],
                    buf_ref.at[buf, e, pl.ds(q * COMBINE_PIECE, COMBINE_PIECE), :], sem.at[buf])
                pl.when(cnt > q * COMBINE_PIECE)(functools.partial(fn, c))

    @pl.when(i == 0)
    def _():
        buf_ref[...] = jnp.zeros_like(buf_ref)
        for_copies(0, 0, lambda c: c.start())

    @pl.when(i + 1 < n_steps)
    def _():
        for_copies(i + 1, 1 - cur, lambda c: c.start())

    for_copies(i, cur, lambda c: c.wait())

    e0, e1 = e0_ref[...], e1_ref[...]
    slot = lax.broadcasted_iota(jnp.int32, (TILE, TILE), 1).astype(F32)
    wide = lambda a, n: jnp.concatenate([a] * n, axis=1)
    acc = x_ref[...]
    for e in range(N_EXPERTS):
        is0, is1 = e0 == e, e1 == e
        picks = wide(jnp.where(is0 | is1, 1.0, 0.0), TILE // LANE)
        rank = _dot(lower_ref[...], picks.astype(BF16))
        picked = jnp.where((rank == slot) & (picks > 0.0), 1.0, 0.0).astype(BF16)
        rows = _dot(picked, buf_ref[cur, e].astype(BF16))
        weight = jnp.where(is0, w0_ref[...], jnp.where(is1, w1_ref[...], 0.0))
        acc = acc + wide(weight, D_MODEL // LANE) * rows

    @pl.when(i < n_steps - 1)
    def _():
        o_ref[...] = acc

    @pl.when(i == n_steps - 1)
    def _():
        o_last_ref[...] = acc


def _combine(x, e0, e1, w0, w1, base, cnt, lower, y_rows):
    T = x.shape[0]
    row = lambda i, *_: (i, 0)
    lane_spec = pl.BlockSpec((TILE, LANE), row)
    vmem = (5 * _nbytes((TILE, D_MODEL), F32) + 2 * N_EXPERTS * _nbytes((TILE, D_MODEL), F32)
            + 8 * _nbytes((TILE, LANE), F32) + 8 * _nbytes((TILE, TILE), F32)
            + 4 * _nbytes((TILE, D_MODEL), F32))
    grid_spec = pltpu.PrefetchScalarGridSpec(
        num_scalar_prefetch=2,
        grid=(T // TILE,),
        in_specs=[pl.BlockSpec((TILE, D_MODEL), row), lane_spec, lane_spec, lane_spec, lane_spec,
                  pl.BlockSpec((TILE, TILE), lambda i, *_: (0, 0)),
                  pl.BlockSpec(memory_space=pl.ANY)],
        out_specs=[pl.BlockSpec((TILE, D_MODEL), lambda i, *_: (jnp.minimum(i, T // TILE - 2), 0)),
                   pl.BlockSpec((TILE, D_MODEL), lambda i, *_: (0, 0))],
        scratch_shapes=[pltpu.VMEM((2, N_EXPERTS, TILE, D_MODEL), F32), pltpu.SemaphoreType.DMA((2,))],
    )
    return pl.pallas_call(
        _combine_kernel,
        grid_spec=grid_spec,
        out_shape=[jax.ShapeDtypeStruct((T - TILE, D_MODEL), F32), jax.ShapeDtypeStruct((TILE, D_MODEL), F32)],
        compiler_params=_params(vmem, ("arbitrary",)),
        name="moe_combine",
    )(base, cnt, x, e0, e1, w0, w1, lower, y_rows)


def _moe(x, g2, wr_pad, we1b, we3b, we2b):
    T = x.shape[0]
    tme = 2 * TILE
    n_tt = T // TILE
    e0, e1, w0, w1 = _router(x, g2, wr_pad)
    experts = jnp.arange(N_EXPERTS, dtype=jnp.int32)
    picks = (e0[:, :1] == experts[None, :]) | (e1[:, :1] == experts[None, :])
    cnt = jnp.sum(picks.reshape(n_tt, TILE, N_EXPERTS).astype(jnp.int32), axis=1)
    cnt = (cnt + RUN_ALIGN - 1) // RUN_ALIGN * RUN_ALIGN
    counts = jnp.sum(cnt, axis=0)
    padded = (counts + tme - 1) // tme * tme
    pend = jnp.cumsum(padded).astype(jnp.int32)
    base = ((pend - padded)[None, :] + jnp.cumsum(cnt, axis=0) - cnt).astype(jnp.int32)
    max_rows = T * TOP_K + n_tt * N_EXPERTS * (RUN_ALIGN - 1)
    n_tiles = -(-max_rows // tme) + N_EXPERTS + 1
    tile_start = jnp.arange(n_tiles, dtype=jnp.int32) * tme
    tile_expert = jnp.minimum(jnp.sum(pend[None, :] <= tile_start[:, None], axis=1), N_EXPERTS - 1).astype(jnp.int32)
    n_used = (pend[N_EXPERTS - 1:] // tme).astype(jnp.int32)
    tok = jnp.arange(TILE)
    upper = (tok[:, None] < tok[None, :]).astype(BF16)
    base_f, cnt_f = base.reshape(-1), cnt.reshape(-1).astype(jnp.int32)
    xs = _dispatch(x, g2, e0, e1, base_f, cnt_f, pend, upper, n_tiles, tme)
    y_rows = _expert_ffn(xs, tile_expert, n_used, we1b, we3b, we2b, tme)
    return _combine(x, e0, e1, w0, w1, base_f, cnt_f, upper.T, y_rows)


def _class_major(a, dil):
    n = a.shape[0]
    return a.reshape((n // TILE, TILE // dil, dil) + a.shape[1:]).swapaxes(1, 2).reshape(a.shape)


def _natural(a, dil):
    n = a.shape[0]
    return a.reshape((n // TILE, dil, TILE // dil) + a.shape[1:]).swapaxes(1, 2).reshape(a.shape)


def _perm_matrices():
    eye = jnp.eye(TILE, dtype=BF16)
    return jnp.stack([_class_major(eye, dil) for _, dil in SWA_PATTERNS[1:]])


def _rope_tables(pos, Tp):
    freqs = ROPE_THETA ** (-jnp.arange(0, SWA_HEAD_DIM, 2, dtype=F32) / SWA_HEAD_DIM)
    rfreqs = 1.0 / (ROPE_THETA ** jnp.linspace(0.0, 1.0, RET_QK_DIM // 2, dtype=F32))

    def swa(p):
        ang = p.astype(F32)[:, None] * freqs[None, :]
        c, s = jnp.cos(ang), jnp.sin(ang)
        return jnp.concatenate([c, c], axis=1), jnp.concatenate([-s, s], axis=1)

    per_group = [swa(jnp.concatenate([_class_major(pos[:Tp], dil), pos[Tp:]])) for _, dil in SWA_PATTERNS]
    cosb = jnp.stack([c for c, _ in per_group])
    sinb = jnp.stack([s for _, s in per_group])
    rang = pos.astype(F32)[:, None] * rfreqs[None, :]
    c, s = jnp.cos(rang), jnp.sin(rang)
    cosc = jnp.concatenate([c, c, c, c], axis=1)
    sinc = jnp.concatenate([-s, s, -s, s], axis=1)
    return cosb, sinb, cosc, sinc


def _ret_tables(c_len):
    log_g = jnp.log1p(-jnp.exp2(-5.0 - jnp.arange(RET_HEADS, dtype=F32)))
    r = jnp.arange(CHUNK)
    i = (r % c_len).astype(F32)
    same = (r[:, None] // c_len) == (r[None, :] // c_len)
    dist = i[:, None] - i[None, :]
    decay = jnp.where(same[None] & (dist >= 0)[None],
                      jnp.exp(log_g[:, None, None] * jnp.maximum(dist, 0.0)[None]), 0.0)
    qin = jnp.repeat(jnp.exp(log_g[None, :] * (i[:, None] + 1.0)), RET_V_DIM, axis=1)
    kout = jnp.repeat(jnp.exp(log_g[None, :] * (c_len - 1.0 - i)[:, None]), RET_QK_DIM, axis=1)
    gc = jnp.broadcast_to(jnp.exp(log_g * c_len)[:, None], (RET_HEADS, LANE))
    gc = jnp.concatenate([gc, jnp.zeros((8 - RET_HEADS, LANE), F32)], axis=0)
    return decay.astype(F32), qin.astype(F32), kout.astype(F32), gc.astype(F32)


def _mixa_tables(w_s, b_s, t_s):
    w_p = jnp.tril(w_s)
    w8 = jnp.tril(w_s[:, :t_s, :t_s])
    eye = jnp.eye(CHUNK // t_s, dtype=w_s.dtype)
    w_smp = jax.vmap(lambda m: jnp.kron(eye, m))(w8)
    w2 = jnp.stack([w_p, w_smp]).astype(BF16)
    b_p = jnp.repeat(b_s.T, LANE, axis=1)
    b_smp = jnp.repeat(jnp.tile(b_s[:, :t_s].T, (CHUNK // t_s, 1)), LANE, axis=1)
    return w2, jnp.stack([b_p, b_smp]).astype(F32)


def kernel(x_prompt, x_sample, cache_swa_kv0, cache_swa_kv1, cache_swa_kv2, state_ret, norm1_g, w_in, norm_v_g, w_s, b_s, q_norm_g, k_norm_g, w_branch, w_out, norm2_g, w1, w3, w2, w_router, we1, we3, we2):
    n_p, s, d = x_prompt.shape
    n_s, t_s, _ = x_sample.shape
    depth = w_in.shape[0]
    Tp, Ts = n_p * s, n_s * t_s
    T = Tp + Ts
    max_win, max_dil = SWA_PATTERNS[-1]
    assert d == D_MODEL and Ts == TILE and CHUNK % t_s == 0
    assert s % (CHUNK * max_dil) == 0 and s >= max_win
    n_pt = Tp // TILE

    xa, xb = x_prompt.reshape(Tp, d), x_sample.reshape(Ts, d)
    pos = jnp.concatenate([jnp.arange(s, dtype=jnp.int32),
                           jnp.tile(PAST_LEN + jnp.arange(t_s, dtype=jnp.int32), n_s)])
    tabs = _rope_tables(pos, s)
    perm = _perm_matrices()
    rt_prompt = _ret_tables(CHUNK)
    rt_sample = _ret_tables(t_s)
    caches = tuple(c.reshape(c.shape[0], c.shape[1], -1, SWA_HEAD_DIM)
                   for c in (cache_swa_kv0, cache_swa_kv1, cache_swa_kv2))

    p_kv = [[] for _ in SWA_PATTERNS]
    s_kv = [[] for _ in SWA_PATTERNS]
    p_ret, s_ret, s_v = [], [], []
    for layer in range(depth):
        P = _inproj(xa, xb, norm1_g[layer][None, :], w_in[layer].astype(BF16), perm, tabs,
                    norm_v_g[layer][None, :], q_norm_g[layer][None, :], k_norm_g[layer][None, :],
                    n_pt, s // TILE)

        w2a, b2a = _mixa_tables(w_s[layer], b_s[layer], t_s)
        out_a = _mixer_a(P, w2a, b2a, Tp // CHUNK)

        Ps = P[Tp:].astype(F32).reshape(n_s, t_s, IN_WIDTH)
        po, plse, so, slse = [], [], [], []
        for g, (win, dil) in enumerate(SWA_PATTERNS):
            o_g, l_g = _swa_prompt(P, g, dil, n_p, s)
            os_g, ls_g = _swa_sample(Ps, caches[g], layer, g, dil)
            po.append(o_g)
            plse.append(l_g)
            so.append(os_g.reshape(Ts, COL).astype(BF16))
            slse.append(ls_g.reshape(Ts, LANE))

        out_c, ret_p = _ret_prompt(P, rt_prompt, n_p, s)
        out_cs, ret_s = _ret_sample(P, state_ret, layer, rt_sample, Tp, t_s)

        x = _branch(xa, xb, out_a, P, w_branch[layer].astype(BF16), w_out[layer].astype(BF16),
                    (*po, *plse, out_c), (*so, *slse, out_cs), n_pt)

        g2 = norm2_g[layer][None, :]
        i = layer // 2
        if layer % 2 == 0:
            x = _dense_ffn(x, g2, w1[i].astype(BF16), w3[i].astype(BF16), w2[i].astype(BF16))
            xa = xb = x
        else:
            wr_pad = jnp.zeros((D_MODEL, LANE), BF16).at[:, :N_EXPERTS].set(w_router[i].astype(BF16))
            xa, xb = _moe(x, g2, wr_pad, we1[i].astype(BF16), we3[i].astype(BF16), we2[i].astype(BF16))

        for g, (win, dil) in enumerate(SWA_PATTERNS):
            kcols = slice((CB_K + g) * COL, (CB_K + g + 1) * COL)
            vcols = slice((CB_V + g) * COL, (CB_V + g + 1) * COL)
            keep = -(-min(win, s) // TILE) * TILE

            def rows(cols):
                blk = jnp.concatenate([P[(b + 1) * s - keep:(b + 1) * s, cols] for b in range(n_p)])
                blk = _natural(blk, dil).reshape(n_p, keep, SWA_HEADS, SWA_HEAD_DIM)
                return blk[:, keep - min(win, s):]

            p_kv[g].append(jnp.stack([rows(kcols), rows(vcols)], axis=2).astype(F32))
            ks = P[Tp:, kcols].reshape(n_s, t_s, SWA_HEADS, SWA_HEAD_DIM)
            vs = P[Tp:, vcols].reshape(n_s, t_s, SWA_HEADS, SWA_HEAD_DIM)
            s_kv[g].append(jnp.stack([ks, vs], axis=2).astype(F32))
        p_ret.append(ret_p)
        s_ret.append(ret_s)
        s_v.append(P[Tp:, CB_AV * COL:(CB_AV + 1) * COL].astype(F32).reshape(n_s, t_s, COL))

    y_prompt = xa[:Tp].reshape(n_p, s, d)
    y_sample = xb[xb.shape[0] - Ts:].reshape(n_s, t_s, d)
    return (y_prompt, y_sample,
            jnp.stack(p_kv[0]), jnp.stack(p_kv[1]), jnp.stack(p_kv[2]), jnp.stack(p_ret),
            jnp.stack(s_kv[0]), jnp.stack(s_kv[1]), jnp.stack(s_kv[2]), jnp.stack(s_ret),
            jnp.stack(s_v))
```

```python
import functools
import math

import jax
import jax.numpy as jnp
from jax import lax
from jax.experimental import pallas as pl
from jax.experimental.pallas import tpu as pltpu

F32 = jnp.float32
BF16 = jnp.bfloat16

PAST_LEN = 16384
EPS = 1e-6
NEG_INF = -1e30
ROPE_THETA = 10000.0

D_MODEL = 1024
LANE = 128
CHUNK = 128
TILE = 256
COL = 512
A_GROUPS = 4
SWA_PATTERNS = ((128, 1), (512, 4), (2048, 16))
SWA_HEADS = 4
SWA_HEAD_DIM = 128
RET_HEADS = 4
RET_QK_DIM = 64
RET_V_DIM = 128
N_EXPERTS = 8
TOP_K = 2
IN_WIDTH = 10240
N_COL = IN_WIDTH // COL

CB_AU, CB_AV, CB_Q, CB_K, CB_V, CB_RQK, CB_RV, CB_RG, CB_GATE = 0, 1, 2, 5, 8, 11, 12, 13, 14

VMEM_INTERNAL_SCRATCH = 8 * 1024 * 1024


def _pick(n, candidates):
    for c in candidates:
        if n % c == 0:
            return c
    raise ValueError(f"no tile in {candidates} divides {n}")


def _params(block_bytes, semantics=None):
    limit = int(block_bytes) + VMEM_INTERNAL_SCRATCH
    return pltpu.CompilerParams(dimension_semantics=semantics, vmem_limit_bytes=limit)


def _nbytes(shape, dtype):
    return math.prod(shape) * jnp.dtype(dtype).itemsize


def _rms(x):
    return x * lax.rsqrt(jnp.mean(x * x, axis=-1, keepdims=True) + EPS)


def _gelu(x):
    return 0.5 * x * (1.0 + lax.erf(x * (0.5 ** 0.5)))


def _sigmoid(x):
    return 1.0 / (1.0 + jnp.exp(-x))


def _idiv(x, n):
    assert n & (n - 1) == 0
    return x >> (n.bit_length() - 1)


def _imod(x, n):
    assert n & (n - 1) == 0
    return x & (n - 1)


def _dot(a, b):
    return jnp.dot(a, b, preferred_element_type=F32)


LSE_LANES = LANE // 4


def _pack_heads(cols):
    rows = cols[0].shape[0]
    grp = _idiv(lax.broadcasted_iota(jnp.int32, (rows, LANE), 1), LSE_LANES)
    out = jnp.broadcast_to(cols[-1], (rows, LANE))
    for h in range(len(cols) - 2, -1, -1):
        out = jnp.where(grp == h, cols[h], out)
    return out


def _dot_nt(a, b):
    return lax.dot_general(a, b, (((1,), (1,)), ((), ())), preferred_element_type=F32)


def _tile_rows(i, n_prompt_tiles, xa_ref, xb_ref):
    return jnp.where(i < n_prompt_tiles, xa_ref[...], xb_ref[...])


def _x_specs(xa, xb):
    last_a, last_b = xa.shape[0] // TILE - 1, xb.shape[0] // TILE - 1
    return [pl.BlockSpec((TILE, D_MODEL), lambda i, *_: (jnp.minimum(i, last_a), 0)),
            pl.BlockSpec((TILE, D_MODEL), lambda i, *_: (last_b, 0))]


def _inproj_kernel(xa_ref, xb_ref, g1_ref, w_ref, perm_ref, cosb_ref, sinb_ref, cosc_ref, sinc_ref,
                   nvg_ref, qg_ref, kg_ref, o_ref, h_ref, *, n_prompt_tiles):
    i = pl.program_id(0)
    hn = (_rms(_tile_rows(i, n_prompt_tiles, xa_ref, xb_ref)) * g1_ref[...]).astype(BF16)
    h_ref[0] = hn

    @pl.when(i < n_prompt_tiles)
    def _():
        for k in range(1, len(SWA_PATTERNS)):
            h_ref[k] = _dot(perm_ref[k - 1], hn).astype(BF16)

    @pl.when(i >= n_prompt_tiles)
    def _():
        for k in range(1, len(SWA_PATTERNS)):
            h_ref[k] = hn

    def qk_heads(acc, g, cols0, gain_ref, scale):
        for hh in range(SWA_HEADS):
            cs = slice(hh * LANE, (hh + 1) * LANE)
            y = _rms(acc[:, cs]) * gain_ref[...]
            rot = y * cosb_ref[g] + pltpu.roll(y, LANE // 2, axis=1) * sinb_ref[g]
            o_ref[:, cols0 + hh * LANE:cols0 + (hh + 1) * LANE] = (rot * scale).astype(o_ref.dtype)

    for j in range(N_COL):
        cols = slice(j * COL, (j + 1) * COL)
        g = (j - CB_Q) % len(SWA_PATTERNS) if CB_Q <= j < CB_RQK else 0
        acc = _dot(h_ref[g], w_ref[:, cols])
        if j == CB_AU:
            o_ref[:, cols] = _gelu(acc).astype(o_ref.dtype)
        elif j == CB_AV:
            o_ref[:, cols] = (_rms(_gelu(acc)) * nvg_ref[...]).astype(o_ref.dtype)
        elif CB_Q <= j < CB_K:
            qk_heads(acc, g, j * COL, qg_ref, SWA_HEAD_DIM ** -0.5)
        elif CB_K <= j < CB_V:
            qk_heads(acc, g, j * COL, kg_ref, 1.0)
        elif j < CB_RQK or j == CB_RV:
            o_ref[:, cols] = acc.astype(o_ref.dtype)
        elif j == CB_RQK:
            lane = lax.broadcasted_iota(jnp.int32, (acc.shape[0], LANE), 1)
            first_half = _imod(lane, RET_QK_DIM) < (RET_QK_DIM // 2)
            for tt in range(COL // LANE):
                y = acc[:, tt * LANE:(tt + 1) * LANE]
                partner = jnp.where(first_half,
                                    pltpu.roll(y, LANE - RET_QK_DIM // 2, axis=1),
                                    pltpu.roll(y, RET_QK_DIM // 2, axis=1))
                rot = y * cosc_ref[...] + partner * sinc_ref[...]
                scale = 1.0 if tt < (COL // LANE) // 2 else RET_QK_DIM ** -0.5
                o_ref[:, j * COL + tt * LANE:j * COL + (tt + 1) * LANE] = (rot * scale).astype(o_ref.dtype)
        elif j == CB_RG:
            o_ref[:, cols] = (acc * _sigmoid(acc)).astype(o_ref.dtype)
        else:
            o_ref[:, cols] = _sigmoid(acc).astype(o_ref.dtype)


def _inproj(xa, xb, g1, w_in_b, perm, tabs, nvg, qg, kg, n_prompt_tiles, tiles_per_seq):
    T = (n_prompt_tiles + 1) * TILE
    cosb, sinb, cosc, sinc = tabs
    n_g = len(SWA_PATTERNS)
    row = lambda i: (i, 0)
    const = lambda i: (0, 0)
    tab_blk = lambda i: jnp.where(i < n_prompt_tiles, i % tiles_per_seq, tiles_per_seq)
    tab3 = pl.BlockSpec((n_g, TILE, LANE), lambda i: (0, tab_blk(i), 0))
    tab = pl.BlockSpec((TILE, LANE), lambda i: (tab_blk(i), 0))
    vmem = (2 * _nbytes((TILE, D_MODEL), F32) + n_g * _nbytes((TILE, D_MODEL), BF16)
            + _nbytes(w_in_b.shape, BF16) + 2 * _nbytes((TILE, IN_WIDTH), BF16)
            + 2 * (2 * n_g + 2) * _nbytes((TILE, LANE), F32) + 2 * _nbytes(perm.shape, BF16)
            + 8 * _nbytes((TILE, COL), F32))
    return pl.pallas_call(
        functools.partial(_inproj_kernel, n_prompt_tiles=n_prompt_tiles),
        grid=(T // TILE,),
        in_specs=_x_specs(xa, xb) + [
            pl.BlockSpec((1, D_MODEL), const),
            pl.BlockSpec(w_in_b.shape, const, pipeline_mode=pl.Buffered(1)),
            pl.BlockSpec(perm.shape, lambda i: (0, 0, 0)),
            tab3, tab3, tab, tab,
            pl.BlockSpec((1, COL), const),
            pl.BlockSpec((1, LANE), const),
            pl.BlockSpec((1, LANE), const),
        ],
        out_specs=pl.BlockSpec((TILE, IN_WIDTH), row),
        out_shape=jax.ShapeDtypeStruct((T, IN_WIDTH), BF16),
        scratch_shapes=[pltpu.VMEM((n_g, TILE, D_MODEL), BF16)],
        compiler_params=_params(vmem, ("parallel",)),
        name="inproj",
    )(xa, xb, g1, w_in_b, perm, cosb, sinb, cosc, sinc, nvg, qg, kg)


def _mixa_kernel(u_ref, v_ref, w_ref, b_ref, o_ref, *, cps, n_prompt_chunks):
    i = pl.program_id(0)
    for c in range(cps):
        var = ((i * cps + c) >= n_prompt_chunks).astype(jnp.int32)
        rows = slice(c * CHUNK, (c + 1) * CHUNK)
        for g in range(A_GROUPS):
            cols = slice(g * LANE, (g + 1) * LANE)
            z = _dot(w_ref[var, g], v_ref[rows, cols]) + b_ref[var, :, cols]
            o_ref[rows, cols] = (u_ref[rows, cols].astype(F32) * z).astype(o_ref.dtype)


def _mixer_a(P, w2, b2, n_prompt_chunks):
    T = P.shape[0]
    n_chunks = T // CHUNK
    cps = _pick(n_chunks, (8, 6, 4, 3, 2, 1))
    rows = cps * CHUNK
    vmem = 6 * _nbytes((rows, COL), BF16) + 2 * _nbytes(w2.shape, BF16) + 2 * _nbytes(b2.shape, F32)
    return pl.pallas_call(
        functools.partial(_mixa_kernel, cps=cps, n_prompt_chunks=n_prompt_chunks),
        grid=(n_chunks // cps,),
        in_specs=[
            pl.BlockSpec((rows, COL), lambda i: (i, CB_AU)),
            pl.BlockSpec((rows, COL), lambda i: (i, CB_AV)),
            pl.BlockSpec(w2.shape, lambda i: (0, 0, 0, 0)),
            pl.BlockSpec(b2.shape, lambda i: (0, 0, 0)),
        ],
        out_specs=pl.BlockSpec((rows, COL), lambda i: (i, 0)),
        out_shape=jax.ShapeDtypeStruct((T, COL), BF16),
        compiler_params=_params(vmem, ("parallel",)),
        name="mixer_a",
    )(P, P, w2, b2)


def _swa_kernel(q_ref, kp_ref, kc_ref, vp_ref, vc_ref, o_ref, l_ref, k_ref, v_ref, s_ref, p_ref, *, qb):
    i = pl.program_id(2)
    rows = qb * CHUNK
    lead = q_ref.shape[:-1]
    k_ref[0:CHUNK] = kp_ref[...].reshape(CHUNK, COL)
    k_ref[CHUNK:CHUNK + rows] = kc_ref[...].reshape(rows, COL)
    v_ref[0:CHUNK] = vp_ref[...].reshape(CHUNK, COL)
    v_ref[CHUNK:CHUNK + rows] = vc_ref[...].reshape(rows, COL)
    row = lax.broadcasted_iota(jnp.int32, (CHUNK, 2 * CHUNK), 0)
    col = lax.broadcasted_iota(jnp.int32, (CHUNK, 2 * CHUNK), 1)
    mask_cur = (col >= CHUNK) & (col - CHUNK <= row)
    mask_all = mask_cur | ((col < CHUNK) & (col >= row))
    mask_first = mask_cur | ((col < CHUNK) & (col >= row) & (i > 0))
    q_all = q_ref[...].reshape(rows, COL)
    for j in range(qb):
        mask = mask_first if j == 0 else mask_all
        for h in range(SWA_HEADS):
            cs = slice(h * LANE, (h + 1) * LANE)
            sc = _dot_nt(q_all[j * CHUNK:(j + 1) * CHUNK, cs], k_ref[j * CHUNK:(j + 2) * CHUNK, cs])
            s_ref[j * SWA_HEADS + h] = jnp.where(mask, sc, NEG_INF)
    s = s_ref[...]
    m = jnp.max(s, axis=-1, keepdims=True)
    p = jnp.exp(s - m)
    den = jnp.sum(p, axis=-1, keepdims=True)
    p_ref[...] = p.astype(BF16)
    lse = m + jnp.log(den)
    for h in range(SWA_HEADS):
        cs = slice(h * LANE, (h + 1) * LANE)
        o_h = [_dot(p_ref[j * SWA_HEADS + h], v_ref[j * CHUNK:(j + 2) * CHUNK, cs]) / den[j * SWA_HEADS + h]
               for j in range(qb)]
        o_ref[..., cs] = jnp.concatenate(o_h, axis=0).astype(o_ref.dtype).reshape(lead + (LANE,))
    packed = [_pack_heads([lse[j * SWA_HEADS + h] for h in range(SWA_HEADS)]) for j in range(qb)]
    l_ref[...] = jnp.concatenate(packed, axis=0).reshape(lead + (LANE,))


def _swa_prompt(P, g, dil, n_p, s):
    T = P.shape[0]
    Tp = n_p * s
    nb = s // dil // CHUNK
    qb = _pick(nb, (8, 4, 2, 1))
    steps = nb // qb
    prev_blk = lambda b, i: b * nb + jnp.maximum(i * qb - 1, 0)
    if dil == 1:
        src = P
        lead_q, lead_p = (qb * CHUNK,), (CHUNK,)
        o_shape = (Tp, COL)
        l_shape = (Tp, LANE)
        q_map = lambda cb: (lambda b, r, i: (b * steps + i, cb + g))
        p_map = lambda cb: (lambda b, r, i: (prev_blk(b, i), cb + g))
        o_map = lambda b, r, i: (b * steps + i, 0)
        l_map = lambda b, r, i: (b * steps + i, 0)
    else:
        rpt = TILE // dil
        tpb = CHUNK // rpt
        src = P.reshape(T // TILE, dil, rpt, IN_WIDTH)
        lead_q, lead_p = (qb * tpb, None, rpt), (tpb, None, rpt)
        o_shape = (Tp // TILE, dil, rpt, COL)
        l_shape = (Tp // TILE, dil, rpt, LANE)
        q_map = lambda cb: (lambda b, r, i: (b * steps + i, r, 0, cb + g))
        p_map = lambda cb: (lambda b, r, i: (prev_blk(b, i), r, 0, cb + g))
        o_map = lambda b, r, i: (b * steps + i, r, 0, 0)
        l_map = lambda b, r, i: (b * steps + i, r, 0, 0)

    q_spec = lambda cb: pl.BlockSpec(lead_q + (COL,), q_map(cb))
    p_spec = lambda cb: pl.BlockSpec(lead_p + (COL,), p_map(cb))
    n_pairs = qb * SWA_HEADS
    vmem = ((10 * qb + 6) * _nbytes((CHUNK, COL), BF16) + 2 * qb * _nbytes((CHUNK, COL), F32)
            + 6 * n_pairs * _nbytes((CHUNK, 2 * CHUNK), F32))
    o, l = pl.pallas_call(
        functools.partial(_swa_kernel, qb=qb),
        grid=(n_p, dil, steps),
        in_specs=[q_spec(CB_Q), p_spec(CB_K), q_spec(CB_K), p_spec(CB_V), q_spec(CB_V)],
        out_specs=[pl.BlockSpec(lead_q + (COL,), o_map),
                   pl.BlockSpec(lead_q + (LANE,), l_map)],
        out_shape=[jax.ShapeDtypeStruct(o_shape, BF16), jax.ShapeDtypeStruct(l_shape, F32)],
        scratch_shapes=[pltpu.VMEM(((qb + 1) * CHUNK, COL), BF16), pltpu.VMEM(((qb + 1) * CHUNK, COL), BF16),
                        pltpu.VMEM((n_pairs, CHUNK, 2 * CHUNK), F32),
                        pltpu.VMEM((n_pairs, CHUNK, 2 * CHUNK), BF16)],
        compiler_params=_params(vmem, ("parallel", "parallel", "arbitrary")),
        name=f"swa_prompt_g{g}",
    )(src, src, src, src, src)
    return o.reshape(Tp, COL), l.reshape(Tp, LANE)


def _swa_sample_kernel(q_ref, kn_ref, vn_ref, cache_ref, o_ref, l_ref, *, dil, lbuf, t_s):
    nq = SWA_HEADS * t_s
    q = q_ref[...]
    qrep = jnp.concatenate([q] * SWA_HEADS, axis=0)
    rq = lax.broadcasted_iota(jnp.int32, (nq, COL), 0)
    cq = lax.broadcasted_iota(jnp.int32, (nq, COL), 1)
    qbd = jnp.where(_idiv(rq, t_s) == _idiv(cq, LANE), qrep, 0.0).astype(BF16)

    per_pos = 2 * SWA_HEADS
    kc = jnp.concatenate([cache_ref[pl.ds(h, lbuf, stride=per_pos), :].astype(BF16)
                          for h in range(SWA_HEADS)], axis=1)
    vc = jnp.concatenate([cache_ref[pl.ds(SWA_HEADS + h, lbuf, stride=per_pos), :].astype(BF16)
                          for h in range(SWA_HEADS)], axis=1)
    kn = kn_ref[...].astype(BF16)
    vn = vn_ref[...].astype(BF16)

    s_c = _dot_nt(qbd, kc)
    s_n = _dot_nt(qbd, kn)
    t_c = _imod(lax.broadcasted_iota(jnp.int32, (nq, lbuf), 0), t_s)
    c_c = lax.broadcasted_iota(jnp.int32, (nq, lbuf), 1)
    diff_c = lbuf + t_c - c_c
    ok_c = (_imod(diff_c, dil) == 0) & (diff_c <= lbuf)
    t_n = _imod(lax.broadcasted_iota(jnp.int32, (nq, t_s), 0), t_s)
    c_n = lax.broadcasted_iota(jnp.int32, (nq, t_s), 1)
    diff_n = t_n - c_n
    ok_n = (diff_n >= 0) & (_imod(diff_n, dil) == 0)
    s_c = jnp.where(ok_c, s_c, NEG_INF)
    s_n = jnp.where(ok_n, s_n, NEG_INF)
    m = jnp.maximum(jnp.max(s_c, axis=-1, keepdims=True), jnp.max(s_n, axis=-1, keepdims=True))
    p_c = jnp.exp(s_c - m)
    p_n = jnp.exp(s_n - m)
    den = jnp.sum(p_c, axis=-1, keepdims=True) + jnp.sum(p_n, axis=-1, keepdims=True)
    o_all = (_dot(p_c.astype(BF16), vc) + _dot(p_n.astype(BF16), vn)) / den
    lse = m + jnp.log(den)
    for h in range(SWA_HEADS):
        cs = slice(h * LANE, (h + 1) * LANE)
        o_ref[:, cs] = o_all[h * t_s:(h + 1) * t_s, cs]
    l_ref[...] = _pack_heads([lse[h * t_s:(h + 1) * t_s] for h in range(SWA_HEADS)])


def _swa_sample(Ps, cache, layer, g, dil):
    n_s, t_s, _ = Ps.shape
    rows = cache.shape[2]
    lbuf = rows // (2 * SWA_HEADS)
    assert lbuf == dil * CHUNK, "window buffer must hold exactly one full window"
    blk = lambda cb: pl.BlockSpec((None, t_s, COL), lambda b: (b, 0, cb + g))
    vmem = (2 * _nbytes((lbuf, 2 * COL), F32) + 2 * _nbytes((lbuf, 2 * COL), BF16)
            + 8 * _nbytes((SWA_HEADS * t_s, lbuf), F32))
    return pl.pallas_call(
        functools.partial(_swa_sample_kernel, dil=dil, lbuf=lbuf, t_s=t_s),
        grid=(n_s,),
        in_specs=[blk(CB_Q), blk(CB_K), blk(CB_V),
                  pl.BlockSpec((None, None, rows, LANE), lambda b: (layer, b, 0, 0))],
        out_specs=[pl.BlockSpec((None, t_s, COL), lambda b: (b, 0, 0)),
                   pl.BlockSpec((None, t_s, LANE), lambda b: (b, 0, 0))],
        out_shape=[jax.ShapeDtypeStruct((n_s, t_s, COL), F32),
                   jax.ShapeDtypeStruct((n_s, t_s, LANE), F32)],
        compiler_params=_params(vmem, ("parallel",)),
        name=f"swa_sample_g{g}",
    )(Ps, Ps, Ps, cache)


def _ret_head_inputs(qk_ref, v_ref, kout_ref, h):
    pair, half = h // 2, h % 2
    lane = lax.broadcasted_iota(jnp.int32, (CHUNK, LANE), 1)
    head_lanes = _idiv(lane, RET_QK_DIM) == half
    qt = qk_ref[:, pair * LANE:(pair + 1) * LANE]
    kt = qk_ref[:, COL // 2 + pair * LANE:COL // 2 + (pair + 1) * LANE]
    qm = jnp.where(head_lanes, qt, jnp.zeros_like(qt))
    kw = jnp.where(head_lanes, kt.astype(F32) * kout_ref[:, pair * LANE:(pair + 1) * LANE], 0.0)
    vh = v_ref[:, h * LANE:(h + 1) * LANE]
    return qm, kt, kw, vh


def _ret_finish(o, gate_ref, o_ref, h):
    cs = slice(h * LANE, (h + 1) * LANE)
    o_ref[:, cs] = (gate_ref[:, cs].astype(F32) * _rms(o)).astype(o_ref.dtype)


def _ret_prompt_kernel(*refs, n_seq):
    ins = refs[:3 * n_seq]
    decay_ref, qin_ref, kout_ref, gc_ref, o_ref, s_out_ref, s_ref, att_ref = refs[3 * n_seq:]
    i = pl.program_id(0)

    @pl.when(i == 0)
    def _():
        s_ref[...] = jnp.zeros_like(s_ref)

    pairs = [(b, h) for b in range(n_seq) for h in range(RET_HEADS)]
    head_in = lambda b, h: _ret_head_inputs(ins[3 * b], ins[3 * b + 1], kout_ref, h)
    for b, h in pairs:
        qm, kt, _, _ = head_in(b, h)
        att_ref[b, h] = (_dot_nt(qm, kt) * decay_ref[h]).astype(BF16)
    for b, h in pairs:
        qm, _, _, vh = head_in(b, h)
        o = (_dot(att_ref[b, h], vh)
             + _dot(qm, s_ref[b, h].astype(BF16)) * qin_ref[:, h * LANE:(h + 1) * LANE])
        _ret_finish(o, ins[3 * b + 2], o_ref.at[b], h)
    for b, h in pairs:
        _, _, kw, vh = head_in(b, h)
        s_ref[b, h] = s_ref[b, h] * gc_ref[h:h + 1, :] + _dot(kw.T.astype(BF16), vh)

    @pl.when(i == pl.num_programs(0) - 1)
    def _():
        for b in range(n_seq):
            for h in range(RET_HEADS):
                lo = (h % 2) * RET_QK_DIM
                s_out_ref[b, h] = s_ref[b, h, lo:lo + RET_QK_DIM, :]


def _ret_prompt(P, rt, n_p, s):
    nblk = s // CHUNK
    decay, qin, kout, gc = rt
    blk = lambda b, cb: pl.BlockSpec((CHUNK, COL), lambda i: (b * nblk + i, cb))
    const2 = lambda i: (0, 0)
    seq_specs = [blk(b, cb) for b in range(n_p) for cb in (CB_RQK, CB_RV, CB_RG)]
    st_shape = (n_p, RET_HEADS, RET_QK_DIM, RET_V_DIM)
    vmem = (8 * n_p * _nbytes((CHUNK, COL), BF16) + 2 * _nbytes(decay.shape, F32) + 4 * _nbytes(qin.shape, F32)
            + 3 * n_p * _nbytes((RET_HEADS, LANE, LANE), F32))
    o, st = pl.pallas_call(
        functools.partial(_ret_prompt_kernel, n_seq=n_p),
        grid=(nblk,),
        in_specs=seq_specs + [pl.BlockSpec(decay.shape, lambda i: (0, 0, 0)),
                              pl.BlockSpec(qin.shape, const2), pl.BlockSpec(kout.shape, const2),
                              pl.BlockSpec(gc.shape, const2)],
        out_specs=[pl.BlockSpec((n_p, CHUNK, COL), lambda i: (0, i, 0)),
                   pl.BlockSpec(st_shape, lambda i: (0, 0, 0, 0))],
        out_shape=[jax.ShapeDtypeStruct((n_p, s, COL), BF16), jax.ShapeDtypeStruct(st_shape, F32)],
        scratch_shapes=[pltpu.VMEM((n_p, RET_HEADS, LANE, LANE), F32),
                        pltpu.VMEM((n_p, RET_HEADS, CHUNK, CHUNK), BF16)],
        compiler_params=_params(vmem, ("arbitrary",)),
        name="ret_prompt",
    )(*([P] * (3 * n_p)), decay, qin, kout, gc)
    return o.reshape(n_p * s, COL), st


def _ret_sample_kernel(qk_ref, v_ref, gate_ref, s0_ref, decay_ref, qin_ref, kout_ref, gc_ref,
                       o_ref, s_out_ref, *, t_s):
    row = lax.broadcasted_iota(jnp.int32, (CHUNK, LANE), 0)
    for h in range(RET_HEADS):
        lo = (h % 2) * RET_QK_DIM
        qm, kt, kw, vh = _ret_head_inputs(qk_ref, v_ref, kout_ref, h)
        att = _dot_nt(qm, kt) * decay_ref[h]
        o = _dot(att.astype(BF16), vh)
        inter = jnp.zeros((CHUNK, LANE), F32)
        for sq in range(CHUNK // t_s):
            seq_rows = _idiv(row, t_s) == sq
            st = s0_ref[sq, h]
            st2 = jnp.concatenate([st, st], axis=0).astype(BF16)
            inter = jnp.where(seq_rows, _dot(qm, st2), inter)
            upd = _dot(jnp.where(seq_rows, kw, 0.0).T.astype(BF16), vh)
            s_out_ref[sq, h] = st * gc_ref[h:h + 1, :] + upd[lo:lo + RET_QK_DIM, :]
        o = o + inter * qin_ref[:, h * LANE:(h + 1) * LANE]
        _ret_finish(o, gate_ref, o_ref, h)


def _ret_sample(P, state, layer, rt, Tp, t_s):
    n_s = state.shape[1]
    spb = CHUNK // t_s
    base = Tp // CHUNK
    decay, qin, kout, gc = rt
    blk = lambda cb: pl.BlockSpec((CHUNK, COL), lambda i: (base + i, cb))
    const2 = lambda i: (0, 0)
    st_shape = (spb, RET_HEADS, RET_QK_DIM, RET_V_DIM)
    vmem = (8 * _nbytes((CHUNK, COL), BF16) + 2 * _nbytes(decay.shape, F32) + 4 * _nbytes(qin.shape, F32)
            + 4 * _nbytes(st_shape, F32))
    return pl.pallas_call(
        functools.partial(_ret_sample_kernel, t_s=t_s),
        grid=(n_s // spb,),
        in_specs=[blk(CB_RQK), blk(CB_RV), blk(CB_RG),
                  pl.BlockSpec((None,) + st_shape, lambda i: (layer, i, 0, 0, 0)),
                  pl.BlockSpec(decay.shape, lambda i: (0, 0, 0)),
                  pl.BlockSpec(qin.shape, const2), pl.BlockSpec(kout.shape, const2),
                  pl.BlockSpec(gc.shape, const2)],
        out_specs=[pl.BlockSpec((CHUNK, COL), lambda i: (i, 0)),
                   pl.BlockSpec(st_shape, lambda i: (i, 0, 0, 0))],
        out_shape=[jax.ShapeDtypeStruct((n_s * t_s, COL), BF16),
                   jax.ShapeDtypeStruct((n_s, RET_HEADS, RET_QK_DIM, RET_V_DIM), F32)],
        compiler_params=_params(vmem, ("parallel",)),
        name="ret_sample",
    )(P, P, P, state, decay, qin, kout, gc)


def _branch_kernel(xa_ref, xb_ref, a_ref, g0_ref, g1_ref, g2_ref, wb_ref, wo_ref, ex_ref,
                   po0, po1, po2, pl0, pl1, pl2, pc, so0, so1, so2, sl0, sl1, sl2, sc,
                   y_ref, on_ref, ln_ref, *, n_prompt_tiles):
    i = pl.program_id(0)

    def spread(packed):
        hi = packed.astype(BF16)
        lo = (packed - hi.astype(F32)).astype(BF16)
        return _dot(jnp.concatenate([hi, lo], axis=1), ex_ref[...])

    def natural(o_ref, l_ref, k, dil):
        if dil == 1:
            return (lambda h: o_ref[:, h * LANE:(h + 1) * LANE].astype(F32)), l_ref[...]
        rpt = TILE // dil
        for r in range(dil):
            rows = slice(r * rpt, (r + 1) * rpt)
            ln_ref[k, pl.ds(r, rpt, stride=dil), :] = l_ref[rows, :]
            for h in range(SWA_HEADS):
                on_ref[k, h, pl.ds(r, rpt, stride=dil), :] = o_ref[rows, h * LANE:(h + 1) * LANE].astype(F32)
        return (lambda h: on_ref[k, h]), ln_ref[k]

    def body(o_refs, l_refs, c_ref, permuted):
        groups = [natural(o_refs[k], l_refs[k], k, SWA_PATTERNS[k][1] if permuted else 1)
                  for k in range(len(SWA_PATTERNS))]
        ls = [l for _, l in groups]
        lmax = functools.reduce(jnp.maximum, ls)
        es = [jnp.exp(l - lmax) for l in ls]
        inv = 1.0 / sum(es)
        weights = [spread(e * inv) for e in es]
        heads = []
        for h in range(SWA_HEADS):
            cs = slice(h * LANE, (h + 1) * LANE)
            heads.append(sum(w[:, cs] * go(h) for w, (go, _) in zip(weights, groups)).astype(BF16))
        mix = jnp.concatenate(heads, axis=1)
        merged = (g0_ref[...].astype(F32) * _dot(a_ref[...], wb_ref[0])
                  + g1_ref[...].astype(F32) * _dot(mix, wb_ref[1])
                  + g2_ref[...].astype(F32) * _dot(c_ref[...], wb_ref[2]))
        y_ref[...] = _tile_rows(i, n_prompt_tiles, xa_ref, xb_ref) + _dot(merged.astype(BF16), wo_ref[...])

    @pl.when(i < n_prompt_tiles)
    def _():
        body((po0, po1, po2), (pl0, pl1, pl2), pc, True)

    @pl.when(i >= n_prompt_tiles)
    def _():
        body((so0, so1, so2), (sl0, sl1, sl2), sc, False)


def _branch(xa, xb, out_a, P, wb, wo, prompt_set, sample_set, n_prompt_tiles):
    T = (n_prompt_tiles + 1) * TILE
    assert sample_set[0].shape[0] == TILE, "the sample rows must form exactly one tile"
    n_g = len(SWA_PATTERNS)
    last = n_prompt_tiles - 1
    row = lambda i: (i, 0)
    gate = lambda k: pl.BlockSpec((TILE, D_MODEL), lambda i: (i, CB_GATE * COL // D_MODEL + k))
    once = pl.Buffered(1)
    p_o = pl.BlockSpec((TILE, COL), lambda i: (jnp.minimum(i, last), 0))
    p_l = pl.BlockSpec((TILE, LANE), lambda i: (jnp.minimum(i, last), 0))
    s_o = pl.BlockSpec((TILE, COL), lambda i: (0, 0))
    s_l = pl.BlockSpec((TILE, LANE), lambda i: (0, 0))
    src = jnp.arange(2 * LANE) % LANE
    dst_head = jnp.arange(COL) // LANE
    expand = (src[:, None] == dst_head[None, :] * LSE_LANES).astype(BF16)
    vmem = (4 * _nbytes((TILE, D_MODEL), F32) + 6 * _nbytes((TILE, D_MODEL), BF16)
            + 2 * (2 * n_g + 2) * (_nbytes((TILE, COL), BF16) + _nbytes((TILE, COL), F32))
            + _nbytes(wb.shape, BF16) + _nbytes(wo.shape, BF16)
            + 4 * n_g * _nbytes((TILE, COL), F32) + 4 * _nbytes((TILE, D_MODEL), F32))
    return pl.pallas_call(
        functools.partial(_branch_kernel, n_prompt_tiles=n_prompt_tiles),
        grid=(T // TILE,),
        in_specs=_x_specs(xa, xb) + [
                  pl.BlockSpec((TILE, COL), row),
                  gate(0), gate(1), gate(2),
                  pl.BlockSpec(wb.shape, lambda i: (0, 0, 0), pipeline_mode=once),
                  pl.BlockSpec(wo.shape, lambda i: (0, 0), pipeline_mode=once),
                  pl.BlockSpec(expand.shape, lambda i: (0, 0)),
                  p_o, p_o, p_o, p_l, p_l, p_l, p_o,
                  s_o, s_o, s_o, s_l, s_l, s_l, s_o],
        out_specs=pl.BlockSpec((TILE, D_MODEL), row),
        out_shape=jax.ShapeDtypeStruct((T, D_MODEL), F32),
        scratch_shapes=[pltpu.VMEM((n_g, SWA_HEADS, TILE, LANE), F32),
                        pltpu.VMEM((n_g, TILE, LANE), F32)],
        compiler_params=_params(vmem, ("parallel",)),
        name="branch_merge",
    )(xa, xb, out_a, P, P, P, wb, wo, expand, *prompt_set, *sample_set)


def _swiglu_into(hb, w1_ref, w3_ref, w2_ref, acc_ref, fc):
    n_fc = w1_ref.shape[-1] // fc
    for f in range(n_fc):
        cols = slice(f * fc, (f + 1) * fc)
        a = _dot(hb, w1_ref[:, cols])
        b = _dot(hb, w3_ref[:, cols])
        part = _dot((a * _sigmoid(a) * b).astype(BF16), w2_ref[cols, :])
        if f == 0:
            acc_ref[...] = part
        else:
            acc_ref[...] += part


def _dense_ffn_kernel(x_ref, g2_ref, w1_ref, w3_ref, w2_ref, y_ref, acc_ref, *, fc):
    x = x_ref[...]
    hb = (_rms(x) * g2_ref[...]).astype(BF16)
    _swiglu_into(hb, w1_ref, w3_ref, w2_ref, acc_ref, fc)
    y_ref[...] = x + acc_ref[...]


def _dense_ffn(x, g2, w1b, w3b, w2b):
    T = x.shape[0]
    F = w1b.shape[1]
    tm = _pick(T, (768, 512, 384, 256, 128))
    fc = _pick(F, (512, 256, 128))
    row = lambda i: (i, 0)
    once = pl.Buffered(1)
    vmem = (5 * _nbytes((tm, D_MODEL), F32) + 3 * _nbytes(w1b.shape, BF16) + _nbytes((tm, D_MODEL), BF16)
            + 4 * _nbytes((tm, fc), F32))
    return pl.pallas_call(
        functools.partial(_dense_ffn_kernel, fc=fc),
        grid=(T // tm,),
        in_specs=[pl.BlockSpec((tm, D_MODEL), row),
                  pl.BlockSpec((1, D_MODEL), lambda i: (0, 0)),
                  pl.BlockSpec(w1b.shape, lambda i: (0, 0), pipeline_mode=once),
                  pl.BlockSpec(w3b.shape, lambda i: (0, 0), pipeline_mode=once),
                  pl.BlockSpec(w2b.shape, lambda i: (0, 0), pipeline_mode=once)],
        out_specs=pl.BlockSpec((tm, D_MODEL), row),
        out_shape=jax.ShapeDtypeStruct((T, D_MODEL), F32),
        scratch_shapes=[pltpu.VMEM((tm, D_MODEL), F32)],
        compiler_params=_params(vmem, ("parallel",)),
        name="dense_ffn",
    )(x, g2, w1b, w3b, w2b)


def _router_kernel(x_ref, g2_ref, wr_ref, e0_ref, e1_ref, w0_ref, w1_ref):
    hb = (_rms(x_ref[...]) * g2_ref[...]).astype(BF16)
    logits = _dot(hb, wr_ref[...])
    lane = lax.broadcasted_iota(jnp.int32, logits.shape, 1)
    logits = jnp.where(lane < N_EXPERTS, logits, -jnp.inf)
    lane_f = lane.astype(F32)
    m1 = jnp.max(logits, axis=-1, keepdims=True)
    i1 = jnp.min(jnp.where(logits == m1, lane_f, float(LANE)), axis=-1, keepdims=True)
    rest = jnp.where(lane_f == i1, -jnp.inf, logits)
    m2 = jnp.max(rest, axis=-1, keepdims=True)
    i2 = jnp.min(jnp.where(rest == m2, lane_f, float(LANE)), axis=-1, keepdims=True)
    e = jnp.exp(m2 - m1)
    e0_ref[...] = jnp.broadcast_to(i1, logits.shape).astype(jnp.int32)
    e1_ref[...] = jnp.broadcast_to(i2, logits.shape).astype(jnp.int32)
    w0_ref[...] = jnp.broadcast_to(1.0 / (1.0 + e), logits.shape)
    w1_ref[...] = jnp.broadcast_to(e / (1.0 + e), logits.shape)


def _router(x, g2, wr_pad):
    T = x.shape[0]
    tm = _pick(T, (768, 512, 384, 256, 128))
    row = lambda i: (i, 0)
    out = pl.BlockSpec((tm, LANE), row)
    vmem = 4 * _nbytes((tm, D_MODEL), F32) + 2 * _nbytes(wr_pad.shape, BF16) + 16 * _nbytes((tm, LANE), F32)
    return pl.pallas_call(
        _router_kernel,
        grid=(T // tm,),
        in_specs=[pl.BlockSpec((tm, D_MODEL), row),
                  pl.BlockSpec((1, D_MODEL), lambda i: (0, 0)),
                  pl.BlockSpec(wr_pad.shape, lambda i: (0, 0))],
        out_specs=[out, out, out, out],
        out_shape=[jax.ShapeDtypeStruct((T, LANE), jnp.int32), jax.ShapeDtypeStruct((T, LANE), jnp.int32),
                   jax.ShapeDtypeStruct((T, LANE), F32), jax.ShapeDtypeStruct((T, LANE), F32)],
        compiler_params=_params(vmem, ("parallel",)),
        name="moe_router",
    )(x, g2, wr_pad)


RUN_ALIGN = 8


def _run_pieces(count, max_rows):
    pieces, off = [], 0
    size = max_rows
    while size >= RUN_ALIGN:
        active = (count & size) != 0
        pieces.append((size, off, active))
        off = off + jnp.where(active, size, 0)
        size //= 2
    return pieces


def _packed_rows(i, e0, e1, lower_ref, loc_ref):
    lane = lax.broadcasted_iota(jnp.int32, e0.shape, 1)
    picks = jnp.where((lane == e0) | (lane == e1), 1.0, 0.0)
    rank = _dot(lower_ref[...], picks.astype(BF16))
    lane1 = lax.broadcasted_iota(jnp.int32, (1, LANE), 1)
    run_start = jnp.zeros((1, LANE), F32)
    for e in range(N_EXPERTS):
        run_start = jnp.where(lane1 == e, loc_ref[i * N_EXPERTS + e].astype(F32), run_start)
    row_of = rank + run_start
    return tuple(jnp.sum(jnp.where(lane == ek, row_of, 0.0), axis=-1, keepdims=True) for ek in (e0, e1))


RUN_ROWS = 640


def _dispatch_kernel(base_ref, cnt_ref, loc_ref, pend_ref, x_ref, g2_ref, e0_ref, e1_ref, lower_ref, xs_ref,
                     comp_ref, sem, *, tme, n_tiles):
    i = pl.program_id(0)
    n_steps = pl.num_programs(0)
    cur = i % 2
    half = tme // 2

    @pl.when(i == 0)
    def _():
        comp_ref[0, 0:half] = jnp.zeros((half, D_MODEL), comp_ref.dtype)

        def zero_tile(start):
            return [pltpu.make_async_copy(
                        comp_ref.at[0, pl.ds(0, half), :],
                        xs_ref.at[pl.ds(pl.multiple_of(start + k * half, half), half), :], sem.at[0])
                    for k in range(2)]

        def tail_loop(fn):
            def body(t, carry):
                for c in zero_tile(t * tme):
                    fn(c)
                return carry
            lax.fori_loop(pend_ref[N_EXPERTS - 1] // tme, n_tiles, body, 0)

        def group_tails(fn):
            for e in range(N_EXPERTS):
                prev_end = pend_ref[e - 1] if e else 0

                @pl.when(pend_ref[e] > prev_end)
                def _():
                    for c in zero_tile(pend_ref[e] - tme):
                        fn(c)

        group_tails(lambda c: c.start())
        tail_loop(lambda c: c.start())
        group_tails(lambda c: c.wait())
        tail_loop(lambda c: c.wait())

    h = (_rms(x_ref[...]) * g2_ref[...]).astype(BF16)
    pos0, pos1 = _packed_rows(i, e0_ref[...], e1_ref[...], lower_ref, loc_ref)
    slot = lax.broadcasted_iota(jnp.int32, (TILE, RUN_ROWS), 1).astype(F32)
    hit = jnp.where((slot == pos0) | (slot == pos1), 1.0, 0.0)
    comp_ref[cur] = _dot(hit.T.astype(BF16), h)

    def for_copies(tile, buf, fn):
        for e in range(N_EXPERTS):
            k = tile * N_EXPERTS + e
            cnt, dst, loc = cnt_ref[k], base_ref[k], loc_ref[k]
            for size, off, active in _run_pieces(cnt, TILE):
                c = pltpu.make_async_copy(
                    comp_ref.at[buf, pl.ds(pl.multiple_of(loc + off, RUN_ALIGN), size), :],
                    xs_ref.at[pl.ds(pl.multiple_of(dst + off, RUN_ALIGN), size), :], sem.at[buf])
                pl.when(active)(functools.partial(fn, c))

    for_copies(i, cur, lambda c: c.start())

    @pl.when(i > 0)
    def _():
        for_copies(i - 1, 1 - cur, lambda c: c.wait())

    @pl.when(i == n_steps - 1)
    def _():
        for_copies(i, cur, lambda c: c.wait())


def _dispatch(x, g2, e0, e1, base, cnt, loc, pend, lower, n_tiles, tme):
    T = x.shape[0]
    assert tme == 2 * TILE
    row = lambda i, *_: (i, 0)
    assert RUN_ROWS >= 2 * TILE + N_EXPERTS * (RUN_ALIGN - 1) and RUN_ROWS % LANE == 0
    vmem = (3 * _nbytes((TILE, D_MODEL), F32) + 2 * _nbytes((RUN_ROWS, D_MODEL), F32)
            + 8 * _nbytes((RUN_ROWS, TILE), F32))
    grid_spec = pltpu.PrefetchScalarGridSpec(
        num_scalar_prefetch=4,
        grid=(T // TILE,),
        in_specs=[pl.BlockSpec((TILE, D_MODEL), row),
                  pl.BlockSpec((1, D_MODEL), lambda i, *_: (0, 0)),
                  pl.BlockSpec((TILE, LANE), row), pl.BlockSpec((TILE, LANE), row),
                  pl.BlockSpec((TILE, TILE), lambda i, *_: (0, 0))],
        out_specs=pl.BlockSpec(memory_space=pl.ANY),
        scratch_shapes=[pltpu.VMEM((2, RUN_ROWS, D_MODEL), F32), pltpu.SemaphoreType.DMA((2,))],
    )
    return pl.pallas_call(
        functools.partial(_dispatch_kernel, tme=tme, n_tiles=n_tiles),
        grid_spec=grid_spec,
        out_shape=jax.ShapeDtypeStruct((n_tiles * tme, D_MODEL), F32),
        compiler_params=_params(vmem, ("arbitrary",)),
        name="moe_dispatch",
    )(base, cnt, loc, pend, x, g2, e0, e1, lower)


def _expert_ffn_kernel(te_ref, used_ref, xs_ref, w1_ref, w3_ref, w2_ref, y_ref, acc_ref, *, fc):
    i = pl.program_id(0)

    @pl.when(i < used_ref[0])
    def _():
        _swiglu_into(xs_ref[...].astype(BF16), w1_ref, w3_ref, w2_ref, acc_ref, fc)
        y_ref[...] = acc_ref[...]

    @pl.when(i >= used_ref[0])
    def _():
        y_ref[...] = jnp.zeros_like(y_ref)


def _expert_ffn(xs, tile_expert, n_used, we1b, we3b, we2b, tme):
    n_rows = xs.shape[0]
    F = we1b.shape[2]
    fc = _pick(F, (512, 256, 128))
    once = pl.Buffered(1)
    wspec = lambda shape: pl.BlockSpec((None,) + shape, lambda i, te, nu: (te[i], 0, 0), pipeline_mode=once)
    vmem = (5 * _nbytes((tme, D_MODEL), F32) + 3 * _nbytes(we1b.shape[1:], BF16) + _nbytes((tme, D_MODEL), BF16)
            + 4 * _nbytes((tme, fc), F32))
    grid_spec = pltpu.PrefetchScalarGridSpec(
        num_scalar_prefetch=2,
        grid=(n_rows // tme,),
        in_specs=[pl.BlockSpec((tme, D_MODEL), lambda i, te, nu: (i, 0)),
                  wspec(we1b.shape[1:]), wspec(we3b.shape[1:]), wspec(we2b.shape[1:])],
        out_specs=pl.BlockSpec((tme, D_MODEL), lambda i, te, nu: (i, 0)),
        scratch_shapes=[pltpu.VMEM((tme, D_MODEL), F32)],
    )
    return pl.pallas_call(
        functools.partial(_expert_ffn_kernel, fc=fc),
        grid_spec=grid_spec,
        out_shape=jax.ShapeDtypeStruct((n_rows, D_MODEL), F32),
        compiler_params=_params(vmem, ("arbitrary",)),
        name="moe_expert_ffn",
    )(tile_expert, n_used, xs, we1b, we3b, we2b)


def _combine_kernel(base_ref, cnt_ref, loc_ref, x_ref, e0_ref, e1_ref, w0_ref, w1_ref, lower_ref, y_hbm_ref,
                    o_ref, o_last_ref, buf_ref, sem):
    i = pl.program_id(0)
    n_steps = pl.num_programs(0)
    cur = i % 2

    def for_copies(tile, buf, fn):
        for e in range(N_EXPERTS):
            k = tile * N_EXPERTS + e
            cnt, src, loc = cnt_ref[k], base_ref[k], loc_ref[k]
            for size, off, active in _run_pieces(cnt, TILE):
                c = pltpu.make_async_copy(
                    y_hbm_ref.at[pl.ds(pl.multiple_of(src + off, RUN_ALIGN), size), :],
                    buf_ref.at[buf, pl.ds(pl.multiple_of(loc + off, RUN_ALIGN), size), :], sem.at[buf])
                pl.when(active)(functools.partial(fn, c))

    @pl.when(i == 0)
    def _():
        buf_ref[...] = jnp.zeros_like(buf_ref)
        for_copies(0, 0, lambda c: c.start())

    @pl.when(i + 1 < n_steps)
    def _():
        for_copies(i + 1, 1 - cur, lambda c: c.start())

    for_copies(i, cur, lambda c: c.wait())

    positions = _packed_rows(i, e0_ref[...], e1_ref[...], lower_ref, loc_ref)
    slot = lax.broadcasted_iota(jnp.int32, (TILE, RUN_ROWS), 1).astype(F32)
    yb = buf_ref[cur].astype(BF16)
    wide = lambda a, n: jnp.concatenate([a] * n, axis=1)
    acc = x_ref[...]
    for pos, wk in zip(positions, (w0_ref, w1_ref)):
        rows = _dot(jnp.where(slot == pos, 1.0, 0.0).astype(BF16), yb)
        acc = acc + wide(wk[...], D_MODEL // LANE) * rows

    @pl.when(i < n_steps - 1)
    def _():
        o_ref[...] = acc

    @pl.when(i == n_steps - 1)
    def _():
        o_last_ref[...] = acc


def _combine(x, e0, e1, w0, w1, base, cnt, loc, lower, y_rows):
    T = x.shape[0]
    row = lambda i, *_: (i, 0)
    lane_spec = pl.BlockSpec((TILE, LANE), row)
    vmem = (5 * _nbytes((TILE, D_MODEL), F32) + 3 * _nbytes((RUN_ROWS, D_MODEL), F32)
            + 8 * _nbytes((TILE, LANE), F32) + 8 * _nbytes((TILE, RUN_ROWS), F32)
            + 4 * _nbytes((TILE, D_MODEL), F32))
    grid_spec = pltpu.PrefetchScalarGridSpec(
        num_scalar_prefetch=3,
        grid=(T // TILE,),
        in_specs=[pl.BlockSpec((TILE, D_MODEL), row), lane_spec, lane_spec, lane_spec, lane_spec,
                  pl.BlockSpec((TILE, TILE), lambda i, *_: (0, 0)),
                  pl.BlockSpec(memory_space=pl.ANY)],
        out_specs=[pl.BlockSpec((TILE, D_MODEL), lambda i, *_: (jnp.minimum(i, T // TILE - 2), 0)),
                   pl.BlockSpec((TILE, D_MODEL), lambda i, *_: (0, 0))],
        scratch_shapes=[pltpu.VMEM((2, RUN_ROWS, D_MODEL), F32), pltpu.SemaphoreType.DMA((2,))],
    )
    return pl.pallas_call(
        _combine_kernel,
        grid_spec=grid_spec,
        out_shape=[jax.ShapeDtypeStruct((T - TILE, D_MODEL), F32), jax.ShapeDtypeStruct((TILE, D_MODEL), F32)],
        compiler_params=_params(vmem, ("arbitrary",)),
        name="moe_combine",
    )(base, cnt, loc, x, e0, e1, w0, w1, lower, y_rows)


def _moe(x, g2, wr_pad, we1b, we3b, we2b):
    T = x.shape[0]
    tme = 2 * TILE
    n_tt = T // TILE
    e0, e1, w0, w1 = _router(x, g2, wr_pad)
    experts = jnp.arange(N_EXPERTS, dtype=jnp.int32)
    picks = (e0[:, :1] == experts[None, :]) | (e1[:, :1] == experts[None, :])
    cnt = jnp.sum(picks.reshape(n_tt, TILE, N_EXPERTS).astype(jnp.int32), axis=1)
    cnt = (cnt + RUN_ALIGN - 1) // RUN_ALIGN * RUN_ALIGN
    counts = jnp.sum(cnt, axis=0)
    padded = (counts + tme - 1) // tme * tme
    pend = jnp.cumsum(padded).astype(jnp.int32)
    base = ((pend - padded)[None, :] + jnp.cumsum(cnt, axis=0) - cnt).astype(jnp.int32)
    loc = (jnp.cumsum(cnt, axis=1) - cnt).astype(jnp.int32)
    max_rows = T * TOP_K + n_tt * N_EXPERTS * (RUN_ALIGN - 1)
    n_tiles = -(-max_rows // tme) + N_EXPERTS + 1
    tile_start = jnp.arange(n_tiles, dtype=jnp.int32) * tme
    tile_expert = jnp.minimum(jnp.sum(pend[None, :] <= tile_start[:, None], axis=1), N_EXPERTS - 1).astype(jnp.int32)
    n_used = (pend[N_EXPERTS - 1:] // tme).astype(jnp.int32)
    tok = jnp.arange(TILE)
    lower = (tok[None, :] < tok[:, None]).astype(BF16)
    base_f, cnt_f, loc_f = base.reshape(-1), cnt.reshape(-1).astype(jnp.int32), loc.reshape(-1)
    xs = _dispatch(x, g2, e0, e1, base_f, cnt_f, loc_f, pend, lower, n_tiles, tme)
    y_rows = _expert_ffn(xs, tile_expert, n_used, we1b, we3b, we2b, tme)
    return _combine(x, e0, e1, w0, w1, base_f, cnt_f, loc_f, lower, y_rows)


def _class_major(a, dil):
    n = a.shape[0]
    return a.reshape((n // TILE, TILE // dil, dil) + a.shape[1:]).swapaxes(1, 2).reshape(a.shape)


def _natural(a, dil):
    n = a.shape[0]
    return a.reshape((n // TILE, dil, TILE // dil) + a.shape[1:]).swapaxes(1, 2).reshape(a.shape)


def _perm_matrices():
    eye = jnp.eye(TILE, dtype=BF16)
    return jnp.stack([_class_major(eye, dil) for _, dil in SWA_PATTERNS[1:]])


def _rope_tables(pos, Tp):
    freqs = ROPE_THETA ** (-jnp.arange(0, SWA_HEAD_DIM, 2, dtype=F32) / SWA_HEAD_DIM)
    rfreqs = 1.0 / (ROPE_THETA ** jnp.linspace(0.0, 1.0, RET_QK_DIM // 2, dtype=F32))

    def swa(p):
        ang = p.astype(F32)[:, None] * freqs[None, :]
        c, s = jnp.cos(ang), jnp.sin(ang)
        return jnp.concatenate([c, c], axis=1), jnp.concatenate([-s, s], axis=1)

    per_group = [swa(jnp.concatenate([_class_major(pos[:Tp], dil), pos[Tp:]])) for _, dil in SWA_PATTERNS]
    cosb = jnp.stack([c for c, _ in per_group])
    sinb = jnp.stack([s for _, s in per_group])
    rang = pos.astype(F32)[:, None] * rfreqs[None, :]
    c, s = jnp.cos(rang), jnp.sin(rang)
    cosc = jnp.concatenate([c, c, c, c], axis=1)
    sinc = jnp.concatenate([-s, s, -s, s], axis=1)
    return cosb, sinb, cosc, sinc


def _ret_tables(c_len):
    log_g = jnp.log1p(-jnp.exp2(-5.0 - jnp.arange(RET_HEADS, dtype=F32)))
    r = jnp.arange(CHUNK)
    i = (r % c_len).astype(F32)
    same = (r[:, None] // c_len) == (r[None, :] // c_len)
    dist = i[:, None] - i[None, :]
    decay = jnp.where(same[None] & (dist >= 0)[None],
                      jnp.exp(log_g[:, None, None] * jnp.maximum(dist, 0.0)[None]), 0.0)
    qin = jnp.repeat(jnp.exp(log_g[None, :] * (i[:, None] + 1.0)), RET_V_DIM, axis=1)
    kout = jnp.repeat(jnp.exp(log_g[None, :] * (c_len - 1.0 - i)[:, None]), RET_QK_DIM, axis=1)
    gc = jnp.broadcast_to(jnp.exp(log_g * c_len)[:, None], (RET_HEADS, LANE))
    gc = jnp.concatenate([gc, jnp.zeros((8 - RET_HEADS, LANE), F32)], axis=0)
    return decay.astype(F32), qin.astype(F32), kout.astype(F32), gc.astype(F32)


def _mixa_tables(w_s, b_s, t_s):
    w_p = jnp.tril(w_s)
    w8 = jnp.tril(w_s[:, :t_s, :t_s])
    eye = jnp.eye(CHUNK // t_s, dtype=w_s.dtype)
    w_smp = jax.vmap(lambda m: jnp.kron(eye, m))(w8)
    w2 = jnp.stack([w_p, w_smp]).astype(BF16)
    b_p = jnp.repeat(b_s.T, LANE, axis=1)
    b_smp = jnp.repeat(jnp.tile(b_s[:, :t_s].T, (CHUNK // t_s, 1)), LANE, axis=1)
    return w2, jnp.stack([b_p, b_smp]).astype(F32)


def kernel(x_prompt, x_sample, cache_swa_kv0, cache_swa_kv1, cache_swa_kv2, state_ret, norm1_g, w_in, norm_v_g, w_s, b_s, q_norm_g, k_norm_g, w_branch, w_out, norm2_g, w1, w3, w2, w_router, we1, we3, we2):
    n_p, s, d = x_prompt.shape
    n_s, t_s, _ = x_sample.shape
    depth = w_in.shape[0]
    Tp, Ts = n_p * s, n_s * t_s
    T = Tp + Ts
    max_win, max_dil = SWA_PATTERNS[-1]
    assert d == D_MODEL and Ts == TILE and CHUNK % t_s == 0
    assert s % (CHUNK * max_dil) == 0 and s >= max_win
    n_pt = Tp // TILE

    xa, xb = x_prompt.reshape(Tp, d), x_sample.reshape(Ts, d)
    pos = jnp.concatenate([jnp.arange(s, dtype=jnp.int32),
                           jnp.tile(PAST_LEN + jnp.arange(t_s, dtype=jnp.int32), n_s)])
    tabs = _rope_tables(pos, s)
    perm = _perm_matrices()
    rt_prompt = _ret_tables(CHUNK)
    rt_sample = _ret_tables(t_s)
    caches = tuple(c.reshape(c.shape[0], c.shape[1], -1, SWA_HEAD_DIM)
                   for c in (cache_swa_kv0, cache_swa_kv1, cache_swa_kv2))

    p_kv = [[] for _ in SWA_PATTERNS]
    s_kv = [[] for _ in SWA_PATTERNS]
    p_ret, s_ret, s_v = [], [], []
    for layer in range(depth):
        P = _inproj(xa, xb, norm1_g[layer][None, :], w_in[layer].astype(BF16), perm, tabs,
                    norm_v_g[layer][None, :], q_norm_g[layer][None, :], k_norm_g[layer][None, :],
                    n_pt, s // TILE)

        w2a, b2a = _mixa_tables(w_s[layer], b_s[layer], t_s)
        out_a = _mixer_a(P, w2a, b2a, Tp // CHUNK)

        Ps = P[Tp:].astype(F32).reshape(n_s, t_s, IN_WIDTH)
        po, plse, so, slse = [], [], [], []
        for g, (win, dil) in enumerate(SWA_PATTERNS):
            o_g, l_g = _swa_prompt(P, g, dil, n_p, s)
            os_g, ls_g = _swa_sample(Ps, caches[g], layer, g, dil)
            po.append(o_g)
            plse.append(l_g)
            so.append(os_g.reshape(Ts, COL).astype(BF16))
            slse.append(ls_g.reshape(Ts, LANE))

        out_c, ret_p = _ret_prompt(P, rt_prompt, n_p, s)
        out_cs, ret_s = _ret_sample(P, state_ret, layer, rt_sample, Tp, t_s)

        x = _branch(xa, xb, out_a, P, w_branch[layer].astype(BF16), w_out[layer].astype(BF16),
                    (*po, *plse, out_c), (*so, *slse, out_cs), n_pt)

        g2 = norm2_g[layer][None, :]
        i = layer // 2
        if layer % 2 == 0:
            x = _dense_ffn(x, g2, w1[i].astype(BF16), w3[i].astype(BF16), w2[i].astype(BF16))
            xa = xb = x
        else:
            wr_pad = jnp.zeros((D_MODEL, LANE), BF16).at[:, :N_EXPERTS].set(w_router[i].astype(BF16))
            xa, xb = _moe(x, g2, wr_pad, we1[i].astype(BF16), we3[i].astype(BF16), we2[i].astype(BF16))

        for g, (win, dil) in enumerate(SWA_PATTERNS):
            kcols = slice((CB_K + g) * COL, (CB_K + g + 1) * COL)
            vcols = slice((CB_V + g) * COL, (CB_V + g + 1) * COL)
            keep = -(-min(win, s) // TILE) * TILE

            def rows(cols):
                blk = jnp.concatenate([P[(b + 1) * s - keep:(b + 1) * s, cols] for b in range(n_p)])
                blk = _natural(blk, dil).reshape(n_p, keep, SWA_HEADS, SWA_HEAD_DIM)
                return blk[:, keep - min(win, s):]

            p_kv[g].append(jnp.stack([rows(kcols), rows(vcols)], axis=2).astype(F32))
            ks = P[Tp:, kcols].reshape(n_s, t_s, SWA_HEADS, SWA_HEAD_DIM)
            vs = P[Tp:, vcols].reshape(n_s, t_s, SWA_HEADS, SWA_HEAD_DIM)
            s_kv[g].append(jnp.stack([ks, vs], axis=2).astype(F32))
        p_ret.append(ret_p)
        s_ret.append(ret_s)
        s_v.append(P[Tp:, CB_AV * COL:(CB_AV + 1) * COL].astype(F32).reshape(n_s, t_s, COL))

    y_prompt = xa[:Tp].reshape(n_p, s, d)
    y_sample = xb[xb.shape[0] - Ts:].reshape(n_s, t_s, d)
    return (y_prompt, y_sample,
            jnp.stack(p_kv[0]), jnp.stack(p_kv[1]), jnp.stack(p_kv[2]), jnp.stack(p_ret),
            jnp.stack(s_kv[0]), jnp.stack(s_kv[1]), jnp.stack(s_kv[2]), jnp.stack(s_ret),
            jnp.stack(s_v))
```

```python
import functools
import math

import jax
import jax.numpy as jnp
from jax import lax
from jax.experimental import pallas as pl
from jax.experimental.pallas import tpu as pltpu

F32 = jnp.float32
BF16 = jnp.bfloat16

PAST_LEN = 16384
EPS = 1e-6
NEG_INF = -1e30
ROPE_THETA = 10000.0

D_MODEL = 1024
LANE = 128
BF16_ROWS = 16
CHUNK = 128
TILE = 256
COL = 512
A_GROUPS = 4
SWA_PATTERNS = ((128, 1), (512, 4), (2048, 16))
SWA_HEADS = 4
SWA_HEAD_DIM = 128
RET_HEADS = 4
RET_QK_DIM = 64
RET_V_DIM = 128
N_EXPERTS = 8
TOP_K = 2
IN_WIDTH = 10240
N_COL = IN_WIDTH // COL

CB_AU, CB_AV, CB_Q, CB_K, CB_V, CB_RQK, CB_RV, CB_RG, CB_GATE = 0, 1, 2, 5, 8, 11, 12, 13, 14

VMEM_INTERNAL_SCRATCH = 8 * 1024 * 1024


def _pick(n, candidates):
    for c in candidates:
        if n % c == 0:
            return c
    raise ValueError(f"no tile in {candidates} divides {n}")


def _params(block_bytes, semantics=None):
    limit = int(block_bytes) + VMEM_INTERNAL_SCRATCH
    return pltpu.CompilerParams(dimension_semantics=semantics, vmem_limit_bytes=limit)


def _nbytes(shape, dtype):
    return math.prod(shape) * jnp.dtype(dtype).itemsize


def _rms(x):
    return x * lax.rsqrt(jnp.mean(x * x, axis=-1, keepdims=True) + EPS)


def _gelu(x):
    return 0.5 * x * (1.0 + lax.erf(x * (0.5 ** 0.5)))


def _sigmoid(x):
    return 1.0 / (1.0 + jnp.exp(-x))


def _idiv(x, n):
    assert n & (n - 1) == 0
    return x >> (n.bit_length() - 1)


def _imod(x, n):
    assert n & (n - 1) == 0
    return x & (n - 1)


def _dot(a, b):
    return jnp.dot(a, b, preferred_element_type=F32)


LSE_LANES = LANE // 4


def _pack_heads(cols):
    rows = cols[0].shape[0]
    grp = _idiv(lax.broadcasted_iota(jnp.int32, (rows, LANE), 1), LSE_LANES)
    out = jnp.broadcast_to(cols[-1], (rows, LANE))
    for h in range(len(cols) - 2, -1, -1):
        out = jnp.where(grp == h, cols[h], out)
    return out


def _dot_nt(a, b):
    return lax.dot_general(a, b, (((1,), (1,)), ((), ())), preferred_element_type=F32)


def _tile_rows(i, n_prompt_tiles, xa_ref, xb_ref):
    return jnp.where(i < n_prompt_tiles, xa_ref[...], xb_ref[...])


def _x_specs(xa, xb):
    last_a, last_b = xa.shape[0] // TILE - 1, xb.shape[0] // TILE - 1
    return [pl.BlockSpec((TILE, D_MODEL), lambda i, *_: (jnp.minimum(i, last_a), 0)),
            pl.BlockSpec((TILE, D_MODEL), lambda i, *_: (last_b, 0))]


def _inproj_kernel(*refs, n_prompt_tiles, n_cast):
    (xa_ref, xb_ref, g1_ref, w_ref, perm_ref, cosb_ref, sinb_ref, cosc_ref, sinc_ref,
     nvg_ref, qg_ref, kg_ref) = refs[:12]
    cast_in, o_ref, cast_out, h_ref = refs[12:12 + n_cast], refs[12 + n_cast], refs[13 + n_cast:-1], refs[-1]
    i = pl.program_id(0)
    for src, dst in zip(cast_in, cast_out):
        dst[...] = src[...].astype(dst.dtype)
    hn = (_rms(_tile_rows(i, n_prompt_tiles, xa_ref, xb_ref)) * g1_ref[...]).astype(BF16)
    h_ref[0] = hn

    @pl.when(i < n_prompt_tiles)
    def _():
        for k in range(1, len(SWA_PATTERNS)):
            h_ref[k] = _dot(perm_ref[k - 1], hn).astype(BF16)

    @pl.when(i >= n_prompt_tiles)
    def _():
        for k in range(1, len(SWA_PATTERNS)):
            h_ref[k] = hn

    def qk_heads(acc, g, cols0, gain_ref, scale):
        for hh in range(SWA_HEADS):
            cs = slice(hh * LANE, (hh + 1) * LANE)
            y = _rms(acc[:, cs]) * gain_ref[...]
            rot = y * cosb_ref[g] + pltpu.roll(y, LANE // 2, axis=1) * sinb_ref[g]
            o_ref[:, cols0 + hh * LANE:cols0 + (hh + 1) * LANE] = (rot * scale).astype(o_ref.dtype)

    for j in range(N_COL):
        cols = slice(j * COL, (j + 1) * COL)
        g = (j - CB_Q) % len(SWA_PATTERNS) if CB_Q <= j < CB_RQK else 0
        acc = _dot(h_ref[g], w_ref[:, cols])
        if j == CB_AU:
            o_ref[:, cols] = _gelu(acc).astype(o_ref.dtype)
        elif j == CB_AV:
            o_ref[:, cols] = (_rms(_gelu(acc)) * nvg_ref[...]).astype(o_ref.dtype)
        elif CB_Q <= j < CB_K:
            qk_heads(acc, g, j * COL, qg_ref, SWA_HEAD_DIM ** -0.5)
        elif CB_K <= j < CB_V:
            qk_heads(acc, g, j * COL, kg_ref, 1.0)
        elif j < CB_RQK or j == CB_RV:
            o_ref[:, cols] = acc.astype(o_ref.dtype)
        elif j == CB_RQK:
            lane = lax.broadcasted_iota(jnp.int32, (acc.shape[0], LANE), 1)
            first_half = _imod(lane, RET_QK_DIM) < (RET_QK_DIM // 2)
            for tt in range(COL // LANE):
                y = acc[:, tt * LANE:(tt + 1) * LANE]
                partner = jnp.where(first_half,
                                    pltpu.roll(y, LANE - RET_QK_DIM // 2, axis=1),
                                    pltpu.roll(y, RET_QK_DIM // 2, axis=1))
                rot = y * cosc_ref[...] + partner * sinc_ref[...]
                scale = 1.0 if tt < (COL // LANE) // 2 else RET_QK_DIM ** -0.5
                o_ref[:, j * COL + tt * LANE:j * COL + (tt + 1) * LANE] = (rot * scale).astype(o_ref.dtype)
        elif j == CB_RG:
            o_ref[:, cols] = (acc * _sigmoid(acc)).astype(o_ref.dtype)
        else:
            o_ref[:, cols] = _sigmoid(acc).astype(o_ref.dtype)


def _cast_blocks(arr, n_steps):
    a2 = arr.reshape(-1, arr.shape[-1])
    rows = a2.shape[0]
    rb = -(-rows // n_steps)
    rb = -(-rb // BF16_ROWS) * BF16_ROWS
    while rows % rb:
        rb += BF16_ROWS
    return a2, rb


def _inproj(xa, xb, g1, w_in_b, perm, tabs, nvg, qg, kg, n_prompt_tiles, tiles_per_seq, to_cast=()):
    T = (n_prompt_tiles + 1) * TILE
    n_steps = T // TILE
    casts = [_cast_blocks(a, n_steps) for a in to_cast]
    cast_specs = [pl.BlockSpec((rb, a2.shape[1]), functools.partial(
        lambda i, last: (jnp.minimum(i, last), 0), last=a2.shape[0] // rb - 1)) for a2, rb in casts]
    cosb, sinb, cosc, sinc = tabs
    n_g = len(SWA_PATTERNS)
    row = lambda i: (i, 0)
    const = lambda i: (0, 0)
    tab_blk = lambda i: jnp.where(i < n_prompt_tiles, i % tiles_per_seq, tiles_per_seq)
    tab3 = pl.BlockSpec((n_g, TILE, LANE), lambda i: (0, tab_blk(i), 0))
    tab = pl.BlockSpec((TILE, LANE), lambda i: (tab_blk(i), 0))
    vmem = (2 * _nbytes((TILE, D_MODEL), F32) + n_g * _nbytes((TILE, D_MODEL), BF16)
            + _nbytes(w_in_b.shape, BF16) + 2 * _nbytes((TILE, IN_WIDTH), BF16)
            + 2 * (2 * n_g + 2) * _nbytes((TILE, LANE), F32) + 2 * _nbytes(perm.shape, BF16)
            + 8 * _nbytes((TILE, COL), F32)
            + sum(2 * (_nbytes((rb, a2.shape[1]), F32) + _nbytes((rb, a2.shape[1]), BF16)) for a2, rb in casts))
    outs = pl.pallas_call(
        functools.partial(_inproj_kernel, n_prompt_tiles=n_prompt_tiles, n_cast=len(casts)),
        grid=(n_steps,),
        in_specs=_x_specs(xa, xb) + [
            pl.BlockSpec((1, D_MODEL), const),
            pl.BlockSpec(w_in_b.shape, const, pipeline_mode=pl.Buffered(1)),
            pl.BlockSpec(perm.shape, lambda i: (0, 0, 0)),
            tab3, tab3, tab, tab,
            pl.BlockSpec((1, COL), const),
            pl.BlockSpec((1, LANE), const),
            pl.BlockSpec((1, LANE), const),
        ] + cast_specs,
        out_specs=[pl.BlockSpec((TILE, IN_WIDTH), row)] + cast_specs,
        out_shape=[jax.ShapeDtypeStruct((T, IN_WIDTH), BF16)]
                  + [jax.ShapeDtypeStruct(a2.shape, BF16) for a2, _ in casts],
        scratch_shapes=[pltpu.VMEM((n_g, TILE, D_MODEL), BF16)],
        compiler_params=_params(vmem, ("arbitrary",)),
        name="inproj",
    )(xa, xb, g1, w_in_b, perm, cosb, sinb, cosc, sinc, nvg, qg, kg, *[a2 for a2, _ in casts])
    return outs[0], [o.reshape(a.shape) for o, a in zip(outs[1:], to_cast)]


def _mixa_kernel(u_ref, v_ref, w_ref, b_ref, o_ref, *, cps, n_prompt_chunks):
    i = pl.program_id(0)
    for c in range(cps):
        var = ((i * cps + c) >= n_prompt_chunks).astype(jnp.int32)
        rows = slice(c * CHUNK, (c + 1) * CHUNK)
        for g in range(A_GROUPS):
            cols = slice(g * LANE, (g + 1) * LANE)
            z = _dot(w_ref[var, g], v_ref[rows, cols]) + b_ref[var, :, cols]
            o_ref[rows, cols] = (u_ref[rows, cols].astype(F32) * z).astype(o_ref.dtype)


def _mixer_a(P, w2, b2, n_prompt_chunks):
    T = P.shape[0]
    n_chunks = T // CHUNK
    cps = _pick(n_chunks, (8, 6, 4, 3, 2, 1))
    rows = cps * CHUNK
    vmem = 6 * _nbytes((rows, COL), BF16) + 2 * _nbytes(w2.shape, BF16) + 2 * _nbytes(b2.shape, F32)
    return pl.pallas_call(
        functools.partial(_mixa_kernel, cps=cps, n_prompt_chunks=n_prompt_chunks),
        grid=(n_chunks // cps,),
        in_specs=[
            pl.BlockSpec((rows, COL), lambda i: (i, CB_AU)),
            pl.BlockSpec((rows, COL), lambda i: (i, CB_AV)),
            pl.BlockSpec(w2.shape, lambda i: (0, 0, 0, 0)),
            pl.BlockSpec(b2.shape, lambda i: (0, 0, 0)),
        ],
        out_specs=pl.BlockSpec((rows, COL), lambda i: (i, 0)),
        out_shape=jax.ShapeDtypeStruct((T, COL), BF16),
        compiler_params=_params(vmem, ("parallel",)),
        name="mixer_a",
    )(P, P, w2, b2)


def _swa_kernel(q_ref, kp_ref, kc_ref, vp_ref, vc_ref, o_ref, l_ref, k_ref, v_ref, s_ref, p_ref, *, qb):
    i = pl.program_id(2)
    rows = qb * CHUNK
    lead = q_ref.shape[:-1]
    k_ref[0:CHUNK] = kp_ref[...].reshape(CHUNK, COL)
    k_ref[CHUNK:CHUNK + rows] = kc_ref[...].reshape(rows, COL)
    v_ref[0:CHUNK] = vp_ref[...].reshape(CHUNK, COL)
    v_ref[CHUNK:CHUNK + rows] = vc_ref[...].reshape(rows, COL)
    row = lax.broadcasted_iota(jnp.int32, (CHUNK, 2 * CHUNK), 0)
    col = lax.broadcasted_iota(jnp.int32, (CHUNK, 2 * CHUNK), 1)
    mask_cur = (col >= CHUNK) & (col - CHUNK <= row)
    mask_all = mask_cur | ((col < CHUNK) & (col >= row))
    mask_first = mask_cur | ((col < CHUNK) & (col >= row) & (i > 0))
    q_all = q_ref[...].reshape(rows, COL)
    for j in range(qb):
        mask = mask_first if j == 0 else mask_all
        for h in range(SWA_HEADS):
            cs = slice(h * LANE, (h + 1) * LANE)
            sc = _dot_nt(q_all[j * CHUNK:(j + 1) * CHUNK, cs], k_ref[j * CHUNK:(j + 2) * CHUNK, cs])
            s_ref[j * SWA_HEADS + h] = jnp.where(mask, sc, NEG_INF)
    s = s_ref[...]
    m = jnp.max(s, axis=-1, keepdims=True)
    p = jnp.exp(s - m)
    den = jnp.sum(p, axis=-1, keepdims=True)
    p_ref[...] = p.astype(BF16)
    lse = m + jnp.log(den)
    for h in range(SWA_HEADS):
        cs = slice(h * LANE, (h + 1) * LANE)
        o_h = [_dot(p_ref[j * SWA_HEADS + h], v_ref[j * CHUNK:(j + 2) * CHUNK, cs]) / den[j * SWA_HEADS + h]
               for j in range(qb)]
        o_ref[..., cs] = jnp.concatenate(o_h, axis=0).astype(o_ref.dtype).reshape(lead + (LANE,))
    packed = [_pack_heads([lse[j * SWA_HEADS + h] for h in range(SWA_HEADS)]) for j in range(qb)]
    l_ref[...] = jnp.concatenate(packed, axis=0).reshape(lead + (LANE,))


def _swa_prompt(P, g, dil, n_p, s):
    T = P.shape[0]
    Tp = n_p * s
    nb = s // dil // CHUNK
    qb = _pick(nb, (8, 4, 2, 1))
    steps = nb // qb
    prev_blk = lambda b, i: b * nb + jnp.maximum(i * qb - 1, 0)
    if dil == 1:
        src = P
        lead_q, lead_p = (qb * CHUNK,), (CHUNK,)
        o_shape = (Tp, COL)
        l_shape = (Tp, LANE)
        q_map = lambda cb: (lambda b, r, i: (b * steps + i, cb + g))
        p_map = lambda cb: (lambda b, r, i: (prev_blk(b, i), cb + g))
        o_map = lambda b, r, i: (b * steps + i, 0)
        l_map = lambda b, r, i: (b * steps + i, 0)
    else:
        rpt = TILE // dil
        tpb = CHUNK // rpt
        src = P.reshape(T // TILE, dil, rpt, IN_WIDTH)
        lead_q, lead_p = (qb * tpb, None, rpt), (tpb, None, rpt)
        o_shape = (Tp // TILE, dil, rpt, COL)
        l_shape = (Tp // TILE, dil, rpt, LANE)
        q_map = lambda cb: (lambda b, r, i: (b * steps + i, r, 0, cb + g))
        p_map = lambda cb: (lambda b, r, i: (prev_blk(b, i), r, 0, cb + g))
        o_map = lambda b, r, i: (b * steps + i, r, 0, 0)
        l_map = lambda b, r, i: (b * steps + i, r, 0, 0)

    q_spec = lambda cb: pl.BlockSpec(lead_q + (COL,), q_map(cb))
    p_spec = lambda cb: pl.BlockSpec(lead_p + (COL,), p_map(cb))
    n_pairs = qb * SWA_HEADS
    vmem = ((10 * qb + 6) * _nbytes((CHUNK, COL), BF16) + 2 * qb * _nbytes((CHUNK, COL), F32)
            + 6 * n_pairs * _nbytes((CHUNK, 2 * CHUNK), F32))
    o, l = pl.pallas_call(
        functools.partial(_swa_kernel, qb=qb),
        grid=(n_p, dil, steps),
        in_specs=[q_spec(CB_Q), p_spec(CB_K), q_spec(CB_K), p_spec(CB_V), q_spec(CB_V)],
        out_specs=[pl.BlockSpec(lead_q + (COL,), o_map),
                   pl.BlockSpec(lead_q + (LANE,), l_map)],
        out_shape=[jax.ShapeDtypeStruct(o_shape, BF16), jax.ShapeDtypeStruct(l_shape, F32)],
        scratch_shapes=[pltpu.VMEM(((qb + 1) * CHUNK, COL), BF16), pltpu.VMEM(((qb + 1) * CHUNK, COL), BF16),
                        pltpu.VMEM((n_pairs, CHUNK, 2 * CHUNK), F32),
                        pltpu.VMEM((n_pairs, CHUNK, 2 * CHUNK), BF16)],
        compiler_params=_params(vmem, ("parallel", "parallel", "arbitrary")),
        name=f"swa_prompt_g{g}",
    )(src, src, src, src, src)
    return o.reshape(Tp, COL), l.reshape(Tp, LANE)


def _swa_sample_kernel(q_ref, kn_ref, vn_ref, cache_ref, o_ref, l_ref, *, dil, lbuf, t_s):
    nq = SWA_HEADS * t_s
    q = q_ref[...]
    qrep = jnp.concatenate([q] * SWA_HEADS, axis=0)
    rq = lax.broadcasted_iota(jnp.int32, (nq, COL), 0)
    cq = lax.broadcasted_iota(jnp.int32, (nq, COL), 1)
    qbd = jnp.where(_idiv(rq, t_s) == _idiv(cq, LANE), qrep, 0.0).astype(BF16)

    per_pos = 2 * SWA_HEADS
    if cache_ref.ndim == 2:
        n_keys = lbuf
        rows_of = lambda first: cache_ref[pl.ds(first, lbuf, stride=per_pos), :]
        key_pos = lambda r: r
    else:
        n_keys = cache_ref.shape[0] * t_s
        rows_of = lambda first: cache_ref[:, pl.ds(first, t_s, stride=per_pos), :].reshape(n_keys, LANE)
        key_pos = lambda r: dil * _idiv(r, t_s) + _imod(r, t_s)
    kc = jnp.concatenate([rows_of(h).astype(BF16) for h in range(SWA_HEADS)], axis=1)
    vc = jnp.concatenate([rows_of(SWA_HEADS + h).astype(BF16) for h in range(SWA_HEADS)], axis=1)
    kn = kn_ref[...].astype(BF16)
    vn = vn_ref[...].astype(BF16)

    s_c = _dot_nt(qbd, kc)
    s_n = _dot_nt(qbd, kn)
    t_c = _imod(lax.broadcasted_iota(jnp.int32, (nq, n_keys), 0), t_s)
    c_c = key_pos(lax.broadcasted_iota(jnp.int32, (nq, n_keys), 1))
    diff_c = lbuf + t_c - c_c
    ok_c = (_imod(diff_c, dil) == 0) & (diff_c <= lbuf)
    t_n = _imod(lax.broadcasted_iota(jnp.int32, (nq, t_s), 0), t_s)
    c_n = lax.broadcasted_iota(jnp.int32, (nq, t_s), 1)
    diff_n = t_n - c_n
    ok_n = (diff_n >= 0) & (_imod(diff_n, dil) == 0)
    s_c = jnp.where(ok_c, s_c, NEG_INF)
    s_n = jnp.where(ok_n, s_n, NEG_INF)
    m = jnp.maximum(jnp.max(s_c, axis=-1, keepdims=True), jnp.max(s_n, axis=-1, keepdims=True))
    p_c = jnp.exp(s_c - m)
    p_n = jnp.exp(s_n - m)
    den = jnp.sum(p_c, axis=-1, keepdims=True) + jnp.sum(p_n, axis=-1, keepdims=True)
    o_all = (_dot(p_c.astype(BF16), vc) + _dot(p_n.astype(BF16), vn)) / den
    lse = m + jnp.log(den)
    for h in range(SWA_HEADS):
        cs = slice(h * LANE, (h + 1) * LANE)
        o_ref[:, cs] = o_all[h * t_s:(h + 1) * t_s, cs]
    l_ref[...] = _pack_heads([lse[h * t_s:(h + 1) * t_s] for h in range(SWA_HEADS)])


def _swa_sample(Ps, cache, layer, g, dil):
    n_s, t_s, _ = Ps.shape
    rows = cache.shape[2]
    lbuf = rows // (2 * SWA_HEADS)
    assert lbuf == dil * CHUNK, "window buffer must hold exactly one full window"
    blk = lambda cb: pl.BlockSpec((None, t_s, COL), lambda b: (b, 0, cb + g))
    if dil > t_s:
        per_pos = rows // lbuf
        cache = cache.reshape(cache.shape[0], n_s, lbuf // dil, dil * per_pos, LANE)
        cache_spec = pl.BlockSpec((None, None, lbuf // dil, t_s * per_pos, LANE), lambda b: (layer, b, 0, 0, 0))
    else:
        cache_spec = pl.BlockSpec((None, None, rows, LANE), lambda b: (layer, b, 0, 0))
    vmem = (2 * _nbytes((lbuf, 2 * COL), F32) + 2 * _nbytes((lbuf, 2 * COL), BF16)
            + 8 * _nbytes((SWA_HEADS * t_s, lbuf), F32))
    return pl.pallas_call(
        functools.partial(_swa_sample_kernel, dil=dil, lbuf=lbuf, t_s=t_s),
        grid=(n_s,),
        in_specs=[blk(CB_Q), blk(CB_K), blk(CB_V), cache_spec],
        out_specs=[pl.BlockSpec((None, t_s, COL), lambda b: (b, 0, 0)),
                   pl.BlockSpec((None, t_s, LANE), lambda b: (b, 0, 0))],
        out_shape=[jax.ShapeDtypeStruct((n_s, t_s, COL), F32),
                   jax.ShapeDtypeStruct((n_s, t_s, LANE), F32)],
        compiler_params=_params(vmem, ("parallel",)),
        name=f"swa_sample_g{g}",
    )(Ps, Ps, Ps, cache)


def _ret_head_inputs(qk_ref, v_ref, kout_ref, h):
    pair, half = h // 2, h % 2
    lane = lax.broadcasted_iota(jnp.int32, (CHUNK, LANE), 1)
    head_lanes = _idiv(lane, RET_QK_DIM) == half
    qt = qk_ref[:, pair * LANE:(pair + 1) * LANE]
    kt = qk_ref[:, COL // 2 + pair * LANE:COL // 2 + (pair + 1) * LANE]
    qm = jnp.where(head_lanes, qt, jnp.zeros_like(qt))
    kw = jnp.where(head_lanes, kt.astype(F32) * kout_ref[:, pair * LANE:(pair + 1) * LANE], 0.0)
    vh = v_ref[:, h * LANE:(h + 1) * LANE]
    return qm, kt, kw, vh


def _ret_finish(o, gate_ref, o_ref, h):
    cs = slice(h * LANE, (h + 1) * LANE)
    o_ref[:, cs] = (gate_ref[:, cs].astype(F32) * _rms(o)).astype(o_ref.dtype)


def _ret_prompt_kernel(*refs, n_seq):
    ins = refs[:3 * n_seq]
    decay_ref, qin_ref, kout_ref, gc_ref, o_ref, s_out_ref, s_ref, att_ref = refs[3 * n_seq:]
    i = pl.program_id(0)

    @pl.when(i == 0)
    def _():
        s_ref[...] = jnp.zeros_like(s_ref)

    pairs = [(b, h) for b in range(n_seq) for h in range(RET_HEADS)]
    head_in = lambda b, h: _ret_head_inputs(ins[3 * b], ins[3 * b + 1], kout_ref, h)
    for b, h in pairs:
        qm, kt, _, _ = head_in(b, h)
        att_ref[b, h] = (_dot_nt(qm, kt) * decay_ref[h]).astype(BF16)
    for b, h in pairs:
        qm, _, _, vh = head_in(b, h)
        o = (_dot(att_ref[b, h], vh)
             + _dot(qm, s_ref[b, h].astype(BF16)) * qin_ref[:, h * LANE:(h + 1) * LANE])
        _ret_finish(o, ins[3 * b + 2], o_ref.at[b], h)
    for b, h in pairs:
        _, _, kw, vh = head_in(b, h)
        s_ref[b, h] = s_ref[b, h] * gc_ref[h:h + 1, :] + _dot(kw.T.astype(BF16), vh)

    @pl.when(i == pl.num_programs(0) - 1)
    def _():
        for b in range(n_seq):
            for h in range(RET_HEADS):
                lo = (h % 2) * RET_QK_DIM
                s_out_ref[b, h] = s_ref[b, h, lo:lo + RET_QK_DIM, :]


def _ret_prompt(P, rt, n_p, s):
    nblk = s // CHUNK
    decay, qin, kout, gc = rt
    blk = lambda b, cb: pl.BlockSpec((CHUNK, COL), lambda i: (b * nblk + i, cb))
    const2 = lambda i: (0, 0)
    seq_specs = [blk(b, cb) for b in range(n_p) for cb in (CB_RQK, CB_RV, CB_RG)]
    st_shape = (n_p, RET_HEADS, RET_QK_DIM, RET_V_DIM)
    vmem = (8 * n_p * _nbytes((CHUNK, COL), BF16) + 2 * _nbytes(decay.shape, F32) + 4 * _nbytes(qin.shape, F32)
            + 3 * n_p * _nbytes((RET_HEADS, LANE, LANE), F32))
    o, st = pl.pallas_call(
        functools.partial(_ret_prompt_kernel, n_seq=n_p),
        grid=(nblk,),
        in_specs=seq_specs + [pl.BlockSpec(decay.shape, lambda i: (0, 0, 0)),
                              pl.BlockSpec(qin.shape, const2), pl.BlockSpec(kout.shape, const2),
                              pl.BlockSpec(gc.shape, const2)],
        out_specs=[pl.BlockSpec((n_p, CHUNK, COL), lambda i: (0, i, 0)),
                   pl.BlockSpec(st_shape, lambda i: (0, 0, 0, 0))],
        out_shape=[jax.ShapeDtypeStruct((n_p, s, COL), BF16), jax.ShapeDtypeStruct(st_shape, F32)],
        scratch_shapes=[pltpu.VMEM((n_p, RET_HEADS, LANE, LANE), F32),
                        pltpu.VMEM((n_p, RET_HEADS, CHUNK, CHUNK), BF16)],
        compiler_params=_params(vmem, ("arbitrary",)),
        name="ret_prompt",
    )(*([P] * (3 * n_p)), decay, qin, kout, gc)
    return o.reshape(n_p * s, COL), st


def _ret_sample_kernel(qk_ref, v_ref, gate_ref, s0_ref, decay_ref, qin_ref, kout_ref, gc_ref,
                       o_ref, s_out_ref, *, t_s):
    row = lax.broadcasted_iota(jnp.int32, (CHUNK, LANE), 0)
    for h in range(RET_HEADS):
        lo = (h % 2) * RET_QK_DIM
        qm, kt, kw, vh = _ret_head_inputs(qk_ref, v_ref, kout_ref, h)
        att = _dot_nt(qm, kt) * decay_ref[h]
        o = _dot(att.astype(BF16), vh)
        inter = jnp.zeros((CHUNK, LANE), F32)
        for sq in range(CHUNK // t_s):
            seq_rows = _idiv(row, t_s) == sq
            st = s0_ref[sq, h]
            st2 = jnp.concatenate([st, st], axis=0).astype(BF16)
            inter = jnp.where(seq_rows, _dot(qm, st2), inter)
            upd = _dot(jnp.where(seq_rows, kw, 0.0).T.astype(BF16), vh)
            s_out_ref[sq, h] = st * gc_ref[h:h + 1, :] + upd[lo:lo + RET_QK_DIM, :]
        o = o + inter * qin_ref[:, h * LANE:(h + 1) * LANE]
        _ret_finish(o, gate_ref, o_ref, h)


def _ret_sample(P, state, layer, rt, Tp, t_s):
    n_s = state.shape[1]
    spb = CHUNK // t_s
    base = Tp // CHUNK
    decay, qin, kout, gc = rt
    blk = lambda cb: pl.BlockSpec((CHUNK, COL), lambda i: (base + i, cb))
    const2 = lambda i: (0, 0)
    st_shape = (spb, RET_HEADS, RET_QK_DIM, RET_V_DIM)
    vmem = (8 * _nbytes((CHUNK, COL), BF16) + 2 * _nbytes(decay.shape, F32) + 4 * _nbytes(qin.shape, F32)
            + 4 * _nbytes(st_shape, F32))
    return pl.pallas_call(
        functools.partial(_ret_sample_kernel, t_s=t_s),
        grid=(n_s // spb,),
        in_specs=[blk(CB_RQK), blk(CB_RV), blk(CB_RG),
                  pl.BlockSpec((None,) + st_shape, lambda i: (layer, i, 0, 0, 0)),
                  pl.BlockSpec(decay.shape, lambda i: (0, 0, 0)),
                  pl.BlockSpec(qin.shape, const2), pl.BlockSpec(kout.shape, const2),
                  pl.BlockSpec(gc.shape, const2)],
        out_specs=[pl.BlockSpec((CHUNK, COL), lambda i: (i, 0)),
                   pl.BlockSpec(st_shape, lambda i: (i, 0, 0, 0))],
        out_shape=[jax.ShapeDtypeStruct((n_s * t_s, COL), BF16),
                   jax.ShapeDtypeStruct((n_s, RET_HEADS, RET_QK_DIM, RET_V_DIM), F32)],
        compiler_params=_params(vmem, ("parallel",)),
        name="ret_sample",
    )(P, P, P, state, decay, qin, kout, gc)


def _branch_kernel(xa_ref, xb_ref, a_ref, g0_ref, g1_ref, g2_ref, wb_ref, wo_ref, ex_ref,
                   po0, po1, po2, pl0, pl1, pl2, pc, so0, so1, so2, sl0, sl1, sl2, sc,
                   y_ref, on_ref, ln_ref, *, n_prompt_tiles):
    i = pl.program_id(0)

    def spread(packed):
        hi = packed.astype(BF16)
        lo = (packed - hi.astype(F32)).astype(BF16)
        return _dot(jnp.concatenate([hi, lo], axis=1), ex_ref[...])

    def natural(o_ref, l_ref, k, dil):
        if dil == 1:
            return (lambda h: o_ref[:, h * LANE:(h + 1) * LANE].astype(F32)), l_ref[...]
        rpt = TILE // dil
        for r in range(dil):
            rows = slice(r * rpt, (r + 1) * rpt)
            ln_ref[k, pl.ds(r, rpt, stride=dil), :] = l_ref[rows, :]
            for h in range(SWA_HEADS):
                on_ref[k, h, pl.ds(r, rpt, stride=dil), :] = o_ref[rows, h * LANE:(h + 1) * LANE].astype(F32)
        return (lambda h: on_ref[k, h]), ln_ref[k]

    def body(o_refs, l_refs, c_ref, permuted):
        groups = [natural(o_refs[k], l_refs[k], k, SWA_PATTERNS[k][1] if permuted else 1)
                  for k in range(len(SWA_PATTERNS))]
        ls = [l for _, l in groups]
        lmax = functools.reduce(jnp.maximum, ls)
        es = [jnp.exp(l - lmax) for l in ls]
        inv = 1.0 / sum(es)
        weights = [spread(e * inv) for e in es]
        heads = []
        for h in range(SWA_HEADS):
            cs = slice(h * LANE, (h + 1) * LANE)
            heads.append(sum(w[:, cs] * go(h) for w, (go, _) in zip(weights, groups)).astype(BF16))
        mix = jnp.concatenate(heads, axis=1)
        merged = (g0_ref[...].astype(F32) * _dot(a_ref[...], wb_ref[0])
                  + g1_ref[...].astype(F32) * _dot(mix, wb_ref[1])
                  + g2_ref[...].astype(F32) * _dot(c_ref[...], wb_ref[2]))
        y_ref[...] = _tile_rows(i, n_prompt_tiles, xa_ref, xb_ref) + _dot(merged.astype(BF16), wo_ref[...])

    @pl.when(i < n_prompt_tiles)
    def _():
        body((po0, po1, po2), (pl0, pl1, pl2), pc, True)

    @pl.when(i >= n_prompt_tiles)
    def _():
        body((so0, so1, so2), (sl0, sl1, sl2), sc, False)


def _branch(xa, xb, out_a, P, wb, wo, prompt_set, sample_set, n_prompt_tiles):
    T = (n_prompt_tiles + 1) * TILE
    assert sample_set[0].shape[0] == TILE, "the sample rows must form exactly one tile"
    n_g = len(SWA_PATTERNS)
    last = n_prompt_tiles - 1
    row = lambda i: (i, 0)
    gate = lambda k: pl.BlockSpec((TILE, D_MODEL), lambda i: (i, CB_GATE * COL // D_MODEL + k))
    once = pl.Buffered(1)
    p_o = pl.BlockSpec((TILE, COL), lambda i: (jnp.minimum(i, last), 0))
    p_l = pl.BlockSpec((TILE, LANE), lambda i: (jnp.minimum(i, last), 0))
    s_o = pl.BlockSpec((TILE, COL), lambda i: (0, 0))
    s_l = pl.BlockSpec((TILE, LANE), lambda i: (0, 0))
    src = jnp.arange(2 * LANE) % LANE
    dst_head = jnp.arange(COL) // LANE
    expand = (src[:, None] == dst_head[None, :] * LSE_LANES).astype(BF16)
    vmem = (4 * _nbytes((TILE, D_MODEL), F32) + 6 * _nbytes((TILE, D_MODEL), BF16)
            + 2 * (2 * n_g + 2) * (_nbytes((TILE, COL), BF16) + _nbytes((TILE, COL), F32))
            + _nbytes(wb.shape, BF16) + _nbytes(wo.shape, BF16)
            + 4 * n_g * _nbytes((TILE, COL), F32) + 4 * _nbytes((TILE, D_MODEL), F32))
    return pl.pallas_call(
        functools.partial(_branch_kernel, n_prompt_tiles=n_prompt_tiles),
        grid=(T // TILE,),
        in_specs=_x_specs(xa, xb) + [
                  pl.BlockSpec((TILE, COL), row),
                  gate(0), gate(1), gate(2),
                  pl.BlockSpec(wb.shape, lambda i: (0, 0, 0), pipeline_mode=once),
                  pl.BlockSpec(wo.shape, lambda i: (0, 0), pipeline_mode=once),
                  pl.BlockSpec(expand.shape, lambda i: (0, 0)),
                  p_o, p_o, p_o, p_l, p_l, p_l, p_o,
                  s_o, s_o, s_o, s_l, s_l, s_l, s_o],
        out_specs=pl.BlockSpec((TILE, D_MODEL), row),
        out_shape=jax.ShapeDtypeStruct((T, D_MODEL), F32),
        scratch_shapes=[pltpu.VMEM((n_g, SWA_HEADS, TILE, LANE), F32),
                        pltpu.VMEM((n_g, TILE, LANE), F32)],
        compiler_params=_params(vmem, ("parallel",)),
        name="branch_merge",
    )(xa, xb, out_a, P, P, P, wb, wo, expand, *prompt_set, *sample_set)


def _swiglu_into(hb, w1_ref, w3_ref, w2_ref, acc_ref, fc):
    n_fc = w1_ref.shape[-1] // fc
    for f in range(n_fc):
        cols = slice(f * fc, (f + 1) * fc)
        a = _dot(hb, w1_ref[:, cols])
        b = _dot(hb, w3_ref[:, cols])
        part = _dot((a * _sigmoid(a) * b).astype(BF16), w2_ref[cols, :])
        if f == 0:
            acc_ref[...] = part
        else:
            acc_ref[...] += part


def _dense_ffn_kernel(x_ref, g2_ref, w1_ref, w3_ref, w2_ref, y_ref, acc_ref, *, fc):
    x = x_ref[...]
    hb = (_rms(x) * g2_ref[...]).astype(BF16)
    _swiglu_into(hb, w1_ref, w3_ref, w2_ref, acc_ref, fc)
    y_ref[...] = x + acc_ref[...]


def _dense_ffn(x, g2, w1b, w3b, w2b):
    T = x.shape[0]
    F = w1b.shape[1]
    tm = _pick(T, (768, 512, 384, 256, 128))
    fc = _pick(F, (512, 256, 128))
    row = lambda i: (i, 0)
    once = pl.Buffered(1)
    vmem = (5 * _nbytes((tm, D_MODEL), F32) + 3 * _nbytes(w1b.shape, BF16) + _nbytes((tm, D_MODEL), BF16)
            + 4 * _nbytes((tm, fc), F32))
    return pl.pallas_call(
        functools.partial(_dense_ffn_kernel, fc=fc),
        grid=(T // tm,),
        in_specs=[pl.BlockSpec((tm, D_MODEL), row),
                  pl.BlockSpec((1, D_MODEL), lambda i: (0, 0)),
                  pl.BlockSpec(w1b.shape, lambda i: (0, 0), pipeline_mode=once),
                  pl.BlockSpec(w3b.shape, lambda i: (0, 0), pipeline_mode=once),
                  pl.BlockSpec(w2b.shape, lambda i: (0, 0), pipeline_mode=once)],
        out_specs=pl.BlockSpec((tm, D_MODEL), row),
        out_shape=jax.ShapeDtypeStruct((T, D_MODEL), F32),
        scratch_shapes=[pltpu.VMEM((tm, D_MODEL), F32)],
        compiler_params=_params(vmem, ("parallel",)),
        name="dense_ffn",
    )(x, g2, w1b, w3b, w2b)


def _router_kernel(x_ref, g2_ref, wr_ref, e0_ref, e1_ref, w0_ref, w1_ref):
    hb = (_rms(x_ref[...]) * g2_ref[...]).astype(BF16)
    logits = _dot(hb, wr_ref[...])
    lane = lax.broadcasted_iota(jnp.int32, logits.shape, 1)
    logits = jnp.where(lane < N_EXPERTS, logits, -jnp.inf)
    lane_f = lane.astype(F32)
    m1 = jnp.max(logits, axis=-1, keepdims=True)
    i1 = jnp.min(jnp.where(logits == m1, lane_f, float(LANE)), axis=-1, keepdims=True)
    rest = jnp.where(lane_f == i1, -jnp.inf, logits)
    m2 = jnp.max(rest, axis=-1, keepdims=True)
    i2 = jnp.min(jnp.where(rest == m2, lane_f, float(LANE)), axis=-1, keepdims=True)
    e = jnp.exp(m2 - m1)
    e0_ref[...] = jnp.broadcast_to(i1, logits.shape).astype(jnp.int32)
    e1_ref[...] = jnp.broadcast_to(i2, logits.shape).astype(jnp.int32)
    w0_ref[...] = jnp.broadcast_to(1.0 / (1.0 + e), logits.shape)
    w1_ref[...] = jnp.broadcast_to(e / (1.0 + e), logits.shape)


def _router(x, g2, wr_pad):
    T = x.shape[0]
    tm = _pick(T, (768, 512, 384, 256, 128))
    row = lambda i: (i, 0)
    out = pl.BlockSpec((tm, LANE), row)
    vmem = 4 * _nbytes((tm, D_MODEL), F32) + 2 * _nbytes(wr_pad.shape, BF16) + 16 * _nbytes((tm, LANE), F32)
    return pl.pallas_call(
        _router_kernel,
        grid=(T // tm,),
        in_specs=[pl.BlockSpec((tm, D_MODEL), row),
                  pl.BlockSpec((1, D_MODEL), lambda i: (0, 0)),
                  pl.BlockSpec(wr_pad.shape, lambda i: (0, 0))],
        out_specs=[out, out, out, out],
        out_shape=[jax.ShapeDtypeStruct((T, LANE), jnp.int32), jax.ShapeDtypeStruct((T, LANE), jnp.int32),
                   jax.ShapeDtypeStruct((T, LANE), F32), jax.ShapeDtypeStruct((T, LANE), F32)],
        compiler_params=_params(vmem, ("parallel",)),
        name="moe_router",
    )(x, g2, wr_pad)


RUN_ALIGN = 8


def _run_pieces(count, max_rows):
    pieces, off = [], 0
    size = max_rows
    while size >= RUN_ALIGN:
        active = (count & size) != 0
        pieces.append((size, off, active))
        off = off + jnp.where(active, size, 0)
        size //= 2
    return pieces


def _packed_rows(i, e0, e1, lower_ref, loc_ref):
    lane = lax.broadcasted_iota(jnp.int32, e0.shape, 1)
    picks = jnp.where((lane == e0) | (lane == e1), 1.0, 0.0)
    rank = _dot(lower_ref[...], picks.astype(BF16))
    lane1 = lax.broadcasted_iota(jnp.int32, (1, LANE), 1)
    run_start = jnp.zeros((1, LANE), F32)
    for e in range(N_EXPERTS):
        run_start = jnp.where(lane1 == e, loc_ref[i * N_EXPERTS + e].astype(F32), run_start)
    row_of = rank + run_start
    return tuple(jnp.sum(jnp.where(lane == ek, row_of, 0.0), axis=-1, keepdims=True) for ek in (e0, e1))


RUN_ROWS = 640


def _dispatch_kernel(base_ref, cnt_ref, loc_ref, pend_ref, x_ref, g2_ref, e0_ref, e1_ref, lower_ref, xs_ref,
                     comp_ref, sem, *, tme, n_tiles):
    i = pl.program_id(0)
    n_steps = pl.num_programs(0)
    cur = i % 2
    half = tme // 2

    @pl.when(i == 0)
    def _():
        comp_ref[0, 0:half] = jnp.zeros((half, D_MODEL), comp_ref.dtype)

        def zero_tile(start):
            return [pltpu.make_async_copy(
                        comp_ref.at[0, pl.ds(0, half), :],
                        xs_ref.at[pl.ds(pl.multiple_of(start + k * half, half), half), :], sem.at[0])
                    for k in range(2)]

        def tail_loop(fn):
            def body(t, carry):
                for c in zero_tile(t * tme):
                    fn(c)
                return carry
            lax.fori_loop(pend_ref[N_EXPERTS - 1] // tme, n_tiles, body, 0)

        def group_tails(fn):
            for e in range(N_EXPERTS):
                prev_end = pend_ref[e - 1] if e else 0

                @pl.when(pend_ref[e] > prev_end)
                def _():
                    for c in zero_tile(pend_ref[e] - tme):
                        fn(c)

        group_tails(lambda c: c.start())
        tail_loop(lambda c: c.start())
        group_tails(lambda c: c.wait())
        tail_loop(lambda c: c.wait())

    h = (_rms(x_ref[...]) * g2_ref[...]).astype(BF16)
    pos0, pos1 = _packed_rows(i, e0_ref[...], e1_ref[...], lower_ref, loc_ref)
    slot = lax.broadcasted_iota(jnp.int32, (TILE, RUN_ROWS), 1).astype(F32)
    hit = jnp.where((slot == pos0) | (slot == pos1), 1.0, 0.0)
    comp_ref[cur] = _dot(hit.T.astype(BF16), h)

    def for_copies(tile, buf, fn):
        for e in range(N_EXPERTS):
            k = tile * N_EXPERTS + e
            cnt, dst, loc = cnt_ref[k], base_ref[k], loc_ref[k]
            for size, off, active in _run_pieces(cnt, TILE):
                c = pltpu.make_async_copy(
                    comp_ref.at[buf, pl.ds(pl.multiple_of(loc + off, RUN_ALIGN), size), :],
                    xs_ref.at[pl.ds(pl.multiple_of(dst + off, RUN_ALIGN), size), :], sem.at[buf])
                pl.when(active)(functools.partial(fn, c))

    for_copies(i, cur, lambda c: c.start())

    @pl.when(i > 0)
    def _():
        for_copies(i - 1, 1 - cur, lambda c: c.wait())

    @pl.when(i == n_steps - 1)
    def _():
        for_copies(i, cur, lambda c: c.wait())


def _dispatch(x, g2, e0, e1, base, cnt, loc, pend, lower, n_tiles, tme):
    T = x.shape[0]
    assert tme == 2 * TILE
    row = lambda i, *_: (i, 0)
    assert RUN_ROWS >= 2 * TILE + N_EXPERTS * (RUN_ALIGN - 1) and RUN_ROWS % LANE == 0
    vmem = (3 * _nbytes((TILE, D_MODEL), F32) + 2 * _nbytes((RUN_ROWS, D_MODEL), F32)
            + 8 * _nbytes((RUN_ROWS, TILE), F32))
    grid_spec = pltpu.PrefetchScalarGridSpec(
        num_scalar_prefetch=4,
        grid=(T // TILE,),
        in_specs=[pl.BlockSpec((TILE, D_MODEL), row),
                  pl.BlockSpec((1, D_MODEL), lambda i, *_: (0, 0)),
                  pl.BlockSpec((TILE, LANE), row), pl.BlockSpec((TILE, LANE), row),
                  pl.BlockSpec((TILE, TILE), lambda i, *_: (0, 0))],
        out_specs=pl.BlockSpec(memory_space=pl.ANY),
        scratch_shapes=[pltpu.VMEM((2, RUN_ROWS, D_MODEL), F32), pltpu.SemaphoreType.DMA((2,))],
    )
    return pl.pallas_call(
        functools.partial(_dispatch_kernel, tme=tme, n_tiles=n_tiles),
        grid_spec=grid_spec,
        out_shape=jax.ShapeDtypeStruct((n_tiles * tme, D_MODEL), F32),
        compiler_params=_params(vmem, ("arbitrary",)),
        name="moe_dispatch",
    )(base, cnt, loc, pend, x, g2, e0, e1, lower)


def _expert_ffn_kernel(te_ref, used_ref, xs_ref, w1_ref, w3_ref, w2_ref, y_ref, acc_ref, *, fc):
    i = pl.program_id(0)

    @pl.when(i < used_ref[0])
    def _():
        _swiglu_into(xs_ref[...].astype(BF16), w1_ref, w3_ref, w2_ref, acc_ref, fc)
        y_ref[...] = acc_ref[...]

    @pl.when(i >= used_ref[0])
    def _():
        y_ref[...] = jnp.zeros_like(y_ref)


def _expert_ffn(xs, tile_expert, n_used, we1b, we3b, we2b, tme):
    n_rows = xs.shape[0]
    F = we1b.shape[2]
    fc = _pick(F, (512, 256, 128))
    once = pl.Buffered(1)
    wspec = lambda shape: pl.BlockSpec((None,) + shape, lambda i, te, nu: (te[i], 0, 0), pipeline_mode=once)
    vmem = (5 * _nbytes((tme, D_MODEL), F32) + 3 * _nbytes(we1b.shape[1:], BF16) + _nbytes((tme, D_MODEL), BF16)
            + 4 * _nbytes((tme, fc), F32))
    grid_spec = pltpu.PrefetchScalarGridSpec(
        num_scalar_prefetch=2,
        grid=(n_rows // tme,),
        in_specs=[pl.BlockSpec((tme, D_MODEL), lambda i, te, nu: (jnp.minimum(i, jnp.maximum(nu[0], 1) - 1), 0)),
                  wspec(we1b.shape[1:]), wspec(we3b.shape[1:]), wspec(we2b.shape[1:])],
        out_specs=pl.BlockSpec((tme, D_MODEL), lambda i, te, nu: (i, 0)),
        scratch_shapes=[pltpu.VMEM((tme, D_MODEL), F32)],
    )
    return pl.pallas_call(
        functools.partial(_expert_ffn_kernel, fc=fc),
        grid_spec=grid_spec,
        out_shape=jax.ShapeDtypeStruct((n_rows, D_MODEL), F32),
        compiler_params=_params(vmem, ("arbitrary",)),
        name="moe_expert_ffn",
    )(tile_expert, n_used, xs, we1b, we3b, we2b)


def _combine_kernel(base_ref, cnt_ref, loc_ref, x_ref, e0_ref, e1_ref, w0_ref, w1_ref, lower_ref, y_hbm_ref,
                    o_ref, o_last_ref, buf_ref, sem):
    i = pl.program_id(0)
    n_steps = pl.num_programs(0)
    cur = i % 2

    def for_copies(tile, buf, fn):
        for e in range(N_EXPERTS):
            k = tile * N_EXPERTS + e
            cnt, src, loc = cnt_ref[k], base_ref[k], loc_ref[k]
            for size, off, active in _run_pieces(cnt, TILE):
                c = pltpu.make_async_copy(
                    y_hbm_ref.at[pl.ds(pl.multiple_of(src + off, RUN_ALIGN), size), :],
                    buf_ref.at[buf, pl.ds(pl.multiple_of(loc + off, RUN_ALIGN), size), :], sem.at[buf])
                pl.when(active)(functools.partial(fn, c))

    @pl.when(i == 0)
    def _():
        buf_ref[...] = jnp.zeros_like(buf_ref)
        for_copies(0, 0, lambda c: c.start())

    @pl.when(i + 1 < n_steps)
    def _():
        for_copies(i + 1, 1 - cur, lambda c: c.start())

    for_copies(i, cur, lambda c: c.wait())

    positions = _packed_rows(i, e0_ref[...], e1_ref[...], lower_ref, loc_ref)
    slot = lax.broadcasted_iota(jnp.int32, (TILE, RUN_ROWS), 1).astype(F32)
    yb = buf_ref[cur].astype(BF16)
    wide = lambda a, n: jnp.concatenate([a] * n, axis=1)
    acc = x_ref[...]
    for pos, wk in zip(positions, (w0_ref, w1_ref)):
        rows = _dot(jnp.where(slot == pos, 1.0, 0.0).astype(BF16), yb)
        acc = acc + wide(wk[...], D_MODEL // LANE) * rows

    @pl.when(i < n_steps - 1)
    def _():
        o_ref[...] = acc

    @pl.when(i == n_steps - 1)
    def _():
        o_last_ref[...] = acc


def _combine(x, e0, e1, w0, w1, base, cnt, loc, lower, y_rows):
    T = x.shape[0]
    row = lambda i, *_: (i, 0)
    lane_spec = pl.BlockSpec((TILE, LANE), row)
    vmem = (5 * _nbytes((TILE, D_MODEL), F32) + 3 * _nbytes((RUN_ROWS, D_MODEL), F32)
            + 8 * _nbytes((TILE, LANE), F32) + 8 * _nbytes((TILE, RUN_ROWS), F32)
            + 4 * _nbytes((TILE, D_MODEL), F32))
    grid_spec = pltpu.PrefetchScalarGridSpec(
        num_scalar_prefetch=3,
        grid=(T // TILE,),
        in_specs=[pl.BlockSpec((TILE, D_MODEL), row), lane_spec, lane_spec, lane_spec, lane_spec,
                  pl.BlockSpec((TILE, TILE), lambda i, *_: (0, 0)),
                  pl.BlockSpec(memory_space=pl.ANY)],
        out_specs=[pl.BlockSpec((TILE, D_MODEL), lambda i, *_: (jnp.minimum(i, T // TILE - 2), 0)),
                   pl.BlockSpec((TILE, D_MODEL), lambda i, *_: (0, 0))],
        scratch_shapes=[pltpu.VMEM((2, RUN_ROWS, D_MODEL), F32), pltpu.SemaphoreType.DMA((2,))],
    )
    return pl.pallas_call(
        _combine_kernel,
        grid_spec=grid_spec,
        out_shape=[jax.ShapeDtypeStruct((T - TILE, D_MODEL), F32), jax.ShapeDtypeStruct((TILE, D_MODEL), F32)],
        compiler_params=_params(vmem, ("arbitrary",)),
        name="moe_combine",
    )(base, cnt, loc, x, e0, e1, w0, w1, lower, y_rows)


def _moe(x, g2, wr_pad, we1b, we3b, we2b):
    T = x.shape[0]
    tme = 2 * TILE
    n_tt = T // TILE
    e0, e1, w0, w1 = _router(x, g2, wr_pad)
    experts = jnp.arange(N_EXPERTS, dtype=jnp.int32)
    picks = (e0[:, :1] == experts[None, :]) | (e1[:, :1] == experts[None, :])
    cnt = jnp.sum(picks.reshape(n_tt, TILE, N_EXPERTS).astype(jnp.int32), axis=1)
    cnt = (cnt + RUN_ALIGN - 1) // RUN_ALIGN * RUN_ALIGN
    counts = jnp.sum(cnt, axis=0)
    padded = (counts + tme - 1) // tme * tme
    pend = jnp.cumsum(padded).astype(jnp.int32)
    base = ((pend - padded)[None, :] + jnp.cumsum(cnt, axis=0) - cnt).astype(jnp.int32)
    loc = (jnp.cumsum(cnt, axis=1) - cnt).astype(jnp.int32)
    max_rows = T * TOP_K + n_tt * N_EXPERTS * (RUN_ALIGN - 1)
    n_tiles = -(-max_rows // tme) + N_EXPERTS
    tile_start = jnp.arange(n_tiles, dtype=jnp.int32) * tme
    tile_expert = jnp.minimum(jnp.sum(pend[None, :] <= tile_start[:, None], axis=1), N_EXPERTS - 1).astype(jnp.int32)
    n_used = (pend[N_EXPERTS - 1:] // tme).astype(jnp.int32)
    tok = jnp.arange(TILE)
    lower = (tok[None, :] < tok[:, None]).astype(BF16)
    base_f, cnt_f, loc_f = base.reshape(-1), cnt.reshape(-1).astype(jnp.int32), loc.reshape(-1)
    xs = _dispatch(x, g2, e0, e1, base_f, cnt_f, loc_f, pend, lower, n_tiles, tme)
    y_rows = _expert_ffn(xs, tile_expert, n_used, we1b, we3b, we2b, tme)
    return _combine(x, e0, e1, w0, w1, base_f, cnt_f, loc_f, lower, y_rows)


def _class_major(a, dil):
    n = a.shape[0]
    return a.reshape((n // TILE, TILE // dil, dil) + a.shape[1:]).swapaxes(1, 2).reshape(a.shape)


def _natural(a, dil):
    n = a.shape[0]
    return a.reshape((n // TILE, dil, TILE // dil) + a.shape[1:]).swapaxes(1, 2).reshape(a.shape)


def _perm_matrices():
    eye = jnp.eye(TILE, dtype=BF16)
    return jnp.stack([_class_major(eye, dil) for _, dil in SWA_PATTERNS[1:]])


def _rope_tables(pos, Tp):
    freqs = ROPE_THETA ** (-jnp.arange(0, SWA_HEAD_DIM, 2, dtype=F32) / SWA_HEAD_DIM)
    rfreqs = 1.0 / (ROPE_THETA ** jnp.linspace(0.0, 1.0, RET_QK_DIM // 2, dtype=F32))

    def swa(p):
        ang = p.astype(F32)[:, None] * freqs[None, :]
        c, s = jnp.cos(ang), jnp.sin(ang)
        return jnp.concatenate([c, c], axis=1), jnp.concatenate([-s, s], axis=1)

    per_group = [swa(jnp.concatenate([_class_major(pos[:Tp], dil), pos[Tp:]])) for _, dil in SWA_PATTERNS]
    cosb = jnp.stack([c for c, _ in per_group])
    sinb = jnp.stack([s for _, s in per_group])
    rang = pos.astype(F32)[:, None] * rfreqs[None, :]
    c, s = jnp.cos(rang), jnp.sin(rang)
    cosc = jnp.concatenate([c, c, c, c], axis=1)
    sinc = jnp.concatenate([-s, s, -s, s], axis=1)
    return cosb, sinb, cosc, sinc


def _ret_tables(c_len):
    log_g = jnp.log1p(-jnp.exp2(-5.0 - jnp.arange(RET_HEADS, dtype=F32)))
    r = jnp.arange(CHUNK)
    i = (r % c_len).astype(F32)
    same = (r[:, None] // c_len) == (r[None, :] // c_len)
    dist = i[:, None] - i[None, :]
    decay = jnp.where(same[None] & (dist >= 0)[None],
                      jnp.exp(log_g[:, None, None] * jnp.maximum(dist, 0.0)[None]), 0.0)
    qin = jnp.repeat(jnp.exp(log_g[None, :] * (i[:, None] + 1.0)), RET_V_DIM, axis=1)
    kout = jnp.repeat(jnp.exp(log_g[None, :] * (c_len - 1.0 - i)[:, None]), RET_QK_DIM, axis=1)
    gc = jnp.broadcast_to(jnp.exp(log_g * c_len)[:, None], (RET_HEADS, LANE))
    gc = jnp.concatenate([gc, jnp.zeros((8 - RET_HEADS, LANE), F32)], axis=0)
    return decay.astype(F32), qin.astype(F32), kout.astype(F32), gc.astype(F32)


def _mixa_tables(w_s, b_s, t_s):
    w_p = jnp.tril(w_s)
    w8 = jnp.tril(w_s[:, :t_s, :t_s])
    eye = jnp.eye(CHUNK // t_s, dtype=w_s.dtype)
    w_smp = jax.vmap(lambda m: jnp.kron(eye, m))(w8)
    w2 = jnp.stack([w_p, w_smp]).astype(BF16)
    b_p = jnp.repeat(b_s.T, LANE, axis=1)
    b_smp = jnp.repeat(jnp.tile(b_s[:, :t_s].T, (CHUNK // t_s, 1)), LANE, axis=1)
    return w2, jnp.stack([b_p, b_smp]).astype(F32)


def kernel(x_prompt, x_sample, cache_swa_kv0, cache_swa_kv1, cache_swa_kv2, state_ret, norm1_g, w_in, norm_v_g, w_s, b_s, q_norm_g, k_norm_g, w_branch, w_out, norm2_g, w1, w3, w2, w_router, we1, we3, we2):
    n_p, s, d = x_prompt.shape
    n_s, t_s, _ = x_sample.shape
    depth = w_in.shape[0]
    Tp, Ts = n_p * s, n_s * t_s
    T = Tp + Ts
    max_win, max_dil = SWA_PATTERNS[-1]
    assert d == D_MODEL and Ts == TILE and CHUNK % t_s == 0
    assert s % (CHUNK * max_dil) == 0 and s >= max_win
    n_pt = Tp // TILE

    xa, xb = x_prompt.reshape(Tp, d), x_sample.reshape(Ts, d)
    pos = jnp.concatenate([jnp.arange(s, dtype=jnp.int32),
                           jnp.tile(PAST_LEN + jnp.arange(t_s, dtype=jnp.int32), n_s)])
    tabs = _rope_tables(pos, s)
    perm = _perm_matrices()
    rt_prompt = _ret_tables(CHUNK)
    rt_sample = _ret_tables(t_s)
    caches = tuple(c.reshape(c.shape[0], c.shape[1], -1, SWA_HEAD_DIM)
                   for c in (cache_swa_kv0, cache_swa_kv1, cache_swa_kv2))

    p_kv = [[] for _ in SWA_PATTERNS]
    s_kv = [[] for _ in SWA_PATTERNS]
    p_ret, s_ret, s_v = [], [], []
    for layer in range(depth):
        if layer == 0:
            later = (w_in[1:], w_branch, w_out, w1, w3, w2, we1, we3, we2)
            P, (w_in_later, w_branch_b, w_out_b, w1_b, w3_b, w2_b, we1_b, we3_b, we2_b) = _inproj(
                xa, xb, norm1_g[0][None, :], w_in[0].astype(BF16), perm, tabs, norm_v_g[0][None, :],
                q_norm_g[0][None, :], k_norm_g[0][None, :], n_pt, s // TILE, later)
        else:
            P, _ = _inproj(xa, xb, norm1_g[layer][None, :], w_in_later[layer - 1], perm, tabs,
                           norm_v_g[layer][None, :], q_norm_g[layer][None, :], k_norm_g[layer][None, :],
                           n_pt, s // TILE)

        w2a, b2a = _mixa_tables(w_s[layer], b_s[layer], t_s)
        out_a = _mixer_a(P, w2a, b2a, Tp // CHUNK)

        Ps = P[Tp:].astype(F32).reshape(n_s, t_s, IN_WIDTH)
        po, plse, so, slse = [], [], [], []
        for g, (win, dil) in enumerate(SWA_PATTERNS):
            o_g, l_g = _swa_prompt(P, g, dil, n_p, s)
            os_g, ls_g = _swa_sample(Ps, caches[g], layer, g, dil)
            po.append(o_g)
            plse.append(l_g)
            so.append(os_g.reshape(Ts, COL).astype(BF16))
            slse.append(ls_g.reshape(Ts, LANE))

        out_c, ret_p = _ret_prompt(P, rt_prompt, n_p, s)
        out_cs, ret_s = _ret_sample(P, state_ret, layer, rt_sample, Tp, t_s)

        x = _branch(xa, xb, out_a, P, w_branch_b[layer], w_out_b[layer],
                    (*po, *plse, out_c), (*so, *slse, out_cs), n_pt)

        g2 = norm2_g[layer][None, :]
        i = layer // 2
        if layer % 2 == 0:
            x = _dense_ffn(x, g2, w1_b[i], w3_b[i], w2_b[i])
            xa = xb = x
        else:
            wr_pad = jnp.zeros((D_MODEL, LANE), BF16).at[:, :N_EXPERTS].set(w_router[i].astype(BF16))
            xa, xb = _moe(x, g2, wr_pad, we1_b[i], we3_b[i], we2_b[i])

        for g, (win, dil) in enumerate(SWA_PATTERNS):
            kcols = slice((CB_K + g) * COL, (CB_K + g + 1) * COL)
            vcols = slice((CB_V + g) * COL, (CB_V + g + 1) * COL)
            keep = -(-min(win, s) // TILE) * TILE

            def rows(cols):
                blk = jnp.concatenate([P[(b + 1) * s - keep:(b + 1) * s, cols] for b in range(n_p)])
                blk = _natural(blk, dil).reshape(n_p, keep, SWA_HEADS, SWA_HEAD_DIM)
                return blk[:, keep - min(win, s):]

            p_kv[g].append(jnp.stack([rows(kcols), rows(vcols)], axis=2).astype(F32))
            ks = P[Tp:, kcols].reshape(n_s, t_s, SWA_HEADS, SWA_HEAD_DIM)
            vs = P[Tp:, vcols].reshape(n_s, t_s, SWA_HEADS, SWA_HEAD_DIM)
            s_kv[g].append(jnp.stack([ks, vs], axis=2).astype(F32))
        p_ret.append(ret_p)
        s_ret.append(ret_s)
        s_v.append(P[Tp:, CB_AV * COL:(CB_AV + 1) * COL].astype(F32).reshape(n_s, t_s, COL))

    y_prompt = xa[:Tp].reshape(n_p, s, d)
    y_sample = xb[xb.shape[0] - Ts:].reshape(n_s, t_s, d)
    return (y_prompt, y_sample,
            jnp.stack(p_kv[0]), jnp.stack(p_kv[1]), jnp.stack(p_kv[2]), jnp.stack(p_ret),
            jnp.stack(s_kv[0]), jnp.stack(s_kv[1]), jnp.stack(s_kv[2]), jnp.stack(s_ret),
            jnp.stack(s_v))
```

```python
import functools
import math

import jax
import jax.numpy as jnp
from jax import lax
from jax.experimental import pallas as pl
from jax.experimental.pallas import tpu as pltpu

F32 = jnp.float32
BF16 = jnp.bfloat16

PAST_LEN = 16384
EPS = 1e-6
NEG_INF = -1e30
ROPE_THETA = 10000.0

D_MODEL = 1024
LANE = 128
BF16_ROWS = 16
CHUNK = 128
TILE = 256
COL = 512
A_GROUPS = 4
SWA_PATTERNS = ((128, 1), (512, 4), (2048, 16))
SWA_HEADS = 4
SWA_HEAD_DIM = 128
RET_HEADS = 4
RET_QK_DIM = 64
RET_V_DIM = 128
N_EXPERTS = 8
TOP_K = 2
IN_WIDTH = 10240
N_COL = IN_WIDTH // COL

CB_AU, CB_AV, CB_Q, CB_K, CB_V, CB_RQK, CB_RV, CB_RG, CB_GATE = 0, 1, 2, 5, 8, 11, 12, 13, 14

VMEM_INTERNAL_SCRATCH = 8 * 1024 * 1024


def _pick(n, candidates):
    for c in candidates:
        if n % c == 0:
            return c
    raise ValueError(f"no tile in {candidates} divides {n}")


def _params(block_bytes, semantics=None):
    limit = int(block_bytes) + VMEM_INTERNAL_SCRATCH
    return pltpu.CompilerParams(dimension_semantics=semantics, vmem_limit_bytes=limit)


def _hbm(*arrays):
    out = tuple(pltpu.with_memory_space_constraint(a, pltpu.HBM) for a in arrays)
    return out if len(out) > 1 else out[0]


def _nbytes(shape, dtype):
    return math.prod(shape) * jnp.dtype(dtype).itemsize


def _rms(x):
    return x * lax.rsqrt(jnp.mean(x * x, axis=-1, keepdims=True) + EPS)


def _gelu(x):
    return 0.5 * x * (1.0 + lax.erf(x * (0.5 ** 0.5)))


def _sigmoid(x):
    return 1.0 / (1.0 + jnp.exp(-x))


def _idiv(x, n):
    assert n & (n - 1) == 0
    return x >> (n.bit_length() - 1)


def _imod(x, n):
    assert n & (n - 1) == 0
    return x & (n - 1)


def _dot(a, b):
    return jnp.dot(a, b, preferred_element_type=F32)


LSE_LANES = LANE // 4


def _pack_heads(cols):
    rows = cols[0].shape[0]
    grp = _idiv(lax.broadcasted_iota(jnp.int32, (rows, LANE), 1), LSE_LANES)
    out = jnp.broadcast_to(cols[-1], (rows, LANE))
    for h in range(len(cols) - 2, -1, -1):
        out = jnp.where(grp == h, cols[h], out)
    return out


def _dot_nt(a, b):
    return lax.dot_general(a, b, (((1,), (1,)), ((), ())), preferred_element_type=F32)


def _tile_rows(i, n_prompt_tiles, xa_ref, xb_ref):
    return jnp.where(i < n_prompt_tiles, xa_ref[...], xb_ref[...])


def _x_specs(xa, xb):
    last_a, last_b = xa.shape[0] // TILE - 1, xb.shape[0] // TILE - 1
    return [pl.BlockSpec((TILE, D_MODEL), lambda i, *_: (jnp.minimum(i, last_a), 0)),
            pl.BlockSpec((TILE, D_MODEL), lambda i, *_: (last_b, 0))]


def _inproj_kernel(*refs, n_prompt_tiles, n_cast):
    (xa_ref, xb_ref, g1_ref, w_ref, perm_ref, cosb_ref, sinb_ref, cosc_ref, sinc_ref,
     nvg_ref, qg_ref, kg_ref) = refs[:12]
    cast_in, o_ref, cast_out, h_ref = refs[12:12 + n_cast], refs[12 + n_cast], refs[13 + n_cast:-1], refs[-1]
    i = pl.program_id(0)
    for src, dst in zip(cast_in, cast_out):
        dst[...] = src[...].astype(dst.dtype)
    hn = (_rms(_tile_rows(i, n_prompt_tiles, xa_ref, xb_ref)) * g1_ref[...]).astype(BF16)
    h_ref[0] = hn

    @pl.when(i < n_prompt_tiles)
    def _():
        for k in range(1, len(SWA_PATTERNS)):
            h_ref[k] = _dot(perm_ref[k - 1], hn).astype(BF16)

    @pl.when(i >= n_prompt_tiles)
    def _():
        for k in range(1, len(SWA_PATTERNS)):
            h_ref[k] = hn

    def qk_heads(acc, g, cols0, gain_ref, scale):
        for hh in range(SWA_HEADS):
            cs = slice(hh * LANE, (hh + 1) * LANE)
            y = _rms(acc[:, cs]) * gain_ref[...]
            rot = y * cosb_ref[g] + pltpu.roll(y, LANE // 2, axis=1) * sinb_ref[g]
            o_ref[:, cols0 + hh * LANE:cols0 + (hh + 1) * LANE] = (rot * scale).astype(o_ref.dtype)

    for j in range(N_COL):
        cols = slice(j * COL, (j + 1) * COL)
        g = (j - CB_Q) % len(SWA_PATTERNS) if CB_Q <= j < CB_RQK else 0
        acc = _dot(h_ref[g], w_ref[:, cols])
        if j == CB_AU:
            o_ref[:, cols] = _gelu(acc).astype(o_ref.dtype)
        elif j == CB_AV:
            o_ref[:, cols] = (_rms(_gelu(acc)) * nvg_ref[...]).astype(o_ref.dtype)
        elif CB_Q <= j < CB_K:
            qk_heads(acc, g, j * COL, qg_ref, SWA_HEAD_DIM ** -0.5)
        elif CB_K <= j < CB_V:
            qk_heads(acc, g, j * COL, kg_ref, 1.0)
        elif j < CB_RQK or j == CB_RV:
            o_ref[:, cols] = acc.astype(o_ref.dtype)
        elif j == CB_RQK:
            lane = lax.broadcasted_iota(jnp.int32, (acc.shape[0], LANE), 1)
            first_half = _imod(lane, RET_QK_DIM) < (RET_QK_DIM // 2)
            for tt in range(COL // LANE):
                y = acc[:, tt * LANE:(tt + 1) * LANE]
                partner = jnp.where(first_half,
                                    pltpu.roll(y, LANE - RET_QK_DIM // 2, axis=1),
                                    pltpu.roll(y, RET_QK_DIM // 2, axis=1))
                rot = y * cosc_ref[...] + partner * sinc_ref[...]
                scale = 1.0 if tt < (COL // LANE) // 2 else RET_QK_DIM ** -0.5
                o_ref[:, j * COL + tt * LANE:j * COL + (tt + 1) * LANE] = (rot * scale).astype(o_ref.dtype)
        elif j == CB_RG:
            o_ref[:, cols] = (acc * _sigmoid(acc)).astype(o_ref.dtype)
        else:
            o_ref[:, cols] = _sigmoid(acc).astype(o_ref.dtype)


def _cast_blocks(arr, n_steps, skip):
    a2 = arr.reshape(-1, arr.shape[-1])
    first_row = skip * (a2.shape[0] // arr.shape[0])
    rows = a2.shape[0] - first_row
    rb = -(-rows // n_steps)
    rb = -(-rb // BF16_ROWS) * BF16_ROWS
    while rows % rb or first_row % rb:
        rb += BF16_ROWS
    return a2, rb, first_row // rb


def _inproj(xa, xb, g1, w_in_b, perm, tabs, nvg, qg, kg, n_prompt_tiles, tiles_per_seq, to_cast=()):
    T = (n_prompt_tiles + 1) * TILE
    n_steps = T // TILE
    casts = [_cast_blocks(a, n_steps, skip) for a, skip in to_cast]
    n_blk = [a2.shape[0] // rb - first for a2, rb, first in casts]
    cast_in = [pl.BlockSpec((rb, a2.shape[1]), functools.partial(
        lambda i, first, last: (first + jnp.minimum(i, last), 0), first=first, last=n - 1))
        for (a2, rb, first), n in zip(casts, n_blk)]
    cast_out = [pl.BlockSpec((rb, a2.shape[1]), functools.partial(
        lambda i, last: (jnp.minimum(i, last), 0), last=n - 1)) for (a2, rb, _), n in zip(casts, n_blk)]
    cosb, sinb, cosc, sinc = tabs
    n_g = len(SWA_PATTERNS)
    row = lambda i: (i, 0)
    const = lambda i: (0, 0)
    tab_blk = lambda i: jnp.where(i < n_prompt_tiles, i % tiles_per_seq, tiles_per_seq)
    tab3 = pl.BlockSpec((n_g, TILE, LANE), lambda i: (0, tab_blk(i), 0))
    tab = pl.BlockSpec((TILE, LANE), lambda i: (tab_blk(i), 0))
    vmem = (2 * _nbytes((TILE, D_MODEL), F32) + n_g * _nbytes((TILE, D_MODEL), BF16)
            + _nbytes(w_in_b.shape, BF16) + 2 * _nbytes((TILE, IN_WIDTH), BF16)
            + 2 * (2 * n_g + 2) * _nbytes((TILE, LANE), F32) + 2 * _nbytes(perm.shape, BF16)
            + 8 * _nbytes((TILE, COL), F32)
            + sum(2 * (_nbytes((rb, a2.shape[1]), F32) + _nbytes((rb, a2.shape[1]), BF16)) for a2, rb, _ in casts))
    outs = pl.pallas_call(
        functools.partial(_inproj_kernel, n_prompt_tiles=n_prompt_tiles, n_cast=len(casts)),
        grid=(n_steps,),
        in_specs=_x_specs(xa, xb) + [
            pl.BlockSpec((1, D_MODEL), const),
            pl.BlockSpec(w_in_b.shape, const, pipeline_mode=pl.Buffered(1)),
            pl.BlockSpec(perm.shape, lambda i: (0, 0, 0)),
            tab3, tab3, tab, tab,
            pl.BlockSpec((1, COL), const),
            pl.BlockSpec((1, LANE), const),
            pl.BlockSpec((1, LANE), const),
        ] + cast_in,
        out_specs=[pl.BlockSpec((TILE, IN_WIDTH), row)] + cast_out,
        out_shape=[jax.ShapeDtypeStruct((T, IN_WIDTH), BF16)]
                  + [jax.ShapeDtypeStruct((n * rb, a2.shape[1]), BF16) for (a2, rb, _), n in zip(casts, n_blk)],
        scratch_shapes=[pltpu.VMEM((n_g, TILE, D_MODEL), BF16)],
        compiler_params=_params(vmem, ("arbitrary",)),
        name="inproj",
    )(xa, xb, g1, w_in_b, perm, cosb, sinb, cosc, sinc, nvg, qg, kg, *[a2 for a2, _, _ in casts])
    return outs[0], [o.reshape((a.shape[0] - skip,) + a.shape[1:]) for o, (a, skip) in zip(outs[1:], to_cast)]


def _mixa_kernel(u_ref, v_ref, w_ref, b_ref, o_ref, *, cps, n_prompt_chunks):
    i = pl.program_id(0)
    for c in range(cps):
        var = ((i * cps + c) >= n_prompt_chunks).astype(jnp.int32)
        rows = slice(c * CHUNK, (c + 1) * CHUNK)
        for g in range(A_GROUPS):
            cols = slice(g * LANE, (g + 1) * LANE)
            z = _dot(w_ref[var, g], v_ref[rows, cols]) + b_ref[var, :, cols]
            o_ref[rows, cols] = (u_ref[rows, cols].astype(F32) * z).astype(o_ref.dtype)


def _mixer_a(P, w2, b2, n_prompt_chunks):
    T = P.shape[0]
    n_chunks = T // CHUNK
    cps = _pick(n_chunks, (8, 6, 4, 3, 2, 1))
    rows = cps * CHUNK
    vmem = 6 * _nbytes((rows, COL), BF16) + 2 * _nbytes(w2.shape, BF16) + 2 * _nbytes(b2.shape, F32)
    return pl.pallas_call(
        functools.partial(_mixa_kernel, cps=cps, n_prompt_chunks=n_prompt_chunks),
        grid=(n_chunks // cps,),
        in_specs=[
            pl.BlockSpec((rows, COL), lambda i: (i, CB_AU)),
            pl.BlockSpec((rows, COL), lambda i: (i, CB_AV)),
            pl.BlockSpec(w2.shape, lambda i: (0, 0, 0, 0)),
            pl.BlockSpec(b2.shape, lambda i: (0, 0, 0)),
        ],
        out_specs=pl.BlockSpec((rows, COL), lambda i: (i, 0)),
        out_shape=jax.ShapeDtypeStruct((T, COL), BF16),
        compiler_params=_params(vmem, ("parallel",)),
        name="mixer_a",
    )(P, P, w2, b2)


def _swa_kernel(q_ref, kp_ref, kc_ref, vp_ref, vc_ref, o_ref, l_ref, k_ref, v_ref, s_ref, p_ref, *, qb):
    i = pl.program_id(2)
    rows = qb * CHUNK
    lead = q_ref.shape[:-1]
    k_ref[0:CHUNK] = kp_ref[...].reshape(CHUNK, COL)
    k_ref[CHUNK:CHUNK + rows] = kc_ref[...].reshape(rows, COL)
    v_ref[0:CHUNK] = vp_ref[...].reshape(CHUNK, COL)
    v_ref[CHUNK:CHUNK + rows] = vc_ref[...].reshape(rows, COL)
    row = lax.broadcasted_iota(jnp.int32, (CHUNK, 2 * CHUNK), 0)
    col = lax.broadcasted_iota(jnp.int32, (CHUNK, 2 * CHUNK), 1)
    mask_cur = (col >= CHUNK) & (col - CHUNK <= row)
    mask_all = mask_cur | ((col < CHUNK) & (col >= row))
    mask_first = mask_cur | ((col < CHUNK) & (col >= row) & (i > 0))
    q_all = q_ref[...].reshape(rows, COL)
    for j in range(qb):
        mask = mask_first if j == 0 else mask_all
        for h in range(SWA_HEADS):
            cs = slice(h * LANE, (h + 1) * LANE)
            sc = _dot_nt(q_all[j * CHUNK:(j + 1) * CHUNK, cs], k_ref[j * CHUNK:(j + 2) * CHUNK, cs])
            s_ref[j * SWA_HEADS + h] = jnp.where(mask, sc, NEG_INF)
    s = s_ref[...]
    m = jnp.max(s, axis=-1, keepdims=True)
    p = jnp.exp(s - m)
    den = jnp.sum(p, axis=-1, keepdims=True)
    p_ref[...] = p.astype(BF16)
    lse = m + jnp.log(den)
    for h in range(SWA_HEADS):
        cs = slice(h * LANE, (h + 1) * LANE)
        o_h = [_dot(p_ref[j * SWA_HEADS + h], v_ref[j * CHUNK:(j + 2) * CHUNK, cs]) / den[j * SWA_HEADS + h]
               for j in range(qb)]
        o_ref[..., cs] = jnp.concatenate(o_h, axis=0).astype(o_ref.dtype).reshape(lead + (LANE,))
    packed = [_pack_heads([lse[j * SWA_HEADS + h] for h in range(SWA_HEADS)]) for j in range(qb)]
    l_ref[...] = jnp.concatenate(packed, axis=0).reshape(lead + (LANE,))


def _swa_prompt(P, g, dil, n_p, s):
    T = P.shape[0]
    Tp = n_p * s
    nb = s // dil // CHUNK
    qb = _pick(nb, (8, 4, 2, 1))
    steps = nb // qb
    prev_blk = lambda b, i: b * nb + jnp.maximum(i * qb - 1, 0)
    if dil == 1:
        src = P
        lead_q, lead_p = (qb * CHUNK,), (CHUNK,)
        o_shape = (Tp, COL)
        l_shape = (Tp, LANE)
        q_map = lambda cb: (lambda b, r, i: (b * steps + i, cb + g))
        p_map = lambda cb: (lambda b, r, i: (prev_blk(b, i), cb + g))
        o_map = lambda b, r, i: (b * steps + i, 0)
        l_map = lambda b, r, i: (b * steps + i, 0)
    else:
        rpt = TILE // dil
        tpb = CHUNK // rpt
        src = P.reshape(T // TILE, dil, rpt, IN_WIDTH)
        lead_q, lead_p = (qb * tpb, None, rpt), (tpb, None, rpt)
        o_shape = (Tp // TILE, dil, rpt, COL)
        l_shape = (Tp // TILE, dil, rpt, LANE)
        q_map = lambda cb: (lambda b, r, i: (b * steps + i, r, 0, cb + g))
        p_map = lambda cb: (lambda b, r, i: (prev_blk(b, i), r, 0, cb + g))
        o_map = lambda b, r, i: (b * steps + i, r, 0, 0)
        l_map = lambda b, r, i: (b * steps + i, r, 0, 0)

    q_spec = lambda cb: pl.BlockSpec(lead_q + (COL,), q_map(cb))
    p_spec = lambda cb: pl.BlockSpec(lead_p + (COL,), p_map(cb))
    n_pairs = qb * SWA_HEADS
    vmem = ((10 * qb + 6) * _nbytes((CHUNK, COL), BF16) + 2 * qb * _nbytes((CHUNK, COL), F32)
            + 6 * n_pairs * _nbytes((CHUNK, 2 * CHUNK), F32))
    o, l = pl.pallas_call(
        functools.partial(_swa_kernel, qb=qb),
        grid=(n_p, dil, steps),
        in_specs=[q_spec(CB_Q), p_spec(CB_K), q_spec(CB_K), p_spec(CB_V), q_spec(CB_V)],
        out_specs=[pl.BlockSpec(lead_q + (COL,), o_map),
                   pl.BlockSpec(lead_q + (LANE,), l_map)],
        out_shape=[jax.ShapeDtypeStruct(o_shape, BF16), jax.ShapeDtypeStruct(l_shape, F32)],
        scratch_shapes=[pltpu.VMEM(((qb + 1) * CHUNK, COL), BF16), pltpu.VMEM(((qb + 1) * CHUNK, COL), BF16),
                        pltpu.VMEM((n_pairs, CHUNK, 2 * CHUNK), F32),
                        pltpu.VMEM((n_pairs, CHUNK, 2 * CHUNK), BF16)],
        compiler_params=_params(vmem, ("parallel", "parallel", "arbitrary")),
        name=f"swa_prompt_g{g}",
    )(src, src, src, src, src)
    return o.reshape(Tp, COL), l.reshape(Tp, LANE)


def _swa_sample_kernel(q_ref, kn_ref, vn_ref, cache_ref, o_ref, l_ref, *, dil, lbuf, t_s):
    nq = SWA_HEADS * t_s
    q = q_ref[...]
    qrep = jnp.concatenate([q] * SWA_HEADS, axis=0)
    rq = lax.broadcasted_iota(jnp.int32, (nq, COL), 0)
    cq = lax.broadcasted_iota(jnp.int32, (nq, COL), 1)
    qbd = jnp.where(_idiv(rq, t_s) == _idiv(cq, LANE), qrep, 0.0).astype(BF16)

    per_pos = 2 * SWA_HEADS
    if cache_ref.ndim == 2:
        n_keys = lbuf
        rows_of = lambda first: cache_ref[pl.ds(first, lbuf, stride=per_pos), :]
        key_pos = lambda r: r
    else:
        n_keys = cache_ref.shape[0] * t_s
        rows_of = lambda first: cache_ref[:, pl.ds(first, t_s, stride=per_pos), :].reshape(n_keys, LANE)
        key_pos = lambda r: dil * _idiv(r, t_s) + _imod(r, t_s)
    kc = jnp.concatenate([rows_of(h).astype(BF16) for h in range(SWA_HEADS)], axis=1)
    vc = jnp.concatenate([rows_of(SWA_HEADS + h).astype(BF16) for h in range(SWA_HEADS)], axis=1)
    kn = kn_ref[...].astype(BF16)
    vn = vn_ref[...].astype(BF16)

    s_c = _dot_nt(qbd, kc)
    s_n = _dot_nt(qbd, kn)
    t_c = _imod(lax.broadcasted_iota(jnp.int32, (nq, n_keys), 0), t_s)
    c_c = key_pos(lax.broadcasted_iota(jnp.int32, (nq, n_keys), 1))
    diff_c = lbuf + t_c - c_c
    ok_c = (_imod(diff_c, dil) == 0) & (diff_c <= lbuf)
    t_n = _imod(lax.broadcasted_iota(jnp.int32, (nq, t_s), 0), t_s)
    c_n = lax.broadcasted_iota(jnp.int32, (nq, t_s), 1)
    diff_n = t_n - c_n
    ok_n = (diff_n >= 0) & (_imod(diff_n, dil) == 0)
    s_c = jnp.where(ok_c, s_c, NEG_INF)
    s_n = jnp.where(ok_n, s_n, NEG_INF)
    m = jnp.maximum(jnp.max(s_c, axis=-1, keepdims=True), jnp.max(s_n, axis=-1, keepdims=True))
    p_c = jnp.exp(s_c - m)
    p_n = jnp.exp(s_n - m)
    den = jnp.sum(p_c, axis=-1, keepdims=True) + jnp.sum(p_n, axis=-1, keepdims=True)
    o_all = (_dot(p_c.astype(BF16), vc) + _dot(p_n.astype(BF16), vn)) / den
    lse = m + jnp.log(den)
    for h in range(SWA_HEADS):
        cs = slice(h * LANE, (h + 1) * LANE)
        o_ref[:, cs] = o_all[h * t_s:(h + 1) * t_s, cs]
    l_ref[...] = _pack_heads([lse[h * t_s:(h + 1) * t_s] for h in range(SWA_HEADS)])


def _swa_sample(Ps, cache, layer, g, dil):
    n_s, t_s, _ = Ps.shape
    rows = cache.shape[2]
    lbuf = rows // (2 * SWA_HEADS)
    assert lbuf == dil * CHUNK, "window buffer must hold exactly one full window"
    blk = lambda cb: pl.BlockSpec((None, t_s, COL), lambda b: (b, 0, cb + g))
    if dil > t_s:
        per_pos = rows // lbuf
        cache = cache.reshape(cache.shape[0], n_s, lbuf // dil, dil * per_pos, LANE)
        cache_spec = pl.BlockSpec((None, None, lbuf // dil, t_s * per_pos, LANE), lambda b: (layer, b, 0, 0, 0))
    else:
        cache_spec = pl.BlockSpec((None, None, rows, LANE), lambda b: (layer, b, 0, 0))
    vmem = (2 * _nbytes((lbuf, 2 * COL), F32) + 2 * _nbytes((lbuf, 2 * COL), BF16)
            + 8 * _nbytes((SWA_HEADS * t_s, lbuf), F32))
    return pl.pallas_call(
        functools.partial(_swa_sample_kernel, dil=dil, lbuf=lbuf, t_s=t_s),
        grid=(n_s,),
        in_specs=[blk(CB_Q), blk(CB_K), blk(CB_V), cache_spec],
        out_specs=[pl.BlockSpec((None, t_s, COL), lambda b: (b, 0, 0)),
                   pl.BlockSpec((None, t_s, LANE), lambda b: (b, 0, 0))],
        out_shape=[jax.ShapeDtypeStruct((n_s, t_s, COL), F32),
                   jax.ShapeDtypeStruct((n_s, t_s, LANE), F32)],
        compiler_params=_params(vmem, ("parallel",)),
        name=f"swa_sample_g{g}",
    )(Ps, Ps, Ps, _hbm(cache))


def _ret_head_inputs(qk_ref, v_ref, kout_ref, h):
    pair, half = h // 2, h % 2
    lane = lax.broadcasted_iota(jnp.int32, (CHUNK, LANE), 1)
    head_lanes = _idiv(lane, RET_QK_DIM) == half
    qt = qk_ref[:, pair * LANE:(pair + 1) * LANE]
    kt = qk_ref[:, COL // 2 + pair * LANE:COL // 2 + (pair + 1) * LANE]
    qm = jnp.where(head_lanes, qt, jnp.zeros_like(qt))
    kw = jnp.where(head_lanes, kt.astype(F32) * kout_ref[:, pair * LANE:(pair + 1) * LANE], 0.0)
    vh = v_ref[:, h * LANE:(h + 1) * LANE]
    return qm, kt, kw, vh


def _ret_finish(o, gate_ref, o_ref, h):
    cs = slice(h * LANE, (h + 1) * LANE)
    o_ref[:, cs] = (gate_ref[:, cs].astype(F32) * _rms(o)).astype(o_ref.dtype)


def _ret_prompt_kernel(*refs, n_seq):
    ins = refs[:3 * n_seq]
    decay_ref, qin_ref, kout_ref, gc_ref, o_ref, s_out_ref, s_ref, att_ref = refs[3 * n_seq:]
    i = pl.program_id(0)

    @pl.when(i == 0)
    def _():
        s_ref[...] = jnp.zeros_like(s_ref)

    pairs = [(b, h) for b in range(n_seq) for h in range(RET_HEADS)]
    head_in = lambda b, h: _ret_head_inputs(ins[3 * b], ins[3 * b + 1], kout_ref, h)
    for b, h in pairs:
        qm, kt, _, _ = head_in(b, h)
        att_ref[b, h] = (_dot_nt(qm, kt) * decay_ref[h]).astype(BF16)
    for b, h in pairs:
        qm, _, _, vh = head_in(b, h)
        o = (_dot(att_ref[b, h], vh)
             + _dot(qm, s_ref[b, h].astype(BF16)) * qin_ref[:, h * LANE:(h + 1) * LANE])
        _ret_finish(o, ins[3 * b + 2], o_ref.at[b], h)
    for b, h in pairs:
        _, _, kw, vh = head_in(b, h)
        s_ref[b, h] = s_ref[b, h] * gc_ref[h:h + 1, :] + _dot(kw.T.astype(BF16), vh)

    @pl.when(i == pl.num_programs(0) - 1)
    def _():
        for b in range(n_seq):
            for h in range(RET_HEADS):
                lo = (h % 2) * RET_QK_DIM
                s_out_ref[b, h] = s_ref[b, h, lo:lo + RET_QK_DIM, :]


def _ret_prompt(P, rt, n_p, s):
    nblk = s // CHUNK
    decay, qin, kout, gc = rt
    blk = lambda b, cb: pl.BlockSpec((CHUNK, COL), lambda i: (b * nblk + i, cb))
    const2 = lambda i: (0, 0)
    seq_specs = [blk(b, cb) for b in range(n_p) for cb in (CB_RQK, CB_RV, CB_RG)]
    st_shape = (n_p, RET_HEADS, RET_QK_DIM, RET_V_DIM)
    vmem = (8 * n_p * _nbytes((CHUNK, COL), BF16) + 2 * _nbytes(decay.shape, F32) + 4 * _nbytes(qin.shape, F32)
            + 3 * n_p * _nbytes((RET_HEADS, LANE, LANE), F32))
    o, st = pl.pallas_call(
        functools.partial(_ret_prompt_kernel, n_seq=n_p),
        grid=(nblk,),
        in_specs=seq_specs + [pl.BlockSpec(decay.shape, lambda i: (0, 0, 0)),
                              pl.BlockSpec(qin.shape, const2), pl.BlockSpec(kout.shape, const2),
                              pl.BlockSpec(gc.shape, const2)],
        out_specs=[pl.BlockSpec((n_p, CHUNK, COL), lambda i: (0, i, 0)),
                   pl.BlockSpec(st_shape, lambda i: (0, 0, 0, 0))],
        out_shape=[jax.ShapeDtypeStruct((n_p, s, COL), BF16), jax.ShapeDtypeStruct(st_shape, F32)],
        scratch_shapes=[pltpu.VMEM((n_p, RET_HEADS, LANE, LANE), F32),
                        pltpu.VMEM((n_p, RET_HEADS, CHUNK, CHUNK), BF16)],
        compiler_params=_params(vmem, ("arbitrary",)),
        name="ret_prompt",
    )(*([P] * (3 * n_p)), decay, qin, kout, gc)
    return o.reshape(n_p * s, COL), st


def _ret_sample_kernel(qk_ref, v_ref, gate_ref, s0_ref, decay_ref, qin_ref, kout_ref, gc_ref,
                       o_ref, s_out_ref, *, t_s):
    row = lax.broadcasted_iota(jnp.int32, (CHUNK, LANE), 0)
    for h in range(RET_HEADS):
        lo = (h % 2) * RET_QK_DIM
        qm, kt, kw, vh = _ret_head_inputs(qk_ref, v_ref, kout_ref, h)
        att = _dot_nt(qm, kt) * decay_ref[h]
        o = _dot(att.astype(BF16), vh)
        inter = jnp.zeros((CHUNK, LANE), F32)
        for sq in range(CHUNK // t_s):
            seq_rows = _idiv(row, t_s) == sq
            st = s0_ref[sq, h]
            st2 = jnp.concatenate([st, st], axis=0).astype(BF16)
            inter = jnp.where(seq_rows, _dot(qm, st2), inter)
            upd = _dot(jnp.where(seq_rows, kw, 0.0).T.astype(BF16), vh)
            s_out_ref[sq, h] = st * gc_ref[h:h + 1, :] + upd[lo:lo + RET_QK_DIM, :]
        o = o + inter * qin_ref[:, h * LANE:(h + 1) * LANE]
        _ret_finish(o, gate_ref, o_ref, h)


def _ret_sample(P, state, layer, rt, Tp, t_s):
    n_s = state.shape[1]
    spb = CHUNK // t_s
    base = Tp // CHUNK
    decay, qin, kout, gc = rt
    blk = lambda cb: pl.BlockSpec((CHUNK, COL), lambda i: (base + i, cb))
    const2 = lambda i: (0, 0)
    st_shape = (spb, RET_HEADS, RET_QK_DIM, RET_V_DIM)
    vmem = (8 * _nbytes((CHUNK, COL), BF16) + 2 * _nbytes(decay.shape, F32) + 4 * _nbytes(qin.shape, F32)
            + 4 * _nbytes(st_shape, F32))
    return pl.pallas_call(
        functools.partial(_ret_sample_kernel, t_s=t_s),
        grid=(n_s // spb,),
        in_specs=[blk(CB_RQK), blk(CB_RV), blk(CB_RG),
                  pl.BlockSpec((None,) + st_shape, lambda i: (layer, i, 0, 0, 0)),
                  pl.BlockSpec(decay.shape, lambda i: (0, 0, 0)),
                  pl.BlockSpec(qin.shape, const2), pl.BlockSpec(kout.shape, const2),
                  pl.BlockSpec(gc.shape, const2)],
        out_specs=[pl.BlockSpec((CHUNK, COL), lambda i: (i, 0)),
                   pl.BlockSpec(st_shape, lambda i: (i, 0, 0, 0))],
        out_shape=[jax.ShapeDtypeStruct((n_s * t_s, COL), BF16),
                   jax.ShapeDtypeStruct((n_s, RET_HEADS, RET_QK_DIM, RET_V_DIM), F32)],
        compiler_params=_params(vmem, ("parallel",)),
        name="ret_sample",
    )(P, P, P, _hbm(state), decay, qin, kout, gc)


def _branch_kernel(xa_ref, xb_ref, a_ref, g0_ref, g1_ref, g2_ref, wb_ref, wo_ref, ex_ref,
                   po0, po1, po2, pl0, pl1, pl2, pc, so0, so1, so2, sl0, sl1, sl2, sc,
                   y_ref, on_ref, ln_ref, *, n_prompt_tiles):
    i = pl.program_id(0)

    def spread(packed):
        hi = packed.astype(BF16)
        lo = (packed - hi.astype(F32)).astype(BF16)
        return _dot(jnp.concatenate([hi, lo], axis=1), ex_ref[...])

    def natural(o_ref, l_ref, k, dil):
        if dil == 1:
            return (lambda h: o_ref[:, h * LANE:(h + 1) * LANE].astype(F32)), l_ref[...]
        rpt = TILE // dil
        for r in range(dil):
            rows = slice(r * rpt, (r + 1) * rpt)
            ln_ref[k, pl.ds(r, rpt, stride=dil), :] = l_ref[rows, :]
            for h in range(SWA_HEADS):
                on_ref[k, h, pl.ds(r, rpt, stride=dil), :] = o_ref[rows, h * LANE:(h + 1) * LANE].astype(F32)
        return (lambda h: on_ref[k, h]), ln_ref[k]

    def body(o_refs, l_refs, c_ref, permuted):
        groups = [natural(o_refs[k], l_refs[k], k, SWA_PATTERNS[k][1] if permuted else 1)
                  for k in range(len(SWA_PATTERNS))]
        ls = [l for _, l in groups]
        lmax = functools.reduce(jnp.maximum, ls)
        es = [jnp.exp(l - lmax) for l in ls]
        inv = 1.0 / sum(es)
        weights = [spread(e * inv) for e in es]
        heads = []
        for h in range(SWA_HEADS):
            cs = slice(h * LANE, (h + 1) * LANE)
            heads.append(sum(w[:, cs] * go(h) for w, (go, _) in zip(weights, groups)).astype(BF16))
        mix = jnp.concatenate(heads, axis=1)
        merged = (g0_ref[...].astype(F32) * _dot(a_ref[...], wb_ref[0])
                  + g1_ref[...].astype(F32) * _dot(mix, wb_ref[1])
                  + g2_ref[...].astype(F32) * _dot(c_ref[...], wb_ref[2]))
        y_ref[...] = _tile_rows(i, n_prompt_tiles, xa_ref, xb_ref) + _dot(merged.astype(BF16), wo_ref[...])

    @pl.when(i < n_prompt_tiles)
    def _():
        body((po0, po1, po2), (pl0, pl1, pl2), pc, True)

    @pl.when(i >= n_prompt_tiles)
    def _():
        body((so0, so1, so2), (sl0, sl1, sl2), sc, False)


def _branch(xa, xb, out_a, P, wb, wo, prompt_set, sample_set, n_prompt_tiles):
    T = (n_prompt_tiles + 1) * TILE
    assert sample_set[0].shape[0] == TILE, "the sample rows must form exactly one tile"
    n_g = len(SWA_PATTERNS)
    last = n_prompt_tiles - 1
    row = lambda i: (i, 0)
    gate = lambda k: pl.BlockSpec((TILE, D_MODEL), lambda i: (i, CB_GATE * COL // D_MODEL + k))
    once = pl.Buffered(1)
    p_o = pl.BlockSpec((TILE, COL), lambda i: (jnp.minimum(i, last), 0))
    p_l = pl.BlockSpec((TILE, LANE), lambda i: (jnp.minimum(i, last), 0))
    s_o = pl.BlockSpec((TILE, COL), lambda i: (0, 0))
    s_l = pl.BlockSpec((TILE, LANE), lambda i: (0, 0))
    src = jnp.arange(2 * LANE) % LANE
    dst_head = jnp.arange(COL) // LANE
    expand = (src[:, None] == dst_head[None, :] * LSE_LANES).astype(BF16)
    vmem = (4 * _nbytes((TILE, D_MODEL), F32) + 6 * _nbytes((TILE, D_MODEL), BF16)
            + 2 * (2 * n_g + 2) * (_nbytes((TILE, COL), BF16) + _nbytes((TILE, COL), F32))
            + _nbytes(wb.shape, BF16) + _nbytes(wo.shape, BF16)
            + 4 * n_g * _nbytes((TILE, COL), F32) + 4 * _nbytes((TILE, D_MODEL), F32))
    return pl.pallas_call(
        functools.partial(_branch_kernel, n_prompt_tiles=n_prompt_tiles),
        grid=(T // TILE,),
        in_specs=_x_specs(xa, xb) + [
                  pl.BlockSpec((TILE, COL), row),
                  gate(0), gate(1), gate(2),
                  pl.BlockSpec(wb.shape, lambda i: (0, 0, 0), pipeline_mode=once),
                  pl.BlockSpec(wo.shape, lambda i: (0, 0), pipeline_mode=once),
                  pl.BlockSpec(expand.shape, lambda i: (0, 0)),
                  p_o, p_o, p_o, p_l, p_l, p_l, p_o,
                  s_o, s_o, s_o, s_l, s_l, s_l, s_o],
        out_specs=pl.BlockSpec((TILE, D_MODEL), row),
        out_shape=jax.ShapeDtypeStruct((T, D_MODEL), F32),
        scratch_shapes=[pltpu.VMEM((n_g, SWA_HEADS, TILE, LANE), F32),
                        pltpu.VMEM((n_g, TILE, LANE), F32)],
        compiler_params=_params(vmem, ("parallel",)),
        name="branch_merge",
    )(xa, xb, _hbm(out_a), P, P, P, wb, wo, expand, *_hbm(*prompt_set), *_hbm(*sample_set))


def _swiglu_into(hb, w1_ref, w3_ref, w2_ref, acc_ref, fc):
    n_fc = w1_ref.shape[-1] // fc
    for f in range(n_fc):
        cols = slice(f * fc, (f + 1) * fc)
        a = _dot(hb, w1_ref[:, cols])
        b = _dot(hb, w3_ref[:, cols])
        part = _dot((a * _sigmoid(a) * b).astype(BF16), w2_ref[cols, :])
        if f == 0:
            acc_ref[...] = part
        else:
            acc_ref[...] += part


def _dense_ffn_kernel(x_ref, g2_ref, w1_ref, w3_ref, w2_ref, y_ref, acc_ref, *, fc):
    x = x_ref[...]
    hb = (_rms(x) * g2_ref[...]).astype(BF16)
    _swiglu_into(hb, w1_ref, w3_ref, w2_ref, acc_ref, fc)
    y_ref[...] = x + acc_ref[...]


def _dense_ffn(x, g2, w1b, w3b, w2b):
    T = x.shape[0]
    F = w1b.shape[1]
    tm = _pick(T, (768, 512, 384, 256, 128))
    fc = _pick(F, (512, 256, 128))
    row = lambda i: (i, 0)
    once = pl.Buffered(1)
    vmem = (5 * _nbytes((tm, D_MODEL), F32) + 3 * _nbytes(w1b.shape, BF16) + _nbytes((tm, D_MODEL), BF16)
            + 4 * _nbytes((tm, fc), F32))
    return pl.pallas_call(
        functools.partial(_dense_ffn_kernel, fc=fc),
        grid=(T // tm,),
        in_specs=[pl.BlockSpec((tm, D_MODEL), row),
                  pl.BlockSpec((1, D_MODEL), lambda i: (0, 0)),
                  pl.BlockSpec(w1b.shape, lambda i: (0, 0), pipeline_mode=once),
                  pl.BlockSpec(w3b.shape, lambda i: (0, 0), pipeline_mode=once),
                  pl.BlockSpec(w2b.shape, lambda i: (0, 0), pipeline_mode=once)],
        out_specs=pl.BlockSpec((tm, D_MODEL), row),
        out_shape=jax.ShapeDtypeStruct((T, D_MODEL), F32),
        scratch_shapes=[pltpu.VMEM((tm, D_MODEL), F32)],
        compiler_params=_params(vmem, ("parallel",)),
        name="dense_ffn",
    )(x, g2, w1b, w3b, w2b)


def _router_kernel(x_ref, g2_ref, wr_ref, e0_ref, e1_ref, w0_ref, w1_ref):
    hb = (_rms(x_ref[...]) * g2_ref[...]).astype(BF16)
    logits = _dot(hb, wr_ref[...])
    lane = lax.broadcasted_iota(jnp.int32, logits.shape, 1)
    logits = jnp.where(lane < N_EXPERTS, logits, -jnp.inf)
    lane_f = lane.astype(F32)
    m1 = jnp.max(logits, axis=-1, keepdims=True)
    i1 = jnp.min(jnp.where(logits == m1, lane_f, float(LANE)), axis=-1, keepdims=True)
    rest = jnp.where(lane_f == i1, -jnp.inf, logits)
    m2 = jnp.max(rest, axis=-1, keepdims=True)
    i2 = jnp.min(jnp.where(rest == m2, lane_f, float(LANE)), axis=-1, keepdims=True)
    e = jnp.exp(m2 - m1)
    e0_ref[...] = jnp.broadcast_to(i1, logits.shape).astype(jnp.int32)
    e1_ref[...] = jnp.broadcast_to(i2, logits.shape).astype(jnp.int32)
    w0_ref[...] = jnp.broadcast_to(1.0 / (1.0 + e), logits.shape)
    w1_ref[...] = jnp.broadcast_to(e / (1.0 + e), logits.shape)


def _router(x, g2, wr_pad):
    T = x.shape[0]
    tm = _pick(T, (768, 512, 384, 256, 128))
    row = lambda i: (i, 0)
    out = pl.BlockSpec((tm, LANE), row)
    vmem = 4 * _nbytes((tm, D_MODEL), F32) + 2 * _nbytes(wr_pad.shape, BF16) + 16 * _nbytes((tm, LANE), F32)
    return pl.pallas_call(
        _router_kernel,
        grid=(T // tm,),
        in_specs=[pl.BlockSpec((tm, D_MODEL), row),
                  pl.BlockSpec((1, D_MODEL), lambda i: (0, 0)),
                  pl.BlockSpec(wr_pad.shape, lambda i: (0, 0))],
        out_specs=[out, out, out, out],
        out_shape=[jax.ShapeDtypeStruct((T, LANE), jnp.int32), jax.ShapeDtypeStruct((T, LANE), jnp.int32),
                   jax.ShapeDtypeStruct((T, LANE), F32), jax.ShapeDtypeStruct((T, LANE), F32)],
        compiler_params=_params(vmem, ("parallel",)),
        name="moe_router",
    )(x, g2, wr_pad)


RUN_ALIGN = 8


def _run_pieces(count, max_rows):
    pieces, off = [], 0
    size = max_rows
    while size >= RUN_ALIGN:
        active = (count & size) != 0
        pieces.append((size, off, active))
        off = off + jnp.where(active, size, 0)
        size //= 2
    return pieces


def _packed_rows(i, e0, e1, lower_ref, loc_ref):
    lane = lax.broadcasted_iota(jnp.int32, e0.shape, 1)
    picks = jnp.where((lane == e0) | (lane == e1), 1.0, 0.0)
    rank = _dot(lower_ref[...], picks.astype(BF16))
    lane1 = lax.broadcasted_iota(jnp.int32, (1, LANE), 1)
    run_start = jnp.zeros((1, LANE), F32)
    for e in range(N_EXPERTS):
        run_start = jnp.where(lane1 == e, loc_ref[i * N_EXPERTS + e].astype(F32), run_start)
    row_of = rank + run_start
    return tuple(jnp.sum(jnp.where(lane == ek, row_of, 0.0), axis=-1, keepdims=True) for ek in (e0, e1))


RUN_ROWS = 640


def _dispatch_kernel(base_ref, cnt_ref, loc_ref, pend_ref, x_ref, g2_ref, e0_ref, e1_ref, lower_ref, xs_ref,
                     comp_ref, sem, *, tme, n_tiles):
    i = pl.program_id(0)
    n_steps = pl.num_programs(0)
    cur = i % 2
    half = tme // 2

    @pl.when(i == 0)
    def _():
        comp_ref[0, 0:half] = jnp.zeros((half, D_MODEL), comp_ref.dtype)

        def zero_tile(start):
            return [pltpu.make_async_copy(
                        comp_ref.at[0, pl.ds(0, half), :],
                        xs_ref.at[pl.ds(pl.multiple_of(start + k * half, half), half), :], sem.at[0])
                    for k in range(2)]

        def tail_loop(fn):
            def body(t, carry):
                for c in zero_tile(t * tme):
                    fn(c)
                return carry
            lax.fori_loop(pend_ref[N_EXPERTS - 1] // tme, n_tiles, body, 0)

        def group_tails(fn):
            for e in range(N_EXPERTS):
                prev_end = pend_ref[e - 1] if e else 0

                @pl.when(pend_ref[e] > prev_end)
                def _():
                    for c in zero_tile(pend_ref[e] - tme):
                        fn(c)

        group_tails(lambda c: c.start())
        tail_loop(lambda c: c.start())
        group_tails(lambda c: c.wait())
        tail_loop(lambda c: c.wait())

    h = (_rms(x_ref[...]) * g2_ref[...]).astype(BF16)
    pos0, pos1 = _packed_rows(i, e0_ref[...], e1_ref[...], lower_ref, loc_ref)
    slot = lax.broadcasted_iota(jnp.int32, (TILE, RUN_ROWS), 1).astype(F32)
    hit = jnp.where((slot == pos0) | (slot == pos1), 1.0, 0.0)
    comp_ref[cur] = _dot(hit.T.astype(BF16), h)

    def for_copies(tile, buf, fn):
        for e in range(N_EXPERTS):
            k = tile * N_EXPERTS + e
            cnt, dst, loc = cnt_ref[k], base_ref[k], loc_ref[k]
            for size, off, active in _run_pieces(cnt, TILE):
                c = pltpu.make_async_copy(
                    comp_ref.at[buf, pl.ds(pl.multiple_of(loc + off, RUN_ALIGN), size), :],
                    xs_ref.at[pl.ds(pl.multiple_of(dst + off, RUN_ALIGN), size), :], sem.at[buf])
                pl.when(active)(functools.partial(fn, c))

    for_copies(i, cur, lambda c: c.start())

    @pl.when(i > 0)
    def _():
        for_copies(i - 1, 1 - cur, lambda c: c.wait())

    @pl.when(i == n_steps - 1)
    def _():
        for_copies(i, cur, lambda c: c.wait())


def _dispatch(x, g2, e0, e1, base, cnt, loc, pend, lower, n_tiles, tme):
    T = x.shape[0]
    assert tme == 2 * TILE
    row = lambda i, *_: (i, 0)
    assert RUN_ROWS >= 2 * TILE + N_EXPERTS * (RUN_ALIGN - 1) and RUN_ROWS % LANE == 0
    vmem = (3 * _nbytes((TILE, D_MODEL), F32) + 2 * _nbytes((RUN_ROWS, D_MODEL), F32)
            + 8 * _nbytes((RUN_ROWS, TILE), F32))
    grid_spec = pltpu.PrefetchScalarGridSpec(
        num_scalar_prefetch=4,
        grid=(T // TILE,),
        in_specs=[pl.BlockSpec((TILE, D_MODEL), row),
                  pl.BlockSpec((1, D_MODEL), lambda i, *_: (0, 0)),
                  pl.BlockSpec((TILE, LANE), row), pl.BlockSpec((TILE, LANE), row),
                  pl.BlockSpec((TILE, TILE), lambda i, *_: (0, 0))],
        out_specs=pl.BlockSpec(memory_space=pl.ANY),
        scratch_shapes=[pltpu.VMEM((2, RUN_ROWS, D_MODEL), F32), pltpu.SemaphoreType.DMA((2,))],
    )
    return pl.pallas_call(
        functools.partial(_dispatch_kernel, tme=tme, n_tiles=n_tiles),
        grid_spec=grid_spec,
        out_shape=jax.ShapeDtypeStruct((n_tiles * tme, D_MODEL), F32),
        compiler_params=_params(vmem, ("arbitrary",)),
        name="moe_dispatch",
    )(base, cnt, loc, pend, x, g2, *_hbm(e0, e1), lower)


def _expert_ffn_kernel(te_ref, used_ref, xs_ref, w1_ref, w3_ref, w2_ref, y_ref, acc_ref, *, fc):
    i = pl.program_id(0)

    @pl.when(i < used_ref[0])
    def _():
        _swiglu_into(xs_ref[...].astype(BF16), w1_ref, w3_ref, w2_ref, acc_ref, fc)
        y_ref[...] = acc_ref[...]

    @pl.when(i >= used_ref[0])
    def _():
        y_ref[...] = jnp.zeros_like(y_ref)


def _expert_ffn(xs, tile_expert, n_used, we1b, we3b, we2b, tme):
    n_rows = xs.shape[0]
    F = we1b.shape[2]
    fc = _pick(F, (512, 256, 128))
    once = pl.Buffered(1)
    wspec = lambda shape: pl.BlockSpec((None,) + shape, lambda i, te, nu: (te[i], 0, 0), pipeline_mode=once)
    vmem = (5 * _nbytes((tme, D_MODEL), F32) + 3 * _nbytes(we1b.shape[1:], BF16) + _nbytes((tme, D_MODEL), BF16)
            + 4 * _nbytes((tme, fc), F32))
    grid_spec = pltpu.PrefetchScalarGridSpec(
        num_scalar_prefetch=2,
        grid=(n_rows // tme,),
        in_specs=[pl.BlockSpec((tme, D_MODEL), lambda i, te, nu: (jnp.minimum(i, jnp.maximum(nu[0], 1) - 1), 0)),
                  wspec(we1b.shape[1:]), wspec(we3b.shape[1:]), wspec(we2b.shape[1:])],
        out_specs=pl.BlockSpec((tme, D_MODEL), lambda i, te, nu: (i, 0)),
        scratch_shapes=[pltpu.VMEM((tme, D_MODEL), F32)],
    )
    return pl.pallas_call(
        functools.partial(_expert_ffn_kernel, fc=fc),
        grid_spec=grid_spec,
        out_shape=jax.ShapeDtypeStruct((n_rows, D_MODEL), F32),
        compiler_params=_params(vmem, ("arbitrary",)),
        name="moe_expert_ffn",
    )(tile_expert, n_used, xs, we1b, we3b, we2b)


def _combine_kernel(base_ref, cnt_ref, loc_ref, x_ref, e0_ref, e1_ref, w0_ref, w1_ref, lower_ref, y_hbm_ref,
                    o_ref, o_last_ref, buf_ref, sem):
    i = pl.program_id(0)
    n_steps = pl.num_programs(0)
    cur = i % 2

    def for_copies(tile, buf, fn):
        for e in range(N_EXPERTS):
            k = tile * N_EXPERTS + e
            cnt, src, loc = cnt_ref[k], base_ref[k], loc_ref[k]
            for size, off, active in _run_pieces(cnt, TILE):
                c = pltpu.make_async_copy(
                    y_hbm_ref.at[pl.ds(pl.multiple_of(src + off, RUN_ALIGN), size), :],
                    buf_ref.at[buf, pl.ds(pl.multiple_of(loc + off, RUN_ALIGN), size), :], sem.at[buf])
                pl.when(active)(functools.partial(fn, c))

    @pl.when(i == 0)
    def _():
        buf_ref[...] = jnp.zeros_like(buf_ref)
        for_copies(0, 0, lambda c: c.start())

    @pl.when(i + 1 < n_steps)
    def _():
        for_copies(i + 1, 1 - cur, lambda c: c.start())

    for_copies(i, cur, lambda c: c.wait())

    positions = _packed_rows(i, e0_ref[...], e1_ref[...], lower_ref, loc_ref)
    slot = lax.broadcasted_iota(jnp.int32, (TILE, RUN_ROWS), 1).astype(F32)
    yb = buf_ref[cur].astype(BF16)
    wide = lambda a, n: jnp.concatenate([a] * n, axis=1)
    acc = x_ref[...]
    for pos, wk in zip(positions, (w0_ref, w1_ref)):
        rows = _dot(jnp.where(slot == pos, 1.0, 0.0).astype(BF16), yb)
        acc = acc + wide(wk[...], D_MODEL // LANE) * rows

    @pl.when(i < n_steps - 1)
    def _():
        o_ref[...] = acc

    @pl.when(i == n_steps - 1)
    def _():
        o_last_ref[...] = acc


def _combine(x, e0, e1, w0, w1, base, cnt, loc, lower, y_rows):
    T = x.shape[0]
    row = lambda i, *_: (i, 0)
    lane_spec = pl.BlockSpec((TILE, LANE), row)
    vmem = (5 * _nbytes((TILE, D_MODEL), F32) + 3 * _nbytes((RUN_ROWS, D_MODEL), F32)
            + 8 * _nbytes((TILE, LANE), F32) + 8 * _nbytes((TILE, RUN_ROWS), F32)
            + 4 * _nbytes((TILE, D_MODEL), F32))
    grid_spec = pltpu.PrefetchScalarGridSpec(
        num_scalar_prefetch=3,
        grid=(T // TILE,),
        in_specs=[pl.BlockSpec((TILE, D_MODEL), row), lane_spec, lane_spec, lane_spec, lane_spec,
                  pl.BlockSpec((TILE, TILE), lambda i, *_: (0, 0)),
                  pl.BlockSpec(memory_space=pl.ANY)],
        out_specs=[pl.BlockSpec((TILE, D_MODEL), lambda i, *_: (jnp.minimum(i, T // TILE - 2), 0)),
                   pl.BlockSpec((TILE, D_MODEL), lambda i, *_: (0, 0))],
        scratch_shapes=[pltpu.VMEM((2, RUN_ROWS, D_MODEL), F32), pltpu.SemaphoreType.DMA((2,))],
    )
    return pl.pallas_call(
        _combine_kernel,
        grid_spec=grid_spec,
        out_shape=[jax.ShapeDtypeStruct((T - TILE, D_MODEL), F32), jax.ShapeDtypeStruct((TILE, D_MODEL), F32)],
        compiler_params=_params(vmem, ("arbitrary",)),
        name="moe_combine",
    )(base, cnt, loc, x, *_hbm(e0, e1, w0, w1), lower, y_rows)


def _moe(x, g2, wr_pad, we1b, we3b, we2b):
    T = x.shape[0]
    tme = 2 * TILE
    n_tt = T // TILE
    e0, e1, w0, w1 = _router(x, g2, wr_pad)
    experts = jnp.arange(N_EXPERTS, dtype=jnp.int32)
    picks = (e0[:, :1] == experts[None, :]) | (e1[:, :1] == experts[None, :])
    cnt = jnp.sum(picks.reshape(n_tt, TILE, N_EXPERTS).astype(jnp.int32), axis=1)
    cnt = (cnt + RUN_ALIGN - 1) // RUN_ALIGN * RUN_ALIGN
    counts = jnp.sum(cnt, axis=0)
    padded = (counts + tme - 1) // tme * tme
    pend = jnp.cumsum(padded).astype(jnp.int32)
    base = ((pend - padded)[None, :] + jnp.cumsum(cnt, axis=0) - cnt).astype(jnp.int32)
    loc = (jnp.cumsum(cnt, axis=1) - cnt).astype(jnp.int32)
    max_rows = T * TOP_K + n_tt * N_EXPERTS * (RUN_ALIGN - 1)
    n_tiles = -(-max_rows // tme) + N_EXPERTS
    tile_start = jnp.arange(n_tiles, dtype=jnp.int32) * tme
    tile_expert = jnp.minimum(jnp.sum(pend[None, :] <= tile_start[:, None], axis=1), N_EXPERTS - 1).astype(jnp.int32)
    n_used = (pend[N_EXPERTS - 1:] // tme).astype(jnp.int32)
    tok = jnp.arange(TILE)
    lower = (tok[None, :] < tok[:, None]).astype(BF16)
    base_f, cnt_f, loc_f = base.reshape(-1), cnt.reshape(-1).astype(jnp.int32), loc.reshape(-1)
    xs = _dispatch(x, g2, e0, e1, base_f, cnt_f, loc_f, pend, lower, n_tiles, tme)
    y_rows = _expert_ffn(xs, tile_expert, n_used, we1b, we3b, we2b, tme)
    return _combine(x, e0, e1, w0, w1, base_f, cnt_f, loc_f, lower, y_rows)


def _class_major(a, dil):
    n = a.shape[0]
    return a.reshape((n // TILE, TILE // dil, dil) + a.shape[1:]).swapaxes(1, 2).reshape(a.shape)


def _natural(a, dil):
    n = a.shape[0]
    return a.reshape((n // TILE, dil, TILE // dil) + a.shape[1:]).swapaxes(1, 2).reshape(a.shape)


def _perm_matrices():
    eye = jnp.eye(TILE, dtype=BF16)
    return jnp.stack([_class_major(eye, dil) for _, dil in SWA_PATTERNS[1:]])


def _rope_tables(pos, Tp):
    freqs = ROPE_THETA ** (-jnp.arange(0, SWA_HEAD_DIM, 2, dtype=F32) / SWA_HEAD_DIM)
    rfreqs = 1.0 / (ROPE_THETA ** jnp.linspace(0.0, 1.0, RET_QK_DIM // 2, dtype=F32))

    def swa(p):
        ang = p.astype(F32)[:, None] * freqs[None, :]
        c, s = jnp.cos(ang), jnp.sin(ang)
        return jnp.concatenate([c, c], axis=1), jnp.concatenate([-s, s], axis=1)

    per_group = [swa(jnp.concatenate([_class_major(pos[:Tp], dil), pos[Tp:]])) for _, dil in SWA_PATTERNS]
    cosb = jnp.stack([c for c, _ in per_group])
    sinb = jnp.stack([s for _, s in per_group])
    rang = pos.astype(F32)[:, None] * rfreqs[None, :]
    c, s = jnp.cos(rang), jnp.sin(rang)
    cosc = jnp.concatenate([c, c, c, c], axis=1)
    sinc = jnp.concatenate([-s, s, -s, s], axis=1)
    return cosb, sinb, cosc, sinc


def _ret_tables(c_len):
    log_g = jnp.log1p(-jnp.exp2(-5.0 - jnp.arange(RET_HEADS, dtype=F32)))
    r = jnp.arange(CHUNK)
    i = (r % c_len).astype(F32)
    same = (r[:, None] // c_len) == (r[None, :] // c_len)
    dist = i[:, None] - i[None, :]
    decay = jnp.where(same[None] & (dist >= 0)[None],
                      jnp.exp(log_g[:, None, None] * jnp.maximum(dist, 0.0)[None]), 0.0)
    qin = jnp.repeat(jnp.exp(log_g[None, :] * (i[:, None] + 1.0)), RET_V_DIM, axis=1)
    kout = jnp.repeat(jnp.exp(log_g[None, :] * (c_len - 1.0 - i)[:, None]), RET_QK_DIM, axis=1)
    gc = jnp.broadcast_to(jnp.exp(log_g * c_len)[:, None], (RET_HEADS, LANE))
    gc = jnp.concatenate([gc, jnp.zeros((8 - RET_HEADS, LANE), F32)], axis=0)
    return decay.astype(F32), qin.astype(F32), kout.astype(F32), gc.astype(F32)


def _mixa_tables(w_s, b_s, t_s):
    w_p = jnp.tril(w_s)
    w8 = jnp.tril(w_s[:, :t_s, :t_s])
    eye = jnp.eye(CHUNK // t_s, dtype=w_s.dtype)
    w_smp = jax.vmap(lambda m: jnp.kron(eye, m))(w8)
    w2 = jnp.stack([w_p, w_smp]).astype(BF16)
    b_p = jnp.repeat(b_s.T, LANE, axis=1)
    b_smp = jnp.repeat(jnp.tile(b_s[:, :t_s].T, (CHUNK // t_s, 1)), LANE, axis=1)
    return w2, jnp.stack([b_p, b_smp]).astype(F32)


def kernel(x_prompt, x_sample, cache_swa_kv0, cache_swa_kv1, cache_swa_kv2, state_ret, norm1_g, w_in, norm_v_g, w_s, b_s, q_norm_g, k_norm_g, w_branch, w_out, norm2_g, w1, w3, w2, w_router, we1, we3, we2):
    n_p, s, d = x_prompt.shape
    n_s, t_s, _ = x_sample.shape
    depth = w_in.shape[0]
    Tp, Ts = n_p * s, n_s * t_s
    T = Tp + Ts
    max_win, max_dil = SWA_PATTERNS[-1]
    assert d == D_MODEL and Ts == TILE and CHUNK % t_s == 0
    assert s % (CHUNK * max_dil) == 0 and s >= max_win
    n_pt = Tp // TILE

    xa, xb = x_prompt.reshape(Tp, d), x_sample.reshape(Ts, d)
    pos = jnp.concatenate([jnp.arange(s, dtype=jnp.int32),
                           jnp.tile(PAST_LEN + jnp.arange(t_s, dtype=jnp.int32), n_s)])
    tabs = _rope_tables(pos, s)
    perm = _perm_matrices()
    rt_prompt = _ret_tables(CHUNK)
    rt_sample = _ret_tables(t_s)
    caches = tuple(c.reshape(c.shape[0], c.shape[1], -1, SWA_HEAD_DIM)
                   for c in (cache_swa_kv0, cache_swa_kv1, cache_swa_kv2))

    p_kv = [[] for _ in SWA_PATTERNS]
    s_kv = [[] for _ in SWA_PATTERNS]
    p_ret, s_ret, s_v = [], [], []
    for layer in range(depth):
        if layer == 0:
            later = [(w_in, 1)] + [(w, 0) for w in (w_branch, w_out, w1, w3, w2, we1, we3, we2)]
            P, (w_in_later, w_branch_b, w_out_b, w1_b, w3_b, w2_b, we1_b, we3_b, we2_b) = _inproj(
                xa, xb, norm1_g[0][None, :], w_in[0].astype(BF16), perm, tabs, norm_v_g[0][None, :],
                q_norm_g[0][None, :], k_norm_g[0][None, :], n_pt, s // TILE, later)
        else:
            P, _ = _inproj(xa, xb, norm1_g[layer][None, :], w_in_later[layer - 1], perm, tabs,
                           norm_v_g[layer][None, :], q_norm_g[layer][None, :], k_norm_g[layer][None, :],
                           n_pt, s // TILE)

        w2a, b2a = _mixa_tables(w_s[layer], b_s[layer], t_s)
        out_a = _mixer_a(P, w2a, b2a, Tp // CHUNK)

        Ps = P[Tp:].astype(F32).reshape(n_s, t_s, IN_WIDTH)
        po, plse, so, slse = [], [], [], []
        for g, (win, dil) in enumerate(SWA_PATTERNS):
            o_g, l_g = _swa_prompt(P, g, dil, n_p, s)
            os_g, ls_g = _swa_sample(Ps, caches[g], layer, g, dil)
            po.append(o_g)
            plse.append(l_g)
            so.append(os_g.reshape(Ts, COL).astype(BF16))
            slse.append(ls_g.reshape(Ts, LANE))

        out_c, ret_p = _ret_prompt(P, rt_prompt, n_p, s)
        out_cs, ret_s = _ret_sample(P, state_ret, layer, rt_sample, Tp, t_s)

        x = _branch(xa, xb, out_a, P, w_branch_b[layer], w_out_b[layer],
                    (*po, *plse, out_c), (*so, *slse, out_cs), n_pt)

        g2 = norm2_g[layer][None, :]
        i = layer // 2
        if layer % 2 == 0:
            x = _dense_ffn(x, g2, w1_b[i], w3_b[i], w2_b[i])
            xa = xb = x
        else:
            wr_pad = jnp.zeros((D_MODEL, LANE), BF16).at[:, :N_EXPERTS].set(w_router[i].astype(BF16))
            xa, xb = _moe(x, g2, wr_pad, we1_b[i], we3_b[i], we2_b[i])

        for g, (win, dil) in enumerate(SWA_PATTERNS):
            kcols = slice((CB_K + g) * COL, (CB_K + g + 1) * COL)
            vcols = slice((CB_V + g) * COL, (CB_V + g + 1) * COL)
            keep = -(-min(win, s) // TILE) * TILE

            def rows(cols):
                blk = jnp.concatenate([P[(b + 1) * s - keep:(b + 1) * s, cols] for b in range(n_p)])
                blk = _natural(blk, dil).reshape(n_p, keep, SWA_HEADS, SWA_HEAD_DIM)
                return blk[:, keep - min(win, s):]

            p_kv[g].append(jnp.stack([rows(kcols), rows(vcols)], axis=2).astype(F32))
            ks = P[Tp:, kcols].reshape(n_s, t_s, SWA_HEADS, SWA_HEAD_DIM)
            vs = P[Tp:, vcols].reshape(n_s, t_s, SWA_HEADS, SWA_HEAD_DIM)
            s_kv[g].append(jnp.stack([ks, vs], axis=2).astype(F32))
        p_ret.append(ret_p)
        s_ret.append(ret_s)
        s_v.append(P[Tp:, CB_AV * COL:(CB_AV + 1) * COL].astype(F32).reshape(n_s, t_s, COL))

    y_prompt = xa[:Tp].reshape(n_p, s, d)
    y_sample = xb[xb.shape[0] - Ts:].reshape(n_s, t_s, d)
    return (y_prompt, y_sample,
            jnp.stack(p_kv[0]), jnp.stack(p_kv[1]), jnp.stack(p_kv[2]), jnp.stack(p_ret),
            jnp.stack(s_kv[0]), jnp.stack(s_kv[1]), jnp.stack(s_kv[2]), jnp.stack(s_ret),
            jnp.stack(s_v))
```

```python
import functools
import math

import jax
import jax.numpy as jnp
from jax import lax
from jax.experimental import pallas as pl
from jax.experimental.pallas import tpu as pltpu

F32 = jnp.float32
BF16 = jnp.bfloat16

PAST_LEN = 16384
EPS = 1e-6
NEG_INF = -1e30
ROPE_THETA = 10000.0

D_MODEL = 1024
LANE = 128
BF16_ROWS = 16
CHUNK = 128
TILE = 256
COL = 512
A_GROUPS = 4
SWA_PATTERNS = ((128, 1), (512, 4), (2048, 16))
SWA_HEADS = 4
SWA_HEAD_DIM = 128
RET_HEADS = 4
RET_QK_DIM = 64
RET_V_DIM = 128
N_EXPERTS = 8
TOP_K = 2
IN_WIDTH = 10240
N_COL = IN_WIDTH // COL

CB_AU, CB_AV, CB_Q, CB_K, CB_V, CB_RQK, CB_RV, CB_RG, CB_GATE = 0, 1, 2, 5, 8, 11, 12, 13, 14

VMEM_INTERNAL_SCRATCH = 8 * 1024 * 1024


def _pick(n, candidates):
    for c in candidates:
        if n % c == 0:
            return c
    raise ValueError(f"no tile in {candidates} divides {n}")


def _params(block_bytes, semantics=None):
    limit = int(block_bytes) + VMEM_INTERNAL_SCRATCH
    return pltpu.CompilerParams(dimension_semantics=semantics, vmem_limit_bytes=limit)


def _nbytes(shape, dtype):
    return math.prod(shape) * jnp.dtype(dtype).itemsize


def _rms(x):
    return x * lax.rsqrt(jnp.mean(x * x, axis=-1, keepdims=True) + EPS)


def _gelu(x):
    return 0.5 * x * (1.0 + lax.erf(x * (0.5 ** 0.5)))


def _sigmoid(x):
    return 1.0 / (1.0 + jnp.exp(-x))


def _idiv(x, n):
    assert n & (n - 1) == 0
    return x >> (n.bit_length() - 1)


def _imod(x, n):
    assert n & (n - 1) == 0
    return x & (n - 1)


def _dot(a, b):
    return jnp.dot(a, b, preferred_element_type=F32)


LSE_LANES = LANE // 4


def _pack_heads(cols):
    rows = cols[0].shape[0]
    grp = _idiv(lax.broadcasted_iota(jnp.int32, (rows, LANE), 1), LSE_LANES)
    out = jnp.broadcast_to(cols[-1], (rows, LANE))
    for h in range(len(cols) - 2, -1, -1):
        out = jnp.where(grp == h, cols[h], out)
    return out


def _dot_nt(a, b):
    return lax.dot_general(a, b, (((1,), (1,)), ((), ())), preferred_element_type=F32)


def _tile_rows(i, n_prompt_tiles, xa_ref, xb_ref):
    return jnp.where(i < n_prompt_tiles, xa_ref[...], xb_ref[...])


def _x_specs(xa, xb):
    last_a, last_b = xa.shape[0] // TILE - 1, xb.shape[0] // TILE - 1
    return [pl.BlockSpec((TILE, D_MODEL), lambda i, *_: (jnp.minimum(i, last_a), 0)),
            pl.BlockSpec((TILE, D_MODEL), lambda i, *_: (last_b, 0))]


def _inproj_kernel(*refs, n_prompt_tiles, n_cast):
    (xa_ref, xb_ref, g1_ref, w_ref, perm_ref, cosb_ref, sinb_ref, cosc_ref, sinc_ref,
     nvg_ref, qg_ref, kg_ref) = refs[:12]
    cast_in, o_ref, cast_out, h_ref = refs[12:12 + n_cast], refs[12 + n_cast], refs[13 + n_cast:-1], refs[-1]
    i = pl.program_id(0)
    for src, dst in zip(cast_in, cast_out):
        dst[...] = src[...].astype(dst.dtype)
    hn = (_rms(_tile_rows(i, n_prompt_tiles, xa_ref, xb_ref)) * g1_ref[...]).astype(BF16)
    h_ref[0] = hn

    @pl.when(i < n_prompt_tiles)
    def _():
        for k in range(1, len(SWA_PATTERNS)):
            h_ref[k] = _dot(perm_ref[k - 1], hn).astype(BF16)

    @pl.when(i >= n_prompt_tiles)
    def _():
        for k in range(1, len(SWA_PATTERNS)):
            h_ref[k] = hn

    def qk_heads(acc, g, cols0, gain_ref, scale):
        for hh in range(SWA_HEADS):
            cs = slice(hh * LANE, (hh + 1) * LANE)
            y = _rms(acc[:, cs]) * gain_ref[...]
            rot = y * cosb_ref[g] + pltpu.roll(y, LANE // 2, axis=1) * sinb_ref[g]
            o_ref[:, cols0 + hh * LANE:cols0 + (hh + 1) * LANE] = (rot * scale).astype(o_ref.dtype)

    for j in range(N_COL):
        cols = slice(j * COL, (j + 1) * COL)
        g = (j - CB_Q) % len(SWA_PATTERNS) if CB_Q <= j < CB_RQK else 0
        acc = _dot(h_ref[g], w_ref[:, cols])
        if j == CB_AU:
            o_ref[:, cols] = _gelu(acc).astype(o_ref.dtype)
        elif j == CB_AV:
            o_ref[:, cols] = (_rms(_gelu(acc)) * nvg_ref[...]).astype(o_ref.dtype)
        elif CB_Q <= j < CB_K:
            qk_heads(acc, g, j * COL, qg_ref, SWA_HEAD_DIM ** -0.5)
        elif CB_K <= j < CB_V:
            qk_heads(acc, g, j * COL, kg_ref, 1.0)
        elif j < CB_RQK or j == CB_RV:
            o_ref[:, cols] = acc.astype(o_ref.dtype)
        elif j == CB_RQK:
            lane = lax.broadcasted_iota(jnp.int32, (acc.shape[0], LANE), 1)
            first_half = _imod(lane, RET_QK_DIM) < (RET_QK_DIM // 2)
            for tt in range(COL // LANE):
                y = acc[:, tt * LANE:(tt + 1) * LANE]
                partner = jnp.where(first_half,
                                    pltpu.roll(y, LANE - RET_QK_DIM // 2, axis=1),
                                    pltpu.roll(y, RET_QK_DIM // 2, axis=1))
                rot = y * cosc_ref[...] + partner * sinc_ref[...]
                scale = 1.0 if tt < (COL // LANE) // 2 else RET_QK_DIM ** -0.5
                o_ref[:, j * COL + tt * LANE:j * COL + (tt + 1) * LANE] = (rot * scale).astype(o_ref.dtype)
        elif j == CB_RG:
            o_ref[:, cols] = (acc * _sigmoid(acc)).astype(o_ref.dtype)
        else:
            o_ref[:, cols] = _sigmoid(acc).astype(o_ref.dtype)


def _cast_blocks(arr, n_steps, skip):
    a2 = arr.reshape(-1, arr.shape[-1])
    first_row = skip * (a2.shape[0] // arr.shape[0])
    rows = a2.shape[0] - first_row
    rb = -(-rows // n_steps)
    rb = -(-rb // BF16_ROWS) * BF16_ROWS
    while rows % rb or first_row % rb:
        rb += BF16_ROWS
    return a2, rb, first_row // rb


def _inproj(xa, xb, g1, w_in_b, perm, tabs, nvg, qg, kg, n_prompt_tiles, tiles_per_seq, to_cast=()):
    T = (n_prompt_tiles + 1) * TILE
    n_steps = T // TILE
    casts = [_cast_blocks(a, n_steps, skip) for a, skip in to_cast]
    n_blk = [a2.shape[0] // rb - first for a2, rb, first in casts]
    cast_in = [pl.BlockSpec((rb, a2.shape[1]), functools.partial(
        lambda i, first, last: (first + jnp.minimum(i, last), 0), first=first, last=n - 1))
        for (a2, rb, first), n in zip(casts, n_blk)]
    cast_out = [pl.BlockSpec((rb, a2.shape[1]), functools.partial(
        lambda i, last: (jnp.minimum(i, last), 0), last=n - 1)) for (a2, rb, _), n in zip(casts, n_blk)]
    cosb, sinb, cosc, sinc = tabs
    n_g = len(SWA_PATTERNS)
    row = lambda i: (i, 0)
    const = lambda i: (0, 0)
    tab_blk = lambda i: jnp.where(i < n_prompt_tiles, i % tiles_per_seq, tiles_per_seq)
    tab3 = pl.BlockSpec((n_g, TILE, LANE), lambda i: (0, tab_blk(i), 0))
    tab = pl.BlockSpec((TILE, LANE), lambda i: (tab_blk(i), 0))
    vmem = (2 * _nbytes((TILE, D_MODEL), F32) + n_g * _nbytes((TILE, D_MODEL), BF16)
            + _nbytes(w_in_b.shape, BF16) + 2 * _nbytes((TILE, IN_WIDTH), BF16)
            + 2 * (2 * n_g + 2) * _nbytes((TILE, LANE), F32) + 2 * _nbytes(perm.shape, BF16)
            + 8 * _nbytes((TILE, COL), F32)
            + sum(2 * (_nbytes((rb, a2.shape[1]), F32) + _nbytes((rb, a2.shape[1]), BF16)) for a2, rb, _ in casts))
    outs = pl.pallas_call(
        functools.partial(_inproj_kernel, n_prompt_tiles=n_prompt_tiles, n_cast=len(casts)),
        grid=(n_steps,),
        in_specs=_x_specs(xa, xb) + [
            pl.BlockSpec((1, D_MODEL), const),
            pl.BlockSpec(w_in_b.shape, const, pipeline_mode=pl.Buffered(1)),
            pl.BlockSpec(perm.shape, lambda i: (0, 0, 0)),
            tab3, tab3, tab, tab,
            pl.BlockSpec((1, COL), const),
            pl.BlockSpec((1, LANE), const),
            pl.BlockSpec((1, LANE), const),
        ] + cast_in,
        out_specs=[pl.BlockSpec((TILE, IN_WIDTH), row)] + cast_out,
        out_shape=[jax.ShapeDtypeStruct((T, IN_WIDTH), BF16)]
                  + [jax.ShapeDtypeStruct((n * rb, a2.shape[1]), BF16) for (a2, rb, _), n in zip(casts, n_blk)],
        scratch_shapes=[pltpu.VMEM((n_g, TILE, D_MODEL), BF16)],
        compiler_params=_params(vmem, ("arbitrary",)),
        name="inproj",
    )(xa, xb, g1, w_in_b, perm, cosb, sinb, cosc, sinc, nvg, qg, kg, *[a2 for a2, _, _ in casts])
    return outs[0], [o.reshape((a.shape[0] - skip,) + a.shape[1:]) for o, (a, skip) in zip(outs[1:], to_cast)]


def _mixa_kernel(u_ref, v_ref, w_ref, b_ref, o_ref, *, cps, n_prompt_chunks):
    i = pl.program_id(0)
    for c in range(cps):
        var = ((i * cps + c) >= n_prompt_chunks).astype(jnp.int32)
        rows = slice(c * CHUNK, (c + 1) * CHUNK)
        for g in range(A_GROUPS):
            cols = slice(g * LANE, (g + 1) * LANE)
            z = _dot(w_ref[var, g], v_ref[rows, cols]) + b_ref[var, :, cols]
            o_ref[rows, cols] = (u_ref[rows, cols].astype(F32) * z).astype(o_ref.dtype)


def _mixer_a(P, w2, b2, n_prompt_chunks):
    T = P.shape[0]
    n_chunks = T // CHUNK
    cps = _pick(n_chunks, (8, 6, 4, 3, 2, 1))
    rows = cps * CHUNK
    vmem = 6 * _nbytes((rows, COL), BF16) + 2 * _nbytes(w2.shape, BF16) + 2 * _nbytes(b2.shape, F32)
    return pl.pallas_call(
        functools.partial(_mixa_kernel, cps=cps, n_prompt_chunks=n_prompt_chunks),
        grid=(n_chunks // cps,),
        in_specs=[
            pl.BlockSpec((rows, COL), lambda i: (i, CB_AU)),
            pl.BlockSpec((rows, COL), lambda i: (i, CB_AV)),
            pl.BlockSpec(w2.shape, lambda i: (0, 0, 0, 0)),
            pl.BlockSpec(b2.shape, lambda i: (0, 0, 0)),
        ],
        out_specs=pl.BlockSpec((rows, COL), lambda i: (i, 0)),
        out_shape=jax.ShapeDtypeStruct((T, COL), BF16),
        compiler_params=_params(vmem, ("parallel",)),
        name="mixer_a",
    )(P, P, w2, b2)


def _swa_kernel(q_ref, kp_ref, kc_ref, vp_ref, vc_ref, o_ref, l_ref, k_ref, v_ref, s_ref, p_ref, *, qb):
    i = pl.program_id(2)
    rows = qb * CHUNK
    lead = q_ref.shape[:-1]
    k_ref[0:CHUNK] = kp_ref[...].reshape(CHUNK, COL)
    k_ref[CHUNK:CHUNK + rows] = kc_ref[...].reshape(rows, COL)
    v_ref[0:CHUNK] = vp_ref[...].reshape(CHUNK, COL)
    v_ref[CHUNK:CHUNK + rows] = vc_ref[...].reshape(rows, COL)
    row = lax.broadcasted_iota(jnp.int32, (CHUNK, 2 * CHUNK), 0)
    col = lax.broadcasted_iota(jnp.int32, (CHUNK, 2 * CHUNK), 1)
    mask_cur = (col >= CHUNK) & (col - CHUNK <= row)
    mask_all = mask_cur | ((col < CHUNK) & (col >= row))
    mask_first = mask_cur | ((col < CHUNK) & (col >= row) & (i > 0))
    q_all = q_ref[...].reshape(rows, COL)
    for j in range(qb):
        mask = mask_first if j == 0 else mask_all
        for h in range(SWA_HEADS):
            cs = slice(h * LANE, (h + 1) * LANE)
            sc = _dot_nt(q_all[j * CHUNK:(j + 1) * CHUNK, cs], k_ref[j * CHUNK:(j + 2) * CHUNK, cs])
            s_ref[j * SWA_HEADS + h] = jnp.where(mask, sc, NEG_INF)
    s = s_ref[...]
    m = jnp.max(s, axis=-1, keepdims=True)
    p = jnp.exp(s - m)
    den = jnp.sum(p, axis=-1, keepdims=True)
    p_ref[...] = p.astype(BF16)
    lse = m + jnp.log(den)
    for h in range(SWA_HEADS):
        cs = slice(h * LANE, (h + 1) * LANE)
        o_h = [_dot(p_ref[j * SWA_HEADS + h], v_ref[j * CHUNK:(j + 2) * CHUNK, cs]) / den[j * SWA_HEADS + h]
               for j in range(qb)]
        o_ref[..., cs] = jnp.concatenate(o_h, axis=0).astype(o_ref.dtype).reshape(lead + (LANE,))
    packed = [_pack_heads([lse[j * SWA_HEADS + h] for h in range(SWA_HEADS)]) for j in range(qb)]
    l_ref[...] = jnp.concatenate(packed, axis=0).reshape(lead + (LANE,))


def _swa_prompt(P, g, dil, n_p, s):
    T = P.shape[0]
    Tp = n_p * s
    nb = s // dil // CHUNK
    qb = _pick(nb, (8, 4, 2, 1))
    steps = nb // qb
    prev_blk = lambda b, i: b * nb + jnp.maximum(i * qb - 1, 0)
    if dil == 1:
        src = P
        lead_q, lead_p = (qb * CHUNK,), (CHUNK,)
        o_shape = (Tp, COL)
        l_shape = (Tp, LANE)
        q_map = lambda cb: (lambda b, r, i: (b * steps + i, cb + g))
        p_map = lambda cb: (lambda b, r, i: (prev_blk(b, i), cb + g))
        o_map = lambda b, r, i: (b * steps + i, 0)
        l_map = lambda b, r, i: (b * steps + i, 0)
    else:
        rpt = TILE // dil
        tpb = CHUNK // rpt
        src = P.reshape(T // TILE, dil, rpt, IN_WIDTH)
        lead_q, lead_p = (qb * tpb, None, rpt), (tpb, None, rpt)
        o_shape = (Tp // TILE, dil, rpt, COL)
        l_shape = (Tp // TILE, dil, rpt, LANE)
        q_map = lambda cb: (lambda b, r, i: (b * steps + i, r, 0, cb + g))
        p_map = lambda cb: (lambda b, r, i: (prev_blk(b, i), r, 0, cb + g))
        o_map = lambda b, r, i: (b * steps + i, r, 0, 0)
        l_map = lambda b, r, i: (b * steps + i, r, 0, 0)

    q_spec = lambda cb: pl.BlockSpec(lead_q + (COL,), q_map(cb))
    p_spec = lambda cb: pl.BlockSpec(lead_p + (COL,), p_map(cb))
    n_pairs = qb * SWA_HEADS
    vmem = ((10 * qb + 6) * _nbytes((CHUNK, COL), BF16) + 2 * qb * _nbytes((CHUNK, COL), F32)
            + 6 * n_pairs * _nbytes((CHUNK, 2 * CHUNK), F32))
    o, l = pl.pallas_call(
        functools.partial(_swa_kernel, qb=qb),
        grid=(n_p, dil, steps),
        in_specs=[q_spec(CB_Q), p_spec(CB_K), q_spec(CB_K), p_spec(CB_V), q_spec(CB_V)],
        out_specs=[pl.BlockSpec(lead_q + (COL,), o_map),
                   pl.BlockSpec(lead_q + (LANE,), l_map)],
        out_shape=[jax.ShapeDtypeStruct(o_shape, BF16), jax.ShapeDtypeStruct(l_shape, F32)],
        scratch_shapes=[pltpu.VMEM(((qb + 1) * CHUNK, COL), BF16), pltpu.VMEM(((qb + 1) * CHUNK, COL), BF16),
                        pltpu.VMEM((n_pairs, CHUNK, 2 * CHUNK), F32),
                        pltpu.VMEM((n_pairs, CHUNK, 2 * CHUNK), BF16)],
        compiler_params=_params(vmem, ("parallel", "parallel", "arbitrary")),
        name=f"swa_prompt_g{g}",
    )(src, src, src, src, src)
    return o.reshape(Tp, COL), l.reshape(Tp, LANE)


def _swa_sample_kernel(q_ref, kn_ref, vn_ref, cache_ref, o_ref, l_ref, *, dil, lbuf, t_s):
    nq = SWA_HEADS * t_s
    q = q_ref[...]
    qrep = jnp.concatenate([q] * SWA_HEADS, axis=0)
    rq = lax.broadcasted_iota(jnp.int32, (nq, COL), 0)
    cq = lax.broadcasted_iota(jnp.int32, (nq, COL), 1)
    qbd = jnp.where(_idiv(rq, t_s) == _idiv(cq, LANE), qrep, 0.0).astype(BF16)

    per_pos = 2 * SWA_HEADS
    if cache_ref.ndim == 2:
        n_keys = lbuf
        rows_of = lambda first: cache_ref[pl.ds(first, lbuf, stride=per_pos), :]
        key_pos = lambda r: r
    else:
        n_keys = cache_ref.shape[0] * t_s
        rows_of = lambda first: cache_ref[:, pl.ds(first, t_s, stride=per_pos), :].reshape(n_keys, LANE)
        key_pos = lambda r: dil * _idiv(r, t_s) + _imod(r, t_s)
    kc = jnp.concatenate([rows_of(h).astype(BF16) for h in range(SWA_HEADS)], axis=1)
    vc = jnp.concatenate([rows_of(SWA_HEADS + h).astype(BF16) for h in range(SWA_HEADS)], axis=1)
    kn = kn_ref[...].astype(BF16)
    vn = vn_ref[...].astype(BF16)

    s_c = _dot_nt(qbd, kc)
    s_n = _dot_nt(qbd, kn)
    t_c = _imod(lax.broadcasted_iota(jnp.int32, (nq, n_keys), 0), t_s)
    c_c = key_pos(lax.broadcasted_iota(jnp.int32, (nq, n_keys), 1))
    diff_c = lbuf + t_c - c_c
    ok_c = (_imod(diff_c, dil) == 0) & (diff_c <= lbuf)
    t_n = _imod(lax.broadcasted_iota(jnp.int32, (nq, t_s), 0), t_s)
    c_n = lax.broadcasted_iota(jnp.int32, (nq, t_s), 1)
    diff_n = t_n - c_n
    ok_n = (diff_n >= 0) & (_imod(diff_n, dil) == 0)
    s_c = jnp.where(ok_c, s_c, NEG_INF)
    s_n = jnp.where(ok_n, s_n, NEG_INF)
    m = jnp.maximum(jnp.max(s_c, axis=-1, keepdims=True), jnp.max(s_n, axis=-1, keepdims=True))
    p_c = jnp.exp(s_c - m)
    p_n = jnp.exp(s_n - m)
    den = jnp.sum(p_c, axis=-1, keepdims=True) + jnp.sum(p_n, axis=-1, keepdims=True)
    o_all = (_dot(p_c.astype(BF16), vc) + _dot(p_n.astype(BF16), vn)) / den
    lse = m + jnp.log(den)
    for h in range(SWA_HEADS):
        cs = slice(h * LANE, (h + 1) * LANE)
        o_ref[:, cs] = o_all[h * t_s:(h + 1) * t_s, cs]
    l_ref[...] = _pack_heads([lse[h * t_s:(h + 1) * t_s] for h in range(SWA_HEADS)])


def _swa_sample(Ps, cache, layer, g, dil):
    n_s, t_s, _ = Ps.shape
    rows = cache.shape[2]
    lbuf = rows // (2 * SWA_HEADS)
    assert lbuf == dil * CHUNK, "window buffer must hold exactly one full window"
    blk = lambda cb: pl.BlockSpec((None, t_s, COL), lambda b: (b, 0, cb + g))
    if dil > t_s:
        per_pos = rows // lbuf
        cache = cache.reshape(cache.shape[0], n_s, lbuf // dil, dil * per_pos, LANE)
        cache_spec = pl.BlockSpec((None, None, lbuf // dil, t_s * per_pos, LANE), lambda b: (layer, b, 0, 0, 0))
    else:
        cache_spec = pl.BlockSpec((None, None, rows, LANE), lambda b: (layer, b, 0, 0))
    vmem = (2 * _nbytes((lbuf, 2 * COL), F32) + 2 * _nbytes((lbuf, 2 * COL), BF16)
            + 8 * _nbytes((SWA_HEADS * t_s, lbuf), F32))
    return pl.pallas_call(
        functools.partial(_swa_sample_kernel, dil=dil, lbuf=lbuf, t_s=t_s),
        grid=(n_s,),
        in_specs=[blk(CB_Q), blk(CB_K), blk(CB_V), cache_spec],
        out_specs=[pl.BlockSpec((None, t_s, COL), lambda b: (b, 0, 0)),
                   pl.BlockSpec((None, t_s, LANE), lambda b: (b, 0, 0))],
        out_shape=[jax.ShapeDtypeStruct((n_s, t_s, COL), F32),
                   jax.ShapeDtypeStruct((n_s, t_s, LANE), F32)],
        compiler_params=_params(vmem, ("parallel",)),
        name=f"swa_sample_g{g}",
    )(Ps, Ps, Ps, cache)


def _ret_head_inputs(qk_ref, v_ref, kout_ref, h):
    pair, half = h // 2, h % 2
    lane = lax.broadcasted_iota(jnp.int32, (CHUNK, LANE), 1)
    head_lanes = _idiv(lane, RET_QK_DIM) == half
    qt = qk_ref[:, pair * LANE:(pair + 1) * LANE]
    kt = qk_ref[:, COL // 2 + pair * LANE:COL // 2 + (pair + 1) * LANE]
    qm = jnp.where(head_lanes, qt, jnp.zeros_like(qt))
    kw = jnp.where(head_lanes, kt.astype(F32) * kout_ref[:, pair * LANE:(pair + 1) * LANE], 0.0)
    vh = v_ref[:, h * LANE:(h + 1) * LANE]
    return qm, kt, kw, vh


def _ret_finish(o, gate_ref, o_ref, h):
    cs = slice(h * LANE, (h + 1) * LANE)
    o_ref[:, cs] = (gate_ref[:, cs].astype(F32) * _rms(o)).astype(o_ref.dtype)


def _ret_prompt_kernel(*refs, n_seq):
    ins = refs[:3 * n_seq]
    decay_ref, qin_ref, kout_ref, gc_ref, o_ref, s_out_ref, s_ref, att_ref = refs[3 * n_seq:]
    i = pl.program_id(0)

    @pl.when(i == 0)
    def _():
        s_ref[...] = jnp.zeros_like(s_ref)

    pairs = [(b, h) for b in range(n_seq) for h in range(RET_HEADS)]
    head_in = lambda b, h: _ret_head_inputs(ins[3 * b], ins[3 * b + 1], kout_ref, h)
    for b, h in pairs:
        qm, kt, _, _ = head_in(b, h)
        att_ref[b, h] = (_dot_nt(qm, kt) * decay_ref[h]).astype(BF16)
    for b, h in pairs:
        qm, _, _, vh = head_in(b, h)
        o = (_dot(att_ref[b, h], vh)
             + _dot(qm, s_ref[b, h].astype(BF16)) * qin_ref[:, h * LANE:(h + 1) * LANE])
        _ret_finish(o, ins[3 * b + 2], o_ref.at[b], h)
    for b, h in pairs:
        _, _, kw, vh = head_in(b, h)
        s_ref[b, h] = s_ref[b, h] * gc_ref[h:h + 1, :] + _dot(kw.T.astype(BF16), vh)

    @pl.when(i == pl.num_programs(0) - 1)
    def _():
        for b in range(n_seq):
            for h in range(RET_HEADS):
                lo = (h % 2) * RET_QK_DIM
                s_out_ref[b, h] = s_ref[b, h, lo:lo + RET_QK_DIM, :]


def _ret_prompt(P, rt, n_p, s):
    nblk = s // CHUNK
    decay, qin, kout, gc = rt
    blk = lambda b, cb: pl.BlockSpec((CHUNK, COL), lambda i: (b * nblk + i, cb))
    const2 = lambda i: (0, 0)
    seq_specs = [blk(b, cb) for b in range(n_p) for cb in (CB_RQK, CB_RV, CB_RG)]
    st_shape = (n_p, RET_HEADS, RET_QK_DIM, RET_V_DIM)
    vmem = (8 * n_p * _nbytes((CHUNK, COL), BF16) + 2 * _nbytes(decay.shape, F32) + 4 * _nbytes(qin.shape, F32)
            + 3 * n_p * _nbytes((RET_HEADS, LANE, LANE), F32))
    o, st = pl.pallas_call(
        functools.partial(_ret_prompt_kernel, n_seq=n_p),
        grid=(nblk,),
        in_specs=seq_specs + [pl.BlockSpec(decay.shape, lambda i: (0, 0, 0)),
                              pl.BlockSpec(qin.shape, const2), pl.BlockSpec(kout.shape, const2),
                              pl.BlockSpec(gc.shape, const2)],
        out_specs=[pl.BlockSpec((n_p, CHUNK, COL), lambda i: (0, i, 0)),
                   pl.BlockSpec(st_shape, lambda i: (0, 0, 0, 0))],
        out_shape=[jax.ShapeDtypeStruct((n_p, s, COL), BF16), jax.ShapeDtypeStruct(st_shape, F32)],
        scratch_shapes=[pltpu.VMEM((n_p, RET_HEADS, LANE, LANE), F32),
                        pltpu.VMEM((n_p, RET_HEADS, CHUNK, CHUNK), BF16)],
        compiler_params=_params(vmem, ("arbitrary",)),
        name="ret_prompt",
    )(*([P] * (3 * n_p)), decay, qin, kout, gc)
    return o.reshape(n_p * s, COL), st


def _ret_sample_kernel(qk_ref, v_ref, gate_ref, s0_ref, decay_ref, qin_ref, kout_ref, gc_ref,
                       o_ref, s_out_ref, *, t_s):
    row = lax.broadcasted_iota(jnp.int32, (CHUNK, LANE), 0)
    for h in range(RET_HEADS):
        lo = (h % 2) * RET_QK_DIM
        qm, kt, kw, vh = _ret_head_inputs(qk_ref, v_ref, kout_ref, h)
        att = _dot_nt(qm, kt) * decay_ref[h]
        o = _dot(att.astype(BF16), vh)
        inter = jnp.zeros((CHUNK, LANE), F32)
        for sq in range(CHUNK // t_s):
            seq_rows = _idiv(row, t_s) == sq
            st = s0_ref[sq, h]
            st2 = jnp.concatenate([st, st], axis=0).astype(BF16)
            inter = jnp.where(seq_rows, _dot(qm, st2), inter)
            upd = _dot(jnp.where(seq_rows, kw, 0.0).T.astype(BF16), vh)
            s_out_ref[sq, h] = st * gc_ref[h:h + 1, :] + upd[lo:lo + RET_QK_DIM, :]
        o = o + inter * qin_ref[:, h * LANE:(h + 1) * LANE]
        _ret_finish(o, gate_ref, o_ref, h)


def _ret_sample(P, state, layer, rt, Tp, t_s):
    n_s = state.shape[1]
    spb = CHUNK // t_s
    base = Tp // CHUNK
    decay, qin, kout, gc = rt
    blk = lambda cb: pl.BlockSpec((CHUNK, COL), lambda i: (base + i, cb))
    const2 = lambda i: (0, 0)
    st_shape = (spb, RET_HEADS, RET_QK_DIM, RET_V_DIM)
    vmem = (8 * _nbytes((CHUNK, COL), BF16) + 2 * _nbytes(decay.shape, F32) + 4 * _nbytes(qin.shape, F32)
            + 4 * _nbytes(st_shape, F32))
    return pl.pallas_call(
        functools.partial(_ret_sample_kernel, t_s=t_s),
        grid=(n_s // spb,),
        in_specs=[blk(CB_RQK), blk(CB_RV), blk(CB_RG),
                  pl.BlockSpec((None,) + st_shape, lambda i: (layer, i, 0, 0, 0)),
                  pl.BlockSpec(decay.shape, lambda i: (0, 0, 0)),
                  pl.BlockSpec(qin.shape, const2), pl.BlockSpec(kout.shape, const2),
                  pl.BlockSpec(gc.shape, const2)],
        out_specs=[pl.BlockSpec((CHUNK, COL), lambda i: (i, 0)),
                   pl.BlockSpec(st_shape, lambda i: (i, 0, 0, 0))],
        out_shape=[jax.ShapeDtypeStruct((n_s * t_s, COL), BF16),
                   jax.ShapeDtypeStruct((n_s, RET_HEADS, RET_QK_DIM, RET_V_DIM), F32)],
        compiler_params=_params(vmem, ("parallel",)),
        name="ret_sample",
    )(P, P, P, state, decay, qin, kout, gc)


def _branch_kernel(xa_ref, xb_ref, a_ref, g0_ref, g1_ref, g2_ref, wb_ref, wo_ref, ex_ref,
                   po0, po1, po2, pl0, pl1, pl2, pc, so0, so1, so2, sl0, sl1, sl2, sc,
                   y_ref, on_ref, ln_ref, *, n_prompt_tiles):
    i = pl.program_id(0)

    def spread(packed):
        hi = packed.astype(BF16)
        lo = (packed - hi.astype(F32)).astype(BF16)
        return _dot(jnp.concatenate([hi, lo], axis=1), ex_ref[...])

    def natural(o_ref, l_ref, k, dil):
        if dil == 1:
            return (lambda h: o_ref[:, h * LANE:(h + 1) * LANE].astype(F32)), l_ref[...]
        rpt = TILE // dil
        for r in range(dil):
            rows = slice(r * rpt, (r + 1) * rpt)
            ln_ref[k, pl.ds(r, rpt, stride=dil), :] = l_ref[rows, :]
            for h in range(SWA_HEADS):
                on_ref[k, h, pl.ds(r, rpt, stride=dil), :] = o_ref[rows, h * LANE:(h + 1) * LANE].astype(F32)
        return (lambda h: on_ref[k, h]), ln_ref[k]

    def body(o_refs, l_refs, c_ref, permuted):
        groups = [natural(o_refs[k], l_refs[k], k, SWA_PATTERNS[k][1] if permuted else 1)
                  for k in range(len(SWA_PATTERNS))]
        ls = [l for _, l in groups]
        lmax = functools.reduce(jnp.maximum, ls)
        es = [jnp.exp(l - lmax) for l in ls]
        inv = 1.0 / sum(es)
        weights = [spread(e * inv) for e in es]
        heads = []
        for h in range(SWA_HEADS):
            cs = slice(h * LANE, (h + 1) * LANE)
            heads.append(sum(w[:, cs] * go(h) for w, (go, _) in zip(weights, groups)).astype(BF16))
        mix = jnp.concatenate(heads, axis=1)
        merged = (g0_ref[...].astype(F32) * _dot(a_ref[...], wb_ref[0])
                  + g1_ref[...].astype(F32) * _dot(mix, wb_ref[1])
                  + g2_ref[...].astype(F32) * _dot(c_ref[...], wb_ref[2]))
        y_ref[...] = _tile_rows(i, n_prompt_tiles, xa_ref, xb_ref) + _dot(merged.astype(BF16), wo_ref[...])

    @pl.when(i < n_prompt_tiles)
    def _():
        body((po0, po1, po2), (pl0, pl1, pl2), pc, True)

    @pl.when(i >= n_prompt_tiles)
    def _():
        body((so0, so1, so2), (sl0, sl1, sl2), sc, False)


def _branch(xa, xb, out_a, P, wb, wo, prompt_set, sample_set, n_prompt_tiles):
    T = (n_prompt_tiles + 1) * TILE
    assert sample_set[0].shape[0] == TILE, "the sample rows must form exactly one tile"
    n_g = len(SWA_PATTERNS)
    last = n_prompt_tiles - 1
    row = lambda i: (i, 0)
    gate = lambda k: pl.BlockSpec((TILE, D_MODEL), lambda i: (i, CB_GATE * COL // D_MODEL + k))
    once = pl.Buffered(1)
    p_o = pl.BlockSpec((TILE, COL), lambda i: (jnp.minimum(i, last), 0))
    p_l = pl.BlockSpec((TILE, LANE), lambda i: (jnp.minimum(i, last), 0))
    s_o = pl.BlockSpec((TILE, COL), lambda i: (0, 0))
    s_l = pl.BlockSpec((TILE, LANE), lambda i: (0, 0))
    src = jnp.arange(2 * LANE) % LANE
    dst_head = jnp.arange(COL) // LANE
    expand = (src[:, None] == dst_head[None, :] * LSE_LANES).astype(BF16)
    vmem = (4 * _nbytes((TILE, D_MODEL), F32) + 6 * _nbytes((TILE, D_MODEL), BF16)
            + 2 * (2 * n_g + 2) * (_nbytes((TILE, COL), BF16) + _nbytes((TILE, COL), F32))
            + _nbytes(wb.shape, BF16) + _nbytes(wo.shape, BF16)
            + 4 * n_g * _nbytes((TILE, COL), F32) + 4 * _nbytes((TILE, D_MODEL), F32))
    return pl.pallas_call(
        functools.partial(_branch_kernel, n_prompt_tiles=n_prompt_tiles),
        grid=(T // TILE,),
        in_specs=_x_specs(xa, xb) + [
                  pl.BlockSpec((TILE, COL), row),
                  gate(0), gate(1), gate(2),
                  pl.BlockSpec(wb.shape, lambda i: (0, 0, 0), pipeline_mode=once),
                  pl.BlockSpec(wo.shape, lambda i: (0, 0), pipeline_mode=once),
                  pl.BlockSpec(expand.shape, lambda i: (0, 0)),
                  p_o, p_o, p_o, p_l, p_l, p_l, p_o,
                  s_o, s_o, s_o, s_l, s_l, s_l, s_o],
        out_specs=pl.BlockSpec((TILE, D_MODEL), row),
        out_shape=jax.ShapeDtypeStruct((T, D_MODEL), F32),
        scratch_shapes=[pltpu.VMEM((n_g, SWA_HEADS, TILE, LANE), F32),
                        pltpu.VMEM((n_g, TILE, LANE), F32)],
        compiler_params=_params(vmem, ("parallel",)),
        name="branch_merge",
    )(xa, xb, out_a, P, P, P, wb, wo, expand, *prompt_set, *sample_set)


def _swiglu_into(hb, w1_ref, w3_ref, w2_ref, acc_ref, fc):
    n_fc = w1_ref.shape[-1] // fc
    for f in range(n_fc):
        cols = slice(f * fc, (f + 1) * fc)
        a = _dot(hb, w1_ref[:, cols])
        b = _dot(hb, w3_ref[:, cols])
        part = _dot((a * _sigmoid(a) * b).astype(BF16), w2_ref[cols, :])
        if f == 0:
            acc_ref[...] = part
        else:
            acc_ref[...] += part


def _dense_ffn_kernel(x_ref, g2_ref, w1_ref, w3_ref, w2_ref, y_ref, acc_ref, *, fc):
    x = x_ref[...]
    hb = (_rms(x) * g2_ref[...]).astype(BF16)
    _swiglu_into(hb, w1_ref, w3_ref, w2_ref, acc_ref, fc)
    y_ref[...] = x + acc_ref[...]


def _dense_ffn(x, g2, w1b, w3b, w2b):
    T = x.shape[0]
    F = w1b.shape[1]
    tm = _pick(T, (768, 512, 384, 256, 128))
    fc = _pick(F, (512, 256, 128))
    row = lambda i: (i, 0)
    once = pl.Buffered(1)
    vmem = (5 * _nbytes((tm, D_MODEL), F32) + 3 * _nbytes(w1b.shape, BF16) + _nbytes((tm, D_MODEL), BF16)
            + 4 * _nbytes((tm, fc), F32))
    return pl.pallas_call(
        functools.partial(_dense_ffn_kernel, fc=fc),
        grid=(T // tm,),
        in_specs=[pl.BlockSpec((tm, D_MODEL), row),
                  pl.BlockSpec((1, D_MODEL), lambda i: (0, 0)),
                  pl.BlockSpec(w1b.shape, lambda i: (0, 0), pipeline_mode=once),
                  pl.BlockSpec(w3b.shape, lambda i: (0, 0), pipeline_mode=once),
                  pl.BlockSpec(w2b.shape, lambda i: (0, 0), pipeline_mode=once)],
        out_specs=pl.BlockSpec((tm, D_MODEL), row),
        out_shape=jax.ShapeDtypeStruct((T, D_MODEL), F32),
        scratch_shapes=[pltpu.VMEM((tm, D_MODEL), F32)],
        compiler_params=_params(vmem, ("parallel",)),
        name="dense_ffn",
    )(x, g2, w1b, w3b, w2b)


def _router_kernel(x_ref, g2_ref, wr_ref, e0_ref, e1_ref, w0_ref, w1_ref):
    hb = (_rms(x_ref[...]) * g2_ref[...]).astype(BF16)
    logits = _dot(hb, wr_ref[...])
    lane = lax.broadcasted_iota(jnp.int32, logits.shape, 1)
    logits = jnp.where(lane < N_EXPERTS, logits, -jnp.inf)
    lane_f = lane.astype(F32)
    m1 = jnp.max(logits, axis=-1, keepdims=True)
    i1 = jnp.min(jnp.where(logits == m1, lane_f, float(LANE)), axis=-1, keepdims=True)
    rest = jnp.where(lane_f == i1, -jnp.inf, logits)
    m2 = jnp.max(rest, axis=-1, keepdims=True)
    i2 = jnp.min(jnp.where(rest == m2, lane_f, float(LANE)), axis=-1, keepdims=True)
    e = jnp.exp(m2 - m1)
    e0_ref[...] = jnp.broadcast_to(i1, logits.shape).astype(jnp.int32)
    e1_ref[...] = jnp.broadcast_to(i2, logits.shape).astype(jnp.int32)
    w0_ref[...] = jnp.broadcast_to(1.0 / (1.0 + e), logits.shape)
    w1_ref[...] = jnp.broadcast_to(e / (1.0 + e), logits.shape)


def _router(x, g2, wr_pad):
    T = x.shape[0]
    tm = _pick(T, (768, 512, 384, 256, 128))
    row = lambda i: (i, 0)
    out = pl.BlockSpec((tm, LANE), row)
    vmem = 4 * _nbytes((tm, D_MODEL), F32) + 2 * _nbytes(wr_pad.shape, BF16) + 16 * _nbytes((tm, LANE), F32)
    return pl.pallas_call(
        _router_kernel,
        grid=(T // tm,),
        in_specs=[pl.BlockSpec((tm, D_MODEL), row),
                  pl.BlockSpec((1, D_MODEL), lambda i: (0, 0)),
                  pl.BlockSpec(wr_pad.shape, lambda i: (0, 0))],
        out_specs=[out, out, out, out],
        out_shape=[jax.ShapeDtypeStruct((T, LANE), jnp.int32), jax.ShapeDtypeStruct((T, LANE), jnp.int32),
                   jax.ShapeDtypeStruct((T, LANE), F32), jax.ShapeDtypeStruct((T, LANE), F32)],
        compiler_params=_params(vmem, ("parallel",)),
        name="moe_router",
    )(x, g2, wr_pad)


RUN_ALIGN = 8


def _run_pieces(count, max_rows):
    pieces, off = [], 0
    size = max_rows
    while size >= RUN_ALIGN:
        active = (count & size) != 0
        pieces.append((size, off, active))
        off = off + jnp.where(active, size, 0)
        size //= 2
    return pieces


def _packed_rows(i, e0, e1, lower_ref, loc_ref):
    lane = lax.broadcasted_iota(jnp.int32, e0.shape, 1)
    picks = jnp.where((lane == e0) | (lane == e1), 1.0, 0.0)
    rank = _dot(lower_ref[...], picks.astype(BF16))
    lane1 = lax.broadcasted_iota(jnp.int32, (1, LANE), 1)
    run_start = jnp.zeros((1, LANE), F32)
    for e in range(N_EXPERTS):
        run_start = jnp.where(lane1 == e, loc_ref[i * N_EXPERTS + e].astype(F32), run_start)
    row_of = rank + run_start
    return tuple(jnp.sum(jnp.where(lane == ek, row_of, 0.0), axis=-1, keepdims=True) for ek in (e0, e1))


RUN_ROWS = 640


def _dispatch_kernel(base_ref, cnt_ref, loc_ref, pend_ref, x_ref, g2_ref, e0_ref, e1_ref, lower_ref, xs_ref,
                     comp_ref, sem, *, tme, n_tiles):
    i = pl.program_id(0)
    n_steps = pl.num_programs(0)
    cur = i % 2
    half = tme // 2

    @pl.when(i == 0)
    def _():
        comp_ref[0, 0:half] = jnp.zeros((half, D_MODEL), comp_ref.dtype)

        def zero_tile(start):
            return [pltpu.make_async_copy(
                        comp_ref.at[0, pl.ds(0, half), :],
                        xs_ref.at[pl.ds(pl.multiple_of(start + k * half, half), half), :], sem.at[0])
                    for k in range(2)]

        def tail_loop(fn):
            def body(t, carry):
                for c in zero_tile(t * tme):
                    fn(c)
                return carry
            lax.fori_loop(pend_ref[N_EXPERTS - 1] // tme, n_tiles, body, 0)

        def group_tails(fn):
            for e in range(N_EXPERTS):
                prev_end = pend_ref[e - 1] if e else 0

                @pl.when(pend_ref[e] > prev_end)
                def _():
                    for c in zero_tile(pend_ref[e] - tme):
                        fn(c)

        group_tails(lambda c: c.start())
        tail_loop(lambda c: c.start())
        group_tails(lambda c: c.wait())
        tail_loop(lambda c: c.wait())

    h = (_rms(x_ref[...]) * g2_ref[...]).astype(BF16)
    pos0, pos1 = _packed_rows(i, e0_ref[...], e1_ref[...], lower_ref, loc_ref)
    slot = lax.broadcasted_iota(jnp.int32, (TILE, RUN_ROWS), 1).astype(F32)
    hit = jnp.where((slot == pos0) | (slot == pos1), 1.0, 0.0)
    comp_ref[cur] = _dot(hit.T.astype(BF16), h)

    def for_copies(tile, buf, fn):
        for e in range(N_EXPERTS):
            k = tile * N_EXPERTS + e
            cnt, dst, loc = cnt_ref[k], base_ref[k], loc_ref[k]
            for size, off, active in _run_pieces(cnt, TILE):
                c = pltpu.make_async_copy(
                    comp_ref.at[buf, pl.ds(pl.multiple_of(loc + off, RUN_ALIGN), size), :],
                    xs_ref.at[pl.ds(pl.multiple_of(dst + off, RUN_ALIGN), size), :], sem.at[buf])
                pl.when(active)(functools.partial(fn, c))

    for_copies(i, cur, lambda c: c.start())

    @pl.when(i > 0)
    def _():
        for_copies(i - 1, 1 - cur, lambda c: c.wait())

    @pl.when(i == n_steps - 1)
    def _():
        for_copies(i, cur, lambda c: c.wait())


def _dispatch(x, g2, e0, e1, base, cnt, loc, pend, lower, n_tiles, tme):
    T = x.shape[0]
    assert tme == 2 * TILE
    row = lambda i, *_: (i, 0)
    assert RUN_ROWS >= 2 * TILE + N_EXPERTS * (RUN_ALIGN - 1) and RUN_ROWS % LANE == 0
    vmem = (3 * _nbytes((TILE, D_MODEL), F32) + 2 * _nbytes((RUN_ROWS, D_MODEL), F32)
            + 8 * _nbytes((RUN_ROWS, TILE), F32))
    grid_spec = pltpu.PrefetchScalarGridSpec(
        num_scalar_prefetch=4,
        grid=(T // TILE,),
        in_specs=[pl.BlockSpec((TILE, D_MODEL), row),
                  pl.BlockSpec((1, D_MODEL), lambda i, *_: (0, 0)),
                  pl.BlockSpec((TILE, LANE), row), pl.BlockSpec((TILE, LANE), row),
                  pl.BlockSpec((TILE, TILE), lambda i, *_: (0, 0))],
        out_specs=pl.BlockSpec(memory_space=pl.ANY),
        scratch_shapes=[pltpu.VMEM((2, RUN_ROWS, D_MODEL), F32), pltpu.SemaphoreType.DMA((2,))],
    )
    return pl.pallas_call(
        functools.partial(_dispatch_kernel, tme=tme, n_tiles=n_tiles),
        grid_spec=grid_spec,
        out_shape=jax.ShapeDtypeStruct((n_tiles * tme, D_MODEL), F32),
        compiler_params=_params(vmem, ("arbitrary",)),
        name="moe_dispatch",
    )(base, cnt, loc, pend, x, g2, e0, e1, lower)


def _expert_ffn_kernel(te_ref, used_ref, xs_ref, w1_ref, w3_ref, w2_ref, y_ref, acc_ref, *, fc):
    i = pl.program_id(0)

    @pl.when(i < used_ref[0])
    def _():
        _swiglu_into(xs_ref[...].astype(BF16), w1_ref, w3_ref, w2_ref, acc_ref, fc)
        y_ref[...] = acc_ref[...]

    @pl.when(i >= used_ref[0])
    def _():
        y_ref[...] = jnp.zeros_like(y_ref)


def _expert_ffn(xs, tile_expert, n_used, we1b, we3b, we2b, tme):
    n_rows = xs.shape[0]
    F = we1b.shape[2]
    fc = _pick(F, (512, 256, 128))
    once = pl.Buffered(1)
    wspec = lambda shape: pl.BlockSpec((None,) + shape, lambda i, te, nu: (te[i], 0, 0), pipeline_mode=once)
    vmem = (5 * _nbytes((tme, D_MODEL), F32) + 3 * _nbytes(we1b.shape[1:], BF16) + _nbytes((tme, D_MODEL), BF16)
            + 4 * _nbytes((tme, fc), F32))
    grid_spec = pltpu.PrefetchScalarGridSpec(
        num_scalar_prefetch=2,
        grid=(n_rows // tme,),
        in_specs=[pl.BlockSpec((tme, D_MODEL), lambda i, te, nu: (jnp.minimum(i, jnp.maximum(nu[0], 1) - 1), 0)),
                  wspec(we1b.shape[1:]), wspec(we3b.shape[1:]), wspec(we2b.shape[1:])],
        out_specs=pl.BlockSpec((tme, D_MODEL), lambda i, te, nu: (i, 0)),
        scratch_shapes=[pltpu.VMEM((tme, D_MODEL), F32)],
    )
    return pl.pallas_call(
        functools.partial(_expert_ffn_kernel, fc=fc),
        grid_spec=grid_spec,
        out_shape=jax.ShapeDtypeStruct((n_rows, D_MODEL), F32),
        compiler_params=_params(vmem, ("arbitrary",)),
        name="moe_expert_ffn",
    )(tile_expert, n_used, xs, we1b, we3b, we2b)


def _combine_kernel(base_ref, cnt_ref, loc_ref, x_ref, e0_ref, e1_ref, w0_ref, w1_ref, lower_ref, y_hbm_ref,
                    o_ref, o_last_ref, buf_ref, sem):
    i = pl.program_id(0)
    n_steps = pl.num_programs(0)
    cur = i % 2

    def for_copies(tile, buf, fn):
        for e in range(N_EXPERTS):
            k = tile * N_EXPERTS + e
            cnt, src, loc = cnt_ref[k], base_ref[k], loc_ref[k]
            for size, off, active in _run_pieces(cnt, TILE):
                c = pltpu.make_async_copy(
                    y_hbm_ref.at[pl.ds(pl.multiple_of(src + off, RUN_ALIGN), size), :],
                    buf_ref.at[buf, pl.ds(pl.multiple_of(loc + off, RUN_ALIGN), size), :], sem.at[buf])
                pl.when(active)(functools.partial(fn, c))

    @pl.when(i == 0)
    def _():
        buf_ref[...] = jnp.zeros_like(buf_ref)
        for_copies(0, 0, lambda c: c.start())

    @pl.when(i + 1 < n_steps)
    def _():
        for_copies(i + 1, 1 - cur, lambda c: c.start())

    for_copies(i, cur, lambda c: c.wait())

    positions = _packed_rows(i, e0_ref[...], e1_ref[...], lower_ref, loc_ref)
    slot = lax.broadcasted_iota(jnp.int32, (TILE, RUN_ROWS), 1).astype(F32)
    yb = buf_ref[cur].astype(BF16)
    wide = lambda a, n: jnp.concatenate([a] * n, axis=1)
    acc = x_ref[...]
    for pos, wk in zip(positions, (w0_ref, w1_ref)):
        rows = _dot(jnp.where(slot == pos, 1.0, 0.0).astype(BF16), yb)
        acc = acc + wide(wk[...], D_MODEL // LANE) * rows

    @pl.when(i < n_steps - 1)
    def _():
        o_ref[...] = acc

    @pl.when(i == n_steps - 1)
    def _():
        o_last_ref[...] = acc


def _combine(x, e0, e1, w0, w1, base, cnt, loc, lower, y_rows):
    T = x.shape[0]
    row = lambda i, *_: (i, 0)
    lane_spec = pl.BlockSpec((TILE, LANE), row)
    vmem = (5 * _nbytes((TILE, D_MODEL), F32) + 3 * _nbytes((RUN_ROWS, D_MODEL), F32)
            + 8 * _nbytes((TILE, LANE), F32) + 8 * _nbytes((TILE, RUN_ROWS), F32)
            + 4 * _nbytes((TILE, D_MODEL), F32))
    grid_spec = pltpu.PrefetchScalarGridSpec(
        num_scalar_prefetch=3,
        grid=(T // TILE,),
        in_specs=[pl.BlockSpec((TILE, D_MODEL), row), lane_spec, lane_spec, lane_spec, lane_spec,
                  pl.BlockSpec((TILE, TILE), lambda i, *_: (0, 0)),
                  pl.BlockSpec(memory_space=pl.ANY)],
        out_specs=[pl.BlockSpec((TILE, D_MODEL), lambda i, *_: (jnp.minimum(i, T // TILE - 2), 0)),
                   pl.BlockSpec((TILE, D_MODEL), lambda i, *_: (0, 0))],
        scratch_shapes=[pltpu.VMEM((2, RUN_ROWS, D_MODEL), F32), pltpu.SemaphoreType.DMA((2,))],
    )
    return pl.pallas_call(
        _combine_kernel,
        grid_spec=grid_spec,
        out_shape=[jax.ShapeDtypeStruct((T - TILE, D_MODEL), F32), jax.ShapeDtypeStruct((TILE, D_MODEL), F32)],
        compiler_params=_params(vmem, ("arbitrary",)),
        name="moe_combine",
    )(base, cnt, loc, x, e0, e1, w0, w1, lower, y_rows)


def _moe(x, g2, wr_pad, we1b, we3b, we2b):
    T = x.shape[0]
    tme = 2 * TILE
    n_tt = T // TILE
    e0, e1, w0, w1 = _router(x, g2, wr_pad)
    experts = jnp.arange(N_EXPERTS, dtype=jnp.int32)
    picks = (e0[:, :1] == experts[None, :]) | (e1[:, :1] == experts[None, :])
    cnt = jnp.sum(picks.reshape(n_tt, TILE, N_EXPERTS).astype(jnp.int32), axis=1)
    cnt = (cnt + RUN_ALIGN - 1) // RUN_ALIGN * RUN_ALIGN
    counts = jnp.sum(cnt, axis=0)
    padded = (counts + tme - 1) // tme * tme
    pend = jnp.cumsum(padded).astype(jnp.int32)
    base = ((pend - padded)[None, :] + jnp.cumsum(cnt, axis=0) - cnt).astype(jnp.int32)
    loc = (jnp.cumsum(cnt, axis=1) - cnt).astype(jnp.int32)
    max_rows = T * TOP_K + n_tt * N_EXPERTS * (RUN_ALIGN - 1)
    n_tiles = -(-max_rows // tme) + N_EXPERTS
    tile_start = jnp.arange(n_tiles, dtype=jnp.int32) * tme
    tile_expert = jnp.minimum(jnp.sum(pend[None, :] <= tile_start[:, None], axis=1), N_EXPERTS - 1).astype(jnp.int32)
    n_used = (pend[N_EXPERTS - 1:] // tme).astype(jnp.int32)
    tok = jnp.arange(TILE)
    lower = (tok[None, :] < tok[:, None]).astype(BF16)
    base_f, cnt_f, loc_f = base.reshape(-1), cnt.reshape(-1).astype(jnp.int32), loc.reshape(-1)
    xs = _dispatch(x, g2, e0, e1, base_f, cnt_f, loc_f, pend, lower, n_tiles, tme)
    y_rows = _expert_ffn(xs, tile_expert, n_used, we1b, we3b, we2b, tme)
    return _combine(x, e0, e1, w0, w1, base_f, cnt_f, loc_f, lower, y_rows)


def _class_major(a, dil):
    n = a.shape[0]
    return a.reshape((n // TILE, TILE // dil, dil) + a.shape[1:]).swapaxes(1, 2).reshape(a.shape)


def _natural(a, dil):
    n = a.shape[0]
    return a.reshape((n // TILE, dil, TILE // dil) + a.shape[1:]).swapaxes(1, 2).reshape(a.shape)


def _perm_matrices():
    eye = jnp.eye(TILE, dtype=BF16)
    return jnp.stack([_class_major(eye, dil) for _, dil in SWA_PATTERNS[1:]])


def _rope_tables(pos, Tp):
    freqs = ROPE_THETA ** (-jnp.arange(0, SWA_HEAD_DIM, 2, dtype=F32) / SWA_HEAD_DIM)
    rfreqs = 1.0 / (ROPE_THETA ** jnp.linspace(0.0, 1.0, RET_QK_DIM // 2, dtype=F32))

    def swa(p):
        ang = p.astype(F32)[:, None] * freqs[None, :]
        c, s = jnp.cos(ang), jnp.sin(ang)
        return jnp.concatenate([c, c], axis=1), jnp.concatenate([-s, s], axis=1)

    per_group = [swa(jnp.concatenate([_class_major(pos[:Tp], dil), pos[Tp:]])) for _, dil in SWA_PATTERNS]
    cosb = jnp.stack([c for c, _ in per_group])
    sinb = jnp.stack([s for _, s in per_group])
    rang = pos.astype(F32)[:, None] * rfreqs[None, :]
    c, s = jnp.cos(rang), jnp.sin(rang)
    cosc = jnp.concatenate([c, c, c, c], axis=1)
    sinc = jnp.concatenate([-s, s, -s, s], axis=1)
    return cosb, sinb, cosc, sinc


def _ret_tables(c_len):
    log_g = jnp.log1p(-jnp.exp2(-5.0 - jnp.arange(RET_HEADS, dtype=F32)))
    r = jnp.arange(CHUNK)
    i = (r % c_len).astype(F32)
    same = (r[:, None] // c_len) == (r[None, :] // c_len)
    dist = i[:, None] - i[None, :]
    decay = jnp.where(same[None] & (dist >= 0)[None],
                      jnp.exp(log_g[:, None, None] * jnp.maximum(dist, 0.0)[None]), 0.0)
    qin = jnp.repeat(jnp.exp(log_g[None, :] * (i[:, None] + 1.0)), RET_V_DIM, axis=1)
    kout = jnp.repeat(jnp.exp(log_g[None, :] * (c_len - 1.0 - i)[:, None]), RET_QK_DIM, axis=1)
    gc = jnp.broadcast_to(jnp.exp(log_g * c_len)[:, None], (RET_HEADS, LANE))
    gc = jnp.concatenate([gc, jnp.zeros((8 - RET_HEADS, LANE), F32)], axis=0)
    return decay.astype(F32), qin.astype(F32), kout.astype(F32), gc.astype(F32)


def _mixa_tables(w_s, b_s, t_s):
    w_p = jnp.tril(w_s)
    w8 = jnp.tril(w_s[:, :t_s, :t_s])
    eye = jnp.eye(CHUNK // t_s, dtype=w_s.dtype)
    w_smp = jax.vmap(lambda m: jnp.kron(eye, m))(w8)
    w2 = jnp.stack([w_p, w_smp]).astype(BF16)
    b_p = jnp.repeat(b_s.T, LANE, axis=1)
    b_smp = jnp.repeat(jnp.tile(b_s[:, :t_s].T, (CHUNK // t_s, 1)), LANE, axis=1)
    return w2, jnp.stack([b_p, b_smp]).astype(F32)


def kernel(x_prompt, x_sample, cache_swa_kv0, cache_swa_kv1, cache_swa_kv2, state_ret, norm1_g, w_in, norm_v_g, w_s, b_s, q_norm_g, k_norm_g, w_branch, w_out, norm2_g, w1, w3, w2, w_router, we1, we3, we2):
    n_p, s, d = x_prompt.shape
    n_s, t_s, _ = x_sample.shape
    depth = w_in.shape[0]
    Tp, Ts = n_p * s, n_s * t_s
    T = Tp + Ts
    max_win, max_dil = SWA_PATTERNS[-1]
    assert d == D_MODEL and Ts == TILE and CHUNK % t_s == 0
    assert s % (CHUNK * max_dil) == 0 and s >= max_win
    n_pt = Tp // TILE

    xa, xb = x_prompt.reshape(Tp, d), x_sample.reshape(Ts, d)
    pos = jnp.concatenate([jnp.arange(s, dtype=jnp.int32),
                           jnp.tile(PAST_LEN + jnp.arange(t_s, dtype=jnp.int32), n_s)])
    tabs = _rope_tables(pos, s)
    perm = _perm_matrices()
    rt_prompt = _ret_tables(CHUNK)
    rt_sample = _ret_tables(t_s)
    caches = tuple(c.reshape(c.shape[0], c.shape[1], -1, SWA_HEAD_DIM)
                   for c in (cache_swa_kv0, cache_swa_kv1, cache_swa_kv2))

    p_kv = [[] for _ in SWA_PATTERNS]
    s_kv = [[] for _ in SWA_PATTERNS]
    p_ret, s_ret, s_v = [], [], []
    for layer in range(depth):
        if layer == 0:
            later = [(w_in, 1)] + [(w, 0) for w in (w_branch, w_out, w1, w3, w2, we1, we3, we2)]
            P, (w_in_later, w_branch_b, w_out_b, w1_b, w3_b, w2_b, we1_b, we3_b, we2_b) = _inproj(
                xa, xb, norm1_g[0][None, :], w_in[0].astype(BF16), perm, tabs, norm_v_g[0][None, :],
                q_norm_g[0][None, :], k_norm_g[0][None, :], n_pt, s // TILE, later)
        else:
            P, _ = _inproj(xa, xb, norm1_g[layer][None, :], w_in_later[layer - 1], perm, tabs,
                           norm_v_g[layer][None, :], q_norm_g[layer][None, :], k_norm_g[layer][None, :],
                           n_pt, s // TILE)

        w2a, b2a = _mixa_tables(w_s[layer], b_s[layer], t_s)
        out_a = _mixer_a(P, w2a, b2a, Tp // CHUNK)

        Ps = P[Tp:].astype(F32).reshape(n_s, t_s, IN_WIDTH)
        po, plse, so, slse = [], [], [], []
        for g, (win, dil) in enumerate(SWA_PATTERNS):
            o_g, l_g = _swa_prompt(P, g, dil, n_p, s)
            os_g, ls_g = _swa_sample(Ps, caches[g], layer, g, dil)
            po.append(o_g)
            plse.append(l_g)
            so.append(os_g.reshape(Ts, COL).astype(BF16))
            slse.append(ls_g.reshape(Ts, LANE))

        out_c, ret_p = _ret_prompt(P, rt_prompt, n_p, s)
        out_cs, ret_s = _ret_sample(P, state_ret, layer, rt_sample, Tp, t_s)

        x = _branch(xa, xb, out_a, P, w_branch_b[layer], w_out_b[layer],
                    (*po, *plse, out_c), (*so, *slse, out_cs), n_pt)

        g2 = norm2_g[layer][None, :]
        i = layer // 2
        if layer % 2 == 0:
            x = _dense_ffn(x, g2, w1_b[i], w3_b[i], w2_b[i])
            xa = xb = x
        else:
            wr_pad = jnp.zeros((D_MODEL, LANE), BF16).at[:, :N_EXPERTS].set(w_router[i].astype(BF16))
            xa, xb = _moe(x, g2, wr_pad, we1_b[i], we3_b[i], we2_b[i])

        for g, (win, dil) in enumerate(SWA_PATTERNS):
            kcols = slice((CB_K + g) * COL, (CB_K + g + 1) * COL)
            vcols = slice((CB_V + g) * COL, (CB_V + g + 1) * COL)
            keep = -(-min(win, s) // TILE) * TILE

            def rows(cols):
                blk = jnp.concatenate([P[(b + 1) * s - keep:(b + 1) * s, cols] for b in range(n_p)])
                blk = _natural(blk, dil).reshape(n_p, keep, SWA_HEADS, SWA_HEAD_DIM)
                return blk[:, keep - min(win, s):]

            p_kv[g].append(jnp.stack([rows(kcols), rows(vcols)], axis=2).astype(F32))
            ks = P[Tp:, kcols].reshape(n_s, t_s, SWA_HEADS, SWA_HEAD_DIM)
            vs = P[Tp:, vcols].reshape(n_s, t_s, SWA_HEADS, SWA_HEAD_DIM)
            s_kv[g].append(jnp.stack([ks, vs], axis=2).astype(F32))
        p_ret.append(ret_p)
        s_ret.append(ret_s)
        s_v.append(P[Tp:, CB_AV * COL:(CB_AV + 1) * COL].astype(F32).reshape(n_s, t_s, COL))

    y_prompt = xa[:Tp].reshape(n_p, s, d)
    y_sample = xb[xb.shape[0] - Ts:].reshape(n_s, t_s, d)
    return (y_prompt, y_sample,
            jnp.stack(p_kv[0]), jnp.stack(p_kv[1]), jnp.stack(p_kv[2]), jnp.stack(p_ret),
            jnp.stack(s_kv[0]), jnp.stack(s_kv[1]), jnp.stack(s_kv[2]), jnp.stack(s_ret),
            jnp.stack(s_v))
```

```python
import functools
import math

import jax
import jax.numpy as jnp
from jax import lax
from jax.experimental import pallas as pl
from jax.experimental.pallas import tpu as pltpu

F32 = jnp.float32
BF16 = jnp.bfloat16

PAST_LEN = 16384
EPS = 1e-6
NEG_INF = -1e30
ROPE_THETA = 10000.0

D_MODEL = 1024
LANE = 128
SUBLANE = 8
BF16_ROWS = 2 * SUBLANE
CHUNK = 128
TILE = 256
COL = 512
A_GROUPS = 4
SWA_PATTERNS = ((128, 1), (512, 4), (2048, 16))
SWA_HEADS = 4
SWA_HEAD_DIM = 128
RET_HEADS = 4
RET_QK_DIM = 64
RET_V_DIM = 128
N_EXPERTS = 8
TOP_K = 2
IN_WIDTH = 10240
N_COL = IN_WIDTH // COL

CB_AU, CB_AV, CB_Q, CB_K, CB_V, CB_RQK, CB_RV, CB_RG, CB_GATE = 0, 1, 2, 5, 8, 11, 12, 13, 14

VMEM_INTERNAL_SCRATCH = 8 * 1024 * 1024


def _pick(n, candidates):
    for c in candidates:
        if n % c == 0:
            return c
    raise ValueError(f"no tile in {candidates} divides {n}")


def _params(block_bytes, semantics=None):
    limit = int(block_bytes) + VMEM_INTERNAL_SCRATCH
    return pltpu.CompilerParams(dimension_semantics=semantics, vmem_limit_bytes=limit)


def _nbytes(shape, dtype):
    return math.prod(shape) * jnp.dtype(dtype).itemsize


def _rms(x):
    return x * lax.rsqrt(jnp.mean(x * x, axis=-1, keepdims=True) + EPS)


def _gelu(x):
    return 0.5 * x * (1.0 + lax.erf(x * (0.5 ** 0.5)))


def _sigmoid(x):
    return 1.0 / (1.0 + jnp.exp(-x))


def _idiv(x, n):
    assert n & (n - 1) == 0
    return x >> (n.bit_length() - 1)


def _imod(x, n):
    assert n & (n - 1) == 0
    return x & (n - 1)


def _dot(a, b):
    return jnp.dot(a, b, preferred_element_type=F32)


LSE_LANES = LANE // SWA_HEADS


def _pack_heads(cols):
    rows = cols[0].shape[0]
    grp = _idiv(lax.broadcasted_iota(jnp.int32, (rows, LANE), 1), LSE_LANES)
    out = jnp.broadcast_to(cols[-1], (rows, LANE))
    for h in range(len(cols) - 2, -1, -1):
        out = jnp.where(grp == h, cols[h], out)
    return out


def _dot_nt(a, b):
    return lax.dot_general(a, b, (((1,), (1,)), ((), ())), preferred_element_type=F32)


def _tile_rows(i, n_prompt_tiles, xa_ref, xb_ref):
    return jnp.where(i < n_prompt_tiles, xa_ref[...], xb_ref[...])


def _x_specs(xa, xb):
    last_a, last_b = xa.shape[0] // TILE - 1, xb.shape[0] // TILE - 1
    return [pl.BlockSpec((TILE, D_MODEL), lambda i, *_: (jnp.minimum(i, last_a), 0)),
            pl.BlockSpec((TILE, D_MODEL), lambda i, *_: (last_b, 0))]


def _inproj_kernel(*refs, n_prompt_tiles, n_cast):
    (xa_ref, xb_ref, g1_ref, w_ref, perm_ref, cosb_ref, sinb_ref, cosc_ref, sinc_ref,
     nvg_ref, qg_ref, kg_ref) = refs[:12]
    cast_in, o_ref, cast_out, h_ref = refs[12:12 + n_cast], refs[12 + n_cast], refs[13 + n_cast:-1], refs[-1]
    i = pl.program_id(0)
    for src, dst in zip(cast_in, cast_out):
        dst[...] = src[...].astype(dst.dtype)
    hn = (_rms(_tile_rows(i, n_prompt_tiles, xa_ref, xb_ref)) * g1_ref[...]).astype(BF16)
    h_ref[0] = hn

    @pl.when(i < n_prompt_tiles)
    def _():
        for k in range(1, len(SWA_PATTERNS)):
            h_ref[k] = _dot(perm_ref[k - 1], hn).astype(BF16)

    @pl.when(i >= n_prompt_tiles)
    def _():
        for k in range(1, len(SWA_PATTERNS)):
            h_ref[k] = hn

    def qk_heads(acc, g, cols0, gain_ref, scale):
        for hh in range(SWA_HEADS):
            cs = slice(hh * LANE, (hh + 1) * LANE)
            y = _rms(acc[:, cs]) * gain_ref[...]
            rot = y * cosb_ref[g] + pltpu.roll(y, LANE // 2, axis=1) * sinb_ref[g]
            o_ref[:, cols0 + hh * LANE:cols0 + (hh + 1) * LANE] = (rot * scale).astype(o_ref.dtype)

    for j in range(N_COL):
        cols = slice(j * COL, (j + 1) * COL)
        g = (j - CB_Q) % len(SWA_PATTERNS) if CB_Q <= j < CB_RQK else 0
        acc = _dot(h_ref[g], w_ref[:, cols])
        if j == CB_AU:
            o_ref[:, cols] = _gelu(acc).astype(o_ref.dtype)
        elif j == CB_AV:
            o_ref[:, cols] = (_rms(_gelu(acc)) * nvg_ref[...]).astype(o_ref.dtype)
        elif CB_Q <= j < CB_K:
            qk_heads(acc, g, j * COL, qg_ref, SWA_HEAD_DIM ** -0.5)
        elif CB_K <= j < CB_V:
            qk_heads(acc, g, j * COL, kg_ref, 1.0)
        elif j < CB_RQK or j == CB_RV:
            o_ref[:, cols] = acc.astype(o_ref.dtype)
        elif j == CB_RQK:
            lane = lax.broadcasted_iota(jnp.int32, (acc.shape[0], LANE), 1)
            first_half = _imod(lane, RET_QK_DIM) < (RET_QK_DIM // 2)
            for tt in range(COL // LANE):
                y = acc[:, tt * LANE:(tt + 1) * LANE]
                partner = jnp.where(first_half,
                                    pltpu.roll(y, LANE - RET_QK_DIM // 2, axis=1),
                                    pltpu.roll(y, RET_QK_DIM // 2, axis=1))
                rot = y * cosc_ref[...] + partner * sinc_ref[...]
                scale = 1.0 if tt < (COL // LANE) // 2 else RET_QK_DIM ** -0.5
                o_ref[:, j * COL + tt * LANE:j * COL + (tt + 1) * LANE] = (rot * scale).astype(o_ref.dtype)
        elif j == CB_RG:
            o_ref[:, cols] = (acc * _sigmoid(acc)).astype(o_ref.dtype)
        else:
            o_ref[:, cols] = _sigmoid(acc).astype(o_ref.dtype)


def _cast_blocks(arr, n_steps, skip):
    a2 = arr.reshape(-1, arr.shape[-1])
    first_row = skip * (a2.shape[0] // arr.shape[0])
    rows = a2.shape[0] - first_row
    rb = -(-rows // n_steps)
    rb = -(-rb // BF16_ROWS) * BF16_ROWS
    while rows % rb or first_row % rb:
        rb += BF16_ROWS
    return a2, rb, first_row // rb


def _inproj(xa, xb, g1, w_in_b, perm, tabs, nvg, qg, kg, n_prompt_tiles, tiles_per_seq, to_cast=()):
    T = (n_prompt_tiles + 1) * TILE
    n_steps = T // TILE
    casts = [_cast_blocks(a, n_steps, skip) for a, skip in to_cast]
    n_blk = [a2.shape[0] // rb - first for a2, rb, first in casts]
    cast_in = [pl.BlockSpec((rb, a2.shape[1]), functools.partial(
        lambda i, first, last: (first + jnp.minimum(i, last), 0), first=first, last=n - 1))
        for (a2, rb, first), n in zip(casts, n_blk)]
    cast_out = [pl.BlockSpec((rb, a2.shape[1]), functools.partial(
        lambda i, last: (jnp.minimum(i, last), 0), last=n - 1)) for (a2, rb, _), n in zip(casts, n_blk)]
    cosb, sinb, cosc, sinc = tabs
    n_g = len(SWA_PATTERNS)
    row = lambda i: (i, 0)
    const = lambda i: (0, 0)
    tab_blk = lambda i: jnp.where(i < n_prompt_tiles, i % tiles_per_seq, tiles_per_seq)
    tab3 = pl.BlockSpec((n_g, TILE, LANE), lambda i: (0, tab_blk(i), 0))
    tab = pl.BlockSpec((TILE, LANE), lambda i: (tab_blk(i), 0))
    vmem = (2 * _nbytes((TILE, D_MODEL), F32) + n_g * _nbytes((TILE, D_MODEL), BF16)
            + _nbytes(w_in_b.shape, BF16) + 2 * _nbytes((TILE, IN_WIDTH), BF16)
            + 2 * (2 * n_g + 2) * _nbytes((TILE, LANE), F32) + 2 * _nbytes(perm.shape, BF16)
            + 8 * _nbytes((TILE, COL), F32)
            + sum(2 * (_nbytes((rb, a2.shape[1]), F32) + _nbytes((rb, a2.shape[1]), BF16)) for a2, rb, _ in casts))
    outs = pl.pallas_call(
        functools.partial(_inproj_kernel, n_prompt_tiles=n_prompt_tiles, n_cast=len(casts)),
        grid=(n_steps,),
        in_specs=_x_specs(xa, xb) + [
            pl.BlockSpec((1, D_MODEL), const),
            pl.BlockSpec(w_in_b.shape, const, pipeline_mode=pl.Buffered(1)),
            pl.BlockSpec(perm.shape, lambda i: (0, 0, 0)),
            tab3, tab3, tab, tab,
            pl.BlockSpec((1, COL), const),
            pl.BlockSpec((1, LANE), const),
            pl.BlockSpec((1, LANE), const),
        ] + cast_in,
        out_specs=[pl.BlockSpec((TILE, IN_WIDTH), row)] + cast_out,
        out_shape=[jax.ShapeDtypeStruct((T, IN_WIDTH), BF16)]
                  + [jax.ShapeDtypeStruct((n * rb, a2.shape[1]), BF16) for (a2, rb, _), n in zip(casts, n_blk)],
        scratch_shapes=[pltpu.VMEM((n_g, TILE, D_MODEL), BF16)],
        compiler_params=_params(vmem, ("arbitrary",)),
        name="inproj",
    )(xa, xb, g1, w_in_b, perm, cosb, sinb, cosc, sinc, nvg, qg, kg, *[a2 for a2, _, _ in casts])
    return outs[0], [o.reshape((a.shape[0] - skip,) + a.shape[1:]) for o, (a, skip) in zip(outs[1:], to_cast)]


def _mixa_kernel(u_ref, v_ref, w_ref, b_ref, o_ref, *, cps, n_prompt_chunks):
    i = pl.program_id(0)
    for c in range(cps):
        var = ((i * cps + c) >= n_prompt_chunks).astype(jnp.int32)
        rows = slice(c * CHUNK, (c + 1) * CHUNK)
        for g in range(A_GROUPS):
            cols = slice(g * LANE, (g + 1) * LANE)
            z = _dot(w_ref[var, g], v_ref[rows, cols]) + b_ref[var, :, cols]
            o_ref[rows, cols] = (u_ref[rows, cols].astype(F32) * z).astype(o_ref.dtype)


def _mixer_a(P, w2, b2, n_prompt_chunks):
    T = P.shape[0]
    n_chunks = T // CHUNK
    cps = _pick(n_chunks, (8, 6, 4, 3, 2, 1))
    rows = cps * CHUNK
    vmem = 6 * _nbytes((rows, COL), BF16) + 2 * _nbytes(w2.shape, BF16) + 2 * _nbytes(b2.shape, F32)
    return pl.pallas_call(
        functools.partial(_mixa_kernel, cps=cps, n_prompt_chunks=n_prompt_chunks),
        grid=(n_chunks // cps,),
        in_specs=[
            pl.BlockSpec((rows, COL), lambda i: (i, CB_AU)),
            pl.BlockSpec((rows, COL), lambda i: (i, CB_AV)),
            pl.BlockSpec(w2.shape, lambda i: (0, 0, 0, 0)),
            pl.BlockSpec(b2.shape, lambda i: (0, 0, 0)),
        ],
        out_specs=pl.BlockSpec((rows, COL), lambda i: (i, 0)),
        out_shape=jax.ShapeDtypeStruct((T, COL), BF16),
        compiler_params=_params(vmem, ("parallel",)),
        name="mixer_a",
    )(P, P, w2, b2)


def _swa_kernel(q_ref, kp_ref, kc_ref, vp_ref, vc_ref, o_ref, l_ref, k_ref, v_ref, s_ref, p_ref, *, qb):
    i = pl.program_id(2)
    rows = qb * CHUNK
    lead = q_ref.shape[:-1]
    k_ref[0:CHUNK] = kp_ref[...].reshape(CHUNK, COL)
    k_ref[CHUNK:CHUNK + rows] = kc_ref[...].reshape(rows, COL)
    v_ref[0:CHUNK] = vp_ref[...].reshape(CHUNK, COL)
    v_ref[CHUNK:CHUNK + rows] = vc_ref[...].reshape(rows, COL)
    row = lax.broadcasted_iota(jnp.int32, (CHUNK, 2 * CHUNK), 0)
    col = lax.broadcasted_iota(jnp.int32, (CHUNK, 2 * CHUNK), 1)
    mask_cur = (col >= CHUNK) & (col - CHUNK <= row)
    mask_all = mask_cur | ((col < CHUNK) & (col >= row))
    mask_first = mask_cur | ((col < CHUNK) & (col >= row) & (i > 0))
    q_all = q_ref[...].reshape(rows, COL)
    for j in range(qb):
        mask = mask_first if j == 0 else mask_all
        for h in range(SWA_HEADS):
            cs = slice(h * LANE, (h + 1) * LANE)
            sc = _dot_nt(q_all[j * CHUNK:(j + 1) * CHUNK, cs], k_ref[j * CHUNK:(j + 2) * CHUNK, cs])
            s_ref[j * SWA_HEADS + h] = jnp.where(mask, sc, NEG_INF)
    s = s_ref[...]
    m = jnp.max(s, axis=-1, keepdims=True)
    p = jnp.exp(s - m)
    den = jnp.sum(p, axis=-1, keepdims=True)
    p_ref[...] = p.astype(BF16)
    lse = m + jnp.log(den)
    for h in range(SWA_HEADS):
        cs = slice(h * LANE, (h + 1) * LANE)
        o_h = [_dot(p_ref[j * SWA_HEADS + h], v_ref[j * CHUNK:(j + 2) * CHUNK, cs]) / den[j * SWA_HEADS + h]
               for j in range(qb)]
        o_ref[..., cs] = jnp.concatenate(o_h, axis=0).astype(o_ref.dtype).reshape(lead + (LANE,))
    packed = [_pack_heads([lse[j * SWA_HEADS + h] for h in range(SWA_HEADS)]) for j in range(qb)]
    l_ref[...] = jnp.concatenate(packed, axis=0).reshape(lead + (LANE,))


def _swa_prompt(P, g, dil, n_p, s):
    T = P.shape[0]
    Tp = n_p * s
    nb = s // dil // CHUNK
    qb = _pick(nb, (8, 4, 2, 1))
    steps = nb // qb
    prev_blk = lambda b, i: b * nb + jnp.maximum(i * qb - 1, 0)
    if dil == 1:
        src = P
        lead_q, lead_p = (qb * CHUNK,), (CHUNK,)
        o_shape = (Tp, COL)
        l_shape = (Tp, LANE)
        q_map = lambda cb: (lambda b, r, i: (b * steps + i, cb + g))
        p_map = lambda cb: (lambda b, r, i: (prev_blk(b, i), cb + g))
        o_map = lambda b, r, i: (b * steps + i, 0)
        l_map = lambda b, r, i: (b * steps + i, 0)
    else:
        rpt = TILE // dil
        tpb = CHUNK // rpt
        src = P.reshape(T // TILE, dil, rpt, IN_WIDTH)
        lead_q, lead_p = (qb * tpb, None, rpt), (tpb, None, rpt)
        o_shape = (Tp // TILE, dil, rpt, COL)
        l_shape = (Tp // TILE, dil, rpt, LANE)
        q_map = lambda cb: (lambda b, r, i: (b * steps + i, r, 0, cb + g))
        p_map = lambda cb: (lambda b, r, i: (prev_blk(b, i), r, 0, cb + g))
        o_map = lambda b, r, i: (b * steps + i, r, 0, 0)
        l_map = lambda b, r, i: (b * steps + i, r, 0, 0)

    q_spec = lambda cb: pl.BlockSpec(lead_q + (COL,), q_map(cb))
    p_spec = lambda cb: pl.BlockSpec(lead_p + (COL,), p_map(cb))
    n_pairs = qb * SWA_HEADS
    vmem = ((10 * qb + 6) * _nbytes((CHUNK, COL), BF16) + 2 * qb * _nbytes((CHUNK, COL), F32)
            + 6 * n_pairs * _nbytes((CHUNK, 2 * CHUNK), F32))
    o, l = pl.pallas_call(
        functools.partial(_swa_kernel, qb=qb),
        grid=(n_p, dil, steps),
        in_specs=[q_spec(CB_Q), p_spec(CB_K), q_spec(CB_K), p_spec(CB_V), q_spec(CB_V)],
        out_specs=[pl.BlockSpec(lead_q + (COL,), o_map),
                   pl.BlockSpec(lead_q + (LANE,), l_map)],
        out_shape=[jax.ShapeDtypeStruct(o_shape, BF16), jax.ShapeDtypeStruct(l_shape, F32)],
        scratch_shapes=[pltpu.VMEM(((qb + 1) * CHUNK, COL), BF16), pltpu.VMEM(((qb + 1) * CHUNK, COL), BF16),
                        pltpu.VMEM((n_pairs, CHUNK, 2 * CHUNK), F32),
                        pltpu.VMEM((n_pairs, CHUNK, 2 * CHUNK), BF16)],
        compiler_params=_params(vmem, ("parallel", "parallel", "arbitrary")),
        name=f"swa_prompt_g{g}",
    )(src, src, src, src, src)
    return o.reshape(Tp, COL), l.reshape(Tp, LANE)


def _swa_sample_kernel(q_ref, kn_ref, vn_ref, cache_ref, o_ref, l_ref, *, dil, lbuf, t_s):
    nq = SWA_HEADS * t_s
    q = q_ref[...]
    qrep = jnp.concatenate([q] * SWA_HEADS, axis=0)
    rq = lax.broadcasted_iota(jnp.int32, (nq, COL), 0)
    cq = lax.broadcasted_iota(jnp.int32, (nq, COL), 1)
    qbd = jnp.where(_idiv(rq, t_s) == _idiv(cq, LANE), qrep, 0.0).astype(BF16)

    per_pos = 2 * SWA_HEADS
    if cache_ref.ndim == 2:
        n_keys = lbuf
        rows_of = lambda first: cache_ref[pl.ds(first, lbuf, stride=per_pos), :]
        key_pos = lambda r: r
    else:
        n_keys = cache_ref.shape[0] * t_s
        rows_of = lambda first: cache_ref[:, pl.ds(first, t_s, stride=per_pos), :].reshape(n_keys, LANE)
        key_pos = lambda r: dil * _idiv(r, t_s) + _imod(r, t_s)
    kc = jnp.concatenate([rows_of(h).astype(BF16) for h in range(SWA_HEADS)], axis=1)
    vc = jnp.concatenate([rows_of(SWA_HEADS + h).astype(BF16) for h in range(SWA_HEADS)], axis=1)
    kn = kn_ref[...].astype(BF16)
    vn = vn_ref[...].astype(BF16)

    s_c = _dot_nt(qbd, kc)
    s_n = _dot_nt(qbd, kn)
    t_c = _imod(lax.broadcasted_iota(jnp.int32, (nq, n_keys), 0), t_s)
    c_c = key_pos(lax.broadcasted_iota(jnp.int32, (nq, n_keys), 1))
    diff_c = lbuf + t_c - c_c
    ok_c = (_imod(diff_c, dil) == 0) & (diff_c <= lbuf)
    t_n = _imod(lax.broadcasted_iota(jnp.int32, (nq, t_s), 0), t_s)
    c_n = lax.broadcasted_iota(jnp.int32, (nq, t_s), 1)
    diff_n = t_n - c_n
    ok_n = (diff_n >= 0) & (_imod(diff_n, dil) == 0)
    s_c = jnp.where(ok_c, s_c, NEG_INF)
    s_n = jnp.where(ok_n, s_n, NEG_INF)
    m = jnp.maximum(jnp.max(s_c, axis=-1, keepdims=True), jnp.max(s_n, axis=-1, keepdims=True))
    p_c = jnp.exp(s_c - m)
    p_n = jnp.exp(s_n - m)
    den = jnp.sum(p_c, axis=-1, keepdims=True) + jnp.sum(p_n, axis=-1, keepdims=True)
    o_all = (_dot(p_c.astype(BF16), vc) + _dot(p_n.astype(BF16), vn)) / den
    lse = m + jnp.log(den)
    for h in range(SWA_HEADS):
        cs = slice(h * LANE, (h + 1) * LANE)
        o_ref[:, cs] = o_all[h * t_s:(h + 1) * t_s, cs]
    l_ref[...] = _pack_heads([lse[h * t_s:(h + 1) * t_s] for h in range(SWA_HEADS)])


def _swa_sample(Ps, cache, layer, g, dil):
    n_s, t_s, _ = Ps.shape
    rows = cache.shape[2]
    lbuf = rows // (2 * SWA_HEADS)
    assert lbuf == dil * CHUNK, "window buffer must hold exactly one full window"
    blk = lambda cb: pl.BlockSpec((None, t_s, COL), lambda b: (b, 0, cb + g))
    if dil > t_s:
        per_pos = rows // lbuf
        cache = cache.reshape(cache.shape[0], n_s, lbuf // dil, dil * per_pos, LANE)
        cache_spec = pl.BlockSpec((None, None, lbuf // dil, t_s * per_pos, LANE), lambda b: (layer, b, 0, 0, 0))
    else:
        cache_spec = pl.BlockSpec((None, None, rows, LANE), lambda b: (layer, b, 0, 0))
    vmem = (2 * _nbytes((lbuf, 2 * COL), F32) + 2 * _nbytes((lbuf, 2 * COL), BF16)
            + 8 * _nbytes((SWA_HEADS * t_s, lbuf), F32))
    return pl.pallas_call(
        functools.partial(_swa_sample_kernel, dil=dil, lbuf=lbuf, t_s=t_s),
        grid=(n_s,),
        in_specs=[blk(CB_Q), blk(CB_K), blk(CB_V), cache_spec],
        out_specs=[pl.BlockSpec((None, t_s, COL), lambda b: (b, 0, 0)),
                   pl.BlockSpec((None, t_s, LANE), lambda b: (b, 0, 0))],
        out_shape=[jax.ShapeDtypeStruct((n_s, t_s, COL), F32),
                   jax.ShapeDtypeStruct((n_s, t_s, LANE), F32)],
        compiler_params=_params(vmem, ("parallel",)),
        name=f"swa_sample_g{g}",
    )(Ps, Ps, Ps, cache)


def _ret_head_inputs(qk_ref, v_ref, kout_ref, h):
    pair, half = h // 2, h % 2
    lane = lax.broadcasted_iota(jnp.int32, (CHUNK, LANE), 1)
    head_lanes = _idiv(lane, RET_QK_DIM) == half
    qt = qk_ref[:, pair * LANE:(pair + 1) * LANE]
    kt = qk_ref[:, COL // 2 + pair * LANE:COL // 2 + (pair + 1) * LANE]
    qm = jnp.where(head_lanes, qt, jnp.zeros_like(qt))
    kw = jnp.where(head_lanes, kt.astype(F32) * kout_ref[:, pair * LANE:(pair + 1) * LANE], 0.0)
    vh = v_ref[:, h * LANE:(h + 1) * LANE]
    return qm, kt, kw, vh


def _ret_finish(o, gate_ref, o_ref, h):
    cs = slice(h * LANE, (h + 1) * LANE)
    o_ref[:, cs] = (gate_ref[:, cs].astype(F32) * _rms(o)).astype(o_ref.dtype)


def _ret_prompt_kernel(*refs, n_seq):
    ins = refs[:3 * n_seq]
    decay_ref, qin_ref, kout_ref, gc_ref, o_ref, s_out_ref, s_ref, att_ref = refs[3 * n_seq:]
    i = pl.program_id(0)

    @pl.when(i == 0)
    def _():
        s_ref[...] = jnp.zeros_like(s_ref)

    pairs = [(b, h) for b in range(n_seq) for h in range(RET_HEADS)]
    head_in = lambda b, h: _ret_head_inputs(ins[3 * b], ins[3 * b + 1], kout_ref, h)
    for b, h in pairs:
        qm, kt, _, _ = head_in(b, h)
        att_ref[b, h] = (_dot_nt(qm, kt) * decay_ref[h]).astype(BF16)
    for b, h in pairs:
        qm, _, _, vh = head_in(b, h)
        o = (_dot(att_ref[b, h], vh)
             + _dot(qm, s_ref[b, h].astype(BF16)) * qin_ref[:, h * LANE:(h + 1) * LANE])
        _ret_finish(o, ins[3 * b + 2], o_ref.at[b], h)
    for b, h in pairs:
        _, _, kw, vh = head_in(b, h)
        s_ref[b, h] = s_ref[b, h] * gc_ref[h:h + 1, :] + _dot(kw.T.astype(BF16), vh)

    @pl.when(i == pl.num_programs(0) - 1)
    def _():
        for b in range(n_seq):
            for h in range(RET_HEADS):
                lo = (h % 2) * RET_QK_DIM
                s_out_ref[b, h] = s_ref[b, h, lo:lo + RET_QK_DIM, :]


def _ret_prompt(P, rt, n_p, s):
    nblk = s // CHUNK
    decay, qin, kout, gc = rt
    blk = lambda b, cb: pl.BlockSpec((CHUNK, COL), lambda i: (b * nblk + i, cb))
    const2 = lambda i: (0, 0)
    seq_specs = [blk(b, cb) for b in range(n_p) for cb in (CB_RQK, CB_RV, CB_RG)]
    st_shape = (n_p, RET_HEADS, RET_QK_DIM, RET_V_DIM)
    vmem = (8 * n_p * _nbytes((CHUNK, COL), BF16) + 2 * _nbytes(decay.shape, F32) + 4 * _nbytes(qin.shape, F32)
            + 3 * n_p * _nbytes((RET_HEADS, LANE, LANE), F32))
    o, st = pl.pallas_call(
        functools.partial(_ret_prompt_kernel, n_seq=n_p),
        grid=(nblk,),
        in_specs=seq_specs + [pl.BlockSpec(decay.shape, lambda i: (0, 0, 0)),
                              pl.BlockSpec(qin.shape, const2), pl.BlockSpec(kout.shape, const2),
                              pl.BlockSpec(gc.shape, const2)],
        out_specs=[pl.BlockSpec((n_p, CHUNK, COL), lambda i: (0, i, 0)),
                   pl.BlockSpec(st_shape, lambda i: (0, 0, 0, 0))],
        out_shape=[jax.ShapeDtypeStruct((n_p, s, COL), BF16), jax.ShapeDtypeStruct(st_shape, F32)],
        scratch_shapes=[pltpu.VMEM((n_p, RET_HEADS, LANE, LANE), F32),
                        pltpu.VMEM((n_p, RET_HEADS, CHUNK, CHUNK), BF16)],
        compiler_params=_params(vmem, ("arbitrary",)),
        name="ret_prompt",
    )(*([P] * (3 * n_p)), decay, qin, kout, gc)
    return o.reshape(n_p * s, COL), st


def _ret_sample_kernel(qk_ref, v_ref, gate_ref, s0_ref, decay_ref, qin_ref, kout_ref, gc_ref,
                       o_ref, s_out_ref, *, t_s):
    row = lax.broadcasted_iota(jnp.int32, (CHUNK, LANE), 0)
    for h in range(RET_HEADS):
        lo = (h % 2) * RET_QK_DIM
        qm, kt, kw, vh = _ret_head_inputs(qk_ref, v_ref, kout_ref, h)
        att = _dot_nt(qm, kt) * decay_ref[h]
        o = _dot(att.astype(BF16), vh)
        inter = jnp.zeros((CHUNK, LANE), F32)
        for sq in range(CHUNK // t_s):
            seq_rows = _idiv(row, t_s) == sq
            st = s0_ref[sq, h]
            st2 = jnp.concatenate([st, st], axis=0).astype(BF16)
            inter = jnp.where(seq_rows, _dot(qm, st2), inter)
            upd = _dot(jnp.where(seq_rows, kw, 0.0).T.astype(BF16), vh)
            s_out_ref[sq, h] = st * gc_ref[h:h + 1, :] + upd[lo:lo + RET_QK_DIM, :]
        o = o + inter * qin_ref[:, h * LANE:(h + 1) * LANE]
        _ret_finish(o, gate_ref, o_ref, h)


def _ret_sample(P, state, layer, rt, Tp, t_s):
    n_s = state.shape[1]
    spb = CHUNK // t_s
    base = Tp // CHUNK
    decay, qin, kout, gc = rt
    blk = lambda cb: pl.BlockSpec((CHUNK, COL), lambda i: (base + i, cb))
    const2 = lambda i: (0, 0)
    st_shape = (spb, RET_HEADS, RET_QK_DIM, RET_V_DIM)
    vmem = (8 * _nbytes((CHUNK, COL), BF16) + 2 * _nbytes(decay.shape, F32) + 4 * _nbytes(qin.shape, F32)
            + 4 * _nbytes(st_shape, F32))
    return pl.pallas_call(
        functools.partial(_ret_sample_kernel, t_s=t_s),
        grid=(n_s // spb,),
        in_specs=[blk(CB_RQK), blk(CB_RV), blk(CB_RG),
                  pl.BlockSpec((None,) + st_shape, lambda i: (layer, i, 0, 0, 0)),
                  pl.BlockSpec(decay.shape, lambda i: (0, 0, 0)),
                  pl.BlockSpec(qin.shape, const2), pl.BlockSpec(kout.shape, const2),
                  pl.BlockSpec(gc.shape, const2)],
        out_specs=[pl.BlockSpec((CHUNK, COL), lambda i: (i, 0)),
                   pl.BlockSpec(st_shape, lambda i: (i, 0, 0, 0))],
        out_shape=[jax.ShapeDtypeStruct((n_s * t_s, COL), BF16),
                   jax.ShapeDtypeStruct((n_s, RET_HEADS, RET_QK_DIM, RET_V_DIM), F32)],
        compiler_params=_params(vmem, ("parallel",)),
        name="ret_sample",
    )(P, P, P, state, decay, qin, kout, gc)


def _branch_kernel(xa_ref, xb_ref, a_ref, g0_ref, g1_ref, g2_ref, wb_ref, wo_ref, ex_ref,
                   po0, po1, po2, pl0, pl1, pl2, pc, so0, so1, so2, sl0, sl1, sl2, sc,
                   y_ref, on_ref, ln_ref, *, n_prompt_tiles):
    i = pl.program_id(0)

    def spread(packed):
        hi = packed.astype(BF16)
        lo = (packed - hi.astype(F32)).astype(BF16)
        return _dot(jnp.concatenate([hi, lo], axis=1), ex_ref[...])

    def natural(o_ref, l_ref, k, dil):
        if dil == 1:
            return (lambda h: o_ref[:, h * LANE:(h + 1) * LANE].astype(F32)), l_ref[...]
        rpt = TILE // dil
        for r in range(dil):
            rows = slice(r * rpt, (r + 1) * rpt)
            ln_ref[k, pl.ds(r, rpt, stride=dil), :] = l_ref[rows, :]
            for h in range(SWA_HEADS):
                on_ref[k, h, pl.ds(r, rpt, stride=dil), :] = o_ref[rows, h * LANE:(h + 1) * LANE].astype(F32)
        return (lambda h: on_ref[k, h]), ln_ref[k]

    def body(o_refs, l_refs, c_ref, permuted):
        groups = [natural(o_refs[k], l_refs[k], k, SWA_PATTERNS[k][1] if permuted else 1)
                  for k in range(len(SWA_PATTERNS))]
        ls = [l for _, l in groups]
        lmax = functools.reduce(jnp.maximum, ls)
        es = [jnp.exp(l - lmax) for l in ls]
        inv = 1.0 / sum(es)
        weights = [spread(e * inv) for e in es]
        heads = []
        for h in range(SWA_HEADS):
            cs = slice(h * LANE, (h + 1) * LANE)
            heads.append(sum(w[:, cs] * go(h) for w, (go, _) in zip(weights, groups)).astype(BF16))
        mix = jnp.concatenate(heads, axis=1)
        merged = (g0_ref[...].astype(F32) * _dot(a_ref[...], wb_ref[0])
                  + g1_ref[...].astype(F32) * _dot(mix, wb_ref[1])
                  + g2_ref[...].astype(F32) * _dot(c_ref[...], wb_ref[2]))
        y_ref[...] = _tile_rows(i, n_prompt_tiles, xa_ref, xb_ref) + _dot(merged.astype(BF16), wo_ref[...])

    @pl.when(i < n_prompt_tiles)
    def _():
        body((po0, po1, po2), (pl0, pl1, pl2), pc, True)

    @pl.when(i >= n_prompt_tiles)
    def _():
        body((so0, so1, so2), (sl0, sl1, sl2), sc, False)


def _branch(xa, xb, out_a, P, wb, wo, prompt_set, sample_set, n_prompt_tiles):
    T = (n_prompt_tiles + 1) * TILE
    assert sample_set[0].shape[0] == TILE, "the sample rows must form exactly one tile"
    n_g = len(SWA_PATTERNS)
    last = n_prompt_tiles - 1
    row = lambda i: (i, 0)
    gate = lambda k: pl.BlockSpec((TILE, D_MODEL), lambda i: (i, CB_GATE * COL // D_MODEL + k))
    once = pl.Buffered(1)
    p_o = pl.BlockSpec((TILE, COL), lambda i: (jnp.minimum(i, last), 0))
    p_l = pl.BlockSpec((TILE, LANE), lambda i: (jnp.minimum(i, last), 0))
    s_o = pl.BlockSpec((TILE, COL), lambda i: (0, 0))
    s_l = pl.BlockSpec((TILE, LANE), lambda i: (0, 0))
    src = jnp.arange(2 * LANE) % LANE
    dst_head = jnp.arange(COL) // LANE
    expand = (src[:, None] == dst_head[None, :] * LSE_LANES).astype(BF16)
    vmem = (4 * _nbytes((TILE, D_MODEL), F32) + 6 * _nbytes((TILE, D_MODEL), BF16)
            + 2 * (2 * n_g + 2) * (_nbytes((TILE, COL), BF16) + _nbytes((TILE, COL), F32))
            + _nbytes(wb.shape, BF16) + _nbytes(wo.shape, BF16)
            + 4 * n_g * _nbytes((TILE, COL), F32) + 4 * _nbytes((TILE, D_MODEL), F32))
    return pl.pallas_call(
        functools.partial(_branch_kernel, n_prompt_tiles=n_prompt_tiles),
        grid=(T // TILE,),
        in_specs=_x_specs(xa, xb) + [
                  pl.BlockSpec((TILE, COL), row),
                  gate(0), gate(1), gate(2),
                  pl.BlockSpec(wb.shape, lambda i: (0, 0, 0), pipeline_mode=once),
                  pl.BlockSpec(wo.shape, lambda i: (0, 0), pipeline_mode=once),
                  pl.BlockSpec(expand.shape, lambda i: (0, 0)),
                  p_o, p_o, p_o, p_l, p_l, p_l, p_o,
                  s_o, s_o, s_o, s_l, s_l, s_l, s_o],
        out_specs=pl.BlockSpec((TILE, D_MODEL), row),
        out_shape=jax.ShapeDtypeStruct((T, D_MODEL), F32),
        scratch_shapes=[pltpu.VMEM((n_g, SWA_HEADS, TILE, LANE), F32),
                        pltpu.VMEM((n_g, TILE, LANE), F32)],
        compiler_params=_params(vmem, ("parallel",)),
        name="branch_merge",
    )(xa, xb, out_a, P, P, P, wb, wo, expand, *prompt_set, *sample_set)


def _swiglu_into(hb, w1_ref, w3_ref, w2_ref, acc_ref, fc):
    n_fc = w1_ref.shape[-1] // fc
    for f in range(n_fc):
        cols = slice(f * fc, (f + 1) * fc)
        a = _dot(hb, w1_ref[:, cols])
        b = _dot(hb, w3_ref[:, cols])
        part = _dot((a * _sigmoid(a) * b).astype(BF16), w2_ref[cols, :])
        if f == 0:
            acc_ref[...] = part
        else:
            acc_ref[...] += part


def _dense_ffn_kernel(x_ref, g2_ref, w1_ref, w3_ref, w2_ref, y_ref, acc_ref, *, fc):
    x = x_ref[...]
    hb = (_rms(x) * g2_ref[...]).astype(BF16)
    _swiglu_into(hb, w1_ref, w3_ref, w2_ref, acc_ref, fc)
    y_ref[...] = x + acc_ref[...]


def _dense_ffn(x, g2, w1b, w3b, w2b):
    T = x.shape[0]
    F = w1b.shape[1]
    tm = _pick(T, (768, 512, 384, 256, 128))
    fc = _pick(F, (512, 256, 128))
    row = lambda i: (i, 0)
    once = pl.Buffered(1)
    vmem = (5 * _nbytes((tm, D_MODEL), F32) + 3 * _nbytes(w1b.shape, BF16) + _nbytes((tm, D_MODEL), BF16)
            + 4 * _nbytes((tm, fc), F32))
    return pl.pallas_call(
        functools.partial(_dense_ffn_kernel, fc=fc),
        grid=(T // tm,),
        in_specs=[pl.BlockSpec((tm, D_MODEL), row),
                  pl.BlockSpec((1, D_MODEL), lambda i: (0, 0)),
                  pl.BlockSpec(w1b.shape, lambda i: (0, 0), pipeline_mode=once),
                  pl.BlockSpec(w3b.shape, lambda i: (0, 0), pipeline_mode=once),
                  pl.BlockSpec(w2b.shape, lambda i: (0, 0), pipeline_mode=once)],
        out_specs=pl.BlockSpec((tm, D_MODEL), row),
        out_shape=jax.ShapeDtypeStruct((T, D_MODEL), F32),
        scratch_shapes=[pltpu.VMEM((tm, D_MODEL), F32)],
        compiler_params=_params(vmem, ("parallel",)),
        name="dense_ffn",
    )(x, g2, w1b, w3b, w2b)


def _router_kernel(x_ref, g2_ref, wr_ref, e0_ref, e1_ref, w0_ref, w1_ref):
    hb = (_rms(x_ref[...]) * g2_ref[...]).astype(BF16)
    logits = _dot(hb, wr_ref[...])
    lane = lax.broadcasted_iota(jnp.int32, logits.shape, 1)
    logits = jnp.where(lane < N_EXPERTS, logits, -jnp.inf)
    lane_f = lane.astype(F32)
    m1 = jnp.max(logits, axis=-1, keepdims=True)
    i1 = jnp.min(jnp.where(logits == m1, lane_f, float(LANE)), axis=-1, keepdims=True)
    rest = jnp.where(lane_f == i1, -jnp.inf, logits)
    m2 = jnp.max(rest, axis=-1, keepdims=True)
    i2 = jnp.min(jnp.where(rest == m2, lane_f, float(LANE)), axis=-1, keepdims=True)
    e = jnp.exp(m2 - m1)
    e0_ref[...] = jnp.broadcast_to(i1, logits.shape).astype(jnp.int32)
    e1_ref[...] = jnp.broadcast_to(i2, logits.shape).astype(jnp.int32)
    w0_ref[...] = jnp.broadcast_to(1.0 / (1.0 + e), logits.shape)
    w1_ref[...] = jnp.broadcast_to(e / (1.0 + e), logits.shape)


def _router(x, g2, wr_pad):
    T = x.shape[0]
    tm = _pick(T, (768, 512, 384, 256, 128))
    row = lambda i: (i, 0)
    out = pl.BlockSpec((tm, LANE), row)
    vmem = 4 * _nbytes((tm, D_MODEL), F32) + 2 * _nbytes(wr_pad.shape, BF16) + 16 * _nbytes((tm, LANE), F32)
    return pl.pallas_call(
        _router_kernel,
        grid=(T // tm,),
        in_specs=[pl.BlockSpec((tm, D_MODEL), row),
                  pl.BlockSpec((1, D_MODEL), lambda i: (0, 0)),
                  pl.BlockSpec(wr_pad.shape, lambda i: (0, 0))],
        out_specs=[out, out, out, out],
        out_shape=[jax.ShapeDtypeStruct((T, LANE), jnp.int32), jax.ShapeDtypeStruct((T, LANE), jnp.int32),
                   jax.ShapeDtypeStruct((T, LANE), F32), jax.ShapeDtypeStruct((T, LANE), F32)],
        compiler_params=_params(vmem, ("parallel",)),
        name="moe_router",
    )(x, g2, wr_pad)


RUN_ALIGN = SUBLANE


def _run_pieces(count, max_rows):
    pieces, off = [], 0
    size = max_rows
    while size >= RUN_ALIGN:
        active = (count & size) != 0
        pieces.append((size, off, active))
        off = off + jnp.where(active, size, 0)
        size //= 2
    return pieces


def _packed_rows(i, e0, e1, lower_ref, loc_ref):
    lane = lax.broadcasted_iota(jnp.int32, e0.shape, 1)
    picks = jnp.where((lane == e0) | (lane == e1), 1.0, 0.0)
    rank = _dot(lower_ref[...], picks.astype(BF16))
    lane1 = lax.broadcasted_iota(jnp.int32, (1, LANE), 1)
    run_start = jnp.zeros((1, LANE), F32)
    for e in range(N_EXPERTS):
        run_start = jnp.where(lane1 == e, loc_ref[i * N_EXPERTS + e].astype(F32), run_start)
    row_of = rank + run_start
    return tuple(jnp.sum(jnp.where(lane == ek, row_of, 0.0), axis=-1, keepdims=True) for ek in (e0, e1))


RUN_ROWS = 640


def _dispatch_kernel(base_ref, cnt_ref, loc_ref, pend_ref, x_ref, g2_ref, e0_ref, e1_ref, lower_ref, xs_ref,
                     comp_ref, sem, *, tme, n_tiles):
    i = pl.program_id(0)
    n_steps = pl.num_programs(0)
    cur = i % 2
    half = tme // 2

    @pl.when(i == 0)
    def _():
        comp_ref[0, 0:half] = jnp.zeros((half, D_MODEL), comp_ref.dtype)

        def zero_tile(start):
            return [pltpu.make_async_copy(
                        comp_ref.at[0, pl.ds(0, half), :],
                        xs_ref.at[pl.ds(pl.multiple_of(start + k * half, half), half), :], sem.at[0])
                    for k in range(2)]

        def tail_loop(fn):
            def body(t, carry):
                for c in zero_tile(t * tme):
                    fn(c)
                return carry
            lax.fori_loop(pend_ref[N_EXPERTS - 1] // tme, n_tiles, body, 0)

        def group_tails(fn):
            for e in range(N_EXPERTS):
                prev_end = pend_ref[e - 1] if e else 0

                @pl.when(pend_ref[e] > prev_end)
                def _():
                    for c in zero_tile(pend_ref[e] - tme):
                        fn(c)

        group_tails(lambda c: c.start())
        tail_loop(lambda c: c.start())
        group_tails(lambda c: c.wait())
        tail_loop(lambda c: c.wait())

    h = (_rms(x_ref[...]) * g2_ref[...]).astype(BF16)
    pos0, pos1 = _packed_rows(i, e0_ref[...], e1_ref[...], lower_ref, loc_ref)
    slot = lax.broadcasted_iota(jnp.int32, (TILE, RUN_ROWS), 1).astype(F32)
    hit = jnp.where((slot == pos0) | (slot == pos1), 1.0, 0.0)
    comp_ref[cur] = _dot(hit.T.astype(BF16), h)

    def for_copies(tile, buf, fn):
        for e in range(N_EXPERTS):
            k = tile * N_EXPERTS + e
            cnt, dst, loc = cnt_ref[k], base_ref[k], loc_ref[k]
            for size, off, active in _run_pieces(cnt, TILE):
                c = pltpu.make_async_copy(
                    comp_ref.at[buf, pl.ds(pl.multiple_of(loc + off, RUN_ALIGN), size), :],
                    xs_ref.at[pl.ds(pl.multiple_of(dst + off, RUN_ALIGN), size), :], sem.at[buf])
                pl.when(active)(functools.partial(fn, c))

    def wait_copies(tile, buf):
        rows = pl.multiple_of(sum(cnt_ref[tile * N_EXPERTS + e] for e in range(N_EXPERTS)), RUN_ALIGN)
        pltpu.make_async_copy(comp_ref.at[buf, pl.ds(0, rows), :], xs_ref.at[pl.ds(0, rows), :],
                              sem.at[buf]).wait()

    for_copies(i, cur, lambda c: c.start())

    @pl.when(i > 0)
    def _():
        wait_copies(i - 1, 1 - cur)

    @pl.when(i == n_steps - 1)
    def _():
        wait_copies(i, cur)


def _dispatch(x, g2, e0, e1, base, cnt, loc, pend, lower, n_tiles, tme):
    T = x.shape[0]
    assert tme == 2 * TILE
    row = lambda i, *_: (i, 0)
    assert RUN_ROWS >= 2 * TILE + N_EXPERTS * (RUN_ALIGN - 1) and RUN_ROWS % LANE == 0
    vmem = (3 * _nbytes((TILE, D_MODEL), F32) + 2 * _nbytes((RUN_ROWS, D_MODEL), F32)
            + 8 * _nbytes((RUN_ROWS, TILE), F32))
    grid_spec = pltpu.PrefetchScalarGridSpec(
        num_scalar_prefetch=4,
        grid=(T // TILE,),
        in_specs=[pl.BlockSpec((TILE, D_MODEL), row),
                  pl.BlockSpec((1, D_MODEL), lambda i, *_: (0, 0)),
                  pl.BlockSpec((TILE, LANE), row), pl.BlockSpec((TILE, LANE), row),
                  pl.BlockSpec((TILE, TILE), lambda i, *_: (0, 0))],
        out_specs=pl.BlockSpec(memory_space=pl.ANY),
        scratch_shapes=[pltpu.VMEM((2, RUN_ROWS, D_MODEL), F32), pltpu.SemaphoreType.DMA((2,))],
    )
    return pl.pallas_call(
        functools.partial(_dispatch_kernel, tme=tme, n_tiles=n_tiles),
        grid_spec=grid_spec,
        out_shape=jax.ShapeDtypeStruct((n_tiles * tme, D_MODEL), F32),
        compiler_params=_params(vmem, ("arbitrary",)),
        name="moe_dispatch",
    )(base, cnt, loc, pend, x, g2, e0, e1, lower)


def _expert_ffn_kernel(te_ref, used_ref, xs_ref, w1_ref, w3_ref, w2_ref, y_ref, acc_ref, *, fc):
    i = pl.program_id(0)

    @pl.when(i < used_ref[0])
    def _():
        _swiglu_into(xs_ref[...].astype(BF16), w1_ref, w3_ref, w2_ref, acc_ref, fc)
        y_ref[...] = acc_ref[...]

    @pl.when(i >= used_ref[0])
    def _():
        y_ref[...] = jnp.zeros_like(y_ref)


def _expert_ffn(xs, tile_expert, n_used, we1b, we3b, we2b, tme):
    n_rows = xs.shape[0]
    F = we1b.shape[2]
    fc = _pick(F, (512, 256, 128))
    once = pl.Buffered(1)
    wspec = lambda shape: pl.BlockSpec((None,) + shape, lambda i, te, nu: (te[i], 0, 0), pipeline_mode=once)
    vmem = (5 * _nbytes((tme, D_MODEL), F32) + 3 * _nbytes(we1b.shape[1:], BF16) + _nbytes((tme, D_MODEL), BF16)
            + 4 * _nbytes((tme, fc), F32))
    grid_spec = pltpu.PrefetchScalarGridSpec(
        num_scalar_prefetch=2,
        grid=(n_rows // tme,),
        in_specs=[pl.BlockSpec((tme, D_MODEL), lambda i, te, nu: (jnp.minimum(i, jnp.maximum(nu[0], 1) - 1), 0)),
                  wspec(we1b.shape[1:]), wspec(we3b.shape[1:]), wspec(we2b.shape[1:])],
        out_specs=pl.BlockSpec((tme, D_MODEL), lambda i, te, nu: (i, 0)),
        scratch_shapes=[pltpu.VMEM((tme, D_MODEL), F32)],
    )
    return pl.pallas_call(
        functools.partial(_expert_ffn_kernel, fc=fc),
        grid_spec=grid_spec,
        out_shape=jax.ShapeDtypeStruct((n_rows, D_MODEL), F32),
        compiler_params=_params(vmem, ("arbitrary",)),
        name="moe_expert_ffn",
    )(tile_expert, n_used, xs, we1b, we3b, we2b)


def _combine_kernel(base_ref, cnt_ref, loc_ref, x_ref, e0_ref, e1_ref, w0_ref, w1_ref, lower_ref, y_hbm_ref,
                    o_ref, o_last_ref, buf_ref, sem):
    i = pl.program_id(0)
    n_steps = pl.num_programs(0)
    cur = i % 2

    def for_copies(tile, buf, fn):
        for e in range(N_EXPERTS):
            k = tile * N_EXPERTS + e
            cnt, src, loc = cnt_ref[k], base_ref[k], loc_ref[k]
            for size, off, active in _run_pieces(cnt, TILE):
                c = pltpu.make_async_copy(
                    y_hbm_ref.at[pl.ds(pl.multiple_of(src + off, RUN_ALIGN), size), :],
                    buf_ref.at[buf, pl.ds(pl.multiple_of(loc + off, RUN_ALIGN), size), :], sem.at[buf])
                pl.when(active)(functools.partial(fn, c))

    @pl.when(i == 0)
    def _():
        buf_ref[...] = jnp.zeros_like(buf_ref)
        for_copies(0, 0, lambda c: c.start())

    @pl.when(i + 1 < n_steps)
    def _():
        for_copies(i + 1, 1 - cur, lambda c: c.start())

    rows = pl.multiple_of(sum(cnt_ref[i * N_EXPERTS + e] for e in range(N_EXPERTS)), RUN_ALIGN)
    pltpu.make_async_copy(y_hbm_ref.at[pl.ds(0, rows), :], buf_ref.at[cur, pl.ds(0, rows), :],
                          sem.at[cur]).wait()

    positions = _packed_rows(i, e0_ref[...], e1_ref[...], lower_ref, loc_ref)
    slot = lax.broadcasted_iota(jnp.int32, (TILE, RUN_ROWS), 1).astype(F32)
    yb = buf_ref[cur].astype(BF16)
    wide = lambda a, n: jnp.concatenate([a] * n, axis=1)
    acc = x_ref[...]
    for pos, wk in zip(positions, (w0_ref, w1_ref)):
        rows = _dot(jnp.where(slot == pos, 1.0, 0.0).astype(BF16), yb)
        acc = acc + wide(wk[...], D_MODEL // LANE) * rows

    @pl.when(i < n_steps - 1)
    def _():
        o_ref[...] = acc

    @pl.when(i == n_steps - 1)
    def _():
        o_last_ref[...] = acc


def _combine(x, e0, e1, w0, w1, base, cnt, loc, lower, y_rows):
    T = x.shape[0]
    row = lambda i, *_: (i, 0)
    lane_spec = pl.BlockSpec((TILE, LANE), row)
    vmem = (5 * _nbytes((TILE, D_MODEL), F32) + 3 * _nbytes((RUN_ROWS, D_MODEL), F32)
            + 8 * _nbytes((TILE, LANE), F32) + 8 * _nbytes((TILE, RUN_ROWS), F32)
            + 4 * _nbytes((TILE, D_MODEL), F32))
    grid_spec = pltpu.PrefetchScalarGridSpec(
        num_scalar_prefetch=3,
        grid=(T // TILE,),
        in_specs=[pl.BlockSpec((TILE, D_MODEL), row), lane_spec, lane_spec, lane_spec, lane_spec,
                  pl.BlockSpec((TILE, TILE), lambda i, *_: (0, 0)),
                  pl.BlockSpec(memory_space=pl.ANY)],
        out_specs=[pl.BlockSpec((TILE, D_MODEL), lambda i, *_: (jnp.minimum(i, T // TILE - 2), 0)),
                   pl.BlockSpec((TILE, D_MODEL), lambda i, *_: (0, 0))],
        scratch_shapes=[pltpu.VMEM((2, RUN_ROWS, D_MODEL), F32), pltpu.SemaphoreType.DMA((2,))],
    )
    return pl.pallas_call(
        _combine_kernel,
        grid_spec=grid_spec,
        out_shape=[jax.ShapeDtypeStruct((T - TILE, D_MODEL), F32), jax.ShapeDtypeStruct((TILE, D_MODEL), F32)],
        compiler_params=_params(vmem, ("arbitrary",)),
        name="moe_combine",
    )(base, cnt, loc, x, e0, e1, w0, w1, lower, y_rows)


def _moe(x, g2, wr_pad, we1b, we3b, we2b):
    T = x.shape[0]
    tme = 2 * TILE
    n_tt = T // TILE
    e0, e1, w0, w1 = _router(x, g2, wr_pad)
    experts = jnp.arange(N_EXPERTS, dtype=jnp.int32)
    picks = (e0[:, :1] == experts[None, :]) | (e1[:, :1] == experts[None, :])
    cnt = jnp.sum(picks.reshape(n_tt, TILE, N_EXPERTS).astype(jnp.int32), axis=1)
    cnt = (cnt + RUN_ALIGN - 1) // RUN_ALIGN * RUN_ALIGN
    counts = jnp.sum(cnt, axis=0)
    padded = (counts + tme - 1) // tme * tme
    pend = jnp.cumsum(padded).astype(jnp.int32)
    base = ((pend - padded)[None, :] + jnp.cumsum(cnt, axis=0) - cnt).astype(jnp.int32)
    loc = (jnp.cumsum(cnt, axis=1) - cnt).astype(jnp.int32)
    max_rows = T * TOP_K + n_tt * N_EXPERTS * (RUN_ALIGN - 1)
    n_tiles = -(-max_rows // tme) + N_EXPERTS
    tile_start = jnp.arange(n_tiles, dtype=jnp.int32) * tme
    tile_expert = jnp.minimum(jnp.sum(pend[None, :] <= tile_start[:, None], axis=1), N_EXPERTS - 1).astype(jnp.int32)
    n_used = (pend[N_EXPERTS - 1:] // tme).astype(jnp.int32)
    tok = jnp.arange(TILE)
    lower = (tok[None, :] < tok[:, None]).astype(BF16)
    base_f, cnt_f, loc_f = base.reshape(-1), cnt.reshape(-1).astype(jnp.int32), loc.reshape(-1)
    xs = _dispatch(x, g2, e0, e1, base_f, cnt_f, loc_f, pend, lower, n_tiles, tme)
    y_rows = _expert_ffn(xs, tile_expert, n_used, we1b, we3b, we2b, tme)
    return _combine(x, e0, e1, w0, w1, base_f, cnt_f, loc_f, lower, y_rows)


def _class_major(a, dil):
    n = a.shape[0]
    return a.reshape((n // TILE, TILE // dil, dil) + a.shape[1:]).swapaxes(1, 2).reshape(a.shape)


def _natural(a, dil):
    n = a.shape[0]
    return a.reshape((n // TILE, dil, TILE // dil) + a.shape[1:]).swapaxes(1, 2).reshape(a.shape)


def _perm_matrices():
    eye = jnp.eye(TILE, dtype=BF16)
    return jnp.stack([_class_major(eye, dil) for _, dil in SWA_PATTERNS[1:]])


def _rope_tables(pos, Tp):
    freqs = ROPE_THETA ** (-jnp.arange(0, SWA_HEAD_DIM, 2, dtype=F32) / SWA_HEAD_DIM)
    rfreqs = 1.0 / (ROPE_THETA ** jnp.linspace(0.0, 1.0, RET_QK_DIM // 2, dtype=F32))

    def swa(p):
        ang = p.astype(F32)[:, None] * freqs[None, :]
        c, s = jnp.cos(ang), jnp.sin(ang)
        return jnp.concatenate([c, c], axis=1), jnp.concatenate([-s, s], axis=1)

    per_group = [swa(jnp.concatenate([_class_major(pos[:Tp], dil), pos[Tp:]])) for _, dil in SWA_PATTERNS]
    cosb = jnp.stack([c for c, _ in per_group])
    sinb = jnp.stack([s for _, s in per_group])
    rang = pos.astype(F32)[:, None] * rfreqs[None, :]
    c, s = jnp.cos(rang), jnp.sin(rang)
    cosc = jnp.concatenate([c, c, c, c], axis=1)
    sinc = jnp.concatenate([-s, s, -s, s], axis=1)
    return cosb, sinb, cosc, sinc


def _ret_tables(c_len):
    log_g = jnp.log1p(-jnp.exp2(-5.0 - jnp.arange(RET_HEADS, dtype=F32)))
    r = jnp.arange(CHUNK)
    i = (r % c_len).astype(F32)
    same = (r[:, None] // c_len) == (r[None, :] // c_len)
    dist = i[:, None] - i[None, :]
    decay = jnp.where(same[None] & (dist >= 0)[None],
                      jnp.exp(log_g[:, None, None] * jnp.maximum(dist, 0.0)[None]), 0.0)
    qin = jnp.repeat(jnp.exp(log_g[None, :] * (i[:, None] + 1.0)), RET_V_DIM, axis=1)
    kout = jnp.repeat(jnp.exp(log_g[None, :] * (c_len - 1.0 - i)[:, None]), RET_QK_DIM, axis=1)
    gc = jnp.broadcast_to(jnp.exp(log_g * c_len)[:, None], (RET_HEADS, LANE))
    gc = jnp.concatenate([gc, jnp.zeros((SUBLANE - RET_HEADS, LANE), F32)], axis=0)
    return decay.astype(F32), qin.astype(F32), kout.astype(F32), gc.astype(F32)


def _mixa_tables(w_s, b_s, t_s):
    w_p = jnp.tril(w_s)
    w8 = jnp.tril(w_s[:, :t_s, :t_s])
    eye = jnp.eye(CHUNK // t_s, dtype=w_s.dtype)
    w_smp = jax.vmap(lambda m: jnp.kron(eye, m))(w8)
    w2 = jnp.stack([w_p, w_smp]).astype(BF16)
    b_p = jnp.repeat(b_s.T, LANE, axis=1)
    b_smp = jnp.repeat(jnp.tile(b_s[:, :t_s].T, (CHUNK // t_s, 1)), LANE, axis=1)
    return w2, jnp.stack([b_p, b_smp]).astype(F32)


def kernel(x_prompt, x_sample, cache_swa_kv0, cache_swa_kv1, cache_swa_kv2, state_ret, norm1_g, w_in, norm_v_g, w_s, b_s, q_norm_g, k_norm_g, w_branch, w_out, norm2_g, w1, w3, w2, w_router, we1, we3, we2):
    n_p, s, d = x_prompt.shape
    n_s, t_s, _ = x_sample.shape
    depth = w_in.shape[0]
    Tp, Ts = n_p * s, n_s * t_s
    T = Tp + Ts
    max_win, max_dil = SWA_PATTERNS[-1]
    assert d == D_MODEL and Ts == TILE and CHUNK % t_s == 0
    assert s % (CHUNK * max_dil) == 0 and s >= max_win
    n_pt = Tp // TILE

    xa, xb = x_prompt.reshape(Tp, d), x_sample.reshape(Ts, d)
    pos = jnp.concatenate([jnp.arange(s, dtype=jnp.int32),
                           jnp.tile(PAST_LEN + jnp.arange(t_s, dtype=jnp.int32), n_s)])
    tabs = _rope_tables(pos, s)
    perm = _perm_matrices()
    rt_prompt = _ret_tables(CHUNK)
    rt_sample = _ret_tables(t_s)
    caches = tuple(c.reshape(c.shape[0], c.shape[1], -1, SWA_HEAD_DIM)
                   for c in (cache_swa_kv0, cache_swa_kv1, cache_swa_kv2))

    p_kv = [[] for _ in SWA_PATTERNS]
    s_kv = [[] for _ in SWA_PATTERNS]
    p_ret, s_ret, s_v = [], [], []
    for layer in range(depth):
        if layer == 0:
            later = [(w_in, 1)] + [(w, 0) for w in (w_branch, w_out, w1, w3, w2, we1, we3, we2)]
            P, (w_in_later, w_branch_b, w_out_b, w1_b, w3_b, w2_b, we1_b, we3_b, we2_b) = _inproj(
                xa, xb, norm1_g[0][None, :], w_in[0].astype(BF16), perm, tabs, norm_v_g[0][None, :],
                q_norm_g[0][None, :], k_norm_g[0][None, :], n_pt, s // TILE, later)
        else:
            P, _ = _inproj(xa, xb, norm1_g[layer][None, :], w_in_later[layer - 1], perm, tabs,
                           norm_v_g[layer][None, :], q_norm_g[layer][None, :], k_norm_g[layer][None, :],
                           n_pt, s // TILE)

        w2a, b2a = _mixa_tables(w_s[layer], b_s[layer], t_s)
        out_a = _mixer_a(P, w2a, b2a, Tp // CHUNK)

        Ps = P[Tp:].astype(F32).reshape(n_s, t_s, IN_WIDTH)
        po, plse, so, slse = [], [], [], []
        for g, (win, dil) in enumerate(SWA_PATTERNS):
            o_g, l_g = _swa_prompt(P, g, dil, n_p, s)
            os_g, ls_g = _swa_sample(Ps, caches[g], layer, g, dil)
            po.append(o_g)
            plse.append(l_g)
            so.append(os_g.reshape(Ts, COL).astype(BF16))
            slse.append(ls_g.reshape(Ts, LANE))

        out_c, ret_p = _ret_prompt(P, rt_prompt, n_p, s)
        out_cs, ret_s = _ret_sample(P, state_ret, layer, rt_sample, Tp, t_s)

        x = _branch(xa, xb, out_a, P, w_branch_b[layer], w_out_b[layer],
                    (*po, *plse, out_c), (*so, *slse, out_cs), n_pt)

        g2 = norm2_g[layer][None, :]
        i = layer // 2
        if layer % 2 == 0:
            x = _dense_ffn(x, g2, w1_b[i], w3_b[i], w2_b[i])
            xa = xb = x
        else:
            wr_pad = jnp.zeros((D_MODEL, LANE), BF16).at[:, :N_EXPERTS].set(w_router[i].astype(BF16))
            xa, xb = _moe(x, g2, wr_pad, we1_b[i], we3_b[i], we2_b[i])

        for g, (win, dil) in enumerate(SWA_PATTERNS):
            kcols = slice((CB_K + g) * COL, (CB_K + g + 1) * COL)
            vcols = slice((CB_V + g) * COL, (CB_V + g + 1) * COL)
            keep = -(-min(win, s) // TILE) * TILE

            def rows(cols):
                blk = jnp.concatenate([P[(b + 1) * s - keep:(b + 1) * s, cols] for b in range(n_p)])
                blk = _natural(blk, dil).reshape(n_p, keep, SWA_HEADS, SWA_HEAD_DIM)
                return blk[:, keep - min(win, s):]

            p_kv[g].append(jnp.stack([rows(kcols), rows(vcols)], axis=2).astype(F32))
            ks = P[Tp:, kcols].reshape(n_s, t_s, SWA_HEADS, SWA_HEAD_DIM)
            vs = P[Tp:, vcols].reshape(n_s, t_s, SWA_HEADS, SWA_HEAD_DIM)
            s_kv[g].append(jnp.stack([ks, vs], axis=2).astype(F32))
        p_ret.append(ret_p)
        s_ret.append(ret_s)
        s_v.append(P[Tp:, CB_AV * COL:(CB_AV + 1) * COL].astype(F32).reshape(n_s, t_s, COL))

    y_prompt = xa[:Tp].reshape(n_p, s, d)
    y_sample = xb[xb.shape[0] - Ts:].reshape(n_s, t_s, d)
    return (y_prompt, y_sample,
            jnp.stack(p_kv[0]), jnp.stack(p_kv[1]), jnp.stack(p_kv[2]), jnp.stack(p_ret),
            jnp.stack(s_kv[0]), jnp.stack(s_kv[1]), jnp.stack(s_kv[2]), jnp.stack(s_ret),
            jnp.stack(s_v))
```

```python
import functools
import math

import jax
import jax.numpy as jnp
from jax import lax
from jax.experimental import pallas as pl
from jax.experimental.pallas import tpu as pltpu

F32 = jnp.float32
BF16 = jnp.bfloat16

PAST_LEN = 16384
EPS = 1e-6
NEG_INF = -1e30
ROPE_THETA = 10000.0

D_MODEL = 1024
LANE = 128
SUBLANE = 8
BF16_ROWS = 2 * SUBLANE
CHUNK = 128
TILE = 256
COL = 512
A_GROUPS = 4
SWA_PATTERNS = ((128, 1), (512, 4), (2048, 16))
SWA_HEADS = 4
SWA_HEAD_DIM = 128
RET_HEADS = 4
RET_QK_DIM = 64
RET_V_DIM = 128
N_EXPERTS = 8
TOP_K = 2
IN_WIDTH = 10240
N_COL = IN_WIDTH // COL

CB_AU, CB_AV, CB_Q, CB_K, CB_V, CB_RQK, CB_RV, CB_RG, CB_GATE = 0, 1, 2, 5, 8, 11, 12, 13, 14

VMEM_INTERNAL_SCRATCH = 8 * 1024 * 1024
VMEM_BYTES = 64 * 1024 * 1024


def _pick(n, candidates):
    for c in candidates:
        if n % c == 0:
            return c
    raise ValueError(f"no tile in {candidates} divides {n}")


def _params(block_bytes, semantics=None):
    limit = min(int(block_bytes) + VMEM_INTERNAL_SCRATCH, VMEM_BYTES)
    return pltpu.CompilerParams(dimension_semantics=semantics, vmem_limit_bytes=limit)


def _nbytes(shape, dtype):
    return math.prod(shape) * jnp.dtype(dtype).itemsize


def _rms(x):
    return x * lax.rsqrt(jnp.mean(x * x, axis=-1, keepdims=True) + EPS)


def _gelu(x):
    return 0.5 * x * (1.0 + lax.erf(x * (0.5 ** 0.5)))


def _sigmoid(x):
    return 1.0 / (1.0 + jnp.exp(-x))


def _idiv(x, n):
    assert n & (n - 1) == 0
    return x >> (n.bit_length() - 1)


def _imod(x, n):
    assert n & (n - 1) == 0
    return x & (n - 1)


def _dot(a, b):
    return jnp.dot(a, b, preferred_element_type=F32)


LSE_LANES = LANE // SWA_HEADS


def _pack_heads(cols):
    rows = cols[0].shape[0]
    grp = _idiv(lax.broadcasted_iota(jnp.int32, (rows, LANE), 1), LSE_LANES)
    out = jnp.broadcast_to(cols[-1], (rows, LANE))
    for h in range(len(cols) - 2, -1, -1):
        out = jnp.where(grp == h, cols[h], out)
    return out


def _dot_nt(a, b):
    return lax.dot_general(a, b, (((1,), (1,)), ((), ())), preferred_element_type=F32)


def _tile_rows(i, n_prompt_tiles, xa_ref, xb_ref):
    return jnp.where(i < n_prompt_tiles, xa_ref[...], xb_ref[...])


def _x_specs(xa, xb):
    last_a, last_b = xa.shape[0] // TILE - 1, xb.shape[0] // TILE - 1
    return [pl.BlockSpec((TILE, D_MODEL), lambda i, *_: (jnp.minimum(i, last_a), 0)),
            pl.BlockSpec((TILE, D_MODEL), lambda i, *_: (last_b, 0))]


def _inproj_kernel(*refs, n_prompt_tiles, n_cast):
    (xa_ref, xb_ref, g1_ref, w_ref, perm_ref, cosb_ref, sinb_ref, cosc_ref, sinc_ref,
     nvg_ref, qg_ref, kg_ref) = refs[:12]
    cast_in, o_ref, cast_out, h_ref = refs[12:12 + n_cast], refs[12 + n_cast], refs[13 + n_cast:-1], refs[-1]
    i = pl.program_id(0)
    for src, dst in zip(cast_in, cast_out):
        dst[...] = src[...].astype(dst.dtype)
    hn = (_rms(_tile_rows(i, n_prompt_tiles, xa_ref, xb_ref)) * g1_ref[...]).astype(BF16)
    h_ref[0] = hn

    @pl.when(i < n_prompt_tiles)
    def _():
        for k in range(1, len(SWA_PATTERNS)):
            h_ref[k] = _dot(perm_ref[k - 1], hn).astype(BF16)

    @pl.when(i >= n_prompt_tiles)
    def _():
        for k in range(1, len(SWA_PATTERNS)):
            h_ref[k] = hn

    def qk_heads(acc, g, cols0, gain_ref, scale):
        for hh in range(SWA_HEADS):
            cs = slice(hh * LANE, (hh + 1) * LANE)
            y = _rms(acc[:, cs]) * gain_ref[...]
            rot = y * cosb_ref[g] + pltpu.roll(y, LANE // 2, axis=1) * sinb_ref[g]
            o_ref[:, cols0 + hh * LANE:cols0 + (hh + 1) * LANE] = (rot * scale).astype(o_ref.dtype)

    for j in range(N_COL):
        cols = slice(j * COL, (j + 1) * COL)
        g = (j - CB_Q) % len(SWA_PATTERNS) if CB_Q <= j < CB_RQK else 0
        acc = _dot(h_ref[g], w_ref[:, cols])
        if j == CB_AU:
            o_ref[:, cols] = _gelu(acc).astype(o_ref.dtype)
        elif j == CB_AV:
            o_ref[:, cols] = (_rms(_gelu(acc)) * nvg_ref[...]).astype(o_ref.dtype)
        elif CB_Q <= j < CB_K:
            qk_heads(acc, g, j * COL, qg_ref, SWA_HEAD_DIM ** -0.5)
        elif CB_K <= j < CB_V:
            qk_heads(acc, g, j * COL, kg_ref, 1.0)
        elif j < CB_RQK or j == CB_RV:
            o_ref[:, cols] = acc.astype(o_ref.dtype)
        elif j == CB_RQK:
            lane = lax.broadcasted_iota(jnp.int32, (acc.shape[0], LANE), 1)
            first_half = _imod(lane, RET_QK_DIM) < (RET_QK_DIM // 2)
            for tt in range(COL // LANE):
                y = acc[:, tt * LANE:(tt + 1) * LANE]
                partner = jnp.where(first_half,
                                    pltpu.roll(y, LANE - RET_QK_DIM // 2, axis=1),
                                    pltpu.roll(y, RET_QK_DIM // 2, axis=1))
                rot = y * cosc_ref[...] + partner * sinc_ref[...]
                scale = 1.0 if tt < (COL // LANE) // 2 else RET_QK_DIM ** -0.5
                o_ref[:, j * COL + tt * LANE:j * COL + (tt + 1) * LANE] = (rot * scale).astype(o_ref.dtype)
        elif j == CB_RG:
            o_ref[:, cols] = (acc * _sigmoid(acc)).astype(o_ref.dtype)
        else:
            o_ref[:, cols] = _sigmoid(acc).astype(o_ref.dtype)


def _cast_blocks(arr, n_steps, skip):
    a2 = arr.reshape(-1, arr.shape[-1])
    first_row = skip * (a2.shape[0] // arr.shape[0])
    rows = a2.shape[0] - first_row
    rb = -(-rows // n_steps)
    rb = -(-rb // BF16_ROWS) * BF16_ROWS
    while rows % rb or first_row % rb:
        rb += BF16_ROWS
    return a2, rb, first_row // rb


def _inproj(xa, xb, g1, w_in_b, perm, tabs, nvg, qg, kg, n_prompt_tiles, tiles_per_seq, to_cast=()):
    T = (n_prompt_tiles + 1) * TILE
    n_steps = T // TILE
    casts = [_cast_blocks(a, n_steps, skip) for a, skip in to_cast]
    n_blk = [a2.shape[0] // rb - first for a2, rb, first in casts]
    cast_in = [pl.BlockSpec((rb, a2.shape[1]), functools.partial(
        lambda i, first, last: (first + jnp.minimum(i, last), 0), first=first, last=n - 1))
        for (a2, rb, first), n in zip(casts, n_blk)]
    cast_out = [pl.BlockSpec((rb, a2.shape[1]), functools.partial(
        lambda i, last: (jnp.minimum(i, last), 0), last=n - 1)) for (a2, rb, _), n in zip(casts, n_blk)]
    cosb, sinb, cosc, sinc = tabs
    n_g = len(SWA_PATTERNS)
    row = lambda i: (i, 0)
    const = lambda i: (0, 0)
    tab_blk = lambda i: jnp.where(i < n_prompt_tiles, i % tiles_per_seq, tiles_per_seq)
    tab3 = pl.BlockSpec((n_g, TILE, LANE), lambda i: (0, tab_blk(i), 0))
    tab = pl.BlockSpec((TILE, LANE), lambda i: (tab_blk(i), 0))
    vmem = (2 * _nbytes((TILE, D_MODEL), F32) + n_g * _nbytes((TILE, D_MODEL), BF16)
            + _nbytes(w_in_b.shape, BF16) + 2 * _nbytes((TILE, IN_WIDTH), BF16)
            + 2 * (2 * n_g + 2) * _nbytes((TILE, LANE), F32) + 2 * _nbytes(perm.shape, BF16)
            + 8 * _nbytes((TILE, COL), F32)
            + sum(2 * (_nbytes((rb, a2.shape[1]), F32) + _nbytes((rb, a2.shape[1]), BF16)) for a2, rb, _ in casts))
    outs = pl.pallas_call(
        functools.partial(_inproj_kernel, n_prompt_tiles=n_prompt_tiles, n_cast=len(casts)),
        grid=(n_steps,),
        in_specs=_x_specs(xa, xb) + [
            pl.BlockSpec((1, D_MODEL), const),
            pl.BlockSpec(w_in_b.shape, const, pipeline_mode=pl.Buffered(1)),
            pl.BlockSpec(perm.shape, lambda i: (0, 0, 0)),
            tab3, tab3, tab, tab,
            pl.BlockSpec((1, COL), const),
            pl.BlockSpec((1, LANE), const),
            pl.BlockSpec((1, LANE), const),
        ] + cast_in,
        out_specs=[pl.BlockSpec((TILE, IN_WIDTH), row)] + cast_out,
        out_shape=[jax.ShapeDtypeStruct((T, IN_WIDTH), BF16)]
                  + [jax.ShapeDtypeStruct((n * rb, a2.shape[1]), BF16) for (a2, rb, _), n in zip(casts, n_blk)],
        scratch_shapes=[pltpu.VMEM((n_g, TILE, D_MODEL), BF16)],
        compiler_params=_params(vmem, ("arbitrary",)),
        name="inproj",
    )(xa, xb, g1, w_in_b, perm, cosb, sinb, cosc, sinc, nvg, qg, kg, *[a2 for a2, _, _ in casts])
    return outs[0], [o.reshape((a.shape[0] - skip,) + a.shape[1:]) for o, (a, skip) in zip(outs[1:], to_cast)]


def _mixa_kernel(u_ref, v_ref, w_ref, b_ref, o_ref, *, cps, n_prompt_chunks):
    i = pl.program_id(0)
    for c in range(cps):
        var = ((i * cps + c) >= n_prompt_chunks).astype(jnp.int32)
        rows = slice(c * CHUNK, (c + 1) * CHUNK)
        for g in range(A_GROUPS):
            cols = slice(g * LANE, (g + 1) * LANE)
            z = _dot(w_ref[var, g], v_ref[rows, cols]) + b_ref[var, :, cols]
            o_ref[rows, cols] = (u_ref[rows, cols].astype(F32) * z).astype(o_ref.dtype)


def _mixer_a(P, w2, b2, n_prompt_chunks):
    T = P.shape[0]
    n_chunks = T // CHUNK
    cps = _pick(n_chunks, (8, 6, 4, 3, 2, 1))
    rows = cps * CHUNK
    vmem = 6 * _nbytes((rows, COL), BF16) + 2 * _nbytes(w2.shape, BF16) + 2 * _nbytes(b2.shape, F32)
    return pl.pallas_call(
        functools.partial(_mixa_kernel, cps=cps, n_prompt_chunks=n_prompt_chunks),
        grid=(n_chunks // cps,),
        in_specs=[
            pl.BlockSpec((rows, COL), lambda i: (i, CB_AU)),
            pl.BlockSpec((rows, COL), lambda i: (i, CB_AV)),
            pl.BlockSpec(w2.shape, lambda i: (0, 0, 0, 0)),
            pl.BlockSpec(b2.shape, lambda i: (0, 0, 0)),
        ],
        out_specs=pl.BlockSpec((rows, COL), lambda i: (i, 0)),
        out_shape=jax.ShapeDtypeStruct((T, COL), BF16),
        compiler_params=_params(vmem, ("parallel",)),
        name="mixer_a",
    )(P, P, w2, b2)


def _swa_kernel(q_ref, kp_ref, kc_ref, vp_ref, vc_ref, o_ref, l_ref, k_ref, v_ref, s_ref, p_ref, *, qb):
    i = pl.program_id(2)
    rows = qb * CHUNK
    lead = q_ref.shape[:-1]
    k_ref[0:CHUNK] = kp_ref[...].reshape(CHUNK, COL)
    k_ref[CHUNK:CHUNK + rows] = kc_ref[...].reshape(rows, COL)
    v_ref[0:CHUNK] = vp_ref[...].reshape(CHUNK, COL)
    v_ref[CHUNK:CHUNK + rows] = vc_ref[...].reshape(rows, COL)
    row = lax.broadcasted_iota(jnp.int32, (CHUNK, 2 * CHUNK), 0)
    col = lax.broadcasted_iota(jnp.int32, (CHUNK, 2 * CHUNK), 1)
    mask_cur = (col >= CHUNK) & (col - CHUNK <= row)
    mask_all = mask_cur | ((col < CHUNK) & (col >= row))
    mask_first = mask_cur | ((col < CHUNK) & (col >= row) & (i > 0))
    q_all = q_ref[...].reshape(rows, COL)
    for j in range(qb):
        mask = mask_first if j == 0 else mask_all
        for h in range(SWA_HEADS):
            cs = slice(h * LANE, (h + 1) * LANE)
            sc = _dot_nt(q_all[j * CHUNK:(j + 1) * CHUNK, cs], k_ref[j * CHUNK:(j + 2) * CHUNK, cs])
            s_ref[j * SWA_HEADS + h] = jnp.where(mask, sc, NEG_INF)
    s = s_ref[...]
    m = jnp.max(s, axis=-1, keepdims=True)
    p = jnp.exp(s - m)
    den = jnp.sum(p, axis=-1, keepdims=True)
    p_ref[...] = p.astype(BF16)
    lse = m + jnp.log(den)
    for h in range(SWA_HEADS):
        cs = slice(h * LANE, (h + 1) * LANE)
        o_h = [_dot(p_ref[j * SWA_HEADS + h], v_ref[j * CHUNK:(j + 2) * CHUNK, cs]) / den[j * SWA_HEADS + h]
               for j in range(qb)]
        o_ref[..., cs] = jnp.concatenate(o_h, axis=0).astype(o_ref.dtype).reshape(lead + (LANE,))
    packed = [_pack_heads([lse[j * SWA_HEADS + h] for h in range(SWA_HEADS)]) for j in range(qb)]
    l_ref[...] = jnp.concatenate(packed, axis=0).reshape(lead + (LANE,))


def _swa_prompt(P, g, dil, n_p, s):
    T = P.shape[0]
    Tp = n_p * s
    nb = s // dil // CHUNK
    qb = _pick(nb, (8, 4, 2, 1))
    steps = nb // qb
    prev_blk = lambda b, i: b * nb + jnp.maximum(i * qb - 1, 0)
    if dil == 1:
        src = P
        lead_q, lead_p = (qb * CHUNK,), (CHUNK,)
        o_shape = (Tp, COL)
        l_shape = (Tp, LANE)
        q_map = lambda cb: (lambda b, r, i: (b * steps + i, cb + g))
        p_map = lambda cb: (lambda b, r, i: (prev_blk(b, i), cb + g))
        o_map = lambda b, r, i: (b * steps + i, 0)
        l_map = lambda b, r, i: (b * steps + i, 0)
    else:
        rpt = TILE // dil
        tpb = CHUNK // rpt
        src = P.reshape(T // TILE, dil, rpt, IN_WIDTH)
        lead_q, lead_p = (qb * tpb, None, rpt), (tpb, None, rpt)
        o_shape = (Tp // TILE, dil, rpt, COL)
        l_shape = (Tp // TILE, dil, rpt, LANE)
        q_map = lambda cb: (lambda b, r, i: (b * steps + i, r, 0, cb + g))
        p_map = lambda cb: (lambda b, r, i: (prev_blk(b, i), r, 0, cb + g))
        o_map = lambda b, r, i: (b * steps + i, r, 0, 0)
        l_map = lambda b, r, i: (b * steps + i, r, 0, 0)

    q_spec = lambda cb: pl.BlockSpec(lead_q + (COL,), q_map(cb))
    p_spec = lambda cb: pl.BlockSpec(lead_p + (COL,), p_map(cb))
    n_pairs = qb * SWA_HEADS
    vmem = ((10 * qb + 6) * _nbytes((CHUNK, COL), BF16) + 2 * qb * _nbytes((CHUNK, COL), F32)
            + 6 * n_pairs * _nbytes((CHUNK, 2 * CHUNK), F32))
    o, l = pl.pallas_call(
        functools.partial(_swa_kernel, qb=qb),
        grid=(n_p, dil, steps),
        in_specs=[q_spec(CB_Q), p_spec(CB_K), q_spec(CB_K), p_spec(CB_V), q_spec(CB_V)],
        out_specs=[pl.BlockSpec(lead_q + (COL,), o_map),
                   pl.BlockSpec(lead_q + (LANE,), l_map)],
        out_shape=[jax.ShapeDtypeStruct(o_shape, BF16), jax.ShapeDtypeStruct(l_shape, F32)],
        scratch_shapes=[pltpu.VMEM(((qb + 1) * CHUNK, COL), BF16), pltpu.VMEM(((qb + 1) * CHUNK, COL), BF16),
                        pltpu.VMEM((n_pairs, CHUNK, 2 * CHUNK), F32),
                        pltpu.VMEM((n_pairs, CHUNK, 2 * CHUNK), BF16)],
        compiler_params=_params(vmem, ("parallel", "parallel", "arbitrary")),
        name=f"swa_prompt_g{g}",
    )(src, src, src, src, src)
    return o.reshape(Tp, COL), l.reshape(Tp, LANE)


def _swa_sample_kernel(q_ref, kn_ref, vn_ref, cache_ref, o_ref, l_ref, *, dil, lbuf, t_s):
    nq = SWA_HEADS * t_s
    q = q_ref[...]
    qrep = jnp.concatenate([q] * SWA_HEADS, axis=0)
    rq = lax.broadcasted_iota(jnp.int32, (nq, COL), 0)
    cq = lax.broadcasted_iota(jnp.int32, (nq, COL), 1)
    qbd = jnp.where(_idiv(rq, t_s) == _idiv(cq, LANE), qrep, 0.0).astype(BF16)

    per_pos = 2 * SWA_HEADS
    if cache_ref.ndim == 2:
        n_keys = lbuf
        rows_of = lambda first: cache_ref[pl.ds(first, lbuf, stride=per_pos), :]
        key_pos = lambda r: r
    else:
        n_keys = cache_ref.shape[0] * t_s
        rows_of = lambda first: cache_ref[:, pl.ds(first, t_s, stride=per_pos), :].reshape(n_keys, LANE)
        key_pos = lambda r: dil * _idiv(r, t_s) + _imod(r, t_s)
    kc = jnp.concatenate([rows_of(h).astype(BF16) for h in range(SWA_HEADS)], axis=1)
    vc = jnp.concatenate([rows_of(SWA_HEADS + h).astype(BF16) for h in range(SWA_HEADS)], axis=1)
    kn = kn_ref[...].astype(BF16)
    vn = vn_ref[...].astype(BF16)

    s_c = _dot_nt(qbd, kc)
    s_n = _dot_nt(qbd, kn)
    t_c = _imod(lax.broadcasted_iota(jnp.int32, (nq, n_keys), 0), t_s)
    c_c = key_pos(lax.broadcasted_iota(jnp.int32, (nq, n_keys), 1))
    diff_c = lbuf + t_c - c_c
    ok_c = (_imod(diff_c, dil) == 0) & (diff_c <= lbuf)
    t_n = _imod(lax.broadcasted_iota(jnp.int32, (nq, t_s), 0), t_s)
    c_n = lax.broadcasted_iota(jnp.int32, (nq, t_s), 1)
    diff_n = t_n - c_n
    ok_n = (diff_n >= 0) & (_imod(diff_n, dil) == 0)
    s_c = jnp.where(ok_c, s_c, NEG_INF)
    s_n = jnp.where(ok_n, s_n, NEG_INF)
    m = jnp.maximum(jnp.max(s_c, axis=-1, keepdims=True), jnp.max(s_n, axis=-1, keepdims=True))
    p_c = jnp.exp(s_c - m)
    p_n = jnp.exp(s_n - m)
    den = jnp.sum(p_c, axis=-1, keepdims=True) + jnp.sum(p_n, axis=-1, keepdims=True)
    o_all = (_dot(p_c.astype(BF16), vc) + _dot(p_n.astype(BF16), vn)) / den
    lse = m + jnp.log(den)
    for h in range(SWA_HEADS):
        cs = slice(h * LANE, (h + 1) * LANE)
        o_ref[:, cs] = o_all[h * t_s:(h + 1) * t_s, cs]
    l_ref[...] = _pack_heads([lse[h * t_s:(h + 1) * t_s] for h in range(SWA_HEADS)])


def _swa_sample(Ps, cache, layer, g, dil):
    n_s, t_s, _ = Ps.shape
    rows = cache.shape[2]
    lbuf = rows // (2 * SWA_HEADS)
    assert lbuf == dil * CHUNK, "window buffer must hold exactly one full window"
    blk = lambda cb: pl.BlockSpec((None, t_s, COL), lambda b: (b, 0, cb + g))
    if dil > t_s:
        per_pos = rows // lbuf
        cache = cache.reshape(cache.shape[0], n_s, lbuf // dil, dil * per_pos, LANE)
        cache_spec = pl.BlockSpec((None, None, lbuf // dil, t_s * per_pos, LANE), lambda b: (layer, b, 0, 0, 0))
    else:
        cache_spec = pl.BlockSpec((None, None, rows, LANE), lambda b: (layer, b, 0, 0))
    vmem = (2 * _nbytes((lbuf, 2 * COL), F32) + 2 * _nbytes((lbuf, 2 * COL), BF16)
            + 8 * _nbytes((SWA_HEADS * t_s, lbuf), F32))
    return pl.pallas_call(
        functools.partial(_swa_sample_kernel, dil=dil, lbuf=lbuf, t_s=t_s),
        grid=(n_s,),
        in_specs=[blk(CB_Q), blk(CB_K), blk(CB_V), cache_spec],
        out_specs=[pl.BlockSpec((None, t_s, COL), lambda b: (b, 0, 0)),
                   pl.BlockSpec((None, t_s, LANE), lambda b: (b, 0, 0))],
        out_shape=[jax.ShapeDtypeStruct((n_s, t_s, COL), F32),
                   jax.ShapeDtypeStruct((n_s, t_s, LANE), F32)],
        compiler_params=_params(vmem, ("parallel",)),
        name=f"swa_sample_g{g}",
    )(Ps, Ps, Ps, cache)


def _ret_head_inputs(qk_ref, v_ref, kout_ref, h):
    pair, half = h // 2, h % 2
    lane = lax.broadcasted_iota(jnp.int32, (CHUNK, LANE), 1)
    head_lanes = _idiv(lane, RET_QK_DIM) == half
    qt = qk_ref[:, pair * LANE:(pair + 1) * LANE]
    kt = qk_ref[:, COL // 2 + pair * LANE:COL // 2 + (pair + 1) * LANE]
    qm = jnp.where(head_lanes, qt, jnp.zeros_like(qt))
    kw = jnp.where(head_lanes, kt.astype(F32) * kout_ref[:, pair * LANE:(pair + 1) * LANE], 0.0)
    vh = v_ref[:, h * LANE:(h + 1) * LANE]
    return qm, kt, kw, vh


def _ret_finish(o, gate_ref, o_ref, h):
    cs = slice(h * LANE, (h + 1) * LANE)
    o_ref[:, cs] = (gate_ref[:, cs].astype(F32) * _rms(o)).astype(o_ref.dtype)


def _ret_prompt_kernel(*refs, n_seq):
    ins = refs[:3 * n_seq]
    decay_ref, qin_ref, kout_ref, gc_ref, o_ref, s_out_ref, s_ref, att_ref = refs[3 * n_seq:]
    i = pl.program_id(0)

    @pl.when(i == 0)
    def _():
        s_ref[...] = jnp.zeros_like(s_ref)

    pairs = [(b, h) for b in range(n_seq) for h in range(RET_HEADS)]
    head_in = lambda b, h: _ret_head_inputs(ins[3 * b], ins[3 * b + 1], kout_ref, h)
    for b, h in pairs:
        qm, kt, _, _ = head_in(b, h)
        att_ref[b, h] = (_dot_nt(qm, kt) * decay_ref[h]).astype(BF16)
    for b, h in pairs:
        qm, _, _, vh = head_in(b, h)
        o = (_dot(att_ref[b, h], vh)
             + _dot(qm, s_ref[b, h].astype(BF16)) * qin_ref[:, h * LANE:(h + 1) * LANE])
        _ret_finish(o, ins[3 * b + 2], o_ref.at[b], h)
    for b, h in pairs:
        _, _, kw, vh = head_in(b, h)
        s_ref[b, h] = s_ref[b, h] * gc_ref[h:h + 1, :] + _dot(kw.T.astype(BF16), vh)

    @pl.when(i == pl.num_programs(0) - 1)
    def _():
        for b in range(n_seq):
            for h in range(RET_HEADS):
                lo = (h % 2) * RET_QK_DIM
                s_out_ref[b, h] = s_ref[b, h, lo:lo + RET_QK_DIM, :]


def _ret_prompt(P, rt, n_p, s):
    nblk = s // CHUNK
    decay, qin, kout, gc = rt
    blk = lambda b, cb: pl.BlockSpec((CHUNK, COL), lambda i: (b * nblk + i, cb))
    const2 = lambda i: (0, 0)
    seq_specs = [blk(b, cb) for b in range(n_p) for cb in (CB_RQK, CB_RV, CB_RG)]
    st_shape = (n_p, RET_HEADS, RET_QK_DIM, RET_V_DIM)
    vmem = (8 * n_p * _nbytes((CHUNK, COL), BF16) + 2 * _nbytes(decay.shape, F32) + 4 * _nbytes(qin.shape, F32)
            + 3 * n_p * _nbytes((RET_HEADS, LANE, LANE), F32))
    o, st = pl.pallas_call(
        functools.partial(_ret_prompt_kernel, n_seq=n_p),
        grid=(nblk,),
        in_specs=seq_specs + [pl.BlockSpec(decay.shape, lambda i: (0, 0, 0)),
                              pl.BlockSpec(qin.shape, const2), pl.BlockSpec(kout.shape, const2),
                              pl.BlockSpec(gc.shape, const2)],
        out_specs=[pl.BlockSpec((n_p, CHUNK, COL), lambda i: (0, i, 0)),
                   pl.BlockSpec(st_shape, lambda i: (0, 0, 0, 0))],
        out_shape=[jax.ShapeDtypeStruct((n_p, s, COL), BF16), jax.ShapeDtypeStruct(st_shape, F32)],
        scratch_shapes=[pltpu.VMEM((n_p, RET_HEADS, LANE, LANE), F32),
                        pltpu.VMEM((n_p, RET_HEADS, CHUNK, CHUNK), BF16)],
        compiler_params=_params(vmem, ("arbitrary",)),
        name="ret_prompt",
    )(*([P] * (3 * n_p)), decay, qin, kout, gc)
    return o.reshape(n_p * s, COL), st


def _ret_sample_kernel(qk_ref, v_ref, gate_ref, s0_ref, decay_ref, qin_ref, kout_ref, gc_ref,
                       o_ref, s_out_ref, *, t_s):
    row = lax.broadcasted_iota(jnp.int32, (CHUNK, LANE), 0)
    for h in range(RET_HEADS):
        lo = (h % 2) * RET_QK_DIM
        qm, kt, kw, vh = _ret_head_inputs(qk_ref, v_ref, kout_ref, h)
        att = _dot_nt(qm, kt) * decay_ref[h]
        o = _dot(att.astype(BF16), vh)
        inter = jnp.zeros((CHUNK, LANE), F32)
        for sq in range(CHUNK // t_s):
            seq_rows = _idiv(row, t_s) == sq
            st = s0_ref[sq, h]
            st2 = jnp.concatenate([st, st], axis=0).astype(BF16)
            inter = jnp.where(seq_rows, _dot(qm, st2), inter)
            upd = _dot(jnp.where(seq_rows, kw, 0.0).T.astype(BF16), vh)
            s_out_ref[sq, h] = st * gc_ref[h:h + 1, :] + upd[lo:lo + RET_QK_DIM, :]
        o = o + inter * qin_ref[:, h * LANE:(h + 1) * LANE]
        _ret_finish(o, gate_ref, o_ref, h)


def _ret_sample(P, state, layer, rt, Tp, t_s):
    n_s = state.shape[1]
    spb = CHUNK // t_s
    base = Tp // CHUNK
    decay, qin, kout, gc = rt
    blk = lambda cb: pl.BlockSpec((CHUNK, COL), lambda i: (base + i, cb))
    const2 = lambda i: (0, 0)
    st_shape = (spb, RET_HEADS, RET_QK_DIM, RET_V_DIM)
    vmem = (8 * _nbytes((CHUNK, COL), BF16) + 2 * _nbytes(decay.shape, F32) + 4 * _nbytes(qin.shape, F32)
            + 4 * _nbytes(st_shape, F32))
    return pl.pallas_call(
        functools.partial(_ret_sample_kernel, t_s=t_s),
        grid=(n_s // spb,),
        in_specs=[blk(CB_RQK), blk(CB_RV), blk(CB_RG),
                  pl.BlockSpec((None,) + st_shape, lambda i: (layer, i, 0, 0, 0)),
                  pl.BlockSpec(decay.shape, lambda i: (0, 0, 0)),
                  pl.BlockSpec(qin.shape, const2), pl.BlockSpec(kout.shape, const2),
                  pl.BlockSpec(gc.shape, const2)],
        out_specs=[pl.BlockSpec((CHUNK, COL), lambda i: (i, 0)),
                   pl.BlockSpec(st_shape, lambda i: (i, 0, 0, 0))],
        out_shape=[jax.ShapeDtypeStruct((n_s * t_s, COL), BF16),
                   jax.ShapeDtypeStruct((n_s, RET_HEADS, RET_QK_DIM, RET_V_DIM), F32)],
        compiler_params=_params(vmem, ("parallel",)),
        name="ret_sample",
    )(P, P, P, state, decay, qin, kout, gc)


def _branch_kernel(xa_ref, xb_ref, a_ref, g0_ref, g1_ref, g2_ref, wb_ref, wo_ref, ex_ref,
                   po0, po1, po2, pl0, pl1, pl2, pc, so0, so1, so2, sl0, sl1, sl2, sc,
                   y_ref, on_ref, ln_ref, *, n_prompt_tiles):
    i = pl.program_id(0)

    def spread(packed):
        hi = packed.astype(BF16)
        lo = (packed - hi.astype(F32)).astype(BF16)
        return _dot(jnp.concatenate([hi, lo], axis=1), ex_ref[...])

    def natural(o_ref, l_ref, k, dil):
        if dil == 1:
            return (lambda h: o_ref[:, h * LANE:(h + 1) * LANE].astype(F32)), l_ref[...]
        rpt = TILE // dil
        for r in range(dil):
            rows = slice(r * rpt, (r + 1) * rpt)
            ln_ref[k, pl.ds(r, rpt, stride=dil), :] = l_ref[rows, :]
            for h in range(SWA_HEADS):
                on_ref[k, h, pl.ds(r, rpt, stride=dil), :] = o_ref[rows, h * LANE:(h + 1) * LANE].astype(F32)
        return (lambda h: on_ref[k, h]), ln_ref[k]

    def body(o_refs, l_refs, c_ref, permuted):
        groups = [natural(o_refs[k], l_refs[k], k, SWA_PATTERNS[k][1] if permuted else 1)
                  for k in range(len(SWA_PATTERNS))]
        ls = [l for _, l in groups]
        lmax = functools.reduce(jnp.maximum, ls)
        es = [jnp.exp(l - lmax) for l in ls]
        inv = 1.0 / sum(es)
        weights = [spread(e * inv) for e in es]
        heads = []
        for h in range(SWA_HEADS):
            cs = slice(h * LANE, (h + 1) * LANE)
            heads.append(sum(w[:, cs] * go(h) for w, (go, _) in zip(weights, groups)).astype(BF16))
        mix = jnp.concatenate(heads, axis=1)
        merged = (g0_ref[...].astype(F32) * _dot(a_ref[...], wb_ref[0])
                  + g1_ref[...].astype(F32) * _dot(mix, wb_ref[1])
                  + g2_ref[...].astype(F32) * _dot(c_ref[...], wb_ref[2]))
        y_ref[...] = _tile_rows(i, n_prompt_tiles, xa_ref, xb_ref) + _dot(merged.astype(BF16), wo_ref[...])

    @pl.when(i < n_prompt_tiles)
    def _():
        body((po0, po1, po2), (pl0, pl1, pl2), pc, True)

    @pl.when(i >= n_prompt_tiles)
    def _():
        body((so0, so1, so2), (sl0, sl1, sl2), sc, False)


def _branch(xa, xb, out_a, P, wb, wo, prompt_set, sample_set, n_prompt_tiles):
    T = (n_prompt_tiles + 1) * TILE
    assert sample_set[0].shape[0] == TILE, "the sample rows must form exactly one tile"
    n_g = len(SWA_PATTERNS)
    last = n_prompt_tiles - 1
    row = lambda i: (i, 0)
    gate = lambda k: pl.BlockSpec((TILE, D_MODEL), lambda i: (i, CB_GATE * COL // D_MODEL + k))
    once = pl.Buffered(1)
    p_o = pl.BlockSpec((TILE, COL), lambda i: (jnp.minimum(i, last), 0))
    p_l = pl.BlockSpec((TILE, LANE), lambda i: (jnp.minimum(i, last), 0))
    s_o = pl.BlockSpec((TILE, COL), lambda i: (0, 0))
    s_l = pl.BlockSpec((TILE, LANE), lambda i: (0, 0))
    src = jnp.arange(2 * LANE) % LANE
    dst_head = jnp.arange(COL) // LANE
    expand = (src[:, None] == dst_head[None, :] * LSE_LANES).astype(BF16)
    vmem = (4 * _nbytes((TILE, D_MODEL), F32) + 6 * _nbytes((TILE, D_MODEL), BF16)
            + 2 * (2 * n_g + 2) * (_nbytes((TILE, COL), BF16) + _nbytes((TILE, COL), F32))
            + _nbytes(wb.shape, BF16) + _nbytes(wo.shape, BF16)
            + 4 * n_g * _nbytes((TILE, COL), F32) + 4 * _nbytes((TILE, D_MODEL), F32))
    return pl.pallas_call(
        functools.partial(_branch_kernel, n_prompt_tiles=n_prompt_tiles),
        grid=(T // TILE,),
        in_specs=_x_specs(xa, xb) + [
                  pl.BlockSpec((TILE, COL), row),
                  gate(0), gate(1), gate(2),
                  pl.BlockSpec(wb.shape, lambda i: (0, 0, 0), pipeline_mode=once),
                  pl.BlockSpec(wo.shape, lambda i: (0, 0), pipeline_mode=once),
                  pl.BlockSpec(expand.shape, lambda i: (0, 0)),
                  p_o, p_o, p_o, p_l, p_l, p_l, p_o,
                  s_o, s_o, s_o, s_l, s_l, s_l, s_o],
        out_specs=pl.BlockSpec((TILE, D_MODEL), row),
        out_shape=jax.ShapeDtypeStruct((T, D_MODEL), F32),
        scratch_shapes=[pltpu.VMEM((n_g, SWA_HEADS, TILE, LANE), F32),
                        pltpu.VMEM((n_g, TILE, LANE), F32)],
        compiler_params=_params(vmem, ("parallel",)),
        name="branch_merge",
    )(xa, xb, out_a, P, P, P, wb, wo, expand, *prompt_set, *sample_set)


def _swiglu_into(hb, w1_ref, w3_ref, w2_ref, acc_ref, fc):
    n_fc = w1_ref.shape[-1] // fc
    for f in range(n_fc):
        cols = slice(f * fc, (f + 1) * fc)
        a = _dot(hb, w1_ref[:, cols])
        b = _dot(hb, w3_ref[:, cols])
        part = _dot((a * _sigmoid(a) * b).astype(BF16), w2_ref[cols, :])
        if f == 0:
            acc_ref[...] = part
        else:
            acc_ref[...] += part


def _dense_ffn_kernel(x_ref, g2_ref, w1_ref, w3_ref, w2_ref, y_ref, acc_ref, *, fc):
    x = x_ref[...]
    hb = (_rms(x) * g2_ref[...]).astype(BF16)
    _swiglu_into(hb, w1_ref, w3_ref, w2_ref, acc_ref, fc)
    y_ref[...] = x + acc_ref[...]


def _dense_ffn(x, g2, w1b, w3b, w2b):
    T = x.shape[0]
    F = w1b.shape[1]
    tm = _pick(T, (768, 512, 384, 256, 128))
    fc = _pick(F, (512, 256, 128))
    row = lambda i: (i, 0)
    once = pl.Buffered(1)
    vmem = (5 * _nbytes((tm, D_MODEL), F32) + 3 * _nbytes(w1b.shape, BF16) + _nbytes((tm, D_MODEL), BF16)
            + 4 * _nbytes((tm, fc), F32))
    return pl.pallas_call(
        functools.partial(_dense_ffn_kernel, fc=fc),
        grid=(T // tm,),
        in_specs=[pl.BlockSpec((tm, D_MODEL), row),
                  pl.BlockSpec((1, D_MODEL), lambda i: (0, 0)),
                  pl.BlockSpec(w1b.shape, lambda i: (0, 0), pipeline_mode=once),
                  pl.BlockSpec(w3b.shape, lambda i: (0, 0), pipeline_mode=once),
                  pl.BlockSpec(w2b.shape, lambda i: (0, 0), pipeline_mode=once)],
        out_specs=pl.BlockSpec((tm, D_MODEL), row),
        out_shape=jax.ShapeDtypeStruct((T, D_MODEL), F32),
        scratch_shapes=[pltpu.VMEM((tm, D_MODEL), F32)],
        compiler_params=_params(vmem, ("parallel",)),
        name="dense_ffn",
    )(x, g2, w1b, w3b, w2b)


def _router_kernel(x_ref, g2_ref, wr_ref, e0_ref, e1_ref, w0_ref, w1_ref):
    hb = (_rms(x_ref[...]) * g2_ref[...]).astype(BF16)
    logits = _dot(hb, wr_ref[...])
    lane = lax.broadcasted_iota(jnp.int32, logits.shape, 1)
    logits = jnp.where(lane < N_EXPERTS, logits, -jnp.inf)
    lane_f = lane.astype(F32)
    m1 = jnp.max(logits, axis=-1, keepdims=True)
    i1 = jnp.min(jnp.where(logits == m1, lane_f, float(LANE)), axis=-1, keepdims=True)
    rest = jnp.where(lane_f == i1, -jnp.inf, logits)
    m2 = jnp.max(rest, axis=-1, keepdims=True)
    i2 = jnp.min(jnp.where(rest == m2, lane_f, float(LANE)), axis=-1, keepdims=True)
    e = jnp.exp(m2 - m1)
    e0_ref[...] = jnp.broadcast_to(i1, logits.shape).astype(jnp.int32)
    e1_ref[...] = jnp.broadcast_to(i2, logits.shape).astype(jnp.int32)
    w0_ref[...] = jnp.broadcast_to(1.0 / (1.0 + e), logits.shape)
    w1_ref[...] = jnp.broadcast_to(e / (1.0 + e), logits.shape)


def _router(x, g2, wr_pad):
    T = x.shape[0]
    tm = _pick(T, (768, 512, 384, 256, 128))
    row = lambda i: (i, 0)
    out = pl.BlockSpec((tm, LANE), row)
    vmem = 4 * _nbytes((tm, D_MODEL), F32) + 2 * _nbytes(wr_pad.shape, BF16) + 16 * _nbytes((tm, LANE), F32)
    return pl.pallas_call(
        _router_kernel,
        grid=(T // tm,),
        in_specs=[pl.BlockSpec((tm, D_MODEL), row),
                  pl.BlockSpec((1, D_MODEL), lambda i: (0, 0)),
                  pl.BlockSpec(wr_pad.shape, lambda i: (0, 0))],
        out_specs=[out, out, out, out],
        out_shape=[jax.ShapeDtypeStruct((T, LANE), jnp.int32), jax.ShapeDtypeStruct((T, LANE), jnp.int32),
                   jax.ShapeDtypeStruct((T, LANE), F32), jax.ShapeDtypeStruct((T, LANE), F32)],
        compiler_params=_params(vmem, ("parallel",)),
        name="moe_router",
    )(x, g2, wr_pad)


RUN_ALIGN = SUBLANE


def _run_pieces(count, max_rows):
    pieces, off = [], 0
    size = max_rows
    while size >= RUN_ALIGN:
        active = (count & size) != 0
        pieces.append((size, off, active))
        off = off + jnp.where(active, size, 0)
        size //= 2
    return pieces


def _packed_rows(i, e0, e1, lower_ref, loc_ref):
    lane = lax.broadcasted_iota(jnp.int32, e0.shape, 1)
    picks = jnp.where((lane == e0) | (lane == e1), 1.0, 0.0)
    rank = _dot(lower_ref[...], picks.astype(BF16))
    lane1 = lax.broadcasted_iota(jnp.int32, (1, LANE), 1)
    run_start = jnp.zeros((1, LANE), F32)
    for e in range(N_EXPERTS):
        run_start = jnp.where(lane1 == e, loc_ref[i * N_EXPERTS + e].astype(F32), run_start)
    row_of = rank + run_start
    return tuple(jnp.sum(jnp.where(lane == ek, row_of, 0.0), axis=-1, keepdims=True) for ek in (e0, e1))


DISPATCH_BUFFERS = 3
RUN_ROWS = 640


def _dispatch_kernel(base_ref, cnt_ref, loc_ref, pend_ref, x_ref, g2_ref, e0_ref, e1_ref, lower_ref, xs_ref,
                     comp_ref, sem, *, tme, n_tiles):
    i = pl.program_id(0)
    n_steps = pl.num_programs(0)
    n_buf = comp_ref.shape[0]
    cur = i % n_buf
    half = tme // 2

    @pl.when(i == 0)
    def _():
        comp_ref[0, 0:half] = jnp.zeros((half, D_MODEL), comp_ref.dtype)

        def zero_tile(start):
            return [pltpu.make_async_copy(
                        comp_ref.at[0, pl.ds(0, half), :],
                        xs_ref.at[pl.ds(pl.multiple_of(start + k * half, half), half), :], sem.at[0])
                    for k in range(2)]

        def tail_loop(fn):
            def body(t, carry):
                for c in zero_tile(t * tme):
                    fn(c)
                return carry
            lax.fori_loop(pend_ref[N_EXPERTS - 1] // tme, n_tiles, body, 0)

        def group_tails(fn):
            for e in range(N_EXPERTS):
                prev_end = pend_ref[e - 1] if e else 0

                @pl.when(pend_ref[e] > prev_end)
                def _():
                    for c in zero_tile(pend_ref[e] - tme):
                        fn(c)

        group_tails(lambda c: c.start())
        tail_loop(lambda c: c.start())
        group_tails(lambda c: c.wait())
        tail_loop(lambda c: c.wait())

    h = (_rms(x_ref[...]) * g2_ref[...]).astype(BF16)
    pos0, pos1 = _packed_rows(i, e0_ref[...], e1_ref[...], lower_ref, loc_ref)
    slot = lax.broadcasted_iota(jnp.int32, (TILE, RUN_ROWS), 1).astype(F32)
    hit = jnp.where((slot == pos0) | (slot == pos1), 1.0, 0.0)
    comp_ref[cur] = _dot(hit.T.astype(BF16), h)

    def for_copies(tile, buf, fn):
        for e in range(N_EXPERTS):
            k = tile * N_EXPERTS + e
            cnt, dst, loc = cnt_ref[k], base_ref[k], loc_ref[k]
            for size, off, active in _run_pieces(cnt, TILE):
                c = pltpu.make_async_copy(
                    comp_ref.at[buf, pl.ds(pl.multiple_of(loc + off, RUN_ALIGN), size), :],
                    xs_ref.at[pl.ds(pl.multiple_of(dst + off, RUN_ALIGN), size), :], sem.at[buf])
                pl.when(active)(functools.partial(fn, c))

    def wait_copies(tile, buf):
        rows = pl.multiple_of(sum(cnt_ref[tile * N_EXPERTS + e] for e in range(N_EXPERTS)), RUN_ALIGN)
        pltpu.make_async_copy(comp_ref.at[buf, pl.ds(0, rows), :], xs_ref.at[pl.ds(0, rows), :],
                              sem.at[buf]).wait()

    for_copies(i, cur, lambda c: c.start())

    @pl.when(i >= n_buf - 1)
    def _():
        wait_copies(i - (n_buf - 1), (i + 1) % n_buf)

    @pl.when(i == n_steps - 1)
    def _():
        for back in range(n_buf - 2, -1, -1):

            @pl.when(i >= back)
            def _():
                wait_copies(i - back, (i - back) % n_buf)


def _dispatch(x, g2, e0, e1, base, cnt, loc, pend, lower, n_tiles, tme):
    T = x.shape[0]
    assert tme == 2 * TILE
    row = lambda i, *_: (i, 0)
    assert RUN_ROWS >= 2 * TILE + N_EXPERTS * (RUN_ALIGN - 1) and RUN_ROWS % LANE == 0
    vmem = (3 * _nbytes((TILE, D_MODEL), F32) + DISPATCH_BUFFERS * _nbytes((RUN_ROWS, D_MODEL), F32)
            + 8 * _nbytes((RUN_ROWS, TILE), F32))
    grid_spec = pltpu.PrefetchScalarGridSpec(
        num_scalar_prefetch=4,
        grid=(T // TILE,),
        in_specs=[pl.BlockSpec((TILE, D_MODEL), row),
                  pl.BlockSpec((1, D_MODEL), lambda i, *_: (0, 0)),
                  pl.BlockSpec((TILE, LANE), row), pl.BlockSpec((TILE, LANE), row),
                  pl.BlockSpec((TILE, TILE), lambda i, *_: (0, 0))],
        out_specs=pl.BlockSpec(memory_space=pl.ANY),
        scratch_shapes=[pltpu.VMEM((DISPATCH_BUFFERS, RUN_ROWS, D_MODEL), F32),
                        pltpu.SemaphoreType.DMA((DISPATCH_BUFFERS,))],
    )
    return pl.pallas_call(
        functools.partial(_dispatch_kernel, tme=tme, n_tiles=n_tiles),
        grid_spec=grid_spec,
        out_shape=jax.ShapeDtypeStruct((n_tiles * tme, D_MODEL), F32),
        compiler_params=_params(vmem, ("arbitrary",)),
        name="moe_dispatch",
    )(base, cnt, loc, pend, x, g2, e0, e1, lower)


def _expert_ffn_kernel(te_ref, used_ref, xs_ref, w1_ref, w3_ref, w2_ref, y_ref, acc_ref, *, fc):
    i = pl.program_id(0)

    @pl.when(i < used_ref[0])
    def _():
        _swiglu_into(xs_ref[...].astype(BF16), w1_ref, w3_ref, w2_ref, acc_ref, fc)
        y_ref[...] = acc_ref[...]

    @pl.when(i >= used_ref[0])
    def _():
        y_ref[...] = jnp.zeros_like(y_ref)


def _expert_ffn(xs, tile_expert, n_used, we1b, we3b, we2b, tme):
    n_rows = xs.shape[0]
    F = we1b.shape[2]
    fc = _pick(F, (512, 256, 128))
    wspec = lambda shape: pl.BlockSpec((None,) + shape, lambda i, te, nu: (te[i], 0, 0))
    vmem = (5 * _nbytes((tme, D_MODEL), F32) + 6 * _nbytes(we1b.shape[1:], BF16) + _nbytes((tme, D_MODEL), BF16)
            + 4 * _nbytes((tme, fc), F32))
    grid_spec = pltpu.PrefetchScalarGridSpec(
        num_scalar_prefetch=2,
        grid=(n_rows // tme,),
        in_specs=[pl.BlockSpec((tme, D_MODEL), lambda i, te, nu: (jnp.minimum(i, jnp.maximum(nu[0], 1) - 1), 0)),
                  wspec(we1b.shape[1:]), wspec(we3b.shape[1:]), wspec(we2b.shape[1:])],
        out_specs=pl.BlockSpec((tme, D_MODEL), lambda i, te, nu: (i, 0)),
        scratch_shapes=[pltpu.VMEM((tme, D_MODEL), F32)],
    )
    return pl.pallas_call(
        functools.partial(_expert_ffn_kernel, fc=fc),
        grid_spec=grid_spec,
        out_shape=jax.ShapeDtypeStruct((n_rows, D_MODEL), F32),
        compiler_params=_params(vmem, ("arbitrary",)),
        name="moe_expert_ffn",
    )(tile_expert, n_used, xs, we1b, we3b, we2b)


def _combine_kernel(base_ref, cnt_ref, loc_ref, x_ref, e0_ref, e1_ref, w0_ref, w1_ref, lower_ref, y_hbm_ref,
                    o_ref, o_last_ref, buf_ref, sem):
    i = pl.program_id(0)
    n_steps = pl.num_programs(0)
    cur = i % 2

    def for_copies(tile, buf, fn):
        for e in range(N_EXPERTS):
            k = tile * N_EXPERTS + e
            cnt, src, loc = cnt_ref[k], base_ref[k], loc_ref[k]
            for size, off, active in _run_pieces(cnt, TILE):
                c = pltpu.make_async_copy(
                    y_hbm_ref.at[pl.ds(pl.multiple_of(src + off, RUN_ALIGN), size), :],
                    buf_ref.at[buf, pl.ds(pl.multiple_of(loc + off, RUN_ALIGN), size), :], sem.at[buf])
                pl.when(active)(functools.partial(fn, c))

    @pl.when(i == 0)
    def _():
        buf_ref[...] = jnp.zeros_like(buf_ref)
        for_copies(0, 0, lambda c: c.start())

    @pl.when(i + 1 < n_steps)
    def _():
        for_copies(i + 1, 1 - cur, lambda c: c.start())

    rows = pl.multiple_of(sum(cnt_ref[i * N_EXPERTS + e] for e in range(N_EXPERTS)), RUN_ALIGN)
    pltpu.make_async_copy(y_hbm_ref.at[pl.ds(0, rows), :], buf_ref.at[cur, pl.ds(0, rows), :],
                          sem.at[cur]).wait()

    positions = _packed_rows(i, e0_ref[...], e1_ref[...], lower_ref, loc_ref)
    slot = lax.broadcasted_iota(jnp.int32, (TILE, RUN_ROWS), 1).astype(F32)
    yb = buf_ref[cur].astype(BF16)
    wide = lambda a, n: jnp.concatenate([a] * n, axis=1)
    acc = x_ref[...]
    for pos, wk in zip(positions, (w0_ref, w1_ref)):
        rows = _dot(jnp.where(slot == pos, 1.0, 0.0).astype(BF16), yb)
        acc = acc + wide(wk[...], D_MODEL // LANE) * rows

    @pl.when(i < n_steps - 1)
    def _():
        o_ref[...] = acc

    @pl.when(i == n_steps - 1)
    def _():
        o_last_ref[...] = acc


def _combine(x, e0, e1, w0, w1, base, cnt, loc, lower, y_rows):
    T = x.shape[0]
    row = lambda i, *_: (i, 0)
    lane_spec = pl.BlockSpec((TILE, LANE), row)
    vmem = (5 * _nbytes((TILE, D_MODEL), F32) + 3 * _nbytes((RUN_ROWS, D_MODEL), F32)
            + 8 * _nbytes((TILE, LANE), F32) + 8 * _nbytes((TILE, RUN_ROWS), F32)
            + 4 * _nbytes((TILE, D_MODEL), F32))
    grid_spec = pltpu.PrefetchScalarGridSpec(
        num_scalar_prefetch=3,
        grid=(T // TILE,),
        in_specs=[pl.BlockSpec((TILE, D_MODEL), row), lane_spec, lane_spec, lane_spec, lane_spec,
                  pl.BlockSpec((TILE, TILE), lambda i, *_: (0, 0)),
                  pl.BlockSpec(memory_space=pl.ANY)],
        out_specs=[pl.BlockSpec((TILE, D_MODEL), lambda i, *_: (jnp.minimum(i, T // TILE - 2), 0)),
                   pl.BlockSpec((TILE, D_MODEL), lambda i, *_: (0, 0))],
        scratch_shapes=[pltpu.VMEM((2, RUN_ROWS, D_MODEL), F32), pltpu.SemaphoreType.DMA((2,))],
    )
    return pl.pallas_call(
        _combine_kernel,
        grid_spec=grid_spec,
        out_shape=[jax.ShapeDtypeStruct((T - TILE, D_MODEL), F32), jax.ShapeDtypeStruct((TILE, D_MODEL), F32)],
        compiler_params=_params(vmem, ("arbitrary",)),
        name="moe_combine",
    )(base, cnt, loc, x, e0, e1, w0, w1, lower, y_rows)


def _moe(x, g2, wr_pad, we1b, we3b, we2b):
    T = x.shape[0]
    tme = 2 * TILE
    n_tt = T // TILE
    e0, e1, w0, w1 = _router(x, g2, wr_pad)
    experts = jnp.arange(N_EXPERTS, dtype=jnp.int32)
    picks = (e0[:, :1] == experts[None, :]) | (e1[:, :1] == experts[None, :])
    cnt = jnp.sum(picks.reshape(n_tt, TILE, N_EXPERTS).astype(jnp.int32), axis=1)
    cnt = (cnt + RUN_ALIGN - 1) // RUN_ALIGN * RUN_ALIGN
    counts = jnp.sum(cnt, axis=0)
    padded = (counts + tme - 1) // tme * tme
    pend = jnp.cumsum(padded).astype(jnp.int32)
    base = ((pend - padded)[None, :] + jnp.cumsum(cnt, axis=0) - cnt).astype(jnp.int32)
    loc = (jnp.cumsum(cnt, axis=1) - cnt).astype(jnp.int32)
    max_rows = T * TOP_K + n_tt * N_EXPERTS * (RUN_ALIGN - 1)
    n_tiles = -(-max_rows // tme) + N_EXPERTS
    tile_start = jnp.arange(n_tiles, dtype=jnp.int32) * tme
    tile_expert = jnp.minimum(jnp.sum(pend[None, :] <= tile_start[:, None], axis=1), N_EXPERTS - 1).astype(jnp.int32)
    n_used = (pend[N_EXPERTS - 1:] // tme).astype(jnp.int32)
    tok = jnp.arange(TILE)
    lower = (tok[None, :] < tok[:, None]).astype(BF16)
    base_f, cnt_f, loc_f = base.reshape(-1), cnt.reshape(-1).astype(jnp.int32), loc.reshape(-1)
    xs = _dispatch(x, g2, e0, e1, base_f, cnt_f, loc_f, pend, lower, n_tiles, tme)
    y_rows = _expert_ffn(xs, tile_expert, n_used, we1b, we3b, we2b, tme)
    return _combine(x, e0, e1, w0, w1, base_f, cnt_f, loc_f, lower, y_rows)


def _class_major(a, dil):
    n = a.shape[0]
    return a.reshape((n // TILE, TILE // dil, dil) + a.shape[1:]).swapaxes(1, 2).reshape(a.shape)


def _natural(a, dil):
    n = a.shape[0]
    return a.reshape((n // TILE, dil, TILE // dil) + a.shape[1:]).swapaxes(1, 2).reshape(a.shape)


def _perm_matrices():
    eye = jnp.eye(TILE, dtype=BF16)
    return jnp.stack([_class_major(eye, dil) for _, dil in SWA_PATTERNS[1:]])


def _rope_tables(pos, Tp):
    freqs = ROPE_THETA ** (-jnp.arange(0, SWA_HEAD_DIM, 2, dtype=F32) / SWA_HEAD_DIM)
    rfreqs = 1.0 / (ROPE_THETA ** jnp.linspace(0.0, 1.0, RET_QK_DIM // 2, dtype=F32))

    def swa(p):
        ang = p.astype(F32)[:, None] * freqs[None, :]
        c, s = jnp.cos(ang), jnp.sin(ang)
        return jnp.concatenate([c, c], axis=1), jnp.concatenate([-s, s], axis=1)

    per_group = [swa(jnp.concatenate([_class_major(pos[:Tp], dil), pos[Tp:]])) for _, dil in SWA_PATTERNS]
    cosb = jnp.stack([c for c, _ in per_group])
    sinb = jnp.stack([s for _, s in per_group])
    rang = pos.astype(F32)[:, None] * rfreqs[None, :]
    c, s = jnp.cos(rang), jnp.sin(rang)
    cosc = jnp.concatenate([c, c, c, c], axis=1)
    sinc = jnp.concatenate([-s, s, -s, s], axis=1)
    return cosb, sinb, cosc, sinc


def _ret_tables(c_len):
    log_g = jnp.log1p(-jnp.exp2(-5.0 - jnp.arange(RET_HEADS, dtype=F32)))
    r = jnp.arange(CHUNK)
    i = (r % c_len).astype(F32)
    same = (r[:, None] // c_len) == (r[None, :] // c_len)
    dist = i[:, None] - i[None, :]
    decay = jnp.where(same[None] & (dist >= 0)[None],
                      jnp.exp(log_g[:, None, None] * jnp.maximum(dist, 0.0)[None]), 0.0)
    qin = jnp.repeat(jnp.exp(log_g[None, :] * (i[:, None] + 1.0)), RET_V_DIM, axis=1)
    kout = jnp.repeat(jnp.exp(log_g[None, :] * (c_len - 1.0 - i)[:, None]), RET_QK_DIM, axis=1)
    gc = jnp.broadcast_to(jnp.exp(log_g * c_len)[:, None], (RET_HEADS, LANE))
    gc = jnp.concatenate([gc, jnp.zeros((SUBLANE - RET_HEADS, LANE), F32)], axis=0)
    return decay.astype(F32), qin.astype(F32), kout.astype(F32), gc.astype(F32)


def _mixa_tables(w_s, b_s, t_s):
    w_p = jnp.tril(w_s)
    w8 = jnp.tril(w_s[:, :t_s, :t_s])
    eye = jnp.eye(CHUNK // t_s, dtype=w_s.dtype)
    w_smp = jax.vmap(lambda m: jnp.kron(eye, m))(w8)
    w2 = jnp.stack([w_p, w_smp]).astype(BF16)
    b_p = jnp.repeat(b_s.T, LANE, axis=1)
    b_smp = jnp.repeat(jnp.tile(b_s[:, :t_s].T, (CHUNK // t_s, 1)), LANE, axis=1)
    return w2, jnp.stack([b_p, b_smp]).astype(F32)


def kernel(x_prompt, x_sample, cache_swa_kv0, cache_swa_kv1, cache_swa_kv2, state_ret, norm1_g, w_in, norm_v_g, w_s, b_s, q_norm_g, k_norm_g, w_branch, w_out, norm2_g, w1, w3, w2, w_router, we1, we3, we2):
    n_p, s, d = x_prompt.shape
    n_s, t_s, _ = x_sample.shape
    depth = w_in.shape[0]
    Tp, Ts = n_p * s, n_s * t_s
    T = Tp + Ts
    max_win, max_dil = SWA_PATTERNS[-1]
    assert d == D_MODEL and Ts == TILE and CHUNK % t_s == 0
    assert s % (CHUNK * max_dil) == 0 and s >= max_win
    n_pt = Tp // TILE

    xa, xb = x_prompt.reshape(Tp, d), x_sample.reshape(Ts, d)
    pos = jnp.concatenate([jnp.arange(s, dtype=jnp.int32),
                           jnp.tile(PAST_LEN + jnp.arange(t_s, dtype=jnp.int32), n_s)])
    tabs = _rope_tables(pos, s)
    perm = _perm_matrices()
    rt_prompt = _ret_tables(CHUNK)
    rt_sample = _ret_tables(t_s)
    caches = tuple(c.reshape(c.shape[0], c.shape[1], -1, SWA_HEAD_DIM)
                   for c in (cache_swa_kv0, cache_swa_kv1, cache_swa_kv2))

    p_kv = [[] for _ in SWA_PATTERNS]
    s_kv = [[] for _ in SWA_PATTERNS]
    p_ret, s_ret, s_v = [], [], []
    for layer in range(depth):
        if layer == 0:
            later = [(w_in, 1)] + [(w, 0) for w in (w_branch, w_out, w1, w3, w2, we1, we3, we2)]
            P, (w_in_later, w_branch_b, w_out_b, w1_b, w3_b, w2_b, we1_b, we3_b, we2_b) = _inproj(
                xa, xb, norm1_g[0][None, :], w_in[0].astype(BF16), perm, tabs, norm_v_g[0][None, :],
                q_norm_g[0][None, :], k_norm_g[0][None, :], n_pt, s // TILE, later)
        else:
            P, _ = _inproj(xa, xb, norm1_g[layer][None, :], w_in_later[layer - 1], perm, tabs,
                           norm_v_g[layer][None, :], q_norm_g[layer][None, :], k_norm_g[layer][None, :],
                           n_pt, s // TILE)

        w2a, b2a = _mixa_tables(w_s[layer], b_s[layer], t_s)
        out_a = _mixer_a(P, w2a, b2a, Tp // CHUNK)

        Ps = P[Tp:].astype(F32).reshape(n_s, t_s, IN_WIDTH)
        po, plse, so, slse = [], [], [], []
        for g, (win, dil) in enumerate(SWA_PATTERNS):
            o_g, l_g = _swa_prompt(P, g, dil, n_p, s)
            os_g, ls_g = _swa_sample(Ps, caches[g], layer, g, dil)
            po.append(o_g)
            plse.append(l_g)
            so.append(os_g.reshape(Ts, COL).astype(BF16))
            slse.append(ls_g.reshape(Ts, LANE))

        out_c, ret_p = _ret_prompt(P, rt_prompt, n_p, s)
        out_cs, ret_s = _ret_sample(P, state_ret, layer, rt_sample, Tp, t_s)

        x = _branch(xa, xb, out_a, P, w_branch_b[layer], w_out_b[layer],
                    (*po, *plse, out_c), (*so, *slse, out_cs), n_pt)

        g2 = norm2_g[layer][None, :]
        i = layer // 2
        if layer % 2 == 0:
            x = _dense_ffn(x, g2, w1_b[i], w3_b[i], w2_b[i])
            xa = xb = x
        else:
            wr_pad = jnp.zeros((D_MODEL, LANE), BF16).at[:, :N_EXPERTS].set(w_router[i].astype(BF16))
            xa, xb = _moe(x, g2, wr_pad, we1_b[i], we3_b[i], we2_b[i])

        for g, (win, dil) in enumerate(SWA_PATTERNS):
            kcols = slice((CB_K + g) * COL, (CB_K + g + 1) * COL)
            vcols = slice((CB_V + g) * COL, (CB_V + g + 1) * COL)
            keep = -(-min(win, s) // TILE) * TILE

            def rows(cols):
                blk = jnp.concatenate([P[(b + 1) * s - keep:(b + 1) * s, cols] for b in range(n_p)])
                blk = _natural(blk, dil).reshape(n_p, keep, SWA_HEADS, SWA_HEAD_DIM)
                return blk[:, keep - min(win, s):]

            p_kv[g].append(jnp.stack([rows(kcols), rows(vcols)], axis=2).astype(F32))
            ks = P[Tp:, kcols].reshape(n_s, t_s, SWA_HEADS, SWA_HEAD_DIM)
            vs = P[Tp:, vcols].reshape(n_s, t_s, SWA_HEADS, SWA_HEAD_DIM)
            s_kv[g].append(jnp.stack([ks, vs], axis=2).astype(F32))
        p_ret.append(ret_p)
        s_ret.append(ret_s)
        s_v.append(P[Tp:, CB_AV * COL:(CB_AV + 1) * COL].astype(F32).reshape(n_s, t_s, COL))

    y_prompt = xa[:Tp].reshape(n_p, s, d)
    y_sample = xb[xb.shape[0] - Ts:].reshape(n_s, t_s, d)
    return (y_prompt, y_sample,
            jnp.stack(p_kv[0]), jnp.stack(p_kv[1]), jnp.stack(p_kv[2]), jnp.stack(p_ret),
            jnp.stack(s_kv[0]), jnp.stack(s_kv[1]), jnp.stack(s_kv[2]), jnp.stack(s_ret),
            jnp.stack(s_v))
```

```python
import functools
import math

import jax
import jax.numpy as jnp
from jax import lax
from jax.experimental import pallas as pl
from jax.experimental.pallas import tpu as pltpu

F32 = jnp.float32
BF16 = jnp.bfloat16

PAST_LEN = 16384
EPS = 1e-6
NEG_INF = -1e30
ROPE_THETA = 10000.0

D_MODEL = 1024
LANE = 128
SUBLANE = 8
BF16_ROWS = 2 * SUBLANE
CHUNK = 128
TILE = 256
COL = 512
A_GROUPS = 4
SWA_PATTERNS = ((128, 1), (512, 4), (2048, 16))
SWA_HEADS = 4
SWA_HEAD_DIM = 128
RET_HEADS = 4
RET_QK_DIM = 64
RET_V_DIM = 128
N_EXPERTS = 8
TOP_K = 2
IN_WIDTH = 10240
N_COL = IN_WIDTH // COL

CB_AU, CB_AV, CB_Q, CB_K, CB_V, CB_RQK, CB_RV, CB_RG, CB_GATE = 0, 1, 2, 5, 8, 11, 12, 13, 14

VMEM_INTERNAL_SCRATCH = 8 * 1024 * 1024
VMEM_BYTES = 64 * 1024 * 1024


def _pick(n, candidates):
    for c in candidates:
        if n % c == 0:
            return c
    raise ValueError(f"no tile in {candidates} divides {n}")


def _params(block_bytes, semantics=None):
    limit = min(int(block_bytes) + VMEM_INTERNAL_SCRATCH, VMEM_BYTES)
    return pltpu.CompilerParams(dimension_semantics=semantics, vmem_limit_bytes=limit)


def _nbytes(shape, dtype):
    return math.prod(shape) * jnp.dtype(dtype).itemsize


def _rms(x):
    return x * lax.rsqrt(jnp.mean(x * x, axis=-1, keepdims=True) + EPS)


def _gelu(x):
    return 0.5 * x * (1.0 + lax.erf(x * (0.5 ** 0.5)))


def _sigmoid(x):
    return 1.0 / (1.0 + jnp.exp(-x))


def _idiv(x, n):
    assert n & (n - 1) == 0
    return x >> (n.bit_length() - 1)


def _imod(x, n):
    assert n & (n - 1) == 0
    return x & (n - 1)


def _dot(a, b):
    return jnp.dot(a, b, preferred_element_type=F32)


LSE_LANES = LANE // SWA_HEADS


def _pack_heads(cols):
    rows = cols[0].shape[0]
    grp = _idiv(lax.broadcasted_iota(jnp.int32, (rows, LANE), 1), LSE_LANES)
    out = jnp.broadcast_to(cols[-1], (rows, LANE))
    for h in range(len(cols) - 2, -1, -1):
        out = jnp.where(grp == h, cols[h], out)
    return out


def _dot_nt(a, b):
    return lax.dot_general(a, b, (((1,), (1,)), ((), ())), preferred_element_type=F32)


def _tile_rows(i, n_prompt_tiles, xa_ref, xb_ref):
    return jnp.where(i < n_prompt_tiles, xa_ref[...], xb_ref[...])


def _x_specs(xa, xb):
    last_a, last_b = xa.shape[0] // TILE - 1, xb.shape[0] // TILE - 1
    return [pl.BlockSpec((TILE, D_MODEL), lambda i, *_: (jnp.minimum(i, last_a), 0)),
            pl.BlockSpec((TILE, D_MODEL), lambda i, *_: (last_b, 0))]


def _inproj_kernel(*refs, n_prompt_tiles, n_cast):
    (xa_ref, xb_ref, g1_ref, w_ref, perm_ref, cosb_ref, sinb_ref, cosc_ref, sinc_ref,
     nvg_ref, qg_ref, kg_ref) = refs[:12]
    cast_in, o_ref, cast_out, h_ref = refs[12:12 + n_cast], refs[12 + n_cast], refs[13 + n_cast:-1], refs[-1]
    i = pl.program_id(0)
    for src, dst in zip(cast_in, cast_out):
        dst[...] = src[...].astype(dst.dtype)
    hn = (_rms(_tile_rows(i, n_prompt_tiles, xa_ref, xb_ref)) * g1_ref[...]).astype(BF16)
    h_ref[0] = hn

    @pl.when(i < n_prompt_tiles)
    def _():
        for k in range(1, len(SWA_PATTERNS)):
            h_ref[k] = _dot(perm_ref[k - 1], hn).astype(BF16)

    @pl.when(i >= n_prompt_tiles)
    def _():
        for k in range(1, len(SWA_PATTERNS)):
            h_ref[k] = hn

    def qk_heads(acc, g, cols0, gain_ref, scale):
        for hh in range(SWA_HEADS):
            cs = slice(hh * LANE, (hh + 1) * LANE)
            y = _rms(acc[:, cs]) * gain_ref[...]
            rot = y * cosb_ref[g] + pltpu.roll(y, LANE // 2, axis=1) * sinb_ref[g]
            o_ref[:, cols0 + hh * LANE:cols0 + (hh + 1) * LANE] = (rot * scale).astype(o_ref.dtype)

    for j in range(N_COL):
        cols = slice(j * COL, (j + 1) * COL)
        g = (j - CB_Q) % len(SWA_PATTERNS) if CB_Q <= j < CB_RQK else 0
        acc = _dot(h_ref[g], w_ref[:, cols])
        if j == CB_AU:
            o_ref[:, cols] = _gelu(acc).astype(o_ref.dtype)
        elif j == CB_AV:
            o_ref[:, cols] = (_rms(_gelu(acc)) * nvg_ref[...]).astype(o_ref.dtype)
        elif CB_Q <= j < CB_K:
            qk_heads(acc, g, j * COL, qg_ref, SWA_HEAD_DIM ** -0.5)
        elif CB_K <= j < CB_V:
            qk_heads(acc, g, j * COL, kg_ref, 1.0)
        elif j < CB_RQK or j == CB_RV:
            o_ref[:, cols] = acc.astype(o_ref.dtype)
        elif j == CB_RQK:
            lane = lax.broadcasted_iota(jnp.int32, (acc.shape[0], LANE), 1)
            first_half = _imod(lane, RET_QK_DIM) < (RET_QK_DIM // 2)
            for tt in range(COL // LANE):
                y = acc[:, tt * LANE:(tt + 1) * LANE]
                partner = jnp.where(first_half,
                                    pltpu.roll(y, LANE - RET_QK_DIM // 2, axis=1),
                                    pltpu.roll(y, RET_QK_DIM // 2, axis=1))
                rot = y * cosc_ref[...] + partner * sinc_ref[...]
                scale = 1.0 if tt < (COL // LANE) // 2 else RET_QK_DIM ** -0.5
                o_ref[:, j * COL + tt * LANE:j * COL + (tt + 1) * LANE] = (rot * scale).astype(o_ref.dtype)
        elif j == CB_RG:
            o_ref[:, cols] = (acc * _sigmoid(acc)).astype(o_ref.dtype)
        else:
            o_ref[:, cols] = _sigmoid(acc).astype(o_ref.dtype)


def _cast_blocks(arr, n_steps, skip):
    a2 = arr.reshape(-1, arr.shape[-1])
    first_row = skip * (a2.shape[0] // arr.shape[0])
    rows = a2.shape[0] - first_row
    rb = -(-rows // n_steps)
    rb = -(-rb // BF16_ROWS) * BF16_ROWS
    while rows % rb or first_row % rb:
        rb += BF16_ROWS
    return a2, rb, first_row // rb


def _inproj(xa, xb, g1, w_in_b, perm, tabs, nvg, qg, kg, n_prompt_tiles, tiles_per_seq, to_cast=()):
    T = (n_prompt_tiles + 1) * TILE
    n_steps = T // TILE
    casts = [_cast_blocks(a, n_steps, skip) for a, skip in to_cast]
    n_blk = [a2.shape[0] // rb - first for a2, rb, first in casts]
    cast_in = [pl.BlockSpec((rb, a2.shape[1]), functools.partial(
        lambda i, first, last: (first + jnp.minimum(i, last), 0), first=first, last=n - 1))
        for (a2, rb, first), n in zip(casts, n_blk)]
    cast_out = [pl.BlockSpec((rb, a2.shape[1]), functools.partial(
        lambda i, last: (jnp.minimum(i, last), 0), last=n - 1)) for (a2, rb, _), n in zip(casts, n_blk)]
    cosb, sinb, cosc, sinc = tabs
    n_g = len(SWA_PATTERNS)
    row = lambda i: (i, 0)
    const = lambda i: (0, 0)
    tab_blk = lambda i: jnp.where(i < n_prompt_tiles, i % tiles_per_seq, tiles_per_seq)
    tab3 = pl.BlockSpec((n_g, TILE, LANE), lambda i: (0, tab_blk(i), 0))
    tab = pl.BlockSpec((TILE, LANE), lambda i: (tab_blk(i), 0))
    vmem = (2 * _nbytes((TILE, D_MODEL), F32) + n_g * _nbytes((TILE, D_MODEL), BF16)
            + _nbytes(w_in_b.shape, BF16) + 2 * _nbytes((TILE, IN_WIDTH), BF16)
            + 2 * (2 * n_g + 2) * _nbytes((TILE, LANE), F32) + 2 * _nbytes(perm.shape, BF16)
            + 8 * _nbytes((TILE, COL), F32)
            + sum(2 * (_nbytes((rb, a2.shape[1]), F32) + _nbytes((rb, a2.shape[1]), BF16)) for a2, rb, _ in casts))
    outs = pl.pallas_call(
        functools.partial(_inproj_kernel, n_prompt_tiles=n_prompt_tiles, n_cast=len(casts)),
        grid=(n_steps,),
        in_specs=_x_specs(xa, xb) + [
            pl.BlockSpec((1, D_MODEL), const),
            pl.BlockSpec(w_in_b.shape, const, pipeline_mode=pl.Buffered(1)),
            pl.BlockSpec(perm.shape, lambda i: (0, 0, 0)),
            tab3, tab3, tab, tab,
            pl.BlockSpec((1, COL), const),
            pl.BlockSpec((1, LANE), const),
            pl.BlockSpec((1, LANE), const),
        ] + cast_in,
        out_specs=[pl.BlockSpec((TILE, IN_WIDTH), row)] + cast_out,
        out_shape=[jax.ShapeDtypeStruct((T, IN_WIDTH), BF16)]
                  + [jax.ShapeDtypeStruct((n * rb, a2.shape[1]), BF16) for (a2, rb, _), n in zip(casts, n_blk)],
        scratch_shapes=[pltpu.VMEM((n_g, TILE, D_MODEL), BF16)],
        compiler_params=_params(vmem, ("arbitrary",)),
        name="inproj",
    )(xa, xb, g1, w_in_b, perm, cosb, sinb, cosc, sinc, nvg, qg, kg, *[a2 for a2, _, _ in casts])
    return outs[0], [o.reshape((a.shape[0] - skip,) + a.shape[1:]) for o, (a, skip) in zip(outs[1:], to_cast)]


def _mixa_kernel(u_ref, v_ref, w_ref, b_ref, o_ref, *, cps, n_prompt_chunks):
    i = pl.program_id(0)
    for c in range(cps):
        var = ((i * cps + c) >= n_prompt_chunks).astype(jnp.int32)
        rows = slice(c * CHUNK, (c + 1) * CHUNK)
        for g in range(A_GROUPS):
            cols = slice(g * LANE, (g + 1) * LANE)
            z = _dot(w_ref[var, g], v_ref[rows, cols]) + b_ref[var, :, cols]
            o_ref[rows, cols] = (u_ref[rows, cols].astype(F32) * z).astype(o_ref.dtype)


def _mixer_a(P, w2, b2, n_prompt_chunks):
    T = P.shape[0]
    n_chunks = T // CHUNK
    cps = _pick(n_chunks, (8, 6, 4, 3, 2, 1))
    rows = cps * CHUNK
    vmem = 6 * _nbytes((rows, COL), BF16) + 2 * _nbytes(w2.shape, BF16) + 2 * _nbytes(b2.shape, F32)
    return pl.pallas_call(
        functools.partial(_mixa_kernel, cps=cps, n_prompt_chunks=n_prompt_chunks),
        grid=(n_chunks // cps,),
        in_specs=[
            pl.BlockSpec((rows, COL), lambda i: (i, CB_AU)),
            pl.BlockSpec((rows, COL), lambda i: (i, CB_AV)),
            pl.BlockSpec(w2.shape, lambda i: (0, 0, 0, 0)),
            pl.BlockSpec(b2.shape, lambda i: (0, 0, 0)),
        ],
        out_specs=pl.BlockSpec((rows, COL), lambda i: (i, 0)),
        out_shape=jax.ShapeDtypeStruct((T, COL), BF16),
        compiler_params=_params(vmem, ("parallel",)),
        name="mixer_a",
    )(P, P, w2, b2)


def _swa_kernel(q_ref, kp_ref, kc_ref, vp_ref, vc_ref, o_ref, l_ref, k_ref, v_ref, s_ref, p_ref, *, qb):
    i = pl.program_id(2)
    rows = qb * CHUNK
    lead = q_ref.shape[:-1]
    k_ref[0:CHUNK] = kp_ref[...].reshape(CHUNK, COL)
    k_ref[CHUNK:CHUNK + rows] = kc_ref[...].reshape(rows, COL)
    v_ref[0:CHUNK] = vp_ref[...].reshape(CHUNK, COL)
    v_ref[CHUNK:CHUNK + rows] = vc_ref[...].reshape(rows, COL)
    row = lax.broadcasted_iota(jnp.int32, (CHUNK, 2 * CHUNK), 0)
    col = lax.broadcasted_iota(jnp.int32, (CHUNK, 2 * CHUNK), 1)
    mask_cur = (col >= CHUNK) & (col - CHUNK <= row)
    mask_all = mask_cur | ((col < CHUNK) & (col >= row))
    mask_first = mask_cur | ((col < CHUNK) & (col >= row) & (i > 0))
    q_all = q_ref[...].reshape(rows, COL)
    for j in range(qb):
        mask = mask_first if j == 0 else mask_all
        for h in range(SWA_HEADS):
            cs = slice(h * LANE, (h + 1) * LANE)
            sc = _dot_nt(q_all[j * CHUNK:(j + 1) * CHUNK, cs], k_ref[j * CHUNK:(j + 2) * CHUNK, cs])
            s_ref[j * SWA_HEADS + h] = jnp.where(mask, sc, NEG_INF)
    s = s_ref[...]
    m = jnp.max(s, axis=-1, keepdims=True)
    p = jnp.exp(s - m)
    den = jnp.sum(p, axis=-1, keepdims=True)
    p_ref[...] = p.astype(BF16)
    lse = m + jnp.log(den)
    for h in range(SWA_HEADS):
        cs = slice(h * LANE, (h + 1) * LANE)
        o_h = [_dot(p_ref[j * SWA_HEADS + h], v_ref[j * CHUNK:(j + 2) * CHUNK, cs]) / den[j * SWA_HEADS + h]
               for j in range(qb)]
        o_ref[..., cs] = jnp.concatenate(o_h, axis=0).astype(o_ref.dtype).reshape(lead + (LANE,))
    packed = [_pack_heads([lse[j * SWA_HEADS + h] for h in range(SWA_HEADS)]) for j in range(qb)]
    l_ref[...] = jnp.concatenate(packed, axis=0).reshape(lead + (LANE,))


def _swa_prompt(P, g, dil, n_p, s):
    T = P.shape[0]
    Tp = n_p * s
    nb = s // dil // CHUNK
    qb = _pick(nb, (8, 4, 2, 1))
    steps = nb // qb
    prev_blk = lambda b, i: b * nb + jnp.maximum(i * qb - 1, 0)
    if dil == 1:
        src = P
        lead_q, lead_p = (qb * CHUNK,), (CHUNK,)
        o_shape = (Tp, COL)
        l_shape = (Tp, LANE)
        q_map = lambda cb: (lambda b, r, i: (b * steps + i, cb + g))
        p_map = lambda cb: (lambda b, r, i: (prev_blk(b, i), cb + g))
        o_map = lambda b, r, i: (b * steps + i, 0)
        l_map = lambda b, r, i: (b * steps + i, 0)
    else:
        rpt = TILE // dil
        tpb = CHUNK // rpt
        src = P.reshape(T // TILE, dil, rpt, IN_WIDTH)
        lead_q, lead_p = (qb * tpb, None, rpt), (tpb, None, rpt)
        o_shape = (Tp // TILE, dil, rpt, COL)
        l_shape = (Tp // TILE, dil, rpt, LANE)
        q_map = lambda cb: (lambda b, r, i: (b * steps + i, r, 0, cb + g))
        p_map = lambda cb: (lambda b, r, i: (prev_blk(b, i), r, 0, cb + g))
        o_map = lambda b, r, i: (b * steps + i, r, 0, 0)
        l_map = lambda b, r, i: (b * steps + i, r, 0, 0)

    q_spec = lambda cb: pl.BlockSpec(lead_q + (COL,), q_map(cb))
    p_spec = lambda cb: pl.BlockSpec(lead_p + (COL,), p_map(cb))
    n_pairs = qb * SWA_HEADS
    vmem = ((10 * qb + 6) * _nbytes((CHUNK, COL), BF16) + 2 * qb * _nbytes((CHUNK, COL), F32)
            + 6 * n_pairs * _nbytes((CHUNK, 2 * CHUNK), F32))
    o, l = pl.pallas_call(
        functools.partial(_swa_kernel, qb=qb),
        grid=(n_p, dil, steps),
        in_specs=[q_spec(CB_Q), p_spec(CB_K), q_spec(CB_K), p_spec(CB_V), q_spec(CB_V)],
        out_specs=[pl.BlockSpec(lead_q + (COL,), o_map),
                   pl.BlockSpec(lead_q + (LANE,), l_map)],
        out_shape=[jax.ShapeDtypeStruct(o_shape, BF16), jax.ShapeDtypeStruct(l_shape, F32)],
        scratch_shapes=[pltpu.VMEM(((qb + 1) * CHUNK, COL), BF16), pltpu.VMEM(((qb + 1) * CHUNK, COL), BF16),
                        pltpu.VMEM((n_pairs, CHUNK, 2 * CHUNK), F32),
                        pltpu.VMEM((n_pairs, CHUNK, 2 * CHUNK), BF16)],
        compiler_params=_params(vmem, ("parallel", "parallel", "arbitrary")),
        name=f"swa_prompt_g{g}",
    )(src, src, src, src, src)
    return o.reshape(Tp, COL), l.reshape(Tp, LANE)


def _swa_sample_kernel(q_ref, kn_ref, vn_ref, cache_ref, o_ref, l_ref, *, dil, lbuf, t_s):
    nq = SWA_HEADS * t_s
    q = q_ref[...]
    qrep = jnp.concatenate([q] * SWA_HEADS, axis=0)
    rq = lax.broadcasted_iota(jnp.int32, (nq, COL), 0)
    cq = lax.broadcasted_iota(jnp.int32, (nq, COL), 1)
    qbd = jnp.where(_idiv(rq, t_s) == _idiv(cq, LANE), qrep, 0.0).astype(BF16)

    per_pos = 2 * SWA_HEADS
    if cache_ref.ndim == 2:
        n_keys = lbuf
        rows_of = lambda first: cache_ref[pl.ds(first, lbuf, stride=per_pos), :]
        key_pos = lambda r: r
    else:
        n_keys = cache_ref.shape[0] * t_s
        rows_of = lambda first: cache_ref[:, pl.ds(first, t_s, stride=per_pos), :].reshape(n_keys, LANE)
        key_pos = lambda r: dil * _idiv(r, t_s) + _imod(r, t_s)
    kc = jnp.concatenate([rows_of(h).astype(BF16) for h in range(SWA_HEADS)], axis=1)
    vc = jnp.concatenate([rows_of(SWA_HEADS + h).astype(BF16) for h in range(SWA_HEADS)], axis=1)
    kn = kn_ref[...].astype(BF16)
    vn = vn_ref[...].astype(BF16)

    s_c = _dot_nt(qbd, kc)
    s_n = _dot_nt(qbd, kn)
    t_c = _imod(lax.broadcasted_iota(jnp.int32, (nq, n_keys), 0), t_s)
    c_c = key_pos(lax.broadcasted_iota(jnp.int32, (nq, n_keys), 1))
    diff_c = lbuf + t_c - c_c
    ok_c = (_imod(diff_c, dil) == 0) & (diff_c <= lbuf)
    t_n = _imod(lax.broadcasted_iota(jnp.int32, (nq, t_s), 0), t_s)
    c_n = lax.broadcasted_iota(jnp.int32, (nq, t_s), 1)
    diff_n = t_n - c_n
    ok_n = (diff_n >= 0) & (_imod(diff_n, dil) == 0)
    s_c = jnp.where(ok_c, s_c, NEG_INF)
    s_n = jnp.where(ok_n, s_n, NEG_INF)
    m = jnp.maximum(jnp.max(s_c, axis=-1, keepdims=True), jnp.max(s_n, axis=-1, keepdims=True))
    p_c = jnp.exp(s_c - m)
    p_n = jnp.exp(s_n - m)
    den = jnp.sum(p_c, axis=-1, keepdims=True) + jnp.sum(p_n, axis=-1, keepdims=True)
    o_all = (_dot(p_c.astype(BF16), vc) + _dot(p_n.astype(BF16), vn)) / den
    lse = m + jnp.log(den)
    for h in range(SWA_HEADS):
        cs = slice(h * LANE, (h + 1) * LANE)
        o_ref[:, cs] = o_all[h * t_s:(h + 1) * t_s, cs]
    l_ref[...] = _pack_heads([lse[h * t_s:(h + 1) * t_s] for h in range(SWA_HEADS)])


def _swa_sample(Ps, cache, layer, g, dil):
    n_s, t_s, _ = Ps.shape
    rows = cache.shape[2]
    lbuf = rows // (2 * SWA_HEADS)
    assert lbuf == dil * CHUNK, "window buffer must hold exactly one full window"
    blk = lambda cb: pl.BlockSpec((None, t_s, COL), lambda b: (b, 0, cb + g))
    if dil > t_s:
        per_pos = rows // lbuf
        cache = cache.reshape(cache.shape[0], n_s, lbuf // dil, dil * per_pos, LANE)
        cache_spec = pl.BlockSpec((None, None, lbuf // dil, t_s * per_pos, LANE), lambda b: (layer, b, 0, 0, 0))
    else:
        cache_spec = pl.BlockSpec((None, None, rows, LANE), lambda b: (layer, b, 0, 0))
    vmem = (2 * _nbytes((lbuf, 2 * COL), F32) + 2 * _nbytes((lbuf, 2 * COL), BF16)
            + 8 * _nbytes((SWA_HEADS * t_s, lbuf), F32))
    return pl.pallas_call(
        functools.partial(_swa_sample_kernel, dil=dil, lbuf=lbuf, t_s=t_s),
        grid=(n_s,),
        in_specs=[blk(CB_Q), blk(CB_K), blk(CB_V), cache_spec],
        out_specs=[pl.BlockSpec((None, t_s, COL), lambda b: (b, 0, 0)),
                   pl.BlockSpec((None, t_s, LANE), lambda b: (b, 0, 0))],
        out_shape=[jax.ShapeDtypeStruct((n_s, t_s, COL), F32),
                   jax.ShapeDtypeStruct((n_s, t_s, LANE), F32)],
        compiler_params=_params(vmem, ("parallel",)),
        name=f"swa_sample_g{g}",
    )(Ps, Ps, Ps, cache)


def _kv_tail_kernel(*refs, dil, skip, n_layers):
    srcs, o_ref = refs[:2 * n_layers], refs[2 * n_layers]
    layer = pl.program_id(0)
    rpt = TILE // dil
    per_pos = 2 * SWA_HEADS

    def body(k_ref, v_ref):
        for kv, ref in enumerate((k_ref, v_ref)):
            for h in range(SWA_HEADS):
                for r in range(dil):
                    rows = ref[r * rpt + skip:(r + 1) * rpt, h * LANE:(h + 1) * LANE].astype(o_ref.dtype)
                    first = per_pos * r + kv * SWA_HEADS + h
                    o_ref[pl.ds(first, rpt - skip, stride=per_pos * dil), :] = rows

    for l in range(n_layers):
        pl.when(layer == l)(functools.partial(body, srcs[2 * l], srcs[2 * l + 1]))


def _kv_tail(Ps, g, win, dil, n_p, s):
    n_layers = len(Ps)
    keep = -(-win // TILE) * TILE
    skip = keep - win
    assert skip == 0 or dil == 1
    n_src = keep // TILE
    tiles_per_seq = s // TILE
    out_rows = (TILE - skip) * 2 * SWA_HEADS

    def src_spec(l, cb):
        def imap(layer, b, j):
            here = layer == l
            return (jnp.where(here, (b + 1) * tiles_per_seq - n_src + j, 0), cb + g)
        return pl.BlockSpec((TILE, COL), imap)

    specs = [src_spec(l, cb) for l in range(n_layers) for cb in (CB_K, CB_V)]
    vmem = 4 * n_layers * _nbytes((TILE, COL), BF16) + 2 * _nbytes((out_rows, LANE), F32)
    out = pl.pallas_call(
        functools.partial(_kv_tail_kernel, dil=dil, skip=skip, n_layers=n_layers),
        grid=(n_layers, n_p, n_src),
        in_specs=specs,
        out_specs=pl.BlockSpec((out_rows, LANE), lambda layer, b, j: ((layer * n_p + b) * n_src + j, 0)),
        out_shape=jax.ShapeDtypeStruct((n_layers * n_p * n_src * out_rows, LANE), F32),
        compiler_params=_params(vmem, ("parallel", "parallel", "parallel")),
        name=f"kv_tail_g{g}",
    )(*[P for P in Ps for _ in range(2)])
    return out.reshape(n_layers, n_p, win, 2, SWA_HEADS, SWA_HEAD_DIM)


def _ret_head_inputs(qk_ref, v_ref, kout_ref, h):
    pair, half = h // 2, h % 2
    lane = lax.broadcasted_iota(jnp.int32, (CHUNK, LANE), 1)
    head_lanes = _idiv(lane, RET_QK_DIM) == half
    qt = qk_ref[:, pair * LANE:(pair + 1) * LANE]
    kt = qk_ref[:, COL // 2 + pair * LANE:COL // 2 + (pair + 1) * LANE]
    qm = jnp.where(head_lanes, qt, jnp.zeros_like(qt))
    kw = jnp.where(head_lanes, kt.astype(F32) * kout_ref[:, pair * LANE:(pair + 1) * LANE], 0.0)
    vh = v_ref[:, h * LANE:(h + 1) * LANE]
    return qm, kt, kw, vh


def _ret_finish(o, gate_ref, o_ref, h):
    cs = slice(h * LANE, (h + 1) * LANE)
    o_ref[:, cs] = (gate_ref[:, cs].astype(F32) * _rms(o)).astype(o_ref.dtype)


def _ret_prompt_kernel(*refs, n_seq):
    ins = refs[:3 * n_seq]
    decay_ref, qin_ref, kout_ref, gc_ref, o_ref, s_out_ref, s_ref, att_ref = refs[3 * n_seq:]
    i = pl.program_id(0)

    @pl.when(i == 0)
    def _():
        s_ref[...] = jnp.zeros_like(s_ref)

    pairs = [(b, h) for b in range(n_seq) for h in range(RET_HEADS)]
    head_in = lambda b, h: _ret_head_inputs(ins[3 * b], ins[3 * b + 1], kout_ref, h)
    for b, h in pairs:
        qm, kt, _, _ = head_in(b, h)
        att_ref[b, h] = (_dot_nt(qm, kt) * decay_ref[h]).astype(BF16)
    for b, h in pairs:
        qm, _, _, vh = head_in(b, h)
        o = (_dot(att_ref[b, h], vh)
             + _dot(qm, s_ref[b, h].astype(BF16)) * qin_ref[:, h * LANE:(h + 1) * LANE])
        _ret_finish(o, ins[3 * b + 2], o_ref.at[b], h)
    for b, h in pairs:
        _, _, kw, vh = head_in(b, h)
        s_ref[b, h] = s_ref[b, h] * gc_ref[h:h + 1, :] + _dot(kw.T.astype(BF16), vh)

    @pl.when(i == pl.num_programs(0) - 1)
    def _():
        for b in range(n_seq):
            for h in range(RET_HEADS):
                lo = (h % 2) * RET_QK_DIM
                s_out_ref[b, h] = s_ref[b, h, lo:lo + RET_QK_DIM, :]


def _ret_prompt(P, rt, n_p, s):
    nblk = s // CHUNK
    decay, qin, kout, gc = rt
    blk = lambda b, cb: pl.BlockSpec((CHUNK, COL), lambda i: (b * nblk + i, cb))
    const2 = lambda i: (0, 0)
    seq_specs = [blk(b, cb) for b in range(n_p) for cb in (CB_RQK, CB_RV, CB_RG)]
    st_shape = (n_p, RET_HEADS, RET_QK_DIM, RET_V_DIM)
    vmem = (8 * n_p * _nbytes((CHUNK, COL), BF16) + 2 * _nbytes(decay.shape, F32) + 4 * _nbytes(qin.shape, F32)
            + 3 * n_p * _nbytes((RET_HEADS, LANE, LANE), F32))
    o, st = pl.pallas_call(
        functools.partial(_ret_prompt_kernel, n_seq=n_p),
        grid=(nblk,),
        in_specs=seq_specs + [pl.BlockSpec(decay.shape, lambda i: (0, 0, 0)),
                              pl.BlockSpec(qin.shape, const2), pl.BlockSpec(kout.shape, const2),
                              pl.BlockSpec(gc.shape, const2)],
        out_specs=[pl.BlockSpec((n_p, CHUNK, COL), lambda i: (0, i, 0)),
                   pl.BlockSpec(st_shape, lambda i: (0, 0, 0, 0))],
        out_shape=[jax.ShapeDtypeStruct((n_p, s, COL), BF16), jax.ShapeDtypeStruct(st_shape, F32)],
        scratch_shapes=[pltpu.VMEM((n_p, RET_HEADS, LANE, LANE), F32),
                        pltpu.VMEM((n_p, RET_HEADS, CHUNK, CHUNK), BF16)],
        compiler_params=_params(vmem, ("arbitrary",)),
        name="ret_prompt",
    )(*([P] * (3 * n_p)), decay, qin, kout, gc)
    return o.reshape(n_p * s, COL), st


def _ret_sample_kernel(qk_ref, v_ref, gate_ref, s0_ref, decay_ref, qin_ref, kout_ref, gc_ref,
                       o_ref, s_out_ref, *, t_s):
    row = lax.broadcasted_iota(jnp.int32, (CHUNK, LANE), 0)
    for h in range(RET_HEADS):
        lo = (h % 2) * RET_QK_DIM
        qm, kt, kw, vh = _ret_head_inputs(qk_ref, v_ref, kout_ref, h)
        att = _dot_nt(qm, kt) * decay_ref[h]
        o = _dot(att.astype(BF16), vh)
        inter = jnp.zeros((CHUNK, LANE), F32)
        for sq in range(CHUNK // t_s):
            seq_rows = _idiv(row, t_s) == sq
            st = s0_ref[sq, h]
            st2 = jnp.concatenate([st, st], axis=0).astype(BF16)
            inter = jnp.where(seq_rows, _dot(qm, st2), inter)
            upd = _dot(jnp.where(seq_rows, kw, 0.0).T.astype(BF16), vh)
            s_out_ref[sq, h] = st * gc_ref[h:h + 1, :] + upd[lo:lo + RET_QK_DIM, :]
        o = o + inter * qin_ref[:, h * LANE:(h + 1) * LANE]
        _ret_finish(o, gate_ref, o_ref, h)


def _ret_sample(P, state, layer, rt, Tp, t_s):
    n_s = state.shape[1]
    spb = CHUNK // t_s
    base = Tp // CHUNK
    decay, qin, kout, gc = rt
    blk = lambda cb: pl.BlockSpec((CHUNK, COL), lambda i: (base + i, cb))
    const2 = lambda i: (0, 0)
    st_shape = (spb, RET_HEADS, RET_QK_DIM, RET_V_DIM)
    vmem = (8 * _nbytes((CHUNK, COL), BF16) + 2 * _nbytes(decay.shape, F32) + 4 * _nbytes(qin.shape, F32)
            + 4 * _nbytes(st_shape, F32))
    return pl.pallas_call(
        functools.partial(_ret_sample_kernel, t_s=t_s),
        grid=(n_s // spb,),
        in_specs=[blk(CB_RQK), blk(CB_RV), blk(CB_RG),
                  pl.BlockSpec((None,) + st_shape, lambda i: (layer, i, 0, 0, 0)),
                  pl.BlockSpec(decay.shape, lambda i: (0, 0, 0)),
                  pl.BlockSpec(qin.shape, const2), pl.BlockSpec(kout.shape, const2),
                  pl.BlockSpec(gc.shape, const2)],
        out_specs=[pl.BlockSpec((CHUNK, COL), lambda i: (i, 0)),
                   pl.BlockSpec(st_shape, lambda i: (i, 0, 0, 0))],
        out_shape=[jax.ShapeDtypeStruct((n_s * t_s, COL), BF16),
                   jax.ShapeDtypeStruct((n_s, RET_HEADS, RET_QK_DIM, RET_V_DIM), F32)],
        compiler_params=_params(vmem, ("parallel",)),
        name="ret_sample",
    )(P, P, P, state, decay, qin, kout, gc)


def _branch_kernel(xa_ref, xb_ref, a_ref, g0_ref, g1_ref, g2_ref, wb_ref, wo_ref, ex_ref,
                   po0, po1, po2, pl0, pl1, pl2, pc, so0, so1, so2, sl0, sl1, sl2, sc,
                   y_ref, on_ref, ln_ref, *, n_prompt_tiles):
    i = pl.program_id(0)

    def spread(packed):
        hi = packed.astype(BF16)
        lo = (packed - hi.astype(F32)).astype(BF16)
        return _dot(jnp.concatenate([hi, lo], axis=1), ex_ref[...])

    def natural(o_ref, l_ref, k, dil):
        if dil == 1:
            return (lambda h: o_ref[:, h * LANE:(h + 1) * LANE].astype(F32)), l_ref[...]
        rpt = TILE // dil
        for r in range(dil):
            rows = slice(r * rpt, (r + 1) * rpt)
            ln_ref[k, pl.ds(r, rpt, stride=dil), :] = l_ref[rows, :]
            for h in range(SWA_HEADS):
                on_ref[k, h, pl.ds(r, rpt, stride=dil), :] = o_ref[rows, h * LANE:(h + 1) * LANE].astype(F32)
        return (lambda h: on_ref[k, h]), ln_ref[k]

    def body(o_refs, l_refs, c_ref, permuted):
        groups = [natural(o_refs[k], l_refs[k], k, SWA_PATTERNS[k][1] if permuted else 1)
                  for k in range(len(SWA_PATTERNS))]
        ls = [l for _, l in groups]
        lmax = functools.reduce(jnp.maximum, ls)
        es = [jnp.exp(l - lmax) for l in ls]
        inv = 1.0 / sum(es)
        weights = [spread(e * inv) for e in es]
        heads = []
        for h in range(SWA_HEADS):
            cs = slice(h * LANE, (h + 1) * LANE)
            heads.append(sum(w[:, cs] * go(h) for w, (go, _) in zip(weights, groups)).astype(BF16))
        mix = jnp.concatenate(heads, axis=1)
        merged = (g0_ref[...].astype(F32) * _dot(a_ref[...], wb_ref[0])
                  + g1_ref[...].astype(F32) * _dot(mix, wb_ref[1])
                  + g2_ref[...].astype(F32) * _dot(c_ref[...], wb_ref[2]))
        y_ref[...] = _tile_rows(i, n_prompt_tiles, xa_ref, xb_ref) + _dot(merged.astype(BF16), wo_ref[...])

    @pl.when(i < n_prompt_tiles)
    def _():
        body((po0, po1, po2), (pl0, pl1, pl2), pc, True)

    @pl.when(i >= n_prompt_tiles)
    def _():
        body((so0, so1, so2), (sl0, sl1, sl2), sc, False)


def _branch(xa, xb, out_a, P, wb, wo, prompt_set, sample_set, n_prompt_tiles):
    T = (n_prompt_tiles + 1) * TILE
    assert sample_set[0].shape[0] == TILE, "the sample rows must form exactly one tile"
    n_g = len(SWA_PATTERNS)
    last = n_prompt_tiles - 1
    row = lambda i: (i, 0)
    gate = lambda k: pl.BlockSpec((TILE, D_MODEL), lambda i: (i, CB_GATE * COL // D_MODEL + k))
    once = pl.Buffered(1)
    p_o = pl.BlockSpec((TILE, COL), lambda i: (jnp.minimum(i, last), 0))
    p_l = pl.BlockSpec((TILE, LANE), lambda i: (jnp.minimum(i, last), 0))
    s_o = pl.BlockSpec((TILE, COL), lambda i: (0, 0))
    s_l = pl.BlockSpec((TILE, LANE), lambda i: (0, 0))
    src = jnp.arange(2 * LANE) % LANE
    dst_head = jnp.arange(COL) // LANE
    expand = (src[:, None] == dst_head[None, :] * LSE_LANES).astype(BF16)
    vmem = (4 * _nbytes((TILE, D_MODEL), F32) + 6 * _nbytes((TILE, D_MODEL), BF16)
            + 2 * (2 * n_g + 2) * (_nbytes((TILE, COL), BF16) + _nbytes((TILE, COL), F32))
            + _nbytes(wb.shape, BF16) + _nbytes(wo.shape, BF16)
            + 4 * n_g * _nbytes((TILE, COL), F32) + 4 * _nbytes((TILE, D_MODEL), F32))
    return pl.pallas_call(
        functools.partial(_branch_kernel, n_prompt_tiles=n_prompt_tiles),
        grid=(T // TILE,),
        in_specs=_x_specs(xa, xb) + [
                  pl.BlockSpec((TILE, COL), row),
                  gate(0), gate(1), gate(2),
                  pl.BlockSpec(wb.shape, lambda i: (0, 0, 0), pipeline_mode=once),
                  pl.BlockSpec(wo.shape, lambda i: (0, 0), pipeline_mode=once),
                  pl.BlockSpec(expand.shape, lambda i: (0, 0)),
                  p_o, p_o, p_o, p_l, p_l, p_l, p_o,
                  s_o, s_o, s_o, s_l, s_l, s_l, s_o],
        out_specs=pl.BlockSpec((TILE, D_MODEL), row),
        out_shape=jax.ShapeDtypeStruct((T, D_MODEL), F32),
        scratch_shapes=[pltpu.VMEM((n_g, SWA_HEADS, TILE, LANE), F32),
                        pltpu.VMEM((n_g, TILE, LANE), F32)],
        compiler_params=_params(vmem, ("parallel",)),
        name="branch_merge",
    )(xa, xb, out_a, P, P, P, wb, wo, expand, *prompt_set, *sample_set)


def _swiglu_into(hb, w1_ref, w3_ref, w2_ref, acc_ref, fc):
    n_fc = w1_ref.shape[-1] // fc
    for f in range(n_fc):
        cols = slice(f * fc, (f + 1) * fc)
        a = _dot(hb, w1_ref[:, cols])
        b = _dot(hb, w3_ref[:, cols])
        part = _dot((a * _sigmoid(a) * b).astype(BF16), w2_ref[cols, :])
        if f == 0:
            acc_ref[...] = part
        else:
            acc_ref[...] += part


def _dense_ffn_kernel(x_ref, g2_ref, w1_ref, w3_ref, w2_ref, y_ref, acc_ref, *, fc):
    x = x_ref[...]
    hb = (_rms(x) * g2_ref[...]).astype(BF16)
    _swiglu_into(hb, w1_ref, w3_ref, w2_ref, acc_ref, fc)
    y_ref[...] = x + acc_ref[...]


def _dense_ffn(x, g2, w1b, w3b, w2b):
    T = x.shape[0]
    F = w1b.shape[1]
    tm = _pick(T, (768, 512, 384, 256, 128))
    fc = _pick(F, (512, 256, 128))
    row = lambda i: (i, 0)
    once = pl.Buffered(1)
    vmem = (5 * _nbytes((tm, D_MODEL), F32) + 3 * _nbytes(w1b.shape, BF16) + _nbytes((tm, D_MODEL), BF16)
            + 4 * _nbytes((tm, fc), F32))
    return pl.pallas_call(
        functools.partial(_dense_ffn_kernel, fc=fc),
        grid=(T // tm,),
        in_specs=[pl.BlockSpec((tm, D_MODEL), row),
                  pl.BlockSpec((1, D_MODEL), lambda i: (0, 0)),
                  pl.BlockSpec(w1b.shape, lambda i: (0, 0), pipeline_mode=once),
                  pl.BlockSpec(w3b.shape, lambda i: (0, 0), pipeline_mode=once),
                  pl.BlockSpec(w2b.shape, lambda i: (0, 0), pipeline_mode=once)],
        out_specs=pl.BlockSpec((tm, D_MODEL), row),
        out_shape=jax.ShapeDtypeStruct((T, D_MODEL), F32),
        scratch_shapes=[pltpu.VMEM((tm, D_MODEL), F32)],
        compiler_params=_params(vmem, ("parallel",)),
        name="dense_ffn",
    )(x, g2, w1b, w3b, w2b)


def _router_kernel(x_ref, g2_ref, wr_ref, e0_ref, e1_ref, w0_ref, w1_ref):
    hb = (_rms(x_ref[...]) * g2_ref[...]).astype(BF16)
    logits = _dot(hb, wr_ref[...])
    lane = lax.broadcasted_iota(jnp.int32, logits.shape, 1)
    logits = jnp.where(lane < N_EXPERTS, logits, -jnp.inf)
    lane_f = lane.astype(F32)
    m1 = jnp.max(logits, axis=-1, keepdims=True)
    i1 = jnp.min(jnp.where(logits == m1, lane_f, float(LANE)), axis=-1, keepdims=True)
    rest = jnp.where(lane_f == i1, -jnp.inf, logits)
    m2 = jnp.max(rest, axis=-1, keepdims=True)
    i2 = jnp.min(jnp.where(rest == m2, lane_f, float(LANE)), axis=-1, keepdims=True)
    e = jnp.exp(m2 - m1)
    e0_ref[...] = jnp.broadcast_to(i1, logits.shape).astype(jnp.int32)
    e1_ref[...] = jnp.broadcast_to(i2, logits.shape).astype(jnp.int32)
    w0_ref[...] = jnp.broadcast_to(1.0 / (1.0 + e), logits.shape)
    w1_ref[...] = jnp.broadcast_to(e / (1.0 + e), logits.shape)


def _router(x, g2, wr_pad):
    T = x.shape[0]
    tm = _pick(T, (768, 512, 384, 256, 128))
    row = lambda i: (i, 0)
    out = pl.BlockSpec((tm, LANE), row)
    vmem = 4 * _nbytes((tm, D_MODEL), F32) + 2 * _nbytes(wr_pad.shape, BF16) + 16 * _nbytes((tm, LANE), F32)
    return pl.pallas_call(
        _router_kernel,
        grid=(T // tm,),
        in_specs=[pl.BlockSpec((tm, D_MODEL), row),
                  pl.BlockSpec((1, D_MODEL), lambda i: (0, 0)),
                  pl.BlockSpec(wr_pad.shape, lambda i: (0, 0))],
        out_specs=[out, out, out, out],
        out_shape=[jax.ShapeDtypeStruct((T, LANE), jnp.int32), jax.ShapeDtypeStruct((T, LANE), jnp.int32),
                   jax.ShapeDtypeStruct((T, LANE), F32), jax.ShapeDtypeStruct((T, LANE), F32)],
        compiler_params=_params(vmem, ("parallel",)),
        name="moe_router",
    )(x, g2, wr_pad)


RUN_ALIGN = SUBLANE


def _run_pieces(count, max_rows):
    pieces, off = [], 0
    size = max_rows
    while size >= RUN_ALIGN:
        active = (count & size) != 0
        pieces.append((size, off, active))
        off = off + jnp.where(active, size, 0)
        size //= 2
    return pieces


def _packed_rows(i, e0, e1, lower_ref, loc_ref):
    lane = lax.broadcasted_iota(jnp.int32, e0.shape, 1)
    picks = jnp.where((lane == e0) | (lane == e1), 1.0, 0.0)
    rank = _dot(lower_ref[...], picks.astype(BF16))
    lane1 = lax.broadcasted_iota(jnp.int32, (1, LANE), 1)
    run_start = jnp.zeros((1, LANE), F32)
    for e in range(N_EXPERTS):
        run_start = jnp.where(lane1 == e, loc_ref[i * N_EXPERTS + e].astype(F32), run_start)
    row_of = rank + run_start
    return tuple(jnp.sum(jnp.where(lane == ek, row_of, 0.0), axis=-1, keepdims=True) for ek in (e0, e1))


DISPATCH_BUFFERS = 3
RUN_ROWS = 640


def _dispatch_kernel(base_ref, cnt_ref, loc_ref, pend_ref, x_ref, g2_ref, e0_ref, e1_ref, lower_ref, xs_ref,
                     comp_ref, sem, *, tme, n_tiles):
    i = pl.program_id(0)
    n_steps = pl.num_programs(0)
    n_buf = comp_ref.shape[0]
    cur = i % n_buf
    half = tme // 2

    @pl.when(i == 0)
    def _():
        comp_ref[0, 0:half] = jnp.zeros((half, D_MODEL), comp_ref.dtype)

        def zero_tile(start):
            return [pltpu.make_async_copy(
                        comp_ref.at[0, pl.ds(0, half), :],
                        xs_ref.at[pl.ds(pl.multiple_of(start + k * half, half), half), :], sem.at[0])
                    for k in range(2)]

        def tail_loop(fn):
            def body(t, carry):
                for c in zero_tile(t * tme):
                    fn(c)
                return carry
            lax.fori_loop(pend_ref[N_EXPERTS - 1] // tme, n_tiles, body, 0)

        def group_tails(fn):
            for e in range(N_EXPERTS):
                prev_end = pend_ref[e - 1] if e else 0

                @pl.when(pend_ref[e] > prev_end)
                def _():
                    for c in zero_tile(pend_ref[e] - tme):
                        fn(c)

        group_tails(lambda c: c.start())
        tail_loop(lambda c: c.start())
        group_tails(lambda c: c.wait())
        tail_loop(lambda c: c.wait())

    h = (_rms(x_ref[...]) * g2_ref[...]).astype(BF16)
    pos0, pos1 = _packed_rows(i, e0_ref[...], e1_ref[...], lower_ref, loc_ref)
    slot = lax.broadcasted_iota(jnp.int32, (TILE, RUN_ROWS), 1).astype(F32)
    hit = jnp.where((slot == pos0) | (slot == pos1), 1.0, 0.0)
    comp_ref[cur] = _dot(hit.T.astype(BF16), h)

    def for_copies(tile, buf, fn):
        for e in range(N_EXPERTS):
            k = tile * N_EXPERTS + e
            cnt, dst, loc = cnt_ref[k], base_ref[k], loc_ref[k]
            for size, off, active in _run_pieces(cnt, TILE):
                c = pltpu.make_async_copy(
                    comp_ref.at[buf, pl.ds(pl.multiple_of(loc + off, RUN_ALIGN), size), :],
                    xs_ref.at[pl.ds(pl.multiple_of(dst + off, RUN_ALIGN), size), :], sem.at[buf])
                pl.when(active)(functools.partial(fn, c))

    def wait_copies(tile, buf):
        rows = pl.multiple_of(sum(cnt_ref[tile * N_EXPERTS + e] for e in range(N_EXPERTS)), RUN_ALIGN)
        pltpu.make_async_copy(comp_ref.at[buf, pl.ds(0, rows), :], xs_ref.at[pl.ds(0, rows), :],
                              sem.at[buf]).wait()

    for_copies(i, cur, lambda c: c.start())

    @pl.when(i >= n_buf - 1)
    def _():
        wait_copies(i - (n_buf - 1), (i + 1) % n_buf)

    @pl.when(i == n_steps - 1)
    def _():
        for back in range(n_buf - 2, -1, -1):

            @pl.when(i >= back)
            def _():
                wait_copies(i - back, (i - back) % n_buf)


def _dispatch(x, g2, e0, e1, base, cnt, loc, pend, lower, n_tiles, tme):
    T = x.shape[0]
    assert tme == 2 * TILE
    row = lambda i, *_: (i, 0)
    assert RUN_ROWS >= 2 * TILE + N_EXPERTS * (RUN_ALIGN - 1) and RUN_ROWS % LANE == 0
    vmem = (3 * _nbytes((TILE, D_MODEL), F32) + DISPATCH_BUFFERS * _nbytes((RUN_ROWS, D_MODEL), F32)
            + 8 * _nbytes((RUN_ROWS, TILE), F32))
    grid_spec = pltpu.PrefetchScalarGridSpec(
        num_scalar_prefetch=4,
        grid=(T // TILE,),
        in_specs=[pl.BlockSpec((TILE, D_MODEL), row),
                  pl.BlockSpec((1, D_MODEL), lambda i, *_: (0, 0)),
                  pl.BlockSpec((TILE, LANE), row), pl.BlockSpec((TILE, LANE), row),
                  pl.BlockSpec((TILE, TILE), lambda i, *_: (0, 0))],
        out_specs=pl.BlockSpec(memory_space=pl.ANY),
        scratch_shapes=[pltpu.VMEM((DISPATCH_BUFFERS, RUN_ROWS, D_MODEL), F32),
                        pltpu.SemaphoreType.DMA((DISPATCH_BUFFERS,))],
    )
    return pl.pallas_call(
        functools.partial(_dispatch_kernel, tme=tme, n_tiles=n_tiles),
        grid_spec=grid_spec,
        out_shape=jax.ShapeDtypeStruct((n_tiles * tme, D_MODEL), F32),
        compiler_params=_params(vmem, ("arbitrary",)),
        name="moe_dispatch",
    )(base, cnt, loc, pend, x, g2, e0, e1, lower)


def _expert_ffn_kernel(te_ref, used_ref, xs_ref, w1_ref, w3_ref, w2_ref, y_ref, acc_ref, *, fc):
    i = pl.program_id(0)

    @pl.when(i < used_ref[0])
    def _():
        _swiglu_into(xs_ref[...].astype(BF16), w1_ref, w3_ref, w2_ref, acc_ref, fc)
        y_ref[...] = acc_ref[...]

    @pl.when(i >= used_ref[0])
    def _():
        y_ref[...] = jnp.zeros_like(y_ref)


def _expert_ffn(xs, tile_expert, n_used, we1b, we3b, we2b, tme):
    n_rows = xs.shape[0]
    F = we1b.shape[2]
    fc = _pick(F, (512, 256, 128))
    wspec = lambda shape: pl.BlockSpec((None,) + shape, lambda i, te, nu: (te[i], 0, 0))
    vmem = (5 * _nbytes((tme, D_MODEL), F32) + 6 * _nbytes(we1b.shape[1:], BF16) + _nbytes((tme, D_MODEL), BF16)
            + 4 * _nbytes((tme, fc), F32))
    grid_spec = pltpu.PrefetchScalarGridSpec(
        num_scalar_prefetch=2,
        grid=(n_rows // tme,),
        in_specs=[pl.BlockSpec((tme, D_MODEL), lambda i, te, nu: (jnp.minimum(i, jnp.maximum(nu[0], 1) - 1), 0)),
                  wspec(we1b.shape[1:]), wspec(we3b.shape[1:]), wspec(we2b.shape[1:])],
        out_specs=pl.BlockSpec((tme, D_MODEL), lambda i, te, nu: (i, 0)),
        scratch_shapes=[pltpu.VMEM((tme, D_MODEL), F32)],
    )
    return pl.pallas_call(
        functools.partial(_expert_ffn_kernel, fc=fc),
        grid_spec=grid_spec,
        out_shape=jax.ShapeDtypeStruct((n_rows, D_MODEL), F32),
        compiler_params=_params(vmem, ("arbitrary",)),
        name="moe_expert_ffn",
    )(tile_expert, n_used, xs, we1b, we3b, we2b)


def _combine_kernel(base_ref, cnt_ref, loc_ref, x_ref, e0_ref, e1_ref, w0_ref, w1_ref, lower_ref, y_hbm_ref,
                    o_ref, o_last_ref, buf_ref, sem):
    i = pl.program_id(0)
    n_steps = pl.num_programs(0)
    cur = i % 2

    def for_copies(tile, buf, fn):
        for e in range(N_EXPERTS):
            k = tile * N_EXPERTS + e
            cnt, src, loc = cnt_ref[k], base_ref[k], loc_ref[k]
            for size, off, active in _run_pieces(cnt, TILE):
                c = pltpu.make_async_copy(
                    y_hbm_ref.at[pl.ds(pl.multiple_of(src + off, RUN_ALIGN), size), :],
                    buf_ref.at[buf, pl.ds(pl.multiple_of(loc + off, RUN_ALIGN), size), :], sem.at[buf])
                pl.when(active)(functools.partial(fn, c))

    @pl.when(i == 0)
    def _():
        buf_ref[...] = jnp.zeros_like(buf_ref)
        for_copies(0, 0, lambda c: c.start())

    @pl.when(i + 1 < n_steps)
    def _():
        for_copies(i + 1, 1 - cur, lambda c: c.start())

    rows = pl.multiple_of(sum(cnt_ref[i * N_EXPERTS + e] for e in range(N_EXPERTS)), RUN_ALIGN)
    pltpu.make_async_copy(y_hbm_ref.at[pl.ds(0, rows), :], buf_ref.at[cur, pl.ds(0, rows), :],
                          sem.at[cur]).wait()

    positions = _packed_rows(i, e0_ref[...], e1_ref[...], lower_ref, loc_ref)
    slot = lax.broadcasted_iota(jnp.int32, (TILE, RUN_ROWS), 1).astype(F32)
    yb = buf_ref[cur].astype(BF16)
    wide = lambda a, n: jnp.concatenate([a] * n, axis=1)
    acc = x_ref[...]
    for pos, wk in zip(positions, (w0_ref, w1_ref)):
        rows = _dot(jnp.where(slot == pos, 1.0, 0.0).astype(BF16), yb)
        acc = acc + wide(wk[...], D_MODEL // LANE) * rows

    @pl.when(i < n_steps - 1)
    def _():
        o_ref[...] = acc

    @pl.when(i == n_steps - 1)
    def _():
        o_last_ref[...] = acc


def _combine(x, e0, e1, w0, w1, base, cnt, loc, lower, y_rows):
    T = x.shape[0]
    row = lambda i, *_: (i, 0)
    lane_spec = pl.BlockSpec((TILE, LANE), row)
    vmem = (5 * _nbytes((TILE, D_MODEL), F32) + 3 * _nbytes((RUN_ROWS, D_MODEL), F32)
            + 8 * _nbytes((TILE, LANE), F32) + 8 * _nbytes((TILE, RUN_ROWS), F32)
            + 4 * _nbytes((TILE, D_MODEL), F32))
    grid_spec = pltpu.PrefetchScalarGridSpec(
        num_scalar_prefetch=3,
        grid=(T // TILE,),
        in_specs=[pl.BlockSpec((TILE, D_MODEL), row), lane_spec, lane_spec, lane_spec, lane_spec,
                  pl.BlockSpec((TILE, TILE), lambda i, *_: (0, 0)),
                  pl.BlockSpec(memory_space=pl.ANY)],
        out_specs=[pl.BlockSpec((TILE, D_MODEL), lambda i, *_: (jnp.minimum(i, T // TILE - 2), 0)),
                   pl.BlockSpec((TILE, D_MODEL), lambda i, *_: (0, 0))],
        scratch_shapes=[pltpu.VMEM((2, RUN_ROWS, D_MODEL), F32), pltpu.SemaphoreType.DMA((2,))],
    )
    return pl.pallas_call(
        _combine_kernel,
        grid_spec=grid_spec,
        out_shape=[jax.ShapeDtypeStruct((T - TILE, D_MODEL), F32), jax.ShapeDtypeStruct((TILE, D_MODEL), F32)],
        compiler_params=_params(vmem, ("arbitrary",)),
        name="moe_combine",
    )(base, cnt, loc, x, e0, e1, w0, w1, lower, y_rows)


def _moe(x, g2, wr_pad, we1b, we3b, we2b):
    T = x.shape[0]
    tme = 2 * TILE
    n_tt = T // TILE
    e0, e1, w0, w1 = _router(x, g2, wr_pad)
    experts = jnp.arange(N_EXPERTS, dtype=jnp.int32)
    picks = (e0[:, :1] == experts[None, :]) | (e1[:, :1] == experts[None, :])
    cnt = jnp.sum(picks.reshape(n_tt, TILE, N_EXPERTS).astype(jnp.int32), axis=1)
    cnt = (cnt + RUN_ALIGN - 1) // RUN_ALIGN * RUN_ALIGN
    counts = jnp.sum(cnt, axis=0)
    padded = (counts + tme - 1) // tme * tme
    pend = jnp.cumsum(padded).astype(jnp.int32)
    base = ((pend - padded)[None, :] + jnp.cumsum(cnt, axis=0) - cnt).astype(jnp.int32)
    loc = (jnp.cumsum(cnt, axis=1) - cnt).astype(jnp.int32)
    max_rows = T * TOP_K + n_tt * N_EXPERTS * (RUN_ALIGN - 1)
    n_tiles = -(-max_rows // tme) + N_EXPERTS
    tile_start = jnp.arange(n_tiles, dtype=jnp.int32) * tme
    tile_expert = jnp.minimum(jnp.sum(pend[None, :] <= tile_start[:, None], axis=1), N_EXPERTS - 1).astype(jnp.int32)
    n_used = (pend[N_EXPERTS - 1:] // tme).astype(jnp.int32)
    tok = jnp.arange(TILE)
    lower = (tok[None, :] < tok[:, None]).astype(BF16)
    base_f, cnt_f, loc_f = base.reshape(-1), cnt.reshape(-1).astype(jnp.int32), loc.reshape(-1)
    xs = _dispatch(x, g2, e0, e1, base_f, cnt_f, loc_f, pend, lower, n_tiles, tme)
    y_rows = _expert_ffn(xs, tile_expert, n_used, we1b, we3b, we2b, tme)
    return _combine(x, e0, e1, w0, w1, base_f, cnt_f, loc_f, lower, y_rows)


def _class_major(a, dil):
    n = a.shape[0]
    return a.reshape((n // TILE, TILE // dil, dil) + a.shape[1:]).swapaxes(1, 2).reshape(a.shape)


def _perm_matrices():
    eye = jnp.eye(TILE, dtype=BF16)
    return jnp.stack([_class_major(eye, dil) for _, dil in SWA_PATTERNS[1:]])


def _rope_tables(pos, Tp):
    freqs = ROPE_THETA ** (-jnp.arange(0, SWA_HEAD_DIM, 2, dtype=F32) / SWA_HEAD_DIM)
    rfreqs = 1.0 / (ROPE_THETA ** jnp.linspace(0.0, 1.0, RET_QK_DIM // 2, dtype=F32))

    def swa(p):
        ang = p.astype(F32)[:, None] * freqs[None, :]
        c, s = jnp.cos(ang), jnp.sin(ang)
        return jnp.concatenate([c, c], axis=1), jnp.concatenate([-s, s], axis=1)

    per_group = [swa(jnp.concatenate([_class_major(pos[:Tp], dil), pos[Tp:]])) for _, dil in SWA_PATTERNS]
    cosb = jnp.stack([c for c, _ in per_group])
    sinb = jnp.stack([s for _, s in per_group])
    rang = pos.astype(F32)[:, None] * rfreqs[None, :]
    c, s = jnp.cos(rang), jnp.sin(rang)
    cosc = jnp.concatenate([c, c, c, c], axis=1)
    sinc = jnp.concatenate([-s, s, -s, s], axis=1)
    return cosb, sinb, cosc, sinc


def _ret_tables(c_len):
    log_g = jnp.log1p(-jnp.exp2(-5.0 - jnp.arange(RET_HEADS, dtype=F32)))
    r = jnp.arange(CHUNK)
    i = (r % c_len).astype(F32)
    same = (r[:, None] // c_len) == (r[None, :] // c_len)
    dist = i[:, None] - i[None, :]
    decay = jnp.where(same[None] & (dist >= 0)[None],
                      jnp.exp(log_g[:, None, None] * jnp.maximum(dist, 0.0)[None]), 0.0)
    qin = jnp.repeat(jnp.exp(log_g[None, :] * (i[:, None] + 1.0)), RET_V_DIM, axis=1)
    kout = jnp.repeat(jnp.exp(log_g[None, :] * (c_len - 1.0 - i)[:, None]), RET_QK_DIM, axis=1)
    gc = jnp.broadcast_to(jnp.exp(log_g * c_len)[:, None], (RET_HEADS, LANE))
    gc = jnp.concatenate([gc, jnp.zeros((SUBLANE - RET_HEADS, LANE), F32)], axis=0)
    return decay.astype(F32), qin.astype(F32), kout.astype(F32), gc.astype(F32)


def _mixa_tables(w_s, b_s, t_s):
    w_p = jnp.tril(w_s)
    w8 = jnp.tril(w_s[:, :t_s, :t_s])
    eye = jnp.eye(CHUNK // t_s, dtype=w_s.dtype)
    w_smp = jax.vmap(lambda m: jnp.kron(eye, m))(w8)
    w2 = jnp.stack([w_p, w_smp]).astype(BF16)
    b_p = jnp.repeat(b_s.T, LANE, axis=1)
    b_smp = jnp.repeat(jnp.tile(b_s[:, :t_s].T, (CHUNK // t_s, 1)), LANE, axis=1)
    return w2, jnp.stack([b_p, b_smp]).astype(F32)


def kernel(x_prompt, x_sample, cache_swa_kv0, cache_swa_kv1, cache_swa_kv2, state_ret, norm1_g, w_in, norm_v_g, w_s, b_s, q_norm_g, k_norm_g, w_branch, w_out, norm2_g, w1, w3, w2, w_router, we1, we3, we2):
    n_p, s, d = x_prompt.shape
    n_s, t_s, _ = x_sample.shape
    depth = w_in.shape[0]
    Tp, Ts = n_p * s, n_s * t_s
    T = Tp + Ts
    max_win, max_dil = SWA_PATTERNS[-1]
    assert d == D_MODEL and Ts == TILE and CHUNK % t_s == 0
    assert s % (CHUNK * max_dil) == 0 and s >= max_win
    n_pt = Tp // TILE

    xa, xb = x_prompt.reshape(Tp, d), x_sample.reshape(Ts, d)
    pos = jnp.concatenate([jnp.arange(s, dtype=jnp.int32),
                           jnp.tile(PAST_LEN + jnp.arange(t_s, dtype=jnp.int32), n_s)])
    tabs = _rope_tables(pos, s)
    perm = _perm_matrices()
    rt_prompt = _ret_tables(CHUNK)
    rt_sample = _ret_tables(t_s)
    caches = tuple(c.reshape(c.shape[0], c.shape[1], -1, SWA_HEAD_DIM)
                   for c in (cache_swa_kv0, cache_swa_kv1, cache_swa_kv2))

    projections = []
    s_kv = [[] for _ in SWA_PATTERNS]
    p_ret, s_ret, s_v = [], [], []
    for layer in range(depth):
        if layer == 0:
            later = [(w_in, 1)] + [(w, 0) for w in (w_branch, w_out, w1, w3, w2, we1, we3, we2)]
            P, (w_in_later, w_branch_b, w_out_b, w1_b, w3_b, w2_b, we1_b, we3_b, we2_b) = _inproj(
                xa, xb, norm1_g[0][None, :], w_in[0].astype(BF16), perm, tabs, norm_v_g[0][None, :],
                q_norm_g[0][None, :], k_norm_g[0][None, :], n_pt, s // TILE, later)
        else:
            P, _ = _inproj(xa, xb, norm1_g[layer][None, :], w_in_later[layer - 1], perm, tabs,
                           norm_v_g[layer][None, :], q_norm_g[layer][None, :], k_norm_g[layer][None, :],
                           n_pt, s // TILE)

        w2a, b2a = _mixa_tables(w_s[layer], b_s[layer], t_s)
        out_a = _mixer_a(P, w2a, b2a, Tp // CHUNK)

        Ps = P[Tp:].astype(F32).reshape(n_s, t_s, IN_WIDTH)
        po, plse, so, slse = [], [], [], []
        for g, (win, dil) in enumerate(SWA_PATTERNS):
            o_g, l_g = _swa_prompt(P, g, dil, n_p, s)
            os_g, ls_g = _swa_sample(Ps, caches[g], layer, g, dil)
            po.append(o_g)
            plse.append(l_g)
            so.append(os_g.reshape(Ts, COL).astype(BF16))
            slse.append(ls_g.reshape(Ts, LANE))

        out_c, ret_p = _ret_prompt(P, rt_prompt, n_p, s)
        out_cs, ret_s = _ret_sample(P, state_ret, layer, rt_sample, Tp, t_s)

        x = _branch(xa, xb, out_a, P, w_branch_b[layer], w_out_b[layer],
                    (*po, *plse, out_c), (*so, *slse, out_cs), n_pt)

        g2 = norm2_g[layer][None, :]
        i = layer // 2
        if layer % 2 == 0:
            x = _dense_ffn(x, g2, w1_b[i], w3_b[i], w2_b[i])
            xa = xb = x
        else:
            wr_pad = jnp.zeros((D_MODEL, LANE), BF16).at[:, :N_EXPERTS].set(w_router[i].astype(BF16))
            xa, xb = _moe(x, g2, wr_pad, we1_b[i], we3_b[i], we2_b[i])

        projections.append(P)
        for g, (win, dil) in enumerate(SWA_PATTERNS):
            kcols = slice((CB_K + g) * COL, (CB_K + g + 1) * COL)
            vcols = slice((CB_V + g) * COL, (CB_V + g + 1) * COL)
            ks = P[Tp:, kcols].reshape(n_s, t_s, SWA_HEADS, SWA_HEAD_DIM)
            vs = P[Tp:, vcols].reshape(n_s, t_s, SWA_HEADS, SWA_HEAD_DIM)
            s_kv[g].append(jnp.stack([ks, vs], axis=2).astype(F32))
        p_ret.append(ret_p)
        s_ret.append(ret_s)
        s_v.append(P[Tp:, CB_AV * COL:(CB_AV + 1) * COL].astype(F32).reshape(n_s, t_s, COL))

    y_prompt = xa[:Tp].reshape(n_p, s, d)
    y_sample = xb[xb.shape[0] - Ts:].reshape(n_s, t_s, d)
    p_kv = [_kv_tail(projections, g, win, dil, n_p, s) for g, (win, dil) in enumerate(SWA_PATTERNS)]
    return (y_prompt, y_sample, p_kv[0], p_kv[1], p_kv[2], jnp.stack(p_ret),
            jnp.stack(s_kv[0]), jnp.stack(s_kv[1]), jnp.stack(s_kv[2]), jnp.stack(s_ret),
            jnp.stack(s_v))
```

```python
import functools
import math

import jax
import jax.numpy as jnp
from jax import lax
from jax.experimental import pallas as pl
from jax.experimental.pallas import tpu as pltpu

F32 = jnp.float32
BF16 = jnp.bfloat16

PAST_LEN = 16384
EPS = 1e-6
NEG_INF = -1e30
ROPE_THETA = 10000.0

D_MODEL = 1024
LANE = 128
SUBLANE = 8
BF16_ROWS = 2 * SUBLANE
CHUNK = 128
TILE = 256
COL = 512
A_GROUPS = 4
SWA_PATTERNS = ((128, 1), (512, 4), (2048, 16))
SWA_HEADS = 4
SWA_HEAD_DIM = 128
RET_HEADS = 4
RET_QK_DIM = 64
RET_V_DIM = 128
N_EXPERTS = 8
TOP_K = 2
IN_WIDTH = 10240
N_COL = IN_WIDTH // COL

CB_AU, CB_AV, CB_Q, CB_K, CB_V, CB_RQK, CB_RV, CB_RG, CB_GATE = 0, 1, 2, 5, 8, 11, 12, 13, 14

VMEM_INTERNAL_SCRATCH = 8 * 1024 * 1024
VMEM_BYTES = 64 * 1024 * 1024


def _pick(n, candidates):
    for c in candidates:
        if n % c == 0:
            return c
    raise ValueError(f"no tile in {candidates} divides {n}")


def _params(block_bytes, semantics=None):
    limit = min(int(block_bytes) + VMEM_INTERNAL_SCRATCH, VMEM_BYTES)
    return pltpu.CompilerParams(dimension_semantics=semantics, vmem_limit_bytes=limit)


def _nbytes(shape, dtype):
    return math.prod(shape) * jnp.dtype(dtype).itemsize


def _rms(x):
    return x * lax.rsqrt(jnp.mean(x * x, axis=-1, keepdims=True) + EPS)


def _gelu(x):
    return 0.5 * x * (1.0 + lax.erf(x * (0.5 ** 0.5)))


def _sigmoid(x):
    return 1.0 / (1.0 + jnp.exp(-x))


def _idiv(x, n):
    assert n & (n - 1) == 0
    return x >> (n.bit_length() - 1)


def _imod(x, n):
    assert n & (n - 1) == 0
    return x & (n - 1)


def _dot(a, b):
    return jnp.dot(a, b, preferred_element_type=F32)


LSE_LANES = LANE // SWA_HEADS


def _pack_heads(cols):
    rows = cols[0].shape[0]
    grp = _idiv(lax.broadcasted_iota(jnp.int32, (rows, LANE), 1), LSE_LANES)
    out = jnp.broadcast_to(cols[-1], (rows, LANE))
    for h in range(len(cols) - 2, -1, -1):
        out = jnp.where(grp == h, cols[h], out)
    return out


def _dot_nt(a, b):
    return lax.dot_general(a, b, (((1,), (1,)), ((), ())), preferred_element_type=F32)


def _tile_rows(i, n_prompt_tiles, xa_ref, xb_ref):
    return jnp.where(i < n_prompt_tiles, xa_ref[...], xb_ref[...])


def _x_specs(xa, xb):
    last_a, last_b = xa.shape[0] // TILE - 1, xb.shape[0] // TILE - 1
    return [pl.BlockSpec((TILE, D_MODEL), lambda i, *_: (jnp.minimum(i, last_a), 0)),
            pl.BlockSpec((TILE, D_MODEL), lambda i, *_: (last_b, 0))]


def _inproj_kernel(*refs, n_prompt_tiles, n_cast):
    (xa_ref, xb_ref, g1_ref, w_ref, perm_ref, cosb_ref, sinb_ref, cosc_ref, sinc_ref,
     nvg_ref, qg_ref, kg_ref) = refs[:12]
    cast_in, o_ref, cast_out, h_ref = refs[12:12 + n_cast], refs[12 + n_cast], refs[13 + n_cast:-1], refs[-1]
    i = pl.program_id(0)
    for src, dst in zip(cast_in, cast_out):
        dst[...] = src[...].astype(dst.dtype)
    hn = (_rms(_tile_rows(i, n_prompt_tiles, xa_ref, xb_ref)) * g1_ref[...]).astype(BF16)
    h_ref[0] = hn

    @pl.when(i < n_prompt_tiles)
    def _():
        for k in range(1, len(SWA_PATTERNS)):
            h_ref[k] = _dot(perm_ref[k - 1], hn).astype(BF16)

    @pl.when(i >= n_prompt_tiles)
    def _():
        for k in range(1, len(SWA_PATTERNS)):
            h_ref[k] = hn

    def qk_heads(acc, g, cols0, gain_ref, scale):
        for hh in range(SWA_HEADS):
            cs = slice(hh * LANE, (hh + 1) * LANE)
            y = _rms(acc[:, cs]) * gain_ref[...]
            rot = y * cosb_ref[g] + pltpu.roll(y, LANE // 2, axis=1) * sinb_ref[g]
            o_ref[:, cols0 + hh * LANE:cols0 + (hh + 1) * LANE] = (rot * scale).astype(o_ref.dtype)

    for j in range(N_COL):
        cols = slice(j * COL, (j + 1) * COL)
        g = (j - CB_Q) % len(SWA_PATTERNS) if CB_Q <= j < CB_RQK else 0
        acc = _dot(h_ref[g], w_ref[:, cols])
        if j == CB_AU:
            o_ref[:, cols] = _gelu(acc).astype(o_ref.dtype)
        elif j == CB_AV:
            o_ref[:, cols] = (_rms(_gelu(acc)) * nvg_ref[...]).astype(o_ref.dtype)
        elif CB_Q <= j < CB_K:
            qk_heads(acc, g, j * COL, qg_ref, SWA_HEAD_DIM ** -0.5)
        elif CB_K <= j < CB_V:
            qk_heads(acc, g, j * COL, kg_ref, 1.0)
        elif j < CB_RQK or j == CB_RV:
            o_ref[:, cols] = acc.astype(o_ref.dtype)
        elif j == CB_RQK:
            lane = lax.broadcasted_iota(jnp.int32, (acc.shape[0], LANE), 1)
            first_half = _imod(lane, RET_QK_DIM) < (RET_QK_DIM // 2)
            for tt in range(COL // LANE):
                y = acc[:, tt * LANE:(tt + 1) * LANE]
                partner = jnp.where(first_half,
                                    pltpu.roll(y, LANE - RET_QK_DIM // 2, axis=1),
                                    pltpu.roll(y, RET_QK_DIM // 2, axis=1))
                rot = y * cosc_ref[...] + partner * sinc_ref[...]
                scale = 1.0 if tt < (COL // LANE) // 2 else RET_QK_DIM ** -0.5
                o_ref[:, j * COL + tt * LANE:j * COL + (tt + 1) * LANE] = (rot * scale).astype(o_ref.dtype)
        elif j == CB_RG:
            o_ref[:, cols] = (acc * _sigmoid(acc)).astype(o_ref.dtype)
        else:
            o_ref[:, cols] = _sigmoid(acc).astype(o_ref.dtype)


def _cast_blocks(arr, n_steps, skip):
    a2 = arr.reshape(-1, arr.shape[-1])
    first_row = skip * (a2.shape[0] // arr.shape[0])
    rows = a2.shape[0] - first_row
    rb = -(-rows // n_steps)
    rb = -(-rb // BF16_ROWS) * BF16_ROWS
    while rows % rb or first_row % rb:
        rb += BF16_ROWS
    return a2, rb, first_row // rb


def _inproj(xa, xb, g1, w_in_b, perm, tabs, nvg, qg, kg, n_prompt_tiles, tiles_per_seq, to_cast=()):
    T = (n_prompt_tiles + 1) * TILE
    n_steps = T // TILE
    casts = [_cast_blocks(a, n_steps, skip) for a, skip in to_cast]
    n_blk = [a2.shape[0] // rb - first for a2, rb, first in casts]
    cast_in = [pl.BlockSpec((rb, a2.shape[1]), functools.partial(
        lambda i, first, last: (first + jnp.minimum(i, last), 0), first=first, last=n - 1))
        for (a2, rb, first), n in zip(casts, n_blk)]
    cast_out = [pl.BlockSpec((rb, a2.shape[1]), functools.partial(
        lambda i, last: (jnp.minimum(i, last), 0), last=n - 1)) for (a2, rb, _), n in zip(casts, n_blk)]
    cosb, sinb, cosc, sinc = tabs
    n_g = len(SWA_PATTERNS)
    row = lambda i: (i, 0)
    const = lambda i: (0, 0)
    tab_blk = lambda i: jnp.where(i < n_prompt_tiles, i % tiles_per_seq, tiles_per_seq)
    tab3 = pl.BlockSpec((n_g, TILE, LANE), lambda i: (0, tab_blk(i), 0))
    tab = pl.BlockSpec((TILE, LANE), lambda i: (tab_blk(i), 0))
    vmem = (2 * _nbytes((TILE, D_MODEL), F32) + n_g * _nbytes((TILE, D_MODEL), BF16)
            + _nbytes(w_in_b.shape, BF16) + 2 * _nbytes((TILE, IN_WIDTH), BF16)
            + 2 * (2 * n_g + 2) * _nbytes((TILE, LANE), F32) + 2 * _nbytes(perm.shape, BF16)
            + 8 * _nbytes((TILE, COL), F32)
            + sum(2 * (_nbytes((rb, a2.shape[1]), F32) + _nbytes((rb, a2.shape[1]), BF16)) for a2, rb, _ in casts))
    outs = pl.pallas_call(
        functools.partial(_inproj_kernel, n_prompt_tiles=n_prompt_tiles, n_cast=len(casts)),
        grid=(n_steps,),
        in_specs=_x_specs(xa, xb) + [
            pl.BlockSpec((1, D_MODEL), const),
            pl.BlockSpec(w_in_b.shape, const, pipeline_mode=pl.Buffered(1)),
            pl.BlockSpec(perm.shape, lambda i: (0, 0, 0)),
            tab3, tab3, tab, tab,
            pl.BlockSpec((1, COL), const),
            pl.BlockSpec((1, LANE), const),
            pl.BlockSpec((1, LANE), const),
        ] + cast_in,
        out_specs=[pl.BlockSpec((TILE, IN_WIDTH), row)] + cast_out,
        out_shape=[jax.ShapeDtypeStruct((T, IN_WIDTH), BF16)]
                  + [jax.ShapeDtypeStruct((n * rb, a2.shape[1]), BF16) for (a2, rb, _), n in zip(casts, n_blk)],
        scratch_shapes=[pltpu.VMEM((n_g, TILE, D_MODEL), BF16)],
        compiler_params=_params(vmem, ("arbitrary",)),
        name="inproj",
    )(xa, xb, g1, w_in_b, perm, cosb, sinb, cosc, sinc, nvg, qg, kg, *[a2 for a2, _, _ in casts])
    return outs[0], [o.reshape((a.shape[0] - skip,) + a.shape[1:]) for o, (a, skip) in zip(outs[1:], to_cast)]


def _mixa_kernel(u_ref, v_ref, w_ref, b_ref, o_ref, *, cps, n_prompt_chunks):
    i = pl.program_id(0)
    for c in range(cps):
        var = ((i * cps + c) >= n_prompt_chunks).astype(jnp.int32)
        rows = slice(c * CHUNK, (c + 1) * CHUNK)
        for g in range(A_GROUPS):
            cols = slice(g * LANE, (g + 1) * LANE)
            z = _dot(w_ref[var, g], v_ref[rows, cols]) + b_ref[var, :, cols]
            o_ref[rows, cols] = (u_ref[rows, cols].astype(F32) * z).astype(o_ref.dtype)


def _mixer_a(P, w2, b2, n_prompt_chunks):
    T = P.shape[0]
    n_chunks = T // CHUNK
    cps = _pick(n_chunks, (8, 6, 4, 3, 2, 1))
    rows = cps * CHUNK
    vmem = 6 * _nbytes((rows, COL), BF16) + 2 * _nbytes(w2.shape, BF16) + 2 * _nbytes(b2.shape, F32)
    return pl.pallas_call(
        functools.partial(_mixa_kernel, cps=cps, n_prompt_chunks=n_prompt_chunks),
        grid=(n_chunks // cps,),
        in_specs=[
            pl.BlockSpec((rows, COL), lambda i: (i, CB_AU)),
            pl.BlockSpec((rows, COL), lambda i: (i, CB_AV)),
            pl.BlockSpec(w2.shape, lambda i: (0, 0, 0, 0)),
            pl.BlockSpec(b2.shape, lambda i: (0, 0, 0)),
        ],
        out_specs=pl.BlockSpec((rows, COL), lambda i: (i, 0)),
        out_shape=jax.ShapeDtypeStruct((T, COL), BF16),
        compiler_params=_params(vmem, ("parallel",)),
        name="mixer_a",
    )(P, P, w2, b2)


def _swa_kernel(q_ref, kp_ref, kc_ref, vp_ref, vc_ref, o_ref, l_ref, k_ref, v_ref, s_ref, p_ref, *, qb):
    i = pl.program_id(2)
    rows = qb * CHUNK
    lead = q_ref.shape[:-1]
    k_ref[0:CHUNK] = kp_ref[...].reshape(CHUNK, COL)
    k_ref[CHUNK:CHUNK + rows] = kc_ref[...].reshape(rows, COL)
    v_ref[0:CHUNK] = vp_ref[...].reshape(CHUNK, COL)
    v_ref[CHUNK:CHUNK + rows] = vc_ref[...].reshape(rows, COL)
    row = lax.broadcasted_iota(jnp.int32, (CHUNK, 2 * CHUNK), 0)
    col = lax.broadcasted_iota(jnp.int32, (CHUNK, 2 * CHUNK), 1)
    mask_cur = (col >= CHUNK) & (col - CHUNK <= row)
    mask_all = mask_cur | ((col < CHUNK) & (col >= row))
    mask_first = mask_cur | ((col < CHUNK) & (col >= row) & (i > 0))
    q_all = q_ref[...].reshape(rows, COL)
    for j in range(qb):
        mask = mask_first if j == 0 else mask_all
        for h in range(SWA_HEADS):
            cs = slice(h * LANE, (h + 1) * LANE)
            sc = _dot_nt(q_all[j * CHUNK:(j + 1) * CHUNK, cs], k_ref[j * CHUNK:(j + 2) * CHUNK, cs])
            s_ref[j * SWA_HEADS + h] = jnp.where(mask, sc, NEG_INF)
    s = s_ref[...]
    m = jnp.max(s, axis=-1, keepdims=True)
    p = jnp.exp(s - m)
    den = jnp.sum(p, axis=-1, keepdims=True)
    p_ref[...] = p.astype(BF16)
    lse = m + jnp.log(den)
    for h in range(SWA_HEADS):
        cs = slice(h * LANE, (h + 1) * LANE)
        o_h = [_dot(p_ref[j * SWA_HEADS + h], v_ref[j * CHUNK:(j + 2) * CHUNK, cs]) / den[j * SWA_HEADS + h]
               for j in range(qb)]
        o_ref[..., cs] = jnp.concatenate(o_h, axis=0).astype(o_ref.dtype).reshape(lead + (LANE,))
    packed = [_pack_heads([lse[j * SWA_HEADS + h] for h in range(SWA_HEADS)]) for j in range(qb)]
    l_ref[...] = jnp.concatenate(packed, axis=0).reshape(lead + (LANE,))


def _swa_prompt(P, g, dil, n_p, s):
    T = P.shape[0]
    Tp = n_p * s
    nb = s // dil // CHUNK
    qb = _pick(nb, (8, 4, 2, 1))
    steps = nb // qb
    prev_blk = lambda b, i: b * nb + jnp.maximum(i * qb - 1, 0)
    if dil == 1:
        src = P
        lead_q, lead_p = (qb * CHUNK,), (CHUNK,)
        o_shape = (Tp, COL)
        l_shape = (Tp, LANE)
        q_map = lambda cb: (lambda b, r, i: (b * steps + i, cb + g))
        p_map = lambda cb: (lambda b, r, i: (prev_blk(b, i), cb + g))
        o_map = lambda b, r, i: (b * steps + i, 0)
        l_map = lambda b, r, i: (b * steps + i, 0)
    else:
        rpt = TILE // dil
        tpb = CHUNK // rpt
        src = P.reshape(T // TILE, dil, rpt, IN_WIDTH)
        lead_q, lead_p = (qb * tpb, None, rpt), (tpb, None, rpt)
        o_shape = (Tp // TILE, dil, rpt, COL)
        l_shape = (Tp // TILE, dil, rpt, LANE)
        q_map = lambda cb: (lambda b, r, i: (b * steps + i, r, 0, cb + g))
        p_map = lambda cb: (lambda b, r, i: (prev_blk(b, i), r, 0, cb + g))
        o_map = lambda b, r, i: (b * steps + i, r, 0, 0)
        l_map = lambda b, r, i: (b * steps + i, r, 0, 0)

    q_spec = lambda cb: pl.BlockSpec(lead_q + (COL,), q_map(cb))
    p_spec = lambda cb: pl.BlockSpec(lead_p + (COL,), p_map(cb))
    n_pairs = qb * SWA_HEADS
    vmem = ((10 * qb + 6) * _nbytes((CHUNK, COL), BF16) + 2 * qb * _nbytes((CHUNK, COL), F32)
            + 6 * n_pairs * _nbytes((CHUNK, 2 * CHUNK), F32))
    o, l = pl.pallas_call(
        functools.partial(_swa_kernel, qb=qb),
        grid=(n_p, dil, steps),
        in_specs=[q_spec(CB_Q), p_spec(CB_K), q_spec(CB_K), p_spec(CB_V), q_spec(CB_V)],
        out_specs=[pl.BlockSpec(lead_q + (COL,), o_map),
                   pl.BlockSpec(lead_q + (LANE,), l_map)],
        out_shape=[jax.ShapeDtypeStruct(o_shape, BF16), jax.ShapeDtypeStruct(l_shape, F32)],
        scratch_shapes=[pltpu.VMEM(((qb + 1) * CHUNK, COL), BF16), pltpu.VMEM(((qb + 1) * CHUNK, COL), BF16),
                        pltpu.VMEM((n_pairs, CHUNK, 2 * CHUNK), F32),
                        pltpu.VMEM((n_pairs, CHUNK, 2 * CHUNK), BF16)],
        compiler_params=_params(vmem, ("parallel", "parallel", "arbitrary")),
        name=f"swa_prompt_g{g}",
    )(src, src, src, src, src)
    return o.reshape(Tp, COL), l.reshape(Tp, LANE)


def _swa_sample_kernel(q_ref, kn_ref, vn_ref, cache_ref, o_ref, l_ref, *, dil, lbuf, t_s):
    nq = SWA_HEADS * t_s
    q = q_ref[...]
    qrep = jnp.concatenate([q] * SWA_HEADS, axis=0)
    rq = lax.broadcasted_iota(jnp.int32, (nq, COL), 0)
    cq = lax.broadcasted_iota(jnp.int32, (nq, COL), 1)
    qbd = jnp.where(_idiv(rq, t_s) == _idiv(cq, LANE), qrep, 0.0).astype(BF16)

    per_pos = 2 * SWA_HEADS
    if cache_ref.ndim == 2:
        n_keys = lbuf
        rows_of = lambda first: cache_ref[pl.ds(first, lbuf, stride=per_pos), :]
        key_pos = lambda r: r
    else:
        n_keys = cache_ref.shape[0] * t_s
        rows_of = lambda first: cache_ref[:, pl.ds(first, t_s, stride=per_pos), :].reshape(n_keys, LANE)
        key_pos = lambda r: dil * _idiv(r, t_s) + _imod(r, t_s)
    kc = jnp.concatenate([rows_of(h).astype(BF16) for h in range(SWA_HEADS)], axis=1)
    vc = jnp.concatenate([rows_of(SWA_HEADS + h).astype(BF16) for h in range(SWA_HEADS)], axis=1)
    kn = kn_ref[...].astype(BF16)
    vn = vn_ref[...].astype(BF16)

    s_c = _dot_nt(qbd, kc)
    s_n = _dot_nt(qbd, kn)
    t_c = _imod(lax.broadcasted_iota(jnp.int32, (nq, n_keys), 0), t_s)
    c_c = key_pos(lax.broadcasted_iota(jnp.int32, (nq, n_keys), 1))
    diff_c = lbuf + t_c - c_c
    ok_c = (_imod(diff_c, dil) == 0) & (diff_c <= lbuf)
    t_n = _imod(lax.broadcasted_iota(jnp.int32, (nq, t_s), 0), t_s)
    c_n = lax.broadcasted_iota(jnp.int32, (nq, t_s), 1)
    diff_n = t_n - c_n
    ok_n = (diff_n >= 0) & (_imod(diff_n, dil) == 0)
    s_c = jnp.where(ok_c, s_c, NEG_INF)
    s_n = jnp.where(ok_n, s_n, NEG_INF)
    m = jnp.maximum(jnp.max(s_c, axis=-1, keepdims=True), jnp.max(s_n, axis=-1, keepdims=True))
    p_c = jnp.exp(s_c - m)
    p_n = jnp.exp(s_n - m)
    den = jnp.sum(p_c, axis=-1, keepdims=True) + jnp.sum(p_n, axis=-1, keepdims=True)
    o_all = (_dot(p_c.astype(BF16), vc) + _dot(p_n.astype(BF16), vn)) / den
    lse = m + jnp.log(den)
    for h in range(SWA_HEADS):
        cs = slice(h * LANE, (h + 1) * LANE)
        o_ref[:, cs] = o_all[h * t_s:(h + 1) * t_s, cs]
    l_ref[...] = _pack_heads([lse[h * t_s:(h + 1) * t_s] for h in range(SWA_HEADS)])


def _swa_sample(Ps, cache, layer, g, dil):
    n_s, t_s, _ = Ps.shape
    rows = cache.shape[2]
    lbuf = rows // (2 * SWA_HEADS)
    assert lbuf == dil * CHUNK, "window buffer must hold exactly one full window"
    blk = lambda cb: pl.BlockSpec((None, t_s, COL), lambda b: (b, 0, cb + g))
    if dil > t_s:
        per_pos = rows // lbuf
        cache = cache.reshape(cache.shape[0], n_s, lbuf // dil, dil * per_pos, LANE)
        cache_spec = pl.BlockSpec((None, None, lbuf // dil, t_s * per_pos, LANE), lambda b: (layer, b, 0, 0, 0))
    else:
        cache_spec = pl.BlockSpec((None, None, rows, LANE), lambda b: (layer, b, 0, 0))
    vmem = (2 * _nbytes((lbuf, 2 * COL), F32) + 2 * _nbytes((lbuf, 2 * COL), BF16)
            + 8 * _nbytes((SWA_HEADS * t_s, lbuf), F32))
    return pl.pallas_call(
        functools.partial(_swa_sample_kernel, dil=dil, lbuf=lbuf, t_s=t_s),
        grid=(n_s,),
        in_specs=[blk(CB_Q), blk(CB_K), blk(CB_V), cache_spec],
        out_specs=[pl.BlockSpec((None, t_s, COL), lambda b: (b, 0, 0)),
                   pl.BlockSpec((None, t_s, LANE), lambda b: (b, 0, 0))],
        out_shape=[jax.ShapeDtypeStruct((n_s, t_s, COL), F32),
                   jax.ShapeDtypeStruct((n_s, t_s, LANE), F32)],
        compiler_params=_params(vmem, ("parallel",)),
        name=f"swa_sample_g{g}",
    )(Ps, Ps, Ps, cache)


def _kv_tail_kernel(*refs, dil, skip, n_layers):
    srcs, o_ref = refs[:2 * n_layers], refs[2 * n_layers]
    layer = pl.program_id(0)
    rpt = TILE // dil
    per_pos = 2 * SWA_HEADS

    def body(k_ref, v_ref):
        for kv, ref in enumerate((k_ref, v_ref)):
            for h in range(SWA_HEADS):
                for r in range(dil):
                    rows = ref[r * rpt + skip:(r + 1) * rpt, h * LANE:(h + 1) * LANE].astype(o_ref.dtype)
                    first = per_pos * r + kv * SWA_HEADS + h
                    o_ref[pl.ds(first, rpt - skip, stride=per_pos * dil), :] = rows

    for l in range(n_layers):
        pl.when(layer == l)(functools.partial(body, srcs[2 * l], srcs[2 * l + 1]))


def _kv_tail(Ps, g, win, dil, n_p, s):
    n_layers = len(Ps)
    keep = -(-win // TILE) * TILE
    skip = keep - win
    assert skip == 0 or dil == 1
    n_src = keep // TILE
    tiles_per_seq = s // TILE
    out_rows = (TILE - skip) * 2 * SWA_HEADS

    def src_spec(l, cb):
        def imap(layer, b, j):
            here = layer == l
            return (jnp.where(here, (b + 1) * tiles_per_seq - n_src + j, 0), cb + g)
        return pl.BlockSpec((TILE, COL), imap)

    specs = [src_spec(l, cb) for l in range(n_layers) for cb in (CB_K, CB_V)]
    vmem = 4 * n_layers * _nbytes((TILE, COL), BF16) + 2 * _nbytes((out_rows, LANE), F32)
    out = pl.pallas_call(
        functools.partial(_kv_tail_kernel, dil=dil, skip=skip, n_layers=n_layers),
        grid=(n_layers, n_p, n_src),
        in_specs=specs,
        out_specs=pl.BlockSpec((out_rows, LANE), lambda layer, b, j: ((layer * n_p + b) * n_src + j, 0)),
        out_shape=jax.ShapeDtypeStruct((n_layers * n_p * n_src * out_rows, LANE), F32),
        compiler_params=_params(vmem, ("parallel", "parallel", "parallel")),
        name=f"kv_tail_g{g}",
    )(*[P for P in Ps for _ in range(2)])
    return out.reshape(n_layers, n_p, win, 2, SWA_HEADS, SWA_HEAD_DIM)


def _ret_head_inputs(qk_ref, v_ref, kout_ref, h):
    pair, half = h // 2, h % 2
    lane = lax.broadcasted_iota(jnp.int32, (CHUNK, LANE), 1)
    head_lanes = _idiv(lane, RET_QK_DIM) == half
    qt = qk_ref[:, pair * LANE:(pair + 1) * LANE]
    kt = qk_ref[:, COL // 2 + pair * LANE:COL // 2 + (pair + 1) * LANE]
    qm = jnp.where(head_lanes, qt, jnp.zeros_like(qt))
    kw = jnp.where(head_lanes, kt.astype(F32) * kout_ref[:, pair * LANE:(pair + 1) * LANE], 0.0)
    vh = v_ref[:, h * LANE:(h + 1) * LANE]
    return qm, kt, kw, vh


def _ret_finish(o, gate_ref, o_ref, h):
    cs = slice(h * LANE, (h + 1) * LANE)
    o_ref[:, cs] = (gate_ref[:, cs].astype(F32) * _rms(o)).astype(o_ref.dtype)


def _ret_prompt_kernel(*refs, n_seq):
    ins = refs[:3 * n_seq]
    decay_ref, qin_ref, kout_ref, gc_ref, o_ref, s_out_ref, s_ref, att_ref = refs[3 * n_seq:]
    i = pl.program_id(0)

    @pl.when(i == 0)
    def _():
        s_ref[...] = jnp.zeros_like(s_ref)

    pairs = [(b, h) for b in range(n_seq) for h in range(RET_HEADS)]
    head_in = lambda b, h: _ret_head_inputs(ins[3 * b], ins[3 * b + 1], kout_ref, h)
    for b, h in pairs:
        qm, kt, _, _ = head_in(b, h)
        att_ref[b, h] = (_dot_nt(qm, kt) * decay_ref[h]).astype(BF16)
    for b, h in pairs:
        qm, _, _, vh = head_in(b, h)
        o = (_dot(att_ref[b, h], vh)
             + _dot(qm, s_ref[b, h].astype(BF16)) * qin_ref[:, h * LANE:(h + 1) * LANE])
        _ret_finish(o, ins[3 * b + 2], o_ref.at[b], h)
    for b, h in pairs:
        _, _, kw, vh = head_in(b, h)
        s_ref[b, h] = s_ref[b, h] * gc_ref[h:h + 1, :] + _dot(kw.T.astype(BF16), vh)

    @pl.when(i == pl.num_programs(0) - 1)
    def _():
        for b in range(n_seq):
            for h in range(RET_HEADS):
                lo = (h % 2) * RET_QK_DIM
                s_out_ref[b, h] = s_ref[b, h, lo:lo + RET_QK_DIM, :]


def _ret_prompt(P, rt, n_p, s):
    nblk = s // CHUNK
    decay, qin, kout, gc = rt
    blk = lambda b, cb: pl.BlockSpec((CHUNK, COL), lambda i: (b * nblk + i, cb))
    const2 = lambda i: (0, 0)
    seq_specs = [blk(b, cb) for b in range(n_p) for cb in (CB_RQK, CB_RV, CB_RG)]
    st_shape = (n_p, RET_HEADS, RET_QK_DIM, RET_V_DIM)
    vmem = (8 * n_p * _nbytes((CHUNK, COL), BF16) + 2 * _nbytes(decay.shape, F32) + 4 * _nbytes(qin.shape, F32)
            + 3 * n_p * _nbytes((RET_HEADS, LANE, LANE), F32))
    o, st = pl.pallas_call(
        functools.partial(_ret_prompt_kernel, n_seq=n_p),
        grid=(nblk,),
        in_specs=seq_specs + [pl.BlockSpec(decay.shape, lambda i: (0, 0, 0)),
                              pl.BlockSpec(qin.shape, const2), pl.BlockSpec(kout.shape, const2),
                              pl.BlockSpec(gc.shape, const2)],
        out_specs=[pl.BlockSpec((n_p, CHUNK, COL), lambda i: (0, i, 0)),
                   pl.BlockSpec(st_shape, lambda i: (0, 0, 0, 0))],
        out_shape=[jax.ShapeDtypeStruct((n_p, s, COL), BF16), jax.ShapeDtypeStruct(st_shape, F32)],
        scratch_shapes=[pltpu.VMEM((n_p, RET_HEADS, LANE, LANE), F32),
                        pltpu.VMEM((n_p, RET_HEADS, CHUNK, CHUNK), BF16)],
        compiler_params=_params(vmem, ("arbitrary",)),
        name="ret_prompt",
    )(*([P] * (3 * n_p)), decay, qin, kout, gc)
    return o.reshape(n_p * s, COL), st


def _ret_sample_kernel(qk_ref, v_ref, gate_ref, s0_ref, decay_ref, qin_ref, kout_ref, gc_ref,
                       o_ref, s_out_ref, *, t_s):
    row = lax.broadcasted_iota(jnp.int32, (CHUNK, LANE), 0)
    for h in range(RET_HEADS):
        lo = (h % 2) * RET_QK_DIM
        qm, kt, kw, vh = _ret_head_inputs(qk_ref, v_ref, kout_ref, h)
        att = _dot_nt(qm, kt) * decay_ref[h]
        o = _dot(att.astype(BF16), vh)
        inter = jnp.zeros((CHUNK, LANE), F32)
        for sq in range(CHUNK // t_s):
            seq_rows = _idiv(row, t_s) == sq
            st = s0_ref[sq, h]
            st2 = jnp.concatenate([st, st], axis=0).astype(BF16)
            inter = jnp.where(seq_rows, _dot(qm, st2), inter)
            upd = _dot(jnp.where(seq_rows, kw, 0.0).T.astype(BF16), vh)
            s_out_ref[sq, h] = st * gc_ref[h:h + 1, :] + upd[lo:lo + RET_QK_DIM, :]
        o = o + inter * qin_ref[:, h * LANE:(h + 1) * LANE]
        _ret_finish(o, gate_ref, o_ref, h)


def _ret_sample(P, state, layer, rt, Tp, t_s):
    n_s = state.shape[1]
    spb = CHUNK // t_s
    base = Tp // CHUNK
    decay, qin, kout, gc = rt
    blk = lambda cb: pl.BlockSpec((CHUNK, COL), lambda i: (base + i, cb))
    const2 = lambda i: (0, 0)
    st_shape = (spb, RET_HEADS, RET_QK_DIM, RET_V_DIM)
    vmem = (8 * _nbytes((CHUNK, COL), BF16) + 2 * _nbytes(decay.shape, F32) + 4 * _nbytes(qin.shape, F32)
            + 4 * _nbytes(st_shape, F32))
    return pl.pallas_call(
        functools.partial(_ret_sample_kernel, t_s=t_s),
        grid=(n_s // spb,),
        in_specs=[blk(CB_RQK), blk(CB_RV), blk(CB_RG),
                  pl.BlockSpec((None,) + st_shape, lambda i: (layer, i, 0, 0, 0)),
                  pl.BlockSpec(decay.shape, lambda i: (0, 0, 0)),
                  pl.BlockSpec(qin.shape, const2), pl.BlockSpec(kout.shape, const2),
                  pl.BlockSpec(gc.shape, const2)],
        out_specs=[pl.BlockSpec((CHUNK, COL), lambda i: (i, 0)),
                   pl.BlockSpec(st_shape, lambda i: (i, 0, 0, 0))],
        out_shape=[jax.ShapeDtypeStruct((n_s * t_s, COL), BF16),
                   jax.ShapeDtypeStruct((n_s, RET_HEADS, RET_QK_DIM, RET_V_DIM), F32)],
        compiler_params=_params(vmem, ("parallel",)),
        name="ret_sample",
    )(P, P, P, state, decay, qin, kout, gc)


def _branch_kernel(xa_ref, xb_ref, a_ref, g0_ref, g1_ref, g2_ref, wb_ref, wo_ref, ex_ref,
                   po0, po1, po2, pl0, pl1, pl2, pc, so0, so1, so2, sl0, sl1, sl2, sc,
                   y_ref, on_ref, ln_ref, *, n_prompt_tiles):
    i = pl.program_id(0)

    def spread(packed):
        hi = packed.astype(BF16)
        lo = (packed - hi.astype(F32)).astype(BF16)
        return _dot(jnp.concatenate([hi, lo], axis=1), ex_ref[...])

    def natural(o_ref, l_ref, k, dil):
        if dil == 1:
            return (lambda h: o_ref[:, h * LANE:(h + 1) * LANE].astype(F32)), l_ref[...]
        rpt = TILE // dil
        for r in range(dil):
            rows = slice(r * rpt, (r + 1) * rpt)
            ln_ref[k, pl.ds(r, rpt, stride=dil), :] = l_ref[rows, :]
            for h in range(SWA_HEADS):
                on_ref[k, h, pl.ds(r, rpt, stride=dil), :] = o_ref[rows, h * LANE:(h + 1) * LANE].astype(F32)
        return (lambda h: on_ref[k, h]), ln_ref[k]

    def body(o_refs, l_refs, c_ref, permuted):
        groups = [natural(o_refs[k], l_refs[k], k, SWA_PATTERNS[k][1] if permuted else 1)
                  for k in range(len(SWA_PATTERNS))]
        ls = [l for _, l in groups]
        lmax = functools.reduce(jnp.maximum, ls)
        es = [jnp.exp(l - lmax) for l in ls]
        inv = 1.0 / sum(es)
        weights = [spread(e * inv) for e in es]
        heads = []
        for h in range(SWA_HEADS):
            cs = slice(h * LANE, (h + 1) * LANE)
            heads.append(sum(w[:, cs] * go(h) for w, (go, _) in zip(weights, groups)).astype(BF16))
        mix = jnp.concatenate(heads, axis=1)
        merged = (g0_ref[...].astype(F32) * _dot(a_ref[...], wb_ref[0])
                  + g1_ref[...].astype(F32) * _dot(mix, wb_ref[1])
                  + g2_ref[...].astype(F32) * _dot(c_ref[...], wb_ref[2]))
        y_ref[...] = _tile_rows(i, n_prompt_tiles, xa_ref, xb_ref) + _dot(merged.astype(BF16), wo_ref[...])

    @pl.when(i < n_prompt_tiles)
    def _():
        body((po0, po1, po2), (pl0, pl1, pl2), pc, True)

    @pl.when(i >= n_prompt_tiles)
    def _():
        body((so0, so1, so2), (sl0, sl1, sl2), sc, False)


def _branch(xa, xb, out_a, P, wb, wo, prompt_set, sample_set, n_prompt_tiles):
    T = (n_prompt_tiles + 1) * TILE
    assert sample_set[0].shape[0] == TILE, "the sample rows must form exactly one tile"
    n_g = len(SWA_PATTERNS)
    last = n_prompt_tiles - 1
    row = lambda i: (i, 0)
    gate = lambda k: pl.BlockSpec((TILE, D_MODEL), lambda i: (i, CB_GATE * COL // D_MODEL + k))
    once = pl.Buffered(1)
    p_o = pl.BlockSpec((TILE, COL), lambda i: (jnp.minimum(i, last), 0))
    p_l = pl.BlockSpec((TILE, LANE), lambda i: (jnp.minimum(i, last), 0))
    s_o = pl.BlockSpec((TILE, COL), lambda i: (0, 0))
    s_l = pl.BlockSpec((TILE, LANE), lambda i: (0, 0))
    src = jnp.arange(2 * LANE) % LANE
    dst_head = jnp.arange(COL) // LANE
    expand = (src[:, None] == dst_head[None, :] * LSE_LANES).astype(BF16)
    vmem = (4 * _nbytes((TILE, D_MODEL), F32) + 6 * _nbytes((TILE, D_MODEL), BF16)
            + 2 * (2 * n_g + 2) * (_nbytes((TILE, COL), BF16) + _nbytes((TILE, COL), F32))
            + _nbytes(wb.shape, BF16) + _nbytes(wo.shape, BF16)
            + 4 * n_g * _nbytes((TILE, COL), F32) + 4 * _nbytes((TILE, D_MODEL), F32))
    return pl.pallas_call(
        functools.partial(_branch_kernel, n_prompt_tiles=n_prompt_tiles),
        grid=(T // TILE,),
        in_specs=_x_specs(xa, xb) + [
                  pl.BlockSpec((TILE, COL), row),
                  gate(0), gate(1), gate(2),
                  pl.BlockSpec(wb.shape, lambda i: (0, 0, 0), pipeline_mode=once),
                  pl.BlockSpec(wo.shape, lambda i: (0, 0), pipeline_mode=once),
                  pl.BlockSpec(expand.shape, lambda i: (0, 0)),
                  p_o, p_o, p_o, p_l, p_l, p_l, p_o,
                  s_o, s_o, s_o, s_l, s_l, s_l, s_o],
        out_specs=pl.BlockSpec((TILE, D_MODEL), row),
        out_shape=jax.ShapeDtypeStruct((T, D_MODEL), F32),
        scratch_shapes=[pltpu.VMEM((n_g, SWA_HEADS, TILE, LANE), F32),
                        pltpu.VMEM((n_g, TILE, LANE), F32)],
        compiler_params=_params(vmem, ("parallel",)),
        name="branch_merge",
    )(xa, xb, out_a, P, P, P, wb, wo, expand, *prompt_set, *sample_set)


def _swiglu_into(hb, w1_ref, w3_ref, w2_ref, acc_ref, fc):
    n_fc = w1_ref.shape[-1] // fc
    for f in range(n_fc):
        cols = slice(f * fc, (f + 1) * fc)
        a = _dot(hb, w1_ref[:, cols])
        b = _dot(hb, w3_ref[:, cols])
        part = _dot((a * _sigmoid(a) * b).astype(BF16), w2_ref[cols, :])
        if f == 0:
            acc_ref[...] = part
        else:
            acc_ref[...] += part


def _dense_ffn_kernel(x_ref, g2_ref, w1_ref, w3_ref, w2_ref, y_ref, acc_ref, *, fc):
    x = x_ref[...]
    hb = (_rms(x) * g2_ref[...]).astype(BF16)
    _swiglu_into(hb, w1_ref, w3_ref, w2_ref, acc_ref, fc)
    y_ref[...] = x + acc_ref[...]


def _dense_ffn(x, g2, w1b, w3b, w2b):
    T = x.shape[0]
    F = w1b.shape[1]
    tm = _pick(T, (768, 512, 384, 256, 128))
    fc = _pick(F, (512, 256, 128))
    row = lambda i: (i, 0)
    once = pl.Buffered(1)
    vmem = (5 * _nbytes((tm, D_MODEL), F32) + 3 * _nbytes(w1b.shape, BF16) + _nbytes((tm, D_MODEL), BF16)
            + 4 * _nbytes((tm, fc), F32))
    return pl.pallas_call(
        functools.partial(_dense_ffn_kernel, fc=fc),
        grid=(T // tm,),
        in_specs=[pl.BlockSpec((tm, D_MODEL), row),
                  pl.BlockSpec((1, D_MODEL), lambda i: (0, 0)),
                  pl.BlockSpec(w1b.shape, lambda i: (0, 0), pipeline_mode=once),
                  pl.BlockSpec(w3b.shape, lambda i: (0, 0), pipeline_mode=once),
                  pl.BlockSpec(w2b.shape, lambda i: (0, 0), pipeline_mode=once)],
        out_specs=pl.BlockSpec((tm, D_MODEL), row),
        out_shape=jax.ShapeDtypeStruct((T, D_MODEL), F32),
        scratch_shapes=[pltpu.VMEM((tm, D_MODEL), F32)],
        compiler_params=_params(vmem, ("parallel",)),
        name="dense_ffn",
    )(x, g2, w1b, w3b, w2b)


def _router_kernel(x_ref, g2_ref, wr_ref, e0_ref, e1_ref, w0_ref, w1_ref):
    hb = (_rms(x_ref[...]) * g2_ref[...]).astype(BF16)
    logits = _dot(hb, wr_ref[...])
    lane = lax.broadcasted_iota(jnp.int32, logits.shape, 1)
    logits = jnp.where(lane < N_EXPERTS, logits, -jnp.inf)
    lane_f = lane.astype(F32)
    m1 = jnp.max(logits, axis=-1, keepdims=True)
    i1 = jnp.min(jnp.where(logits == m1, lane_f, float(LANE)), axis=-1, keepdims=True)
    rest = jnp.where(lane_f == i1, -jnp.inf, logits)
    m2 = jnp.max(rest, axis=-1, keepdims=True)
    i2 = jnp.min(jnp.where(rest == m2, lane_f, float(LANE)), axis=-1, keepdims=True)
    e = jnp.exp(m2 - m1)
    e0_ref[...] = jnp.broadcast_to(i1, logits.shape).astype(jnp.int32)
    e1_ref[...] = jnp.broadcast_to(i2, logits.shape).astype(jnp.int32)
    w0_ref[...] = jnp.broadcast_to(1.0 / (1.0 + e), logits.shape)
    w1_ref[...] = jnp.broadcast_to(e / (1.0 + e), logits.shape)


def _router(x, g2, wr_pad):
    T = x.shape[0]
    tm = _pick(T, (768, 512, 384, 256, 128))
    row = lambda i: (i, 0)
    out = pl.BlockSpec((tm, LANE), row)
    vmem = 4 * _nbytes((tm, D_MODEL), F32) + 2 * _nbytes(wr_pad.shape, BF16) + 16 * _nbytes((tm, LANE), F32)
    return pl.pallas_call(
        _router_kernel,
        grid=(T // tm,),
        in_specs=[pl.BlockSpec((tm, D_MODEL), row),
                  pl.BlockSpec((1, D_MODEL), lambda i: (0, 0)),
                  pl.BlockSpec(wr_pad.shape, lambda i: (0, 0))],
        out_specs=[out, out, out, out],
        out_shape=[jax.ShapeDtypeStruct((T, LANE), jnp.int32), jax.ShapeDtypeStruct((T, LANE), jnp.int32),
                   jax.ShapeDtypeStruct((T, LANE), F32), jax.ShapeDtypeStruct((T, LANE), F32)],
        compiler_params=_params(vmem, ("parallel",)),
        name="moe_router",
    )(x, g2, wr_pad)


RUN_ALIGN = SUBLANE


def _run_pieces(count, max_rows):
    pieces, off = [], 0
    size = max_rows
    while size >= RUN_ALIGN:
        active = (count & size) != 0
        pieces.append((size, off, active))
        off = off + jnp.where(active, size, 0)
        size //= 2
    return pieces


def _packed_rows(i, e0, e1, lower_ref, loc_ref):
    lane = lax.broadcasted_iota(jnp.int32, e0.shape, 1)
    picks = jnp.where((lane == e0) | (lane == e1), 1.0, 0.0)
    rank = _dot(lower_ref[...], picks.astype(BF16))
    lane1 = lax.broadcasted_iota(jnp.int32, (1, LANE), 1)
    run_start = jnp.zeros((1, LANE), F32)
    for e in range(N_EXPERTS):
        run_start = jnp.where(lane1 == e, loc_ref[i * N_EXPERTS + e].astype(F32), run_start)
    row_of = rank + run_start
    return tuple(jnp.sum(jnp.where(lane == ek, row_of, 0.0), axis=-1, keepdims=True) for ek in (e0, e1))


DISPATCH_BUFFERS = 3
RUN_ROWS = 640


def _dispatch_kernel(base_ref, cnt_ref, loc_ref, pend_ref, x_ref, g2_ref, e0_ref, e1_ref, lower_ref, xs_ref,
                     comp_ref, sem, *, tme, n_tiles):
    i = pl.program_id(0)
    n_steps = pl.num_programs(0)
    n_buf = comp_ref.shape[0]
    cur = i % n_buf
    half = tme // 2

    @pl.when(i == 0)
    def _():
        comp_ref[0, 0:half] = jnp.zeros((half, D_MODEL), comp_ref.dtype)

        def zero_tile(start):
            return [pltpu.make_async_copy(
                        comp_ref.at[0, pl.ds(0, half), :],
                        xs_ref.at[pl.ds(pl.multiple_of(start + k * half, half), half), :], sem.at[0])
                    for k in range(2)]

        def tail_loop(fn):
            def body(t, carry):
                for c in zero_tile(t * tme):
                    fn(c)
                return carry
            lax.fori_loop(pend_ref[N_EXPERTS - 1] // tme, n_tiles, body, 0)

        def group_tails(fn):
            for e in range(N_EXPERTS):
                prev_end = pend_ref[e - 1] if e else 0

                @pl.when(pend_ref[e] > prev_end)
                def _():
                    for c in zero_tile(pend_ref[e] - tme):
                        fn(c)

        group_tails(lambda c: c.start())
        tail_loop(lambda c: c.start())
        group_tails(lambda c: c.wait())
        tail_loop(lambda c: c.wait())

    h = (_rms(x_ref[...]) * g2_ref[...]).astype(BF16)
    pos0, pos1 = _packed_rows(i, e0_ref[...], e1_ref[...], lower_ref, loc_ref)
    slot = lax.broadcasted_iota(jnp.int32, (TILE, RUN_ROWS), 1).astype(F32)
    hit = jnp.where((slot == pos0) | (slot == pos1), 1.0, 0.0)
    comp_ref[cur] = _dot(hit.T.astype(BF16), h)

    def for_copies(tile, buf, fn):
        for e in range(N_EXPERTS):
            k = tile * N_EXPERTS + e
            cnt, dst, loc = cnt_ref[k], base_ref[k], loc_ref[k]
            for size, off, active in _run_pieces(cnt, TILE):
                c = pltpu.make_async_copy(
                    comp_ref.at[buf, pl.ds(pl.multiple_of(loc + off, RUN_ALIGN), size), :],
                    xs_ref.at[pl.ds(pl.multiple_of(dst + off, RUN_ALIGN), size), :], sem.at[buf])
                pl.when(active)(functools.partial(fn, c))

    def wait_copies(tile, buf):
        rows = pl.multiple_of(sum(cnt_ref[tile * N_EXPERTS + e] for e in range(N_EXPERTS)), RUN_ALIGN)
        pltpu.make_async_copy(comp_ref.at[buf, pl.ds(0, rows), :], xs_ref.at[pl.ds(0, rows), :],
                              sem.at[buf]).wait()

    for_copies(i, cur, lambda c: c.start())

    @pl.when(i >= n_buf - 1)
    def _():
        wait_copies(i - (n_buf - 1), (i + 1) % n_buf)

    @pl.when(i == n_steps - 1)
    def _():
        for back in range(n_buf - 2, -1, -1):

            @pl.when(i >= back)
            def _():
                wait_copies(i - back, (i - back) % n_buf)


def _dispatch(x, g2, e0, e1, base, cnt, loc, pend, lower, n_tiles, tme):
    T = x.shape[0]
    assert tme == 2 * TILE
    row = lambda i, *_: (i, 0)
    assert RUN_ROWS >= 2 * TILE + N_EXPERTS * (RUN_ALIGN - 1) and RUN_ROWS % LANE == 0
    vmem = (3 * _nbytes((TILE, D_MODEL), F32) + DISPATCH_BUFFERS * _nbytes((RUN_ROWS, D_MODEL), F32)
            + 8 * _nbytes((RUN_ROWS, TILE), F32))
    grid_spec = pltpu.PrefetchScalarGridSpec(
        num_scalar_prefetch=4,
        grid=(T // TILE,),
        in_specs=[pl.BlockSpec((TILE, D_MODEL), row),
                  pl.BlockSpec((1, D_MODEL), lambda i, *_: (0, 0)),
                  pl.BlockSpec((TILE, LANE), row), pl.BlockSpec((TILE, LANE), row),
                  pl.BlockSpec((TILE, TILE), lambda i, *_: (0, 0))],
        out_specs=pl.BlockSpec(memory_space=pl.ANY),
        scratch_shapes=[pltpu.VMEM((DISPATCH_BUFFERS, RUN_ROWS, D_MODEL), F32),
                        pltpu.SemaphoreType.DMA((DISPATCH_BUFFERS,))],
    )
    return pl.pallas_call(
        functools.partial(_dispatch_kernel, tme=tme, n_tiles=n_tiles),
        grid_spec=grid_spec,
        out_shape=jax.ShapeDtypeStruct((n_tiles * tme, D_MODEL), F32),
        compiler_params=_params(vmem, ("arbitrary",)),
        name="moe_dispatch",
    )(base, cnt, loc, pend, x, g2, e0, e1, lower)


def _expert_ffn_kernel(te_ref, used_ref, xs_ref, w1_ref, w3_ref, w2_ref, y_ref, acc_ref, *, fc):
    i = pl.program_id(0)

    @pl.when(i < used_ref[0])
    def _():
        _swiglu_into(xs_ref[...].astype(BF16), w1_ref, w3_ref, w2_ref, acc_ref, fc)
        y_ref[...] = acc_ref[...]

    @pl.when(i >= used_ref[0])
    def _():
        y_ref[...] = jnp.zeros_like(y_ref)


def _expert_ffn(xs, tile_expert, n_used, we1b, we3b, we2b, tme):
    n_rows = xs.shape[0]
    F = we1b.shape[2]
    fc = _pick(F, (512, 256, 128))
    wspec = lambda shape: pl.BlockSpec((None,) + shape, lambda i, te, nu: (te[i], 0, 0))
    vmem = (5 * _nbytes((tme, D_MODEL), F32) + 6 * _nbytes(we1b.shape[1:], BF16) + _nbytes((tme, D_MODEL), BF16)
            + 4 * _nbytes((tme, fc), F32))
    grid_spec = pltpu.PrefetchScalarGridSpec(
        num_scalar_prefetch=2,
        grid=(n_rows // tme,),
        in_specs=[pl.BlockSpec((tme, D_MODEL), lambda i, te, nu: (jnp.minimum(i, jnp.maximum(nu[0], 1) - 1), 0)),
                  wspec(we1b.shape[1:]), wspec(we3b.shape[1:]), wspec(we2b.shape[1:])],
        out_specs=pl.BlockSpec((tme, D_MODEL), lambda i, te, nu: (i, 0)),
        scratch_shapes=[pltpu.VMEM((tme, D_MODEL), F32)],
    )
    return pl.pallas_call(
        functools.partial(_expert_ffn_kernel, fc=fc),
        grid_spec=grid_spec,
        out_shape=jax.ShapeDtypeStruct((n_rows, D_MODEL), F32),
        compiler_params=_params(vmem, ("arbitrary",)),
        name="moe_expert_ffn",
    )(tile_expert, n_used, xs, we1b, we3b, we2b)


def _combine_kernel(base_ref, cnt_ref, loc_ref, x_ref, e0_ref, e1_ref, w0_ref, w1_ref, lower_ref, y_hbm_ref,
                    o_ref, o_last_ref, buf_ref, sem):
    i = pl.program_id(0)
    n_steps = pl.num_programs(0)
    cur = i % 2

    def for_copies(tile, buf, fn):
        for e in range(N_EXPERTS):
            k = tile * N_EXPERTS + e
            cnt, src, loc = cnt_ref[k], base_ref[k], loc_ref[k]
            for size, off, active in _run_pieces(cnt, TILE):
                c = pltpu.make_async_copy(
                    y_hbm_ref.at[pl.ds(pl.multiple_of(src + off, RUN_ALIGN), size), :],
                    buf_ref.at[buf, pl.ds(pl.multiple_of(loc + off, RUN_ALIGN), size), :], sem.at[buf])
                pl.when(active)(functools.partial(fn, c))

    @pl.when(i == 0)
    def _():
        buf_ref[...] = jnp.zeros_like(buf_ref)
        for_copies(0, 0, lambda c: c.start())

    @pl.when(i + 1 < n_steps)
    def _():
        for_copies(i + 1, 1 - cur, lambda c: c.start())

    rows = pl.multiple_of(sum(cnt_ref[i * N_EXPERTS + e] for e in range(N_EXPERTS)), RUN_ALIGN)
    pltpu.make_async_copy(y_hbm_ref.at[pl.ds(0, rows), :], buf_ref.at[cur, pl.ds(0, rows), :],
                          sem.at[cur]).wait()

    positions = _packed_rows(i, e0_ref[...], e1_ref[...], lower_ref, loc_ref)
    slot = lax.broadcasted_iota(jnp.int32, (TILE, RUN_ROWS), 1).astype(F32)
    yb = buf_ref[cur].astype(BF16)
    wide = lambda a, n: jnp.concatenate([a] * n, axis=1)
    acc = x_ref[...]
    for pos, wk in zip(positions, (w0_ref, w1_ref)):
        rows = _dot(jnp.where(slot == pos, 1.0, 0.0).astype(BF16), yb)
        acc = acc + wide(wk[...], D_MODEL // LANE) * rows

    @pl.when(i < n_steps - 1)
    def _():
        o_ref[...] = acc

    @pl.when(i == n_steps - 1)
    def _():
        o_last_ref[...] = acc


def _combine(x, e0, e1, w0, w1, base, cnt, loc, lower, y_rows):
    T = x.shape[0]
    row = lambda i, *_: (i, 0)
    lane_spec = pl.BlockSpec((TILE, LANE), row)
    vmem = (5 * _nbytes((TILE, D_MODEL), F32) + 3 * _nbytes((RUN_ROWS, D_MODEL), F32)
            + 8 * _nbytes((TILE, LANE), F32) + 8 * _nbytes((TILE, RUN_ROWS), F32)
            + 4 * _nbytes((TILE, D_MODEL), F32))
    grid_spec = pltpu.PrefetchScalarGridSpec(
        num_scalar_prefetch=3,
        grid=(T // TILE,),
        in_specs=[pl.BlockSpec((TILE, D_MODEL), row), lane_spec, lane_spec, lane_spec, lane_spec,
                  pl.BlockSpec((TILE, TILE), lambda i, *_: (0, 0)),
                  pl.BlockSpec(memory_space=pl.ANY)],
        out_specs=[pl.BlockSpec((TILE, D_MODEL), lambda i, *_: (jnp.minimum(i, T // TILE - 2), 0)),
                   pl.BlockSpec((TILE, D_MODEL), lambda i, *_: (0, 0))],
        scratch_shapes=[pltpu.VMEM((2, RUN_ROWS, D_MODEL), F32), pltpu.SemaphoreType.DMA((2,))],
    )
    return pl.pallas_call(
        _combine_kernel,
        grid_spec=grid_spec,
        out_shape=[jax.ShapeDtypeStruct((T - TILE, D_MODEL), F32), jax.ShapeDtypeStruct((TILE, D_MODEL), F32)],
        compiler_params=_params(vmem, ("arbitrary",)),
        name="moe_combine",
    )(base, cnt, loc, x, e0, e1, w0, w1, lower, y_rows)


def _moe(x, g2, wr_pad, we1b, we3b, we2b):
    T = x.shape[0]
    tme = 2 * TILE
    n_tt = T // TILE
    e0, e1, w0, w1 = _router(x, g2, wr_pad)
    experts = jnp.arange(N_EXPERTS, dtype=jnp.int32)
    picks = (e0[:, :1] == experts[None, :]) | (e1[:, :1] == experts[None, :])
    cnt = jnp.sum(picks.reshape(n_tt, TILE, N_EXPERTS).astype(jnp.int32), axis=1)
    cnt = (cnt + RUN_ALIGN - 1) // RUN_ALIGN * RUN_ALIGN
    counts = jnp.sum(cnt, axis=0)
    padded = (counts + tme - 1) // tme * tme
    pend = jnp.cumsum(padded).astype(jnp.int32)
    base = ((pend - padded)[None, :] + jnp.cumsum(cnt, axis=0) - cnt).astype(jnp.int32)
    loc = (jnp.cumsum(cnt, axis=1) - cnt).astype(jnp.int32)
    max_rows = T * TOP_K + n_tt * N_EXPERTS * (RUN_ALIGN - 1)
    n_tiles = -(-max_rows // tme) + N_EXPERTS
    tile_start = jnp.arange(n_tiles, dtype=jnp.int32) * tme
    tile_expert = jnp.minimum(jnp.sum(pend[None, :] <= tile_start[:, None], axis=1), N_EXPERTS - 1).astype(jnp.int32)
    n_used = (pend[N_EXPERTS - 1:] // tme).astype(jnp.int32)
    tok = jnp.arange(TILE)
    lower = (tok[None, :] < tok[:, None]).astype(BF16)
    base_f, cnt_f, loc_f = base.reshape(-1), cnt.reshape(-1).astype(jnp.int32), loc.reshape(-1)
    xs = _dispatch(x, g2, e0, e1, base_f, cnt_f, loc_f, pend, lower, n_tiles, tme)
    y_rows = _expert_ffn(xs, tile_expert, n_used, we1b, we3b, we2b, tme)
    return _combine(x, e0, e1, w0, w1, base_f, cnt_f, loc_f, lower, y_rows)


def _class_major(a, dil):
    n = a.shape[0]
    return a.reshape((n // TILE, TILE // dil, dil) + a.shape[1:]).swapaxes(1, 2).reshape(a.shape)


def _perm_matrices():
    eye = jnp.eye(TILE, dtype=BF16)
    return jnp.stack([_class_major(eye, dil) for _, dil in SWA_PATTERNS[1:]])


def _rope_tables(pos, Tp):
    freqs = ROPE_THETA ** (-jnp.arange(0, SWA_HEAD_DIM, 2, dtype=F32) / SWA_HEAD_DIM)
    rfreqs = 1.0 / (ROPE_THETA ** jnp.linspace(0.0, 1.0, RET_QK_DIM // 2, dtype=F32))

    ang = pos.astype(F32)[:, None] * freqs[None, :]
    c, s = jnp.cos(ang), jnp.sin(ang)
    cos2, sin2 = jnp.concatenate([c, c], axis=1), jnp.concatenate([-s, s], axis=1)
    per_group = lambda t: jnp.stack([jnp.concatenate([_class_major(t[:Tp], dil), t[Tp:]])
                                     for _, dil in SWA_PATTERNS])
    cosb, sinb = per_group(cos2), per_group(sin2)
    rang = pos.astype(F32)[:, None] * rfreqs[None, :]
    c, s = jnp.cos(rang), jnp.sin(rang)
    cosc = jnp.concatenate([c, c, c, c], axis=1)
    sinc = jnp.concatenate([-s, s, -s, s], axis=1)
    return cosb, sinb, cosc, sinc


def _ret_tables(c_len):
    log_g = jnp.log1p(-jnp.exp2(-5.0 - jnp.arange(RET_HEADS, dtype=F32)))
    r = jnp.arange(CHUNK)
    i = (r % c_len).astype(F32)
    same = (r[:, None] // c_len) == (r[None, :] // c_len)
    dist = i[:, None] - i[None, :]
    decay = jnp.where(same[None] & (dist >= 0)[None],
                      jnp.exp(log_g[:, None, None] * jnp.maximum(dist, 0.0)[None]), 0.0)
    qin = jnp.repeat(jnp.exp(log_g[None, :] * (i[:, None] + 1.0)), RET_V_DIM, axis=1)
    kout = jnp.repeat(jnp.exp(log_g[None, :] * (c_len - 1.0 - i)[:, None]), RET_QK_DIM, axis=1)
    gc = jnp.broadcast_to(jnp.exp(log_g * c_len)[:, None], (RET_HEADS, LANE))
    gc = jnp.concatenate([gc, jnp.zeros((SUBLANE - RET_HEADS, LANE), F32)], axis=0)
    return decay.astype(F32), qin.astype(F32), kout.astype(F32), gc.astype(F32)


def _mixa_tables(w_s, b_s, t_s):
    w_p = jnp.tril(w_s)
    w8 = jnp.tril(w_s[:, :t_s, :t_s])
    eye = jnp.eye(CHUNK // t_s, dtype=w_s.dtype)
    w_smp = jax.vmap(lambda m: jnp.kron(eye, m))(w8)
    w2 = jnp.stack([w_p, w_smp]).astype(BF16)
    b_p = jnp.repeat(b_s.T, LANE, axis=1)
    b_smp = jnp.repeat(jnp.tile(b_s[:, :t_s].T, (CHUNK // t_s, 1)), LANE, axis=1)
    return w2, jnp.stack([b_p, b_smp]).astype(F32)


def kernel(x_prompt, x_sample, cache_swa_kv0, cache_swa_kv1, cache_swa_kv2, state_ret, norm1_g, w_in, norm_v_g, w_s, b_s, q_norm_g, k_norm_g, w_branch, w_out, norm2_g, w1, w3, w2, w_router, we1, we3, we2):
    n_p, s, d = x_prompt.shape
    n_s, t_s, _ = x_sample.shape
    depth = w_in.shape[0]
    Tp, Ts = n_p * s, n_s * t_s
    T = Tp + Ts
    max_win, max_dil = SWA_PATTERNS[-1]
    assert d == D_MODEL and Ts == TILE and CHUNK % t_s == 0
    assert s % (CHUNK * max_dil) == 0 and s >= max_win
    n_pt = Tp // TILE

    xa, xb = x_prompt.reshape(Tp, d), x_sample.reshape(Ts, d)
    pos = jnp.concatenate([jnp.arange(s, dtype=jnp.int32),
                           jnp.tile(PAST_LEN + jnp.arange(t_s, dtype=jnp.int32), n_s)])
    tabs = _rope_tables(pos, s)
    perm = _perm_matrices()
    rt_prompt = _ret_tables(CHUNK)
    rt_sample = _ret_tables(t_s)
    caches = tuple(c.reshape(c.shape[0], c.shape[1], -1, SWA_HEAD_DIM)
                   for c in (cache_swa_kv0, cache_swa_kv1, cache_swa_kv2))

    projections = []
    s_kv = [[] for _ in SWA_PATTERNS]
    p_ret, s_ret, s_v = [], [], []
    for layer in range(depth):
        if layer == 0:
            later = [(w_in, 1)] + [(w, 0) for w in (w_branch, w_out, w1, w3, w2, we1, we3, we2)]
            P, (w_in_later, w_branch_b, w_out_b, w1_b, w3_b, w2_b, we1_b, we3_b, we2_b) = _inproj(
                xa, xb, norm1_g[0][None, :], w_in[0].astype(BF16), perm, tabs, norm_v_g[0][None, :],
                q_norm_g[0][None, :], k_norm_g[0][None, :], n_pt, s // TILE, later)
        else:
            P, _ = _inproj(xa, xb, norm1_g[layer][None, :], w_in_later[layer - 1], perm, tabs,
                           norm_v_g[layer][None, :], q_norm_g[layer][None, :], k_norm_g[layer][None, :],
                           n_pt, s // TILE)

        w2a, b2a = _mixa_tables(w_s[layer], b_s[layer], t_s)
        out_a = _mixer_a(P, w2a, b2a, Tp // CHUNK)

        Ps = P[Tp:].astype(F32).reshape(n_s, t_s, IN_WIDTH)
        po, plse, so, slse = [], [], [], []
        for g, (win, dil) in enumerate(SWA_PATTERNS):
            o_g, l_g = _swa_prompt(P, g, dil, n_p, s)
            os_g, ls_g = _swa_sample(Ps, caches[g], layer, g, dil)
            po.append(o_g)
            plse.append(l_g)
            so.append(os_g.reshape(Ts, COL).astype(BF16))
            slse.append(ls_g.reshape(Ts, LANE))

        out_c, ret_p = _ret_prompt(P, rt_prompt, n_p, s)
        out_cs, ret_s = _ret_sample(P, state_ret, layer, rt_sample, Tp, t_s)

        x = _branch(xa, xb, out_a, P, w_branch_b[layer], w_out_b[layer],
                    (*po, *plse, out_c), (*so, *slse, out_cs), n_pt)

        g2 = norm2_g[layer][None, :]
        i = layer // 2
        if layer % 2 == 0:
            x = _dense_ffn(x, g2, w1_b[i], w3_b[i], w2_b[i])
            xa = xb = x
        else:
            wr_pad = jnp.zeros((D_MODEL, LANE), BF16).at[:, :N_EXPERTS].set(w_router[i].astype(BF16))
            xa, xb = _moe(x, g2, wr_pad, we1_b[i], we3_b[i], we2_b[i])

        projections.append(P)
        for g, (win, dil) in enumerate(SWA_PATTERNS):
            kcols = slice((CB_K + g) * COL, (CB_K + g + 1) * COL)
            vcols = slice((CB_V + g) * COL, (CB_V + g + 1) * COL)
            ks = P[Tp:, kcols].reshape(n_s, t_s, SWA_HEADS, SWA_HEAD_DIM)
            vs = P[Tp:, vcols].reshape(n_s, t_s, SWA_HEADS, SWA_HEAD_DIM)
            s_kv[g].append(jnp.stack([ks, vs], axis=2).astype(F32))
        p_ret.append(ret_p)
        s_ret.append(ret_s)
        s_v.append(P[Tp:, CB_AV * COL:(CB_AV + 1) * COL].astype(F32).reshape(n_s, t_s, COL))

    y_prompt = xa[:Tp].reshape(n_p, s, d)
    y_sample = xb[xb.shape[0] - Ts:].reshape(n_s, t_s, d)
    p_kv = [_kv_tail(projections, g, win, dil, n_p, s) for g, (win, dil) in enumerate(SWA_PATTERNS)]
    return (y_prompt, y_sample, p_kv[0], p_kv[1], p_kv[2], jnp.stack(p_ret),
            jnp.stack(s_kv[0]), jnp.stack(s_kv[1]), jnp.stack(s_kv[2]), jnp.stack(s_ret),
            jnp.stack(s_v))
```

```python
import functools
import math

import jax
import jax.numpy as jnp
from jax import lax
from jax.experimental import pallas as pl
from jax.experimental.pallas import tpu as pltpu

F32 = jnp.float32
BF16 = jnp.bfloat16

PAST_LEN = 16384
EPS = 1e-6
NEG_INF = -1e30
ROPE_THETA = 10000.0

D_MODEL = 1024
LANE = 128
SUBLANE = 8
BF16_ROWS = 2 * SUBLANE
CHUNK = 128
TILE = 256
COL = 512
A_GROUPS = 4
SWA_PATTERNS = ((128, 1), (512, 4), (2048, 16))
SWA_HEADS = 4
SWA_HEAD_DIM = 128
RET_HEADS = 4
RET_QK_DIM = 64
RET_V_DIM = 128
N_EXPERTS = 8
TOP_K = 2
IN_WIDTH = 10240
N_COL = IN_WIDTH // COL

CB_AU, CB_AV, CB_Q, CB_K, CB_V, CB_RQK, CB_RV, CB_RG, CB_GATE = 0, 1, 2, 5, 8, 11, 12, 13, 14

VMEM_INTERNAL_SCRATCH = 8 * 1024 * 1024
VMEM_BYTES = 64 * 1024 * 1024


def _pick(n, candidates):
    for c in candidates:
        if n % c == 0:
            return c
    raise ValueError(f"no tile in {candidates} divides {n}")


def _params(block_bytes, semantics=None):
    limit = min(int(block_bytes) + VMEM_INTERNAL_SCRATCH, VMEM_BYTES)
    return pltpu.CompilerParams(dimension_semantics=semantics, vmem_limit_bytes=limit)


def _nbytes(shape, dtype):
    return math.prod(shape) * jnp.dtype(dtype).itemsize


def _rms(x):
    return x * lax.rsqrt(jnp.mean(x * x, axis=-1, keepdims=True) + EPS)


def _gelu(x):
    return 0.5 * x * (1.0 + lax.erf(x * (0.5 ** 0.5)))


def _sigmoid(x):
    return 1.0 / (1.0 + jnp.exp(-x))


def _idiv(x, n):
    assert n & (n - 1) == 0
    return x >> (n.bit_length() - 1)


def _imod(x, n):
    assert n & (n - 1) == 0
    return x & (n - 1)


def _dot(a, b):
    return jnp.dot(a, b, preferred_element_type=F32)


LSE_LANES = LANE // SWA_HEADS
RET_PHASE_PAIRS = 8


def _pack_heads(cols):
    rows = cols[0].shape[0]
    grp = _idiv(lax.broadcasted_iota(jnp.int32, (rows, LANE), 1), LSE_LANES)
    out = jnp.broadcast_to(cols[-1], (rows, LANE))
    for h in range(len(cols) - 2, -1, -1):
        out = jnp.where(grp == h, cols[h], out)
    return out


def _dot_nt(a, b):
    return lax.dot_general(a, b, (((1,), (1,)), ((), ())), preferred_element_type=F32)


def _tile_rows(i, n_prompt_tiles, xa_ref, xb_ref):
    return jnp.where(i < n_prompt_tiles, xa_ref[...], xb_ref[...])


def _x_specs(xa, xb):
    last_a, last_b = xa.shape[0] // TILE - 1, xb.shape[0] // TILE - 1
    return [pl.BlockSpec((TILE, D_MODEL), lambda i, *_: (jnp.minimum(i, last_a), 0)),
            pl.BlockSpec((TILE, D_MODEL), lambda i, *_: (last_b, 0))]


def _inproj_kernel(*refs, n_prompt_tiles, n_cast):
    (xa_ref, xb_ref, g1_ref, w_ref, perm_ref, cosb_ref, sinb_ref, cosc_ref, sinc_ref,
     nvg_ref, qg_ref, kg_ref) = refs[:12]
    cast_in, o_ref, cast_out, h_ref = refs[12:12 + n_cast], refs[12 + n_cast], refs[13 + n_cast:-1], refs[-1]
    i = pl.program_id(0)
    for src, dst in zip(cast_in, cast_out):
        dst[...] = src[...].astype(dst.dtype)
    hn = (_rms(_tile_rows(i, n_prompt_tiles, xa_ref, xb_ref)) * g1_ref[...]).astype(BF16)
    h_ref[0] = hn

    @pl.when(i < n_prompt_tiles)
    def _():
        for k in range(1, len(SWA_PATTERNS)):
            h_ref[k] = _dot(perm_ref[k - 1], hn).astype(BF16)

    @pl.when(i >= n_prompt_tiles)
    def _():
        for k in range(1, len(SWA_PATTERNS)):
            h_ref[k] = hn

    def qk_heads(acc, g, cols0, gain_ref, scale):
        for hh in range(SWA_HEADS):
            cs = slice(hh * LANE, (hh + 1) * LANE)
            y = _rms(acc[:, cs]) * gain_ref[...]
            rot = y * cosb_ref[g] + pltpu.roll(y, LANE // 2, axis=1) * sinb_ref[g]
            o_ref[:, cols0 + hh * LANE:cols0 + (hh + 1) * LANE] = (rot * scale).astype(o_ref.dtype)

    for j in range(N_COL):
        cols = slice(j * COL, (j + 1) * COL)
        g = (j - CB_Q) % len(SWA_PATTERNS) if CB_Q <= j < CB_RQK else 0
        acc = _dot(h_ref[g], w_ref[:, cols])
        if j == CB_AU:
            o_ref[:, cols] = _gelu(acc).astype(o_ref.dtype)
        elif j == CB_AV:
            o_ref[:, cols] = (_rms(_gelu(acc)) * nvg_ref[...]).astype(o_ref.dtype)
        elif CB_Q <= j < CB_K:
            qk_heads(acc, g, j * COL, qg_ref, SWA_HEAD_DIM ** -0.5)
        elif CB_K <= j < CB_V:
            qk_heads(acc, g, j * COL, kg_ref, 1.0)
        elif j < CB_RQK or j == CB_RV:
            o_ref[:, cols] = acc.astype(o_ref.dtype)
        elif j == CB_RQK:
            lane = lax.broadcasted_iota(jnp.int32, (acc.shape[0], LANE), 1)
            first_half = _imod(lane, RET_QK_DIM) < (RET_QK_DIM // 2)
            for tt in range(COL // LANE):
                y = acc[:, tt * LANE:(tt + 1) * LANE]
                partner = jnp.where(first_half,
                                    pltpu.roll(y, LANE - RET_QK_DIM // 2, axis=1),
                                    pltpu.roll(y, RET_QK_DIM // 2, axis=1))
                rot = y * cosc_ref[...] + partner * sinc_ref[...]
                scale = 1.0 if tt < (COL // LANE) // 2 else RET_QK_DIM ** -0.5
                o_ref[:, j * COL + tt * LANE:j * COL + (tt + 1) * LANE] = (rot * scale).astype(o_ref.dtype)
        elif j == CB_RG:
            o_ref[:, cols] = (acc * _sigmoid(acc)).astype(o_ref.dtype)
        else:
            o_ref[:, cols] = _sigmoid(acc).astype(o_ref.dtype)


def _cast_blocks(arr, n_steps, skip):
    a2 = arr.reshape(-1, arr.shape[-1])
    first_row = skip * (a2.shape[0] // arr.shape[0])
    rows = a2.shape[0] - first_row
    rb = -(-rows // n_steps)
    rb = -(-rb // BF16_ROWS) * BF16_ROWS
    while rows % rb or first_row % rb:
        rb += BF16_ROWS
    return a2, rb, first_row // rb


def _inproj(xa, xb, g1, w_in_b, perm, tabs, nvg, qg, kg, n_prompt_tiles, tiles_per_seq, to_cast=()):
    T = (n_prompt_tiles + 1) * TILE
    n_steps = T // TILE
    casts = [_cast_blocks(a, n_steps, skip) for a, skip in to_cast]
    n_blk = [a2.shape[0] // rb - first for a2, rb, first in casts]
    cast_in = [pl.BlockSpec((rb, a2.shape[1]), functools.partial(
        lambda i, first, last: (first + jnp.minimum(i, last), 0), first=first, last=n - 1))
        for (a2, rb, first), n in zip(casts, n_blk)]
    cast_out = [pl.BlockSpec((rb, a2.shape[1]), functools.partial(
        lambda i, last: (jnp.minimum(i, last), 0), last=n - 1)) for (a2, rb, _), n in zip(casts, n_blk)]
    cosb, sinb, cosc, sinc = tabs
    n_g = len(SWA_PATTERNS)
    row = lambda i: (i, 0)
    const = lambda i: (0, 0)
    tab_blk = lambda i: jnp.where(i < n_prompt_tiles, i % tiles_per_seq, tiles_per_seq)
    tab3 = pl.BlockSpec((n_g, TILE, LANE), lambda i: (0, tab_blk(i), 0))
    tab = pl.BlockSpec((TILE, LANE), lambda i: (tab_blk(i), 0))
    vmem = (2 * _nbytes((TILE, D_MODEL), F32) + n_g * _nbytes((TILE, D_MODEL), BF16)
            + _nbytes(w_in_b.shape, BF16) + 2 * _nbytes((TILE, IN_WIDTH), BF16)
            + 2 * (2 * n_g + 2) * _nbytes((TILE, LANE), F32) + 2 * _nbytes(perm.shape, BF16)
            + 8 * _nbytes((TILE, COL), F32)
            + sum(2 * (_nbytes((rb, a2.shape[1]), F32) + _nbytes((rb, a2.shape[1]), BF16)) for a2, rb, _ in casts))
    outs = pl.pallas_call(
        functools.partial(_inproj_kernel, n_prompt_tiles=n_prompt_tiles, n_cast=len(casts)),
        grid=(n_steps,),
        in_specs=_x_specs(xa, xb) + [
            pl.BlockSpec((1, D_MODEL), const),
            pl.BlockSpec(w_in_b.shape, const, pipeline_mode=pl.Buffered(1)),
            pl.BlockSpec(perm.shape, lambda i: (0, 0, 0)),
            tab3, tab3, tab, tab,
            pl.BlockSpec((1, COL), const),
            pl.BlockSpec((1, LANE), const),
            pl.BlockSpec((1, LANE), const),
        ] + cast_in,
        out_specs=[pl.BlockSpec((TILE, IN_WIDTH), row)] + cast_out,
        out_shape=[jax.ShapeDtypeStruct((T, IN_WIDTH), BF16)]
                  + [jax.ShapeDtypeStruct((n * rb, a2.shape[1]), BF16) for (a2, rb, _), n in zip(casts, n_blk)],
        scratch_shapes=[pltpu.VMEM((n_g, TILE, D_MODEL), BF16)],
        compiler_params=_params(vmem, ("arbitrary",)),
        name="inproj",
    )(xa, xb, g1, w_in_b, perm, cosb, sinb, cosc, sinc, nvg, qg, kg, *[a2 for a2, _, _ in casts])
    return outs[0], [o.reshape((a.shape[0] - skip,) + a.shape[1:]) for o, (a, skip) in zip(outs[1:], to_cast)]


def _mixa_kernel(u_ref, v_ref, w_ref, b_ref, o_ref, *, cps, n_prompt_chunks):
    i = pl.program_id(0)
    for c in range(cps):
        var = ((i * cps + c) >= n_prompt_chunks).astype(jnp.int32)
        rows = slice(c * CHUNK, (c + 1) * CHUNK)
        for g in range(A_GROUPS):
            cols = slice(g * LANE, (g + 1) * LANE)
            z = _dot(w_ref[var, g], v_ref[rows, cols]) + b_ref[var, :, cols]
            o_ref[rows, cols] = (u_ref[rows, cols].astype(F32) * z).astype(o_ref.dtype)


def _mixer_a(P, w2, b2, n_prompt_chunks):
    T = P.shape[0]
    n_chunks = T // CHUNK
    cps = _pick(n_chunks, (8, 6, 4, 3, 2, 1))
    rows = cps * CHUNK
    vmem = 6 * _nbytes((rows, COL), BF16) + 2 * _nbytes(w2.shape, BF16) + 2 * _nbytes(b2.shape, F32)
    return pl.pallas_call(
        functools.partial(_mixa_kernel, cps=cps, n_prompt_chunks=n_prompt_chunks),
        grid=(n_chunks // cps,),
        in_specs=[
            pl.BlockSpec((rows, COL), lambda i: (i, CB_AU)),
            pl.BlockSpec((rows, COL), lambda i: (i, CB_AV)),
            pl.BlockSpec(w2.shape, lambda i: (0, 0, 0, 0)),
            pl.BlockSpec(b2.shape, lambda i: (0, 0, 0)),
        ],
        out_specs=pl.BlockSpec((rows, COL), lambda i: (i, 0)),
        out_shape=jax.ShapeDtypeStruct((T, COL), BF16),
        compiler_params=_params(vmem, ("parallel",)),
        name="mixer_a",
    )(P, P, w2, b2)


def _swa_kernel(q_ref, kp_ref, kc_ref, vp_ref, vc_ref, o_ref, l_ref, k_ref, v_ref, s_ref, p_ref, *, qb):
    i = pl.program_id(2)
    rows = qb * CHUNK
    lead = q_ref.shape[:-1]
    k_ref[0:CHUNK] = kp_ref[...].reshape(CHUNK, COL)
    k_ref[CHUNK:CHUNK + rows] = kc_ref[...].reshape(rows, COL)
    v_ref[0:CHUNK] = vp_ref[...].reshape(CHUNK, COL)
    v_ref[CHUNK:CHUNK + rows] = vc_ref[...].reshape(rows, COL)
    row = lax.broadcasted_iota(jnp.int32, (CHUNK, 2 * CHUNK), 0)
    col = lax.broadcasted_iota(jnp.int32, (CHUNK, 2 * CHUNK), 1)
    mask_cur = (col >= CHUNK) & (col - CHUNK <= row)
    mask_all = mask_cur | ((col < CHUNK) & (col >= row))
    mask_first = mask_cur | ((col < CHUNK) & (col >= row) & (i > 0))
    q_all = q_ref[...].reshape(rows, COL)
    for j in range(qb):
        mask = mask_first if j == 0 else mask_all
        for h in range(SWA_HEADS):
            cs = slice(h * LANE, (h + 1) * LANE)
            sc = _dot_nt(q_all[j * CHUNK:(j + 1) * CHUNK, cs], k_ref[j * CHUNK:(j + 2) * CHUNK, cs])
            s_ref[j * SWA_HEADS + h] = jnp.where(mask, sc, NEG_INF)
    s = s_ref[...]
    m = jnp.max(s, axis=-1, keepdims=True)
    p = jnp.exp(s - m)
    den = jnp.sum(p, axis=-1, keepdims=True)
    p_ref[...] = p.astype(BF16)
    lse = m + jnp.log(den)
    for h in range(SWA_HEADS):
        cs = slice(h * LANE, (h + 1) * LANE)
        o_h = [_dot(p_ref[j * SWA_HEADS + h], v_ref[j * CHUNK:(j + 2) * CHUNK, cs]) / den[j * SWA_HEADS + h]
               for j in range(qb)]
        o_ref[..., cs] = jnp.concatenate(o_h, axis=0).astype(o_ref.dtype).reshape(lead + (LANE,))
    packed = [_pack_heads([lse[j * SWA_HEADS + h] for h in range(SWA_HEADS)]) for j in range(qb)]
    l_ref[...] = jnp.concatenate(packed, axis=0).reshape(lead + (LANE,))


def _swa_prompt(P, g, dil, n_p, s):
    T = P.shape[0]
    Tp = n_p * s
    nb = s // dil // CHUNK
    qb = _pick(nb, (8, 4, 2, 1))
    steps = nb // qb
    prev_blk = lambda b, i: b * nb + jnp.maximum(i * qb - 1, 0)
    if dil == 1:
        src = P
        lead_q, lead_p = (qb * CHUNK,), (CHUNK,)
        o_shape = (Tp, COL)
        l_shape = (Tp, LANE)
        q_map = lambda cb: (lambda b, r, i: (b * steps + i, cb + g))
        p_map = lambda cb: (lambda b, r, i: (prev_blk(b, i), cb + g))
        o_map = lambda b, r, i: (b * steps + i, 0)
        l_map = lambda b, r, i: (b * steps + i, 0)
    else:
        rpt = TILE // dil
        tpb = CHUNK // rpt
        src = P.reshape(T // TILE, dil, rpt, IN_WIDTH)
        lead_q, lead_p = (qb * tpb, None, rpt), (tpb, None, rpt)
        o_shape = (Tp // TILE, dil, rpt, COL)
        l_shape = (Tp // TILE, dil, rpt, LANE)
        q_map = lambda cb: (lambda b, r, i: (b * steps + i, r, 0, cb + g))
        p_map = lambda cb: (lambda b, r, i: (prev_blk(b, i), r, 0, cb + g))
        o_map = lambda b, r, i: (b * steps + i, r, 0, 0)
        l_map = lambda b, r, i: (b * steps + i, r, 0, 0)

    q_spec = lambda cb: pl.BlockSpec(lead_q + (COL,), q_map(cb))
    p_spec = lambda cb: pl.BlockSpec(lead_p + (COL,), p_map(cb))
    n_pairs = qb * SWA_HEADS
    vmem = ((10 * qb + 6) * _nbytes((CHUNK, COL), BF16) + 2 * qb * _nbytes((CHUNK, COL), F32)
            + 6 * n_pairs * _nbytes((CHUNK, 2 * CHUNK), F32))
    o, l = pl.pallas_call(
        functools.partial(_swa_kernel, qb=qb),
        grid=(n_p, dil, steps),
        in_specs=[q_spec(CB_Q), p_spec(CB_K), q_spec(CB_K), p_spec(CB_V), q_spec(CB_V)],
        out_specs=[pl.BlockSpec(lead_q + (COL,), o_map),
                   pl.BlockSpec(lead_q + (LANE,), l_map)],
        out_shape=[jax.ShapeDtypeStruct(o_shape, BF16), jax.ShapeDtypeStruct(l_shape, F32)],
        scratch_shapes=[pltpu.VMEM(((qb + 1) * CHUNK, COL), BF16), pltpu.VMEM(((qb + 1) * CHUNK, COL), BF16),
                        pltpu.VMEM((n_pairs, CHUNK, 2 * CHUNK), F32),
                        pltpu.VMEM((n_pairs, CHUNK, 2 * CHUNK), BF16)],
        compiler_params=_params(vmem, ("parallel", "parallel", "arbitrary")),
        name=f"swa_prompt_g{g}",
    )(src, src, src, src, src)
    return o.reshape(Tp, COL), l.reshape(Tp, LANE)


def _swa_sample_kernel(q_ref, kn_ref, vn_ref, cache_ref, o_ref, l_ref, *, dil, lbuf, t_s):
    nq = SWA_HEADS * t_s
    q = q_ref[...]
    qrep = jnp.concatenate([q] * SWA_HEADS, axis=0)
    rq = lax.broadcasted_iota(jnp.int32, (nq, COL), 0)
    cq = lax.broadcasted_iota(jnp.int32, (nq, COL), 1)
    qbd = jnp.where(_idiv(rq, t_s) == _idiv(cq, LANE), qrep, 0.0).astype(BF16)

    per_pos = 2 * SWA_HEADS
    if cache_ref.ndim == 2:
        n_keys = lbuf
        rows_of = lambda first: cache_ref[pl.ds(first, lbuf, stride=per_pos), :]
        key_pos = lambda r: r
    else:
        n_keys = cache_ref.shape[0] * t_s
        rows_of = lambda first: cache_ref[:, pl.ds(first, t_s, stride=per_pos), :].reshape(n_keys, LANE)
        key_pos = lambda r: dil * _idiv(r, t_s) + _imod(r, t_s)
    kc = jnp.concatenate([rows_of(h).astype(BF16) for h in range(SWA_HEADS)], axis=1)
    vc = jnp.concatenate([rows_of(SWA_HEADS + h).astype(BF16) for h in range(SWA_HEADS)], axis=1)
    kn = kn_ref[...].astype(BF16)
    vn = vn_ref[...].astype(BF16)

    s_c = _dot_nt(qbd, kc)
    s_n = _dot_nt(qbd, kn)
    t_c = _imod(lax.broadcasted_iota(jnp.int32, (nq, n_keys), 0), t_s)
    c_c = key_pos(lax.broadcasted_iota(jnp.int32, (nq, n_keys), 1))
    diff_c = lbuf + t_c - c_c
    ok_c = (_imod(diff_c, dil) == 0) & (diff_c <= lbuf)
    t_n = _imod(lax.broadcasted_iota(jnp.int32, (nq, t_s), 0), t_s)
    c_n = lax.broadcasted_iota(jnp.int32, (nq, t_s), 1)
    diff_n = t_n - c_n
    ok_n = (diff_n >= 0) & (_imod(diff_n, dil) == 0)
    s_c = jnp.where(ok_c, s_c, NEG_INF)
    s_n = jnp.where(ok_n, s_n, NEG_INF)
    m = jnp.maximum(jnp.max(s_c, axis=-1, keepdims=True), jnp.max(s_n, axis=-1, keepdims=True))
    p_c = jnp.exp(s_c - m)
    p_n = jnp.exp(s_n - m)
    den = jnp.sum(p_c, axis=-1, keepdims=True) + jnp.sum(p_n, axis=-1, keepdims=True)
    o_all = (_dot(p_c.astype(BF16), vc) + _dot(p_n.astype(BF16), vn)) / den
    lse = m + jnp.log(den)
    for h in range(SWA_HEADS):
        cs = slice(h * LANE, (h + 1) * LANE)
        o_ref[:, cs] = o_all[h * t_s:(h + 1) * t_s, cs]
    l_ref[...] = _pack_heads([lse[h * t_s:(h + 1) * t_s] for h in range(SWA_HEADS)])


def _swa_sample(Ps, cache, layer, g, dil):
    n_s, t_s, _ = Ps.shape
    rows = cache.shape[2]
    lbuf = rows // (2 * SWA_HEADS)
    assert lbuf == dil * CHUNK, "window buffer must hold exactly one full window"
    blk = lambda cb: pl.BlockSpec((None, t_s, COL), lambda b: (b, 0, cb + g))
    if dil > t_s:
        per_pos = rows // lbuf
        cache = cache.reshape(cache.shape[0], n_s, lbuf // dil, dil * per_pos, LANE)
        cache_spec = pl.BlockSpec((None, None, lbuf // dil, t_s * per_pos, LANE), lambda b: (layer, b, 0, 0, 0))
    else:
        cache_spec = pl.BlockSpec((None, None, rows, LANE), lambda b: (layer, b, 0, 0))
    vmem = (2 * _nbytes((lbuf, 2 * COL), F32) + 2 * _nbytes((lbuf, 2 * COL), BF16)
            + 8 * _nbytes((SWA_HEADS * t_s, lbuf), F32))
    return pl.pallas_call(
        functools.partial(_swa_sample_kernel, dil=dil, lbuf=lbuf, t_s=t_s),
        grid=(n_s,),
        in_specs=[blk(CB_Q), blk(CB_K), blk(CB_V), cache_spec],
        out_specs=[pl.BlockSpec((None, t_s, COL), lambda b: (b, 0, 0)),
                   pl.BlockSpec((None, t_s, LANE), lambda b: (b, 0, 0))],
        out_shape=[jax.ShapeDtypeStruct((n_s, t_s, COL), F32),
                   jax.ShapeDtypeStruct((n_s, t_s, LANE), F32)],
        compiler_params=_params(vmem, ("parallel",)),
        name=f"swa_sample_g{g}",
    )(Ps, Ps, Ps, cache)


def _kv_tail_kernel(*refs, dil, skip, n_layers):
    srcs, o_ref = refs[:2 * n_layers], refs[2 * n_layers]
    layer = pl.program_id(0)
    rpt = TILE // dil
    per_pos = 2 * SWA_HEADS

    def body(k_ref, v_ref):
        for kv, ref in enumerate((k_ref, v_ref)):
            for h in range(SWA_HEADS):
                for r in range(dil):
                    rows = ref[r * rpt + skip:(r + 1) * rpt, h * LANE:(h + 1) * LANE].astype(o_ref.dtype)
                    first = per_pos * r + kv * SWA_HEADS + h
                    o_ref[pl.ds(first, rpt - skip, stride=per_pos * dil), :] = rows

    for l in range(n_layers):
        pl.when(layer == l)(functools.partial(body, srcs[2 * l], srcs[2 * l + 1]))


def _kv_tail(Ps, g, win, dil, n_p, s):
    n_layers = len(Ps)
    keep = -(-win // TILE) * TILE
    skip = keep - win
    assert skip == 0 or dil == 1
    n_src = keep // TILE
    tiles_per_seq = s // TILE
    out_rows = (TILE - skip) * 2 * SWA_HEADS

    def src_spec(l, cb):
        def imap(layer, b, j):
            here = layer == l
            return (jnp.where(here, (b + 1) * tiles_per_seq - n_src + j, 0), cb + g)
        return pl.BlockSpec((TILE, COL), imap)

    specs = [src_spec(l, cb) for l in range(n_layers) for cb in (CB_K, CB_V)]
    vmem = 4 * n_layers * _nbytes((TILE, COL), BF16) + 2 * _nbytes((out_rows, LANE), F32)
    out = pl.pallas_call(
        functools.partial(_kv_tail_kernel, dil=dil, skip=skip, n_layers=n_layers),
        grid=(n_layers, n_p, n_src),
        in_specs=specs,
        out_specs=pl.BlockSpec((out_rows, LANE), lambda layer, b, j: ((layer * n_p + b) * n_src + j, 0)),
        out_shape=jax.ShapeDtypeStruct((n_layers * n_p * n_src * out_rows, LANE), F32),
        compiler_params=_params(vmem, ("parallel", "parallel", "parallel")),
        name=f"kv_tail_g{g}",
    )(*[P for P in Ps for _ in range(2)])
    return out.reshape(n_layers, n_p, win, 2, SWA_HEADS, SWA_HEAD_DIM)


def _ret_head_inputs(qk_ref, v_ref, kout_ref, h):
    pair, half = h // 2, h % 2
    lane = lax.broadcasted_iota(jnp.int32, (CHUNK, LANE), 1)
    head_lanes = _idiv(lane, RET_QK_DIM) == half
    qt = qk_ref[:, pair * LANE:(pair + 1) * LANE]
    kt = qk_ref[:, COL // 2 + pair * LANE:COL // 2 + (pair + 1) * LANE]
    qm = jnp.where(head_lanes, qt, jnp.zeros_like(qt))
    kw = jnp.where(head_lanes, kt.astype(F32) * kout_ref[:, pair * LANE:(pair + 1) * LANE], 0.0)
    vh = v_ref[:, h * LANE:(h + 1) * LANE]
    return qm, kt, kw, vh


def _ret_finish(o, gate_ref, o_ref, h):
    cs = slice(h * LANE, (h + 1) * LANE)
    o_ref[:, cs] = (gate_ref[:, cs].astype(F32) * _rms(o)).astype(o_ref.dtype)


def _ret_prompt_kernel(*refs, n_seq):
    ins = refs[:3 * n_seq]
    decay_ref, qin_ref, kout_ref, gc_ref, o_ref, s_out_ref, s_ref, att_ref = refs[3 * n_seq:]
    i = pl.program_id(0)

    @pl.when(i == 0)
    def _():
        s_ref[...] = jnp.zeros_like(s_ref)

    pairs = [(b, h) for b in range(n_seq) for h in range(RET_HEADS)]
    head_in = lambda b, h: _ret_head_inputs(ins[3 * b], ins[3 * b + 1], kout_ref, h)
    for first in range(0, len(pairs), RET_PHASE_PAIRS):
        group = pairs[first:first + RET_PHASE_PAIRS]
        for b, h in group:
            qm, kt, _, _ = head_in(b, h)
            att_ref[b, h] = (_dot_nt(qm, kt) * decay_ref[h]).astype(BF16)
        for b, h in group:
            qm, _, _, vh = head_in(b, h)
            o = (_dot(att_ref[b, h], vh)
                 + _dot(qm, s_ref[b, h].astype(BF16)) * qin_ref[:, h * LANE:(h + 1) * LANE])
            _ret_finish(o, ins[3 * b + 2], o_ref.at[b], h)
        for b, h in group:
            _, _, kw, vh = head_in(b, h)
            s_ref[b, h] = s_ref[b, h] * gc_ref[h:h + 1, :] + _dot(kw.T.astype(BF16), vh)

    @pl.when(i == pl.num_programs(0) - 1)
    def _():
        for b in range(n_seq):
            for h in range(RET_HEADS):
                lo = (h % 2) * RET_QK_DIM
                s_out_ref[b, h] = s_ref[b, h, lo:lo + RET_QK_DIM, :]


def _ret_prompt(P, rt, n_p, s):
    nblk = s // CHUNK
    decay, qin, kout, gc = rt
    blk = lambda b, cb: pl.BlockSpec((CHUNK, COL), lambda i: (b * nblk + i, cb))
    const2 = lambda i: (0, 0)
    seq_specs = [blk(b, cb) for b in range(n_p) for cb in (CB_RQK, CB_RV, CB_RG)]
    st_shape = (n_p, RET_HEADS, RET_QK_DIM, RET_V_DIM)
    vmem = (8 * n_p * _nbytes((CHUNK, COL), BF16) + 2 * _nbytes(decay.shape, F32) + 4 * _nbytes(qin.shape, F32)
            + 3 * n_p * _nbytes((RET_HEADS, LANE, LANE), F32))
    o, st = pl.pallas_call(
        functools.partial(_ret_prompt_kernel, n_seq=n_p),
        grid=(nblk,),
        in_specs=seq_specs + [pl.BlockSpec(decay.shape, lambda i: (0, 0, 0)),
                              pl.BlockSpec(qin.shape, const2), pl.BlockSpec(kout.shape, const2),
                              pl.BlockSpec(gc.shape, const2)],
        out_specs=[pl.BlockSpec((n_p, CHUNK, COL), lambda i: (0, i, 0)),
                   pl.BlockSpec(st_shape, lambda i: (0, 0, 0, 0))],
        out_shape=[jax.ShapeDtypeStruct((n_p, s, COL), BF16), jax.ShapeDtypeStruct(st_shape, F32)],
        scratch_shapes=[pltpu.VMEM((n_p, RET_HEADS, LANE, LANE), F32),
                        pltpu.VMEM((n_p, RET_HEADS, CHUNK, CHUNK), BF16)],
        compiler_params=_params(vmem, ("arbitrary",)),
        name="ret_prompt",
    )(*([P] * (3 * n_p)), decay, qin, kout, gc)
    return o.reshape(n_p * s, COL), st


def _ret_sample_kernel(qk_ref, v_ref, gate_ref, s0_ref, decay_ref, qin_ref, kout_ref, gc_ref,
                       o_ref, s_out_ref, *, t_s):
    row = lax.broadcasted_iota(jnp.int32, (CHUNK, LANE), 0)
    for h in range(RET_HEADS):
        lo = (h % 2) * RET_QK_DIM
        qm, kt, kw, vh = _ret_head_inputs(qk_ref, v_ref, kout_ref, h)
        att = _dot_nt(qm, kt) * decay_ref[h]
        o = _dot(att.astype(BF16), vh)
        inter = jnp.zeros((CHUNK, LANE), F32)
        for sq in range(CHUNK // t_s):
            seq_rows = _idiv(row, t_s) == sq
            st = s0_ref[sq, h]
            st2 = jnp.concatenate([st, st], axis=0).astype(BF16)
            inter = jnp.where(seq_rows, _dot(qm, st2), inter)
            upd = _dot(jnp.where(seq_rows, kw, 0.0).T.astype(BF16), vh)
            s_out_ref[sq, h] = st * gc_ref[h:h + 1, :] + upd[lo:lo + RET_QK_DIM, :]
        o = o + inter * qin_ref[:, h * LANE:(h + 1) * LANE]
        _ret_finish(o, gate_ref, o_ref, h)


def _ret_sample(P, state, layer, rt, Tp, t_s):
    n_s = state.shape[1]
    spb = CHUNK // t_s
    base = Tp // CHUNK
    decay, qin, kout, gc = rt
    blk = lambda cb: pl.BlockSpec((CHUNK, COL), lambda i: (base + i, cb))
    const2 = lambda i: (0, 0)
    st_shape = (spb, RET_HEADS, RET_QK_DIM, RET_V_DIM)
    vmem = (8 * _nbytes((CHUNK, COL), BF16) + 2 * _nbytes(decay.shape, F32) + 4 * _nbytes(qin.shape, F32)
            + 4 * _nbytes(st_shape, F32))
    return pl.pallas_call(
        functools.partial(_ret_sample_kernel, t_s=t_s),
        grid=(n_s // spb,),
        in_specs=[blk(CB_RQK), blk(CB_RV), blk(CB_RG),
                  pl.BlockSpec((None,) + st_shape, lambda i: (layer, i, 0, 0, 0)),
                  pl.BlockSpec(decay.shape, lambda i: (0, 0, 0)),
                  pl.BlockSpec(qin.shape, const2), pl.BlockSpec(kout.shape, const2),
                  pl.BlockSpec(gc.shape, const2)],
        out_specs=[pl.BlockSpec((CHUNK, COL), lambda i: (i, 0)),
                   pl.BlockSpec(st_shape, lambda i: (i, 0, 0, 0))],
        out_shape=[jax.ShapeDtypeStruct((n_s * t_s, COL), BF16),
                   jax.ShapeDtypeStruct((n_s, RET_HEADS, RET_QK_DIM, RET_V_DIM), F32)],
        compiler_params=_params(vmem, ("parallel",)),
        name="ret_sample",
    )(P, P, P, state, decay, qin, kout, gc)


def _branch_kernel(xa_ref, xb_ref, a_ref, g0_ref, g1_ref, g2_ref, wb_ref, wo_ref, ex_ref,
                   po0, po1, po2, pl0, pl1, pl2, pc, so0, so1, so2, sl0, sl1, sl2, sc,
                   y_ref, on_ref, ln_ref, *, n_prompt_tiles):
    i = pl.program_id(0)

    def spread(packed):
        hi = packed.astype(BF16)
        lo = (packed - hi.astype(F32)).astype(BF16)
        return _dot(jnp.concatenate([hi, lo], axis=1), ex_ref[...])

    def natural(o_ref, l_ref, k, dil):
        if dil == 1:
            return (lambda h: o_ref[:, h * LANE:(h + 1) * LANE].astype(F32)), l_ref[...]
        rpt = TILE // dil
        for r in range(dil):
            rows = slice(r * rpt, (r + 1) * rpt)
            ln_ref[k, pl.ds(r, rpt, stride=dil), :] = l_ref[rows, :]
            for h in range(SWA_HEADS):
                on_ref[k, h, pl.ds(r, rpt, stride=dil), :] = o_ref[rows, h * LANE:(h + 1) * LANE].astype(F32)
        return (lambda h: on_ref[k, h]), ln_ref[k]

    def body(o_refs, l_refs, c_ref, permuted):
        groups = [natural(o_refs[k], l_refs[k], k, SWA_PATTERNS[k][1] if permuted else 1)
                  for k in range(len(SWA_PATTERNS))]
        ls = [l for _, l in groups]
        lmax = functools.reduce(jnp.maximum, ls)
        es = [jnp.exp(l - lmax) for l in ls]
        inv = 1.0 / sum(es)
        weights = [spread(e * inv) for e in es]
        heads = []
        for h in range(SWA_HEADS):
            cs = slice(h * LANE, (h + 1) * LANE)
            heads.append(sum(w[:, cs] * go(h) for w, (go, _) in zip(weights, groups)).astype(BF16))
        mix = jnp.concatenate(heads, axis=1)
        merged = (g0_ref[...].astype(F32) * _dot(a_ref[...], wb_ref[0])
                  + g1_ref[...].astype(F32) * _dot(mix, wb_ref[1])
                  + g2_ref[...].astype(F32) * _dot(c_ref[...], wb_ref[2]))
        y_ref[...] = _tile_rows(i, n_prompt_tiles, xa_ref, xb_ref) + _dot(merged.astype(BF16), wo_ref[...])

    @pl.when(i < n_prompt_tiles)
    def _():
        body((po0, po1, po2), (pl0, pl1, pl2), pc, True)

    @pl.when(i >= n_prompt_tiles)
    def _():
        body((so0, so1, so2), (sl0, sl1, sl2), sc, False)


def _branch(xa, xb, out_a, P, wb, wo, prompt_set, sample_set, n_prompt_tiles):
    T = (n_prompt_tiles + 1) * TILE
    assert sample_set[0].shape[0] == TILE, "the sample rows must form exactly one tile"
    n_g = len(SWA_PATTERNS)
    last = n_prompt_tiles - 1
    row = lambda i: (i, 0)
    gate = lambda k: pl.BlockSpec((TILE, D_MODEL), lambda i: (i, CB_GATE * COL // D_MODEL + k))
    once = pl.Buffered(1)
    p_o = pl.BlockSpec((TILE, COL), lambda i: (jnp.minimum(i, last), 0))
    p_l = pl.BlockSpec((TILE, LANE), lambda i: (jnp.minimum(i, last), 0))
    s_o = pl.BlockSpec((TILE, COL), lambda i: (0, 0))
    s_l = pl.BlockSpec((TILE, LANE), lambda i: (0, 0))
    src = jnp.arange(2 * LANE) % LANE
    dst_head = jnp.arange(COL) // LANE
    expand = (src[:, None] == dst_head[None, :] * LSE_LANES).astype(BF16)
    vmem = (4 * _nbytes((TILE, D_MODEL), F32) + 6 * _nbytes((TILE, D_MODEL), BF16)
            + 2 * (2 * n_g + 2) * (_nbytes((TILE, COL), BF16) + _nbytes((TILE, COL), F32))
            + _nbytes(wb.shape, BF16) + _nbytes(wo.shape, BF16)
            + 4 * n_g * _nbytes((TILE, COL), F32) + 4 * _nbytes((TILE, D_MODEL), F32))
    return pl.pallas_call(
        functools.partial(_branch_kernel, n_prompt_tiles=n_prompt_tiles),
        grid=(T // TILE,),
        in_specs=_x_specs(xa, xb) + [
                  pl.BlockSpec((TILE, COL), row),
                  gate(0), gate(1), gate(2),
                  pl.BlockSpec(wb.shape, lambda i: (0, 0, 0), pipeline_mode=once),
                  pl.BlockSpec(wo.shape, lambda i: (0, 0), pipeline_mode=once),
                  pl.BlockSpec(expand.shape, lambda i: (0, 0)),
                  p_o, p_o, p_o, p_l, p_l, p_l, p_o,
                  s_o, s_o, s_o, s_l, s_l, s_l, s_o],
        out_specs=pl.BlockSpec((TILE, D_MODEL), row),
        out_shape=jax.ShapeDtypeStruct((T, D_MODEL), F32),
        scratch_shapes=[pltpu.VMEM((n_g, SWA_HEADS, TILE, LANE), F32),
                        pltpu.VMEM((n_g, TILE, LANE), F32)],
        compiler_params=_params(vmem, ("parallel",)),
        name="branch_merge",
    )(xa, xb, out_a, P, P, P, wb, wo, expand, *prompt_set, *sample_set)


def _swiglu_into(hb, w1_ref, w3_ref, w2_ref, acc_ref, fc):
    n_fc = w1_ref.shape[-1] // fc
    for f in range(n_fc):
        cols = slice(f * fc, (f + 1) * fc)
        a = _dot(hb, w1_ref[:, cols])
        b = _dot(hb, w3_ref[:, cols])
        part = _dot((a * _sigmoid(a) * b).astype(BF16), w2_ref[cols, :])
        if f == 0:
            acc_ref[...] = part
        else:
            acc_ref[...] += part


def _dense_ffn_kernel(x_ref, g2_ref, w1_ref, w3_ref, w2_ref, y_ref, acc_ref, *, fc):
    x = x_ref[...]
    hb = (_rms(x) * g2_ref[...]).astype(BF16)
    _swiglu_into(hb, w1_ref, w3_ref, w2_ref, acc_ref, fc)
    y_ref[...] = x + acc_ref[...]


def _dense_ffn(x, g2, w1b, w3b, w2b):
    T = x.shape[0]
    F = w1b.shape[1]
    tm = _pick(T, (768, 512, 384, 256, 128))
    fc = _pick(F, (512, 256, 128))
    row = lambda i: (i, 0)
    once = pl.Buffered(1)
    vmem = (5 * _nbytes((tm, D_MODEL), F32) + 3 * _nbytes(w1b.shape, BF16) + _nbytes((tm, D_MODEL), BF16)
            + 4 * _nbytes((tm, fc), F32))
    return pl.pallas_call(
        functools.partial(_dense_ffn_kernel, fc=fc),
        grid=(T // tm,),
        in_specs=[pl.BlockSpec((tm, D_MODEL), row),
                  pl.BlockSpec((1, D_MODEL), lambda i: (0, 0)),
                  pl.BlockSpec(w1b.shape, lambda i: (0, 0), pipeline_mode=once),
                  pl.BlockSpec(w3b.shape, lambda i: (0, 0), pipeline_mode=once),
                  pl.BlockSpec(w2b.shape, lambda i: (0, 0), pipeline_mode=once)],
        out_specs=pl.BlockSpec((tm, D_MODEL), row),
        out_shape=jax.ShapeDtypeStruct((T, D_MODEL), F32),
        scratch_shapes=[pltpu.VMEM((tm, D_MODEL), F32)],
        compiler_params=_params(vmem, ("parallel",)),
        name="dense_ffn",
    )(x, g2, w1b, w3b, w2b)


def _router_kernel(x_ref, g2_ref, wr_ref, e0_ref, e1_ref, w0_ref, w1_ref):
    hb = (_rms(x_ref[...]) * g2_ref[...]).astype(BF16)
    logits = _dot(hb, wr_ref[...])
    lane = lax.broadcasted_iota(jnp.int32, logits.shape, 1)
    logits = jnp.where(lane < N_EXPERTS, logits, -jnp.inf)
    lane_f = lane.astype(F32)
    m1 = jnp.max(logits, axis=-1, keepdims=True)
    i1 = jnp.min(jnp.where(logits == m1, lane_f, float(LANE)), axis=-1, keepdims=True)
    rest = jnp.where(lane_f == i1, -jnp.inf, logits)
    m2 = jnp.max(rest, axis=-1, keepdims=True)
    i2 = jnp.min(jnp.where(rest == m2, lane_f, float(LANE)), axis=-1, keepdims=True)
    e = jnp.exp(m2 - m1)
    e0_ref[...] = jnp.broadcast_to(i1, logits.shape).astype(jnp.int32)
    e1_ref[...] = jnp.broadcast_to(i2, logits.shape).astype(jnp.int32)
    w0_ref[...] = jnp.broadcast_to(1.0 / (1.0 + e), logits.shape)
    w1_ref[...] = jnp.broadcast_to(e / (1.0 + e), logits.shape)


def _router(x, g2, wr_pad):
    T = x.shape[0]
    tm = _pick(T, (768, 512, 384, 256, 128))
    row = lambda i: (i, 0)
    out = pl.BlockSpec((tm, LANE), row)
    vmem = 4 * _nbytes((tm, D_MODEL), F32) + 2 * _nbytes(wr_pad.shape, BF16) + 16 * _nbytes((tm, LANE), F32)
    return pl.pallas_call(
        _router_kernel,
        grid=(T // tm,),
        in_specs=[pl.BlockSpec((tm, D_MODEL), row),
                  pl.BlockSpec((1, D_MODEL), lambda i: (0, 0)),
                  pl.BlockSpec(wr_pad.shape, lambda i: (0, 0))],
        out_specs=[out, out, out, out],
        out_shape=[jax.ShapeDtypeStruct((T, LANE), jnp.int32), jax.ShapeDtypeStruct((T, LANE), jnp.int32),
                   jax.ShapeDtypeStruct((T, LANE), F32), jax.ShapeDtypeStruct((T, LANE), F32)],
        compiler_params=_params(vmem, ("parallel",)),
        name="moe_router",
    )(x, g2, wr_pad)


RUN_ALIGN = SUBLANE


def _run_pieces(count, max_rows):
    pieces, off = [], 0
    size = max_rows
    while size >= RUN_ALIGN:
        active = (count & size) != 0
        pieces.append((size, off, active))
        off = off + jnp.where(active, size, 0)
        size //= 2
    return pieces


def _packed_rows(i, e0, e1, lower_ref, loc_ref):
    lane = lax.broadcasted_iota(jnp.int32, e0.shape, 1)
    picks = jnp.where((lane == e0) | (lane == e1), 1.0, 0.0)
    rank = _dot(lower_ref[...], picks.astype(BF16))
    lane1 = lax.broadcasted_iota(jnp.int32, (1, LANE), 1)
    run_start = jnp.zeros((1, LANE), F32)
    for e in range(N_EXPERTS):
        run_start = jnp.where(lane1 == e, loc_ref[i * N_EXPERTS + e].astype(F32), run_start)
    row_of = rank + run_start
    return tuple(jnp.sum(jnp.where(lane == ek, row_of, 0.0), axis=-1, keepdims=True) for ek in (e0, e1))


DISPATCH_BUFFERS = 3
RUN_ROWS = 640


def _dispatch_kernel(base_ref, cnt_ref, loc_ref, pend_ref, x_ref, g2_ref, e0_ref, e1_ref, lower_ref, xs_ref,
                     comp_ref, sem, *, tme, n_tiles):
    i = pl.program_id(0)
    n_steps = pl.num_programs(0)
    n_buf = comp_ref.shape[0]
    cur = i % n_buf
    half = tme // 2

    @pl.when(i == 0)
    def _():
        comp_ref[0, 0:half] = jnp.zeros((half, D_MODEL), comp_ref.dtype)

        def zero_tile(start):
            return [pltpu.make_async_copy(
                        comp_ref.at[0, pl.ds(0, half), :],
                        xs_ref.at[pl.ds(pl.multiple_of(start + k * half, half), half), :], sem.at[0])
                    for k in range(2)]

        def tail_loop(fn):
            def body(t, carry):
                for c in zero_tile(t * tme):
                    fn(c)
                return carry
            lax.fori_loop(pend_ref[N_EXPERTS - 1] // tme, n_tiles, body, 0)

        def group_tails(fn):
            for e in range(N_EXPERTS):
                prev_end = pend_ref[e - 1] if e else 0

                @pl.when(pend_ref[e] > prev_end)
                def _():
                    for c in zero_tile(pend_ref[e] - tme):
                        fn(c)

        group_tails(lambda c: c.start())
        tail_loop(lambda c: c.start())
        group_tails(lambda c: c.wait())
        tail_loop(lambda c: c.wait())

    h = (_rms(x_ref[...]) * g2_ref[...]).astype(BF16)
    pos0, pos1 = _packed_rows(i, e0_ref[...], e1_ref[...], lower_ref, loc_ref)
    slot = lax.broadcasted_iota(jnp.int32, (TILE, RUN_ROWS), 1).astype(F32)
    hit = jnp.where((slot == pos0) | (slot == pos1), 1.0, 0.0)
    comp_ref[cur] = _dot(hit.T.astype(BF16), h)

    def for_copies(tile, buf, fn):
        for e in range(N_EXPERTS):
            k = tile * N_EXPERTS + e
            cnt, dst, loc = cnt_ref[k], base_ref[k], loc_ref[k]
            for size, off, active in _run_pieces(cnt, TILE):
                c = pltpu.make_async_copy(
                    comp_ref.at[buf, pl.ds(pl.multiple_of(loc + off, RUN_ALIGN), size), :],
                    xs_ref.at[pl.ds(pl.multiple_of(dst + off, RUN_ALIGN), size), :], sem.at[buf])
                pl.when(active)(functools.partial(fn, c))

    def wait_copies(tile, buf):
        rows = pl.multiple_of(sum(cnt_ref[tile * N_EXPERTS + e] for e in range(N_EXPERTS)), RUN_ALIGN)
        pltpu.make_async_copy(comp_ref.at[buf, pl.ds(0, rows), :], xs_ref.at[pl.ds(0, rows), :],
                              sem.at[buf]).wait()

    for_copies(i, cur, lambda c: c.start())

    @pl.when(i >= n_buf - 1)
    def _():
        wait_copies(i - (n_buf - 1), (i + 1) % n_buf)

    @pl.when(i == n_steps - 1)
    def _():
        for back in range(n_buf - 2, -1, -1):

            @pl.when(i >= back)
            def _():
                wait_copies(i - back, (i - back) % n_buf)


def _dispatch(x, g2, e0, e1, base, cnt, loc, pend, lower, n_tiles, tme):
    T = x.shape[0]
    assert tme == 2 * TILE
    row = lambda i, *_: (i, 0)
    assert RUN_ROWS >= 2 * TILE + N_EXPERTS * (RUN_ALIGN - 1) and RUN_ROWS % LANE == 0
    vmem = (3 * _nbytes((TILE, D_MODEL), F32) + DISPATCH_BUFFERS * _nbytes((RUN_ROWS, D_MODEL), F32)
            + 8 * _nbytes((RUN_ROWS, TILE), F32))
    grid_spec = pltpu.PrefetchScalarGridSpec(
        num_scalar_prefetch=4,
        grid=(T // TILE,),
        in_specs=[pl.BlockSpec((TILE, D_MODEL), row),
                  pl.BlockSpec((1, D_MODEL), lambda i, *_: (0, 0)),
                  pl.BlockSpec((TILE, LANE), row), pl.BlockSpec((TILE, LANE), row),
                  pl.BlockSpec((TILE, TILE), lambda i, *_: (0, 0))],
        out_specs=pl.BlockSpec(memory_space=pl.ANY),
        scratch_shapes=[pltpu.VMEM((DISPATCH_BUFFERS, RUN_ROWS, D_MODEL), F32),
                        pltpu.SemaphoreType.DMA((DISPATCH_BUFFERS,))],
    )
    return pl.pallas_call(
        functools.partial(_dispatch_kernel, tme=tme, n_tiles=n_tiles),
        grid_spec=grid_spec,
        out_shape=jax.ShapeDtypeStruct((n_tiles * tme, D_MODEL), F32),
        compiler_params=_params(vmem, ("arbitrary",)),
        name="moe_dispatch",
    )(base, cnt, loc, pend, x, g2, e0, e1, lower)


def _expert_ffn_kernel(te_ref, used_ref, xs_ref, w1_ref, w3_ref, w2_ref, y_ref, acc_ref, *, fc):
    i = pl.program_id(0)

    @pl.when(i < used_ref[0])
    def _():
        _swiglu_into(xs_ref[...].astype(BF16), w1_ref, w3_ref, w2_ref, acc_ref, fc)
        y_ref[...] = acc_ref[...]

    @pl.when(i >= used_ref[0])
    def _():
        y_ref[...] = jnp.zeros_like(y_ref)


def _expert_ffn(xs, tile_expert, n_used, we1b, we3b, we2b, tme):
    n_rows = xs.shape[0]
    F = we1b.shape[2]
    fc = _pick(F, (512, 256, 128))
    wspec = lambda shape: pl.BlockSpec((None,) + shape, lambda i, te, nu: (te[i], 0, 0))
    vmem = (5 * _nbytes((tme, D_MODEL), F32) + 6 * _nbytes(we1b.shape[1:], BF16) + _nbytes((tme, D_MODEL), BF16)
            + 4 * _nbytes((tme, fc), F32))
    grid_spec = pltpu.PrefetchScalarGridSpec(
        num_scalar_prefetch=2,
        grid=(n_rows // tme,),
        in_specs=[pl.BlockSpec((tme, D_MODEL), lambda i, te, nu: (jnp.minimum(i, jnp.maximum(nu[0], 1) - 1), 0)),
                  wspec(we1b.shape[1:]), wspec(we3b.shape[1:]), wspec(we2b.shape[1:])],
        out_specs=pl.BlockSpec((tme, D_MODEL), lambda i, te, nu: (i, 0)),
        scratch_shapes=[pltpu.VMEM((tme, D_MODEL), F32)],
    )
    return pl.pallas_call(
        functools.partial(_expert_ffn_kernel, fc=fc),
        grid_spec=grid_spec,
        out_shape=jax.ShapeDtypeStruct((n_rows, D_MODEL), F32),
        compiler_params=_params(vmem, ("arbitrary",)),
        name="moe_expert_ffn",
    )(tile_expert, n_used, xs, we1b, we3b, we2b)


def _combine_kernel(base_ref, cnt_ref, loc_ref, x_ref, e0_ref, e1_ref, w0_ref, w1_ref, lower_ref, y_hbm_ref,
                    o_ref, o_last_ref, buf_ref, sem):
    i = pl.program_id(0)
    n_steps = pl.num_programs(0)
    cur = i % 2

    def for_copies(tile, buf, fn):
        for e in range(N_EXPERTS):
            k = tile * N_EXPERTS + e
            cnt, src, loc = cnt_ref[k], base_ref[k], loc_ref[k]
            for size, off, active in _run_pieces(cnt, TILE):
                c = pltpu.make_async_copy(
                    y_hbm_ref.at[pl.ds(pl.multiple_of(src + off, RUN_ALIGN), size), :],
                    buf_ref.at[buf, pl.ds(pl.multiple_of(loc + off, RUN_ALIGN), size), :], sem.at[buf])
                pl.when(active)(functools.partial(fn, c))

    @pl.when(i == 0)
    def _():
        buf_ref[...] = jnp.zeros_like(buf_ref)
        for_copies(0, 0, lambda c: c.start())

    @pl.when(i + 1 < n_steps)
    def _():
        for_copies(i + 1, 1 - cur, lambda c: c.start())

    rows = pl.multiple_of(sum(cnt_ref[i * N_EXPERTS + e] for e in range(N_EXPERTS)), RUN_ALIGN)
    pltpu.make_async_copy(y_hbm_ref.at[pl.ds(0, rows), :], buf_ref.at[cur, pl.ds(0, rows), :],
                          sem.at[cur]).wait()

    positions = _packed_rows(i, e0_ref[...], e1_ref[...], lower_ref, loc_ref)
    slot = lax.broadcasted_iota(jnp.int32, (TILE, RUN_ROWS), 1).astype(F32)
    yb = buf_ref[cur].astype(BF16)
    wide = lambda a, n: jnp.concatenate([a] * n, axis=1)
    acc = x_ref[...]
    for pos, wk in zip(positions, (w0_ref, w1_ref)):
        rows = _dot(jnp.where(slot == pos, 1.0, 0.0).astype(BF16), yb)
        acc = acc + wide(wk[...], D_MODEL // LANE) * rows

    @pl.when(i < n_steps - 1)
    def _():
        o_ref[...] = acc

    @pl.when(i == n_steps - 1)
    def _():
        o_last_ref[...] = acc


def _combine(x, e0, e1, w0, w1, base, cnt, loc, lower, y_rows):
    T = x.shape[0]
    row = lambda i, *_: (i, 0)
    lane_spec = pl.BlockSpec((TILE, LANE), row)
    vmem = (5 * _nbytes((TILE, D_MODEL), F32) + 3 * _nbytes((RUN_ROWS, D_MODEL), F32)
            + 8 * _nbytes((TILE, LANE), F32) + 8 * _nbytes((TILE, RUN_ROWS), F32)
            + 4 * _nbytes((TILE, D_MODEL), F32))
    grid_spec = pltpu.PrefetchScalarGridSpec(
        num_scalar_prefetch=3,
        grid=(T // TILE,),
        in_specs=[pl.BlockSpec((TILE, D_MODEL), row), lane_spec, lane_spec, lane_spec, lane_spec,
                  pl.BlockSpec((TILE, TILE), lambda i, *_: (0, 0)),
                  pl.BlockSpec(memory_space=pl.ANY)],
        out_specs=[pl.BlockSpec((TILE, D_MODEL), lambda i, *_: (jnp.minimum(i, T // TILE - 2), 0)),
                   pl.BlockSpec((TILE, D_MODEL), lambda i, *_: (0, 0))],
        scratch_shapes=[pltpu.VMEM((2, RUN_ROWS, D_MODEL), F32), pltpu.SemaphoreType.DMA((2,))],
    )
    return pl.pallas_call(
        _combine_kernel,
        grid_spec=grid_spec,
        out_shape=[jax.ShapeDtypeStruct((T - TILE, D_MODEL), F32), jax.ShapeDtypeStruct((TILE, D_MODEL), F32)],
        compiler_params=_params(vmem, ("arbitrary",)),
        name="moe_combine",
    )(base, cnt, loc, x, e0, e1, w0, w1, lower, y_rows)


def _moe(x, g2, wr_pad, we1b, we3b, we2b):
    T = x.shape[0]
    tme = 2 * TILE
    n_tt = T // TILE
    e0, e1, w0, w1 = _router(x, g2, wr_pad)
    experts = jnp.arange(N_EXPERTS, dtype=jnp.int32)
    picks = (e0[:, :1] == experts[None, :]) | (e1[:, :1] == experts[None, :])
    cnt = jnp.sum(picks.reshape(n_tt, TILE, N_EXPERTS).astype(jnp.int32), axis=1)
    cnt = (cnt + RUN_ALIGN - 1) // RUN_ALIGN * RUN_ALIGN
    counts = jnp.sum(cnt, axis=0)
    padded = (counts + tme - 1) // tme * tme
    pend = jnp.cumsum(padded).astype(jnp.int32)
    base = ((pend - padded)[None, :] + jnp.cumsum(cnt, axis=0) - cnt).astype(jnp.int32)
    loc = (jnp.cumsum(cnt, axis=1) - cnt).astype(jnp.int32)
    max_rows = T * TOP_K + n_tt * N_EXPERTS * (RUN_ALIGN - 1)
    n_tiles = -(-max_rows // tme) + N_EXPERTS
    tile_start = jnp.arange(n_tiles, dtype=jnp.int32) * tme
    tile_expert = jnp.minimum(jnp.sum(pend[None, :] <= tile_start[:, None], axis=1), N_EXPERTS - 1).astype(jnp.int32)
    n_used = (pend[N_EXPERTS - 1:] // tme).astype(jnp.int32)
    tok = jnp.arange(TILE)
    lower = (tok[None, :] < tok[:, None]).astype(BF16)
    base_f, cnt_f, loc_f = base.reshape(-1), cnt.reshape(-1).astype(jnp.int32), loc.reshape(-1)
    xs = _dispatch(x, g2, e0, e1, base_f, cnt_f, loc_f, pend, lower, n_tiles, tme)
    y_rows = _expert_ffn(xs, tile_expert, n_used, we1b, we3b, we2b, tme)
    return _combine(x, e0, e1, w0, w1, base_f, cnt_f, loc_f, lower, y_rows)


def _class_major(a, dil):
    n = a.shape[0]
    return a.reshape((n // TILE, TILE // dil, dil) + a.shape[1:]).swapaxes(1, 2).reshape(a.shape)


def _perm_matrices():
    eye = jnp.eye(TILE, dtype=BF16)
    return jnp.stack([_class_major(eye, dil) for _, dil in SWA_PATTERNS[1:]])


def _rope_tables(pos, Tp):
    freqs = ROPE_THETA ** (-jnp.arange(0, SWA_HEAD_DIM, 2, dtype=F32) / SWA_HEAD_DIM)
    rfreqs = 1.0 / (ROPE_THETA ** jnp.linspace(0.0, 1.0, RET_QK_DIM // 2, dtype=F32))

    ang = pos.astype(F32)[:, None] * freqs[None, :]
    c, s = jnp.cos(ang), jnp.sin(ang)
    cos2, sin2 = jnp.concatenate([c, c], axis=1), jnp.concatenate([-s, s], axis=1)
    per_group = lambda t: jnp.stack([jnp.concatenate([_class_major(t[:Tp], dil), t[Tp:]])
                                     for _, dil in SWA_PATTERNS])
    cosb, sinb = per_group(cos2), per_group(sin2)
    rang = pos.astype(F32)[:, None] * rfreqs[None, :]
    c, s = jnp.cos(rang), jnp.sin(rang)
    cosc = jnp.concatenate([c, c, c, c], axis=1)
    sinc = jnp.concatenate([-s, s, -s, s], axis=1)
    return cosb, sinb, cosc, sinc


def _ret_tables(c_len):
    log_g = jnp.log1p(-jnp.exp2(-5.0 - jnp.arange(RET_HEADS, dtype=F32)))
    r = jnp.arange(CHUNK)
    i = (r % c_len).astype(F32)
    same = (r[:, None] // c_len) == (r[None, :] // c_len)
    dist = i[:, None] - i[None, :]
    decay = jnp.where(same[None] & (dist >= 0)[None],
                      jnp.exp(log_g[:, None, None] * jnp.maximum(dist, 0.0)[None]), 0.0)
    qin = jnp.repeat(jnp.exp(log_g[None, :] * (i[:, None] + 1.0)), RET_V_DIM, axis=1)
    kout = jnp.repeat(jnp.exp(log_g[None, :] * (c_len - 1.0 - i)[:, None]), RET_QK_DIM, axis=1)
    gc = jnp.broadcast_to(jnp.exp(log_g * c_len)[:, None], (RET_HEADS, LANE))
    gc = jnp.concatenate([gc, jnp.zeros((SUBLANE - RET_HEADS, LANE), F32)], axis=0)
    return decay.astype(F32), qin.astype(F32), kout.astype(F32), gc.astype(F32)


def _mixa_tables(w_s, b_s, t_s):
    w_p = jnp.tril(w_s)
    w8 = jnp.tril(w_s[:, :t_s, :t_s])
    eye = jnp.eye(CHUNK // t_s, dtype=w_s.dtype)
    w_smp = jax.vmap(lambda m: jnp.kron(eye, m))(w8)
    w2 = jnp.stack([w_p, w_smp]).astype(BF16)
    b_p = jnp.repeat(b_s.T, LANE, axis=1)
    b_smp = jnp.repeat(jnp.tile(b_s[:, :t_s].T, (CHUNK // t_s, 1)), LANE, axis=1)
    return w2, jnp.stack([b_p, b_smp]).astype(F32)


def kernel(x_prompt, x_sample, cache_swa_kv0, cache_swa_kv1, cache_swa_kv2, state_ret, norm1_g, w_in, norm_v_g, w_s, b_s, q_norm_g, k_norm_g, w_branch, w_out, norm2_g, w1, w3, w2, w_router, we1, we3, we2):
    n_p, s, d = x_prompt.shape
    n_s, t_s, _ = x_sample.shape
    depth = w_in.shape[0]
    Tp, Ts = n_p * s, n_s * t_s
    T = Tp + Ts
    max_win, max_dil = SWA_PATTERNS[-1]
    assert d == D_MODEL and Ts == TILE and CHUNK % t_s == 0
    assert s % (CHUNK * max_dil) == 0 and s >= max_win
    n_pt = Tp // TILE

    xa, xb = x_prompt.reshape(Tp, d), x_sample.reshape(Ts, d)
    pos = jnp.concatenate([jnp.arange(s, dtype=jnp.int32),
                           jnp.tile(PAST_LEN + jnp.arange(t_s, dtype=jnp.int32), n_s)])
    tabs = _rope_tables(pos, s)
    perm = _perm_matrices()
    rt_prompt = _ret_tables(CHUNK)
    rt_sample = _ret_tables(t_s)
    caches = tuple(c.reshape(c.shape[0], c.shape[1], -1, SWA_HEAD_DIM)
                   for c in (cache_swa_kv0, cache_swa_kv1, cache_swa_kv2))

    projections = []
    s_kv = [[] for _ in SWA_PATTERNS]
    p_ret, s_ret, s_v = [], [], []
    for layer in range(depth):
        if layer == 0:
            later = [(w_in, 1)] + [(w, 0) for w in (w_branch, w_out, w1, w3, w2, we1, we3, we2)]
            P, (w_in_later, w_branch_b, w_out_b, w1_b, w3_b, w2_b, we1_b, we3_b, we2_b) = _inproj(
                xa, xb, norm1_g[0][None, :], w_in[0].astype(BF16), perm, tabs, norm_v_g[0][None, :],
                q_norm_g[0][None, :], k_norm_g[0][None, :], n_pt, s // TILE, later)
        else:
            P, _ = _inproj(xa, xb, norm1_g[layer][None, :], w_in_later[layer - 1], perm, tabs,
                           norm_v_g[layer][None, :], q_norm_g[layer][None, :], k_norm_g[layer][None, :],
                           n_pt, s // TILE)

        w2a, b2a = _mixa_tables(w_s[layer], b_s[layer], t_s)
        out_a = _mixer_a(P, w2a, b2a, Tp // CHUNK)

        Ps = P[Tp:].astype(F32).reshape(n_s, t_s, IN_WIDTH)
        po, plse, so, slse = [], [], [], []
        for g, (win, dil) in enumerate(SWA_PATTERNS):
            o_g, l_g = _swa_prompt(P, g, dil, n_p, s)
            os_g, ls_g = _swa_sample(Ps, caches[g], layer, g, dil)
            po.append(o_g)
            plse.append(l_g)
            so.append(os_g.reshape(Ts, COL).astype(BF16))
            slse.append(ls_g.reshape(Ts, LANE))

        out_c, ret_p = _ret_prompt(P, rt_prompt, n_p, s)
        out_cs, ret_s = _ret_sample(P, state_ret, layer, rt_sample, Tp, t_s)

        x = _branch(xa, xb, out_a, P, w_branch_b[layer], w_out_b[layer],
                    (*po, *plse, out_c), (*so, *slse, out_cs), n_pt)

        g2 = norm2_g[layer][None, :]
        i = layer // 2
        if layer % 2 == 0:
            x = _dense_ffn(x, g2, w1_b[i], w3_b[i], w2_b[i])
            xa = xb = x
        else:
            wr_pad = jnp.zeros((D_MODEL, LANE), BF16).at[:, :N_EXPERTS].set(w_router[i].astype(BF16))
            xa, xb = _moe(x, g2, wr_pad, we1_b[i], we3_b[i], we2_b[i])

        projections.append(P)
        for g, (win, dil) in enumerate(SWA_PATTERNS):
            kcols = slice((CB_K + g) * COL, (CB_K + g + 1) * COL)
            vcols = slice((CB_V + g) * COL, (CB_V + g + 1) * COL)
            ks = P[Tp:, kcols].reshape(n_s, t_s, SWA_HEADS, SWA_HEAD_DIM)
            vs = P[Tp:, vcols].reshape(n_s, t_s, SWA_HEADS, SWA_HEAD_DIM)
            s_kv[g].append(jnp.stack([ks, vs], axis=2).astype(F32))
        p_ret.append(ret_p)
        s_ret.append(ret_s)
        s_v.append(P[Tp:, CB_AV * COL:(CB_AV + 1) * COL].astype(F32).reshape(n_s, t_s, COL))

    y_prompt = xa[:Tp].reshape(n_p, s, d)
    y_sample = xb[xb.shape[0] - Ts:].reshape(n_s, t_s, d)
    p_kv = [_kv_tail(projections, g, win, dil, n_p, s) for g, (win, dil) in enumerate(SWA_PATTERNS)]
    return (y_prompt, y_sample, p_kv[0], p_kv[1], p_kv[2], jnp.stack(p_ret),
            jnp.stack(s_kv[0]), jnp.stack(s_kv[1]), jnp.stack(s_kv[2]), jnp.stack(s_ret),
            jnp.stack(s_v))
```
